```python
import math
import jax
import jax.numpy as jnp
from jax import lax
import numpy as np

D_MODEL = 2048
BATCH = 2
SEQ = 4096
DEPTH = 4
DEC_BATCH = 32
DEC_SEQ = 4
PAST_LEN = 16384
PAGE_SIZE = 128

N_MIXERS = 3
N_A = (DEPTH + 2) // 3
N_B = (DEPTH + 1) // 3
N_C = DEPTH // 3
N_HEADS = 16
N_KV_HEADS = 4
HEAD_DIM = 128
GROUP = N_HEADS // N_KV_HEADS
Q_WIDTH = N_HEADS * HEAD_DIM
KV_WIDTH = N_KV_HEADS * HEAD_DIM
QKV_WIDTH = Q_WIDTH + 2 * KV_WIDTH
CONV_W = 31
FFN_CONV_W = 3
D_FF = 5632
WINDOW = 128
MOBA_BLOCK = 256
MOBA_TOPK = 3
MOBA_QCHUNK = 32
NUM_BUCKETS = 32
REL_MAX_DISTANCE = 1024
N_MOD = 6
RMS_EPS = 1e-6
LN_EPS = 1e-5
ATTN_SCALE = HEAD_DIM ** -0.5
N_PAGES = PAST_LEN // PAGE_SIZE
N_POOL_PAGES = (DEC_BATCH * N_PAGES * 5) // 4
PAGES_PER_BLOCK = MOBA_BLOCK // PAGE_SIZE

kernel_name = 'hybrid_conv_swa_moba_decoder_step'


def _rmsnorm(x, g):
    x32 = x.astype(jnp.float32)
    y = x32 * lax.rsqrt(jnp.mean(x32 * x32, -1, keepdims=True) + RMS_EPS)
    return (y * g.astype(jnp.float32)).astype(x.dtype)


def _layernorm(x, g, b):
    x32 = x.astype(jnp.float32)
    xc = x32 - jnp.mean(x32, -1, keepdims=True)
    y = xc * lax.rsqrt(jnp.mean(xc * xc, -1, keepdims=True) + LN_EPS)
    return (y * g.astype(jnp.float32) + b.astype(jnp.float32)).astype(x.dtype)


def _dwconv(x, w):
    return lax.conv_general_dilated(x, w.astype(x.dtype)[:, None, :], (1,), 'VALID',
                                    dimension_numbers=('NWC', 'WIO', 'NWC'),
                                    feature_group_count=x.shape[-1])


def _rel_bucket(dist):
    n = jnp.maximum(dist, 0)
    max_exact = NUM_BUCKETS // 2
    nf = jnp.maximum(n, 1).astype(jnp.float32)
    large = max_exact + (jnp.log(nf / max_exact) / math.log(REL_MAX_DISTANCE / max_exact)
                         * (NUM_BUCKETS - max_exact)).astype(jnp.int32)
    return jnp.where(n < max_exact, n, jnp.minimum(large, NUM_BUCKETS - 1))


def _rel_bias(table, dist):
    b = table.astype(jnp.float32)[_rel_bucket(dist)]
    return jnp.moveaxis(b, -1, 0).reshape(N_KV_HEADS, GROUP, *dist.shape)


def _qkv(h, w):
    bsz, t, _ = h.shape
    p = h @ w
    q = p[..., :Q_WIDTH].reshape(bsz, t, N_KV_HEADS, GROUP, HEAD_DIM)
    k = p[..., Q_WIDTH:Q_WIDTH + KV_WIDTH].reshape(bsz, t, N_KV_HEADS, HEAD_DIM)
    v = p[..., Q_WIDTH + KV_WIDTH:].reshape(bsz, t, N_KV_HEADS, HEAD_DIM)
    return q, k, v


def _sink_attention(q, k, v, bias, mask, sinks):
    s = jnp.einsum('...qhgd,...khd->...hgqk', q, k).astype(jnp.float32) * ATTN_SCALE + bias
    s = jnp.where(mask, s, -jnp.inf)
    sink = jnp.broadcast_to(sinks.astype(jnp.float32).reshape(N_KV_HEADS, GROUP, 1, 1), s.shape[:-1] + (1,))
    p = jax.nn.softmax(jnp.concatenate([s, sink], axis=-1), axis=-1)[..., :-1]
    return jnp.einsum('...hgqk,...khd->...qhgd', p.astype(v.dtype), v)


def _swa_prompt(h, wqkv, wo, sinks, table):
    bsz, t, _ = h.shape
    q, k, v = _qkv(h, wqkv)
    nb = t // WINDOW
    qb = q.reshape(bsz, nb, WINDOW, N_KV_HEADS, GROUP, HEAD_DIM)

    def band(z):
        zp = jnp.pad(z, ((0, 0), (WINDOW, 0), (0, 0), (0, 0)))
        prev = zp[:, :t].reshape(bsz, nb, WINDOW, N_KV_HEADS, HEAD_DIM)
        cur = zp[:, WINDOW:].reshape(bsz, nb, WINDOW, N_KV_HEADS, HEAD_DIM)
        return jnp.concatenate([prev, cur], axis=2)

    qi = jnp.arange(WINDOW)[:, None]
    kj = jnp.arange(2 * WINDOW)[None, :]
    dist = qi + WINDOW - kj
    inband = (dist >= 0) & (dist < WINDOW)
    exists = (jnp.arange(nb)[:, None, None] > 0) | (kj[None] >= WINDOW)
    mask = (inband[None] & exists)[:, None, None]
    o = _sink_attention(qb, band(k), band(v), _rel_bias(table, dist), mask, sinks)
    return o.reshape(bsz, t, Q_WIDTH) @ wo, k[:, -WINDOW:], v[:, -WINDOW:]


def _swa_sample(h, buf_k, buf_v, wqkv, wo, sinks, table):
    bsz, t, _ = h.shape
    q, k, v = _qkv(h, wqkv)
    kc = jnp.concatenate([buf_k.astype(k.dtype), k], axis=1)
    vc = jnp.concatenate([buf_v.astype(v.dtype), v], axis=1)
    dist = jnp.arange(t)[:, None] + WINDOW - jnp.arange(WINDOW + t)[None, :]
    mask = (dist >= 0) & (dist < WINDOW)
    o = _sink_attention(q, kc, vc, _rel_bias(table, dist), mask, sinks)
    return o.reshape(bsz, t, Q_WIDTH) @ wo, kc[:, -WINDOW:], vc[:, -WINDOW:]


def _moba_attend(q, q_pos, ks, vs, ks_pos, ks_valid, ko, vo, ko_pos, table):
    n_sel = ks.shape[-2]
    bias_hg = table.astype(jnp.float32).T.reshape(N_KV_HEADS, GROUP, NUM_BUCKETS)
    kv_i = jnp.arange(N_KV_HEADS)[:, None, None]
    g_i = jnp.arange(GROUP)[:, None]
    dist_s = q_pos[None, :, None, None, None] - ks_pos
    s_sel = (jnp.einsum('nqhgd,nqhgsd->nqhgs', q, ks).astype(jnp.float32) * ATTN_SCALE
             + bias_hg[kv_i, g_i, _rel_bucket(dist_s)])
    s_sel = jnp.where(ks_valid, s_sel, -jnp.inf)
    dist_o = q_pos[:, None] - ko_pos[None, :]
    bias_o = jnp.transpose(_rel_bias(table, dist_o), (2, 0, 1, 3))
    s_own = jnp.einsum('nqhgd,nkhd->nqhgk', q, ko).astype(jnp.float32) * ATTN_SCALE + bias_o
    s_own = jnp.where((dist_o >= 0)[:, None, None, :], s_own, -jnp.inf)
    p = jax.nn.softmax(jnp.concatenate([s_sel, s_own], axis=-1), axis=-1).astype(vs.dtype)
    return (jnp.einsum('nqhgs,nqhgsd->nqhgd', p[..., :n_sel], vs)
            + jnp.einsum('nqhgk,nkhd->nqhgd', p[..., n_sel:], vo))


def _moba_prompt(h, wqkv, wo, table):
    bsz, t, _ = h.shape
    q, k, v = _qkv(h, wqkv)
    nb = -(-t // MOBA_BLOCK)
    pad = nb * MOBA_BLOCK - t
    k_pad = jnp.pad(k, ((0, 0), (0, pad), (0, 0), (0, 0)))
    v_pad = jnp.pad(v, ((0, 0), (0, pad), (0, 0), (0, 0)))
    kb = k_pad.reshape(bsz, nb, MOBA_BLOCK, N_KV_HEADS, HEAD_DIM)
    vb = v_pad.reshape(bsz, nb, MOBA_BLOCK, N_KV_HEADS, HEAD_DIM)
    means = jnp.mean(kb.astype(jnp.float32), axis=2)
    n_sel = min(MOBA_TOPK, nb)
    b_i = jnp.arange(bsz)[:, None, None, None, None]
    kv_i = jnp.arange(N_KV_HEADS)[:, None, None]

    def chunk(ci):
        start = ci * MOBA_QCHUNK
        qc = lax.dynamic_slice_in_dim(q, start, MOBA_QCHUNK, axis=1)
        q_pos = start + jnp.arange(MOBA_QCHUNK)
        q_blk = start // MOBA_BLOCK
        gate = jnp.einsum('bqhgd,bnhd->bqhgn', qc.astype(jnp.float32), means)
        gate = jnp.where(jnp.arange(nb) < q_blk, gate, -jnp.inf)
        _, sel = lax.top_k(gate, n_sel)
        valid = sel < q_blk
        shp = (bsz, MOBA_QCHUNK, N_KV_HEADS, GROUP, n_sel * MOBA_BLOCK)
        ks = kb[b_i, sel, :, kv_i].reshape(shp + (HEAD_DIM,))
        vs = vb[b_i, sel, :, kv_i].reshape(shp + (HEAD_DIM,))
        ks_pos = (sel[..., None] * MOBA_BLOCK + jnp.arange(MOBA_BLOCK)).reshape(shp)
        ks_valid = jnp.repeat(valid, MOBA_BLOCK, axis=-1)
        own = q_blk * MOBA_BLOCK
        ko = lax.dynamic_slice_in_dim(k_pad, own, MOBA_BLOCK, axis=1)
        vo = lax.dynamic_slice_in_dim(v_pad, own, MOBA_BLOCK, axis=1)
        ko_pos = own + jnp.arange(MOBA_BLOCK)
        return _moba_attend(qc, q_pos, ks, vs, ks_pos, ks_valid, ko, vo, ko_pos, table)

    o = lax.map(chunk, jnp.arange(t // MOBA_QCHUNK))
    o = jnp.moveaxis(o, 0, 1).reshape(bsz, t, Q_WIDTH) @ wo
    new_k = k.reshape(bsz, t // PAGE_SIZE, PAGE_SIZE, N_KV_HEADS, HEAD_DIM)
    new_v = v.reshape(bsz, t // PAGE_SIZE, PAGE_SIZE, N_KV_HEADS, HEAD_DIM)
    return o, new_k, new_v


def _moba_sample(h, pool_k, pool_v, page_table, wqkv, wo, table):
    bsz, t, _ = h.shape
    q, k, v = _qkv(h, wqkv)
    n_full = PAST_LEN // MOBA_BLOCK
    nbc = max(n_full, 1)
    n_sel = min(MOBA_TOPK, nbc)
    if n_full > 0:
        rows = pool_k[page_table[:, :n_full * PAGES_PER_BLOCK]]
        means = jnp.mean(rows.astype(jnp.float32).reshape(bsz, n_full, MOBA_BLOCK, N_KV_HEADS, HEAD_DIM), axis=2)
    else:
        means = jnp.zeros((bsz, 1, N_KV_HEADS, HEAD_DIM), jnp.float32)
    own_pages = page_table[:, n_full * PAGES_PER_BLOCK:]
    ko_past = pool_k[own_pages].reshape(bsz, -1, N_KV_HEADS, HEAD_DIM)
    vo_past = pool_v[own_pages].reshape(bsz, -1, N_KV_HEADS, HEAD_DIM)
    ko = jnp.concatenate([ko_past.astype(k.dtype), k], axis=1)
    vo = jnp.concatenate([vo_past.astype(v.dtype), v], axis=1)
    ko_pos = n_full * MOBA_BLOCK + jnp.arange(ko.shape[1])
    b_i = jnp.arange(bsz)[:, None, None, None, None, None]
    kv_i = jnp.arange(N_KV_HEADS)[:, None, None, None]
    cand = jnp.arange(nbc) < n_full

    def token(j):
        qj = lax.dynamic_slice_in_dim(q, j, 1, axis=1)
        q_pos = jnp.reshape(PAST_LEN + j, (1,))
        gate = jnp.einsum('nqhgd,nbhd->nqhgb', qj.astype(jnp.float32), means)
        gate = jnp.where(cand, gate, -jnp.inf)
        _, sel = lax.top_k(gate, n_sel)
        valid = sel < n_full
        logical = jnp.minimum(sel[..., None] * PAGES_PER_BLOCK + jnp.arange(PAGES_PER_BLOCK), N_PAGES - 1)
        phys = page_table[b_i, logical]
        shp = (bsz, 1, N_KV_HEADS, GROUP, n_sel * MOBA_BLOCK)
        ks = pool_k[phys, :, kv_i].reshape(shp + (HEAD_DIM,)).astype(q.dtype)
        vs = pool_v[phys, :, kv_i].reshape(shp + (HEAD_DIM,)).astype(q.dtype)
        ks_pos = (sel[..., None] * MOBA_BLOCK + jnp.arange(MOBA_BLOCK)).reshape(shp)
        ks_valid = jnp.repeat(valid, MOBA_BLOCK, axis=-1)
        return _moba_attend(qj, q_pos, ks, vs, ks_pos, ks_valid, ko, vo, ko_pos, table)[:, 0]

    o = lax.map(token, jnp.arange(t))
    o = jnp.moveaxis(o, 0, 1).reshape(bsz, t, Q_WIDTH) @ wo
    return o, k, v


def _conv_module(h, prev, w1, b1, dw, dw_b, ln_g, ln_b, w2, b2):
    a, g = jnp.split(h @ w1 + b1, 2, axis=-1)
    u = a * jax.nn.sigmoid(g)
    full = jnp.concatenate([prev.astype(u.dtype), u], axis=1)
    y = _dwconv(full, dw) + dw_b
    y = jax.nn.silu(_layernorm(y, ln_g, ln_b))
    return y @ w2 + b2, full[:, -(CONV_W - 1):]


def _conv_ffn(h, prev, w_up, dw, dw_b, w_down):
    g, v = jnp.split(h @ w_up, 2, axis=-1)
    full = jnp.concatenate([prev.astype(g.dtype), g], axis=1)
    gc = _dwconv(full, dw) + dw_b
    return (jax.nn.silu(gc) * v) @ w_down, full[:, -(FFN_CONV_W - 1):]


def _trunk(x, c, is_prompt, state_conv, cache_swa_k, cache_swa_v, cache_moba_k, cache_moba_v,
           page_table, state_ffn, p):
    bsz = x.shape[0]
    conv_st, swa_k_st, swa_v_st, moba_k_st, moba_v_st, ffn_st = [], [], [], [], [], []
    c_act = jax.nn.silu(c)
    for i in range(DEPTH):
        kind, j = i % N_MIXERS, i // N_MIXERS
        mod = c_act @ p['ada_w'][i] + p['ada_b'][i]
        sh1, sc1, g1, sh2, sc2, g2 = jnp.split(mod[:, None, :], N_MOD, axis=-1)
        h = _rmsnorm(x, p['norm_mix'][i]) * (1 + sc1) + sh1
        if kind == 0:
            prev = jnp.zeros((bsz, CONV_W - 1, D_MODEL), x.dtype) if is_prompt else state_conv[j]
            o, st = _conv_module(h, prev, p['conv_w1'][j], p['conv_b1'][j], p['conv_dw'][j], p['conv_dw_b'][j],
                                 p['conv_ln_g'][j], p['conv_ln_b'][j], p['conv_w2'][j], p['conv_b2'][j])
            conv_st.append(st)
        elif kind == 1:
            if is_prompt:
                o, sk, sv = _swa_prompt(h, p['swa_wqkv'][j], p['swa_wo'][j], p['swa_sinks'][j], p['rel_bias'])
            else:
                o, sk, sv = _swa_sample(h, cache_swa_k[j], cache_swa_v[j], p['swa_wqkv'][j], p['swa_wo'][j],
                                        p['swa_sinks'][j], p['rel_bias'])
            swa_k_st.append(sk)
            swa_v_st.append(sv)
        else:
            if is_prompt:
                o, mk, mv = _moba_prompt(h, p['moba_wqkv'][j], p['moba_wo'][j], p['rel_bias'])
            else:
                o, mk, mv = _moba_sample(h, cache_moba_k[j], cache_moba_v[j], page_table, p['moba_wqkv'][j],
                                         p['moba_wo'][j], p['rel_bias'])
            moba_k_st.append(mk)
            moba_v_st.append(mv)
        x = x + g1 * o
        h = _rmsnorm(x, p['norm_ffn'][i]) * (1 + sc2) + sh2
        prev = jnp.zeros((bsz, FFN_CONV_W - 1, D_FF), x.dtype) if is_prompt else state_ffn[i]
        o, st = _conv_ffn(h, prev, p['ffn_w_up'][i], p['ffn_dw'][i], p['ffn_dw_b'][i], p['ffn_w_down'][i])
        ffn_st.append(st)
        x = x + g2 * o
    y = _rmsnorm(x, p['norm_final'])
    return (y, jnp.stack(conv_st), jnp.stack(swa_k_st), jnp.stack(swa_v_st),
            jnp.stack(moba_k_st), jnp.stack(moba_v_st), jnp.stack(ffn_st))


def setup_inputs(seed: int = 0) -> dict:
    key = jax.random.key(seed)
    keys = iter(jax.random.split(key, 48))

    def nrm(shape, scale):
        return jax.random.normal(next(keys), shape, jnp.float32) * scale

    page_table = jax.random.permutation(next(keys), N_POOL_PAGES)[:DEC_BATCH * N_PAGES]
    page_table = page_table.reshape(DEC_BATCH, N_PAGES).astype(jnp.int32)
    d = D_MODEL
    return {
        'x_prompt': nrm((BATCH, SEQ, d), 1.0),
        'x_sample': nrm((DEC_BATCH, DEC_SEQ, d), 1.0),
        'c_prompt': nrm((BATCH, d), 1.0),
        'c_sample': nrm((DEC_BATCH, d), 1.0),
        'state_conv': nrm((N_A, DEC_BATCH, CONV_W - 1, d), 0.5),
        'cache_swa_k': nrm((N_B, DEC_BATCH, WINDOW, N_KV_HEADS, HEAD_DIM), 1.0),
        'cache_swa_v': nrm((N_B, DEC_BATCH, WINDOW, N_KV_HEADS, HEAD_DIM), 1.0),
        'cache_moba_k': nrm((N_C, N_POOL_PAGES, PAGE_SIZE, N_KV_HEADS, HEAD_DIM), 1.0),
        'cache_moba_v': nrm((N_C, N_POOL_PAGES, PAGE_SIZE, N_KV_HEADS, HEAD_DIM), 1.0),
        'page_table': page_table,
        'state_ffn': nrm((DEPTH, DEC_BATCH, FFN_CONV_W - 1, D_FF), 1.0),
        'ada_w': nrm((DEPTH, d, N_MOD * d), 0.5 * d ** -0.5),
        'ada_b': nrm((DEPTH, N_MOD * d), 0.02),
        'norm_mix': 1.0 + nrm((DEPTH, d), 0.05),
        'norm_ffn': 1.0 + nrm((DEPTH, d), 0.05),
        'norm_final': 1.0 + nrm((d,), 0.05),
        'rel_bias': nrm((NUM_BUCKETS, N_HEADS), 0.5),
        'conv_w1': nrm((N_A, d, 2 * d), d ** -0.5),
        'conv_b1': nrm((N_A, 2 * d), 0.02),
        'conv_dw': nrm((N_A, CONV_W, d), CONV_W ** -0.5),
        'conv_dw_b': nrm((N_A, d), 0.02),
        'conv_ln_g': 1.0 + nrm((N_A, d), 0.05),
        'conv_ln_b': nrm((N_A, d), 0.02),
        'conv_w2': nrm((N_A, d, d), d ** -0.5),
        'conv_b2': nrm((N_A, d), 0.02),
        'swa_wqkv': nrm((N_B, d, QKV_WIDTH), d ** -0.5),
        'swa_wo': nrm((N_B, Q_WIDTH, d), Q_WIDTH ** -0.5),
        'swa_sinks': nrm((N_B, N_HEADS), 1.0),
        'moba_wqkv': nrm((N_C, d, QKV_WIDTH), d ** -0.5),
        'moba_wo': nrm((N_C, Q_WIDTH, d), Q_WIDTH ** -0.5),
        'ffn_w_up': nrm((DEPTH, d, 2 * D_FF), d ** -0.5),
        'ffn_dw': nrm((DEPTH, FFN_CONV_W, D_FF), FFN_CONV_W ** -0.5),
        'ffn_dw_b': nrm((DEPTH, D_FF), 0.02),
        'ffn_w_down': nrm((DEPTH, D_FF, d), D_FF ** -0.5),
    }


def reference(x_prompt, x_sample, c_prompt, c_sample, state_conv, cache_swa_k, cache_swa_v,
              cache_moba_k, cache_moba_v, page_table, state_ffn, ada_w, ada_b, norm_mix, norm_ffn,
              norm_final, rel_bias, conv_w1, conv_b1, conv_dw, conv_dw_b, conv_ln_g, conv_ln_b,
              conv_w2, conv_b2, swa_wqkv, swa_wo, swa_sinks, moba_wqkv, moba_wo, ffn_w_up, ffn_dw,
              ffn_dw_b, ffn_w_down):
    p = dict(ada_w=ada_w, ada_b=ada_b, norm_mix=norm_mix, norm_ffn=norm_ffn, norm_final=norm_final,
             rel_bias=rel_bias, conv_w1=conv_w1, conv_b1=conv_b1, conv_dw=conv_dw, conv_dw_b=conv_dw_b,
             conv_ln_g=conv_ln_g, conv_ln_b=conv_ln_b, conv_w2=conv_w2, conv_b2=conv_b2,
             swa_wqkv=swa_wqkv, swa_wo=swa_wo, swa_sinks=swa_sinks, moba_wqkv=moba_wqkv, moba_wo=moba_wo,
             ffn_w_up=ffn_w_up, ffn_dw=ffn_dw, ffn_dw_b=ffn_dw_b, ffn_w_down=ffn_w_down)
    (y_prompt, conv_p, swa_k_p, swa_v_p, moba_k_p, moba_v_p, ffn_p) = _trunk(
        x_prompt, c_prompt, True, None, None, None, None, None, None, None, p)
    (y_sample, conv_s, swa_k_s, swa_v_s, moba_k_s, moba_v_s, ffn_s) = _trunk(
        x_sample, c_sample, False, state_conv, cache_swa_k, cache_swa_v, cache_moba_k, cache_moba_v,
        page_table, state_ffn, p)
    return (y_prompt, y_sample, conv_p, conv_s, swa_k_p, swa_v_p, swa_k_s, swa_v_s,
            moba_k_p, moba_v_p, moba_k_s, moba_v_s, ffn_p, ffn_s)
```

```python
import functools
import math

import numpy as np
import jax
import jax.numpy as jnp
from jax import lax
from jax.experimental import pallas as pl
from jax.experimental.pallas import tpu as pltpu

MOBA_BLOCK = 256
MOBA_TOPK = 3
REL_MAX_DISTANCE = 1024
RMS_EPS = 1e-6
LN_EPS = 1e-5
N_MOD = 6

V7X_LANES = 128
V7X_SUBLANES = 8
V7X_VMEM_LIMIT_BYTES = 56 * 1024 * 1024

NEG = -1e30
BF16 = jnp.bfloat16
F32 = jnp.float32


def _cparams(*sem):
    return pltpu.CompilerParams(dimension_semantics=sem, vmem_limit_bytes=V7X_VMEM_LIMIT_BYTES)


def _dot(a, b):
    return jnp.dot(a, b, preferred_element_type=F32)


def _dot_nt(a, b, precision=None):
    return lax.dot_general(a, b, (((1,), (1,)), ((), ())), precision=precision,
                           preferred_element_type=F32)


def _silu(x):
    return x * jax.nn.sigmoid(x)


def _bucket_np(dist, num_buckets):
    n = np.maximum(dist, 0)
    max_exact = num_buckets // 2
    nf = np.maximum(n, 1).astype(np.float32)
    ratio = np.log(nf / np.float32(max_exact)) / np.float32(math.log(REL_MAX_DISTANCE / max_exact))
    large = max_exact + (ratio * np.float32(num_buckets - max_exact)).astype(np.int32)
    return np.where(n < max_exact, n, np.minimum(large, num_buckets - 1)).astype(np.int32)


def _saturation_distance(num_buckets):
    d = np.arange(0, 4 * REL_MAX_DISTANCE)
    b = _bucket_np(d, num_buckets)
    below = np.nonzero(b < num_buckets - 1)[0]
    return int(below.max()) + 1


def _bias_expand_kernel(tab_ref, idx_ref, o_ref, *, tiles):
    h = pl.program_id(0)
    for r0, nr, buckets in tiles:
        idx = idx_ref[r0:r0 + nr, :]
        acc = jnp.full(idx.shape, NEG, F32)
        for b in buckets:
            acc = jnp.where(idx == b, tab_ref[b, h], acc)
        o_ref[r0:r0 + nr, :] = acc


def _bias_expand(table, idx_np, tile_rows, out_shape, out_block, out_index):
    n_heads = table.shape[1]
    rows, cols = idx_np.shape
    tiles = []
    for r0 in range(0, rows, tile_rows):
        present = np.unique(idx_np[r0:r0 + tile_rows])
        tiles.append((r0, min(tile_rows, rows - r0), tuple(int(b) for b in present if b >= 0)))
    return pl.pallas_call(
        functools.partial(_bias_expand_kernel, tiles=tuple(tiles)),
        grid=(n_heads,),
        in_specs=[pl.BlockSpec(memory_space=pltpu.SMEM),
                  pl.BlockSpec((rows, cols), lambda h: (0, 0))],
        out_specs=pl.BlockSpec(out_block, out_index),
        out_shape=jax.ShapeDtypeStruct(out_shape, F32),
        compiler_params=_cparams("arbitrary"),
    )(table, jnp.asarray(idx_np))


def _ada_kernel(c_ref, w_ref, b_ref, o_ref):
    a = _silu(c_ref[...]).astype(BF16)
    o_ref[...] = _dot(a, w_ref[...].astype(BF16)) + b_ref[...]


def _ada(c_all, ada_w, ada_b):
    depth, d, n = ada_w.shape
    rows = c_all.shape[0]
    tn = min(n, 2048)
    return pl.pallas_call(
        _ada_kernel,
        grid=(depth, n // tn),
        in_specs=[pl.BlockSpec((rows, d), lambda l, j: (0, 0)),
                  pl.BlockSpec((None, d, tn), lambda l, j: (l, 0, j)),
                  pl.BlockSpec((None, 1, tn), lambda l, j: (l, 0, j))],
        out_specs=pl.BlockSpec((None, rows, tn), lambda l, j: (l, 0, j)),
        out_shape=jax.ShapeDtypeStruct((depth, rows, n), F32),
        compiler_params=_cparams("arbitrary", "arbitrary"),
    )(c_all, ada_w, ada_b.reshape(depth, 1, n))


def _norm_mod_kernel(x_ref, g_ref, sc_ref, sh_ref, o_ref):
    x = x_ref[...]
    ms = jnp.mean(x * x, axis=-1, keepdims=True)
    y = x * lax.rsqrt(ms + RMS_EPS) * g_ref[...]
    o_ref[...] = (y * (1.0 + sc_ref[...]) + sh_ref[...]).astype(o_ref.dtype)


def _rms_kernel(x_ref, g_ref, o_ref):
    x = x_ref[...]
    ms = jnp.mean(x * x, axis=-1, keepdims=True)
    o_ref[...] = (x * lax.rsqrt(ms + RMS_EPS) * g_ref[...]).astype(o_ref.dtype)


def _norm_mod(x3, g, sc3, sh3, rows_per_step, out_dtype=BF16):
    a, r, d = x3.shape
    tr = min(rows_per_step, r)
    g3 = g.reshape(1, 1, d)
    if sc3 is None:
        return pl.pallas_call(
            _rms_kernel,
            grid=(a, r // tr),
            in_specs=[pl.BlockSpec((1, tr, d), lambda i, j: (i, j, 0)),
                      pl.BlockSpec((1, 1, d), lambda i, j: (0, 0, 0))],
            out_specs=pl.BlockSpec((1, tr, d), lambda i, j: (i, j, 0)),
            out_shape=jax.ShapeDtypeStruct(x3.shape, out_dtype),
            compiler_params=_cparams("arbitrary", "arbitrary"),
        )(x3, g3)
    per_seq = sc3.shape[0] == a and sc3.shape[1] == 1
    if per_seq:
        mspec = pl.BlockSpec((1, 1, d), lambda i, j: (i, 0, 0))
    else:
        assert sc3.shape[0] == 1 and sc3.shape[1] == r and tr == r
        mspec = pl.BlockSpec((1, r, d), lambda i, j: (0, 0, 0))
    return pl.pallas_call(
        _norm_mod_kernel,
        grid=(a, r // tr),
        in_specs=[pl.BlockSpec((1, tr, d), lambda i, j: (i, j, 0)),
                  pl.BlockSpec((1, 1, d), lambda i, j: (0, 0, 0)),
                  mspec, mspec],
        out_specs=pl.BlockSpec((1, tr, d), lambda i, j: (i, j, 0)),
        out_shape=jax.ShapeDtypeStruct(x3.shape, out_dtype),
        compiler_params=_cparams("arbitrary", "arbitrary"),
    )(x3, g3, sc3, sh3)


def _mm_kernel(*refs, has_bias, has_res):
    x_ref, w_ref = refs[0], refs[1]
    pos = 2
    acc = _dot(x_ref[...], w_ref[...].astype(BF16))
    if has_bias:
        acc = acc + refs[pos][...]
        pos += 1
    if has_res:
        acc = refs[pos][...] + refs[pos + 1][...] * acc
        pos += 2
    o_ref = refs[pos]
    o_ref[...] = acc.astype(o_ref.dtype)


def _mm(x, w, layer, *, bias=None, res=None, gate=None, out_dtype=F32, tm=1024, tn=512):
    m, k = x.shape
    n = w.shape[2]
    tm, tn = min(tm, m), min(tn, n)
    assert m % tm == 0 and n % tn == 0
    in_specs = [pl.BlockSpec((tm, k), lambda i, j: (i, 0)),
                pl.BlockSpec((None, k, tn), lambda i, j: (layer, 0, j))]
    args = [x, w]
    if bias is not None:
        in_specs.append(pl.BlockSpec((None, 1, tn), lambda i, j: (layer, 0, j)))
        args.append(bias.reshape(bias.shape[0], 1, n))
    if res is not None:
        gs, gr, _ = gate.shape
        assert gr in (1, tm) and (m // tm) % gs == 0
        tiles_per_gate = (m // tm) // gs
        in_specs.append(pl.BlockSpec((tm, tn), lambda i, j: (i, j)))
        in_specs.append(pl.BlockSpec((None, gr, tn), lambda i, j: (i // tiles_per_gate, 0, j)))
        args += [res, gate]
    return pl.pallas_call(
        functools.partial(_mm_kernel, has_bias=bias is not None, has_res=res is not None),
        grid=(m // tm, n // tn),
        in_specs=in_specs,
        out_specs=pl.BlockSpec((tm, tn), lambda i, j: (i, j)),
        out_shape=jax.ShapeDtypeStruct((m, n), out_dtype),
        compiler_params=_cparams("arbitrary", "arbitrary"),
    )(*args)


def _glu_kernel(x_ref, wa_ref, wg_ref, ba_ref, bg_ref, o_ref, wa_s, wg_s):
    @pl.when(pl.program_id(1) == 0)
    def _():
        wa_s[...] = wa_ref[...].astype(BF16)
        wg_s[...] = wg_ref[...].astype(BF16)

    x = x_ref[...]
    a = _dot(x, wa_s[...]) + ba_ref[...]
    g = _dot(x, wg_s[...]) + bg_ref[...]
    o_ref[...] = a * jax.nn.sigmoid(g)


def _glu(x, w1, b1, layer, *, tm=1024, tn=512):
    m, k = x.shape
    n = w1.shape[2] // 2
    tm, tn = min(tm, m), min(tn, n)
    nj = n // tn
    b3 = b1.reshape(b1.shape[0], 1, 2 * n)
    return pl.pallas_call(
        _glu_kernel,
        grid=(nj, m // tm),
        in_specs=[pl.BlockSpec((tm, k), lambda j, i: (i, 0)),
                  pl.BlockSpec((None, k, tn), lambda j, i: (layer, 0, j)),
                  pl.BlockSpec((None, k, tn), lambda j, i: (layer, 0, nj + j)),
                  pl.BlockSpec((None, 1, tn), lambda j, i: (layer, 0, j)),
                  pl.BlockSpec((None, 1, tn), lambda j, i: (layer, 0, nj + j))],
        out_specs=pl.BlockSpec((tm, tn), lambda j, i: (i, j)),
        out_shape=jax.ShapeDtypeStruct((m, n), F32),
        scratch_shapes=[pltpu.VMEM((k, tn), BF16), pltpu.VMEM((k, tn), BF16)],
        compiler_params=_cparams("arbitrary", "arbitrary"),
    )(x, w1, w1, b3, b3)


def _layernorm_silu(y, lg, lb):
    mu = jnp.mean(y, axis=-1, keepdims=True)
    yc = y - mu
    var = jnp.mean(yc * yc, axis=-1, keepdims=True)
    return _silu(yc * lax.rsqrt(var + LN_EPS) * lg + lb)


def _conv_ln_prompt_kernel(u_ref, halo_ref, dw_ref, dwb_ref, lg_ref, lb_ref, o_ref, full_s, y_s,
                           *, tm, halo, cw, rc, lw):
    i = pl.program_id(1)
    d = u_ref.shape[-1]
    full_s[halo:halo + tm, :] = u_ref[...]

    @pl.when(i == 0)
    def _():
        full_s[0:halo, :] = jnp.zeros((halo, d), F32)

    @pl.when(i > 0)
    def _():
        full_s[0:halo, :] = halo_ref[...]

    off = halo - (cw - 1)
    win = rc + halo
    sub = V7X_SUBLANES

    def body(r, carry):
        r0 = pl.multiple_of(r * rc, rc)
        for c0 in range(0, d, lw):
            window = full_s[pl.ds(r0, win), c0:c0 + lw]
            acc = jnp.zeros((rc, lw), F32) + dwb_ref[:, c0:c0 + lw]
            for s in range(sub):
                taps = [k for k in range(cw) if (off + k) % sub == s]
                if not taps:
                    continue
                shifted = window if s == 0 else pltpu.roll(window, win - s, 0)
                for k in taps:
                    a = off + k - s
                    acc = acc + dw_ref[k:k + 1, c0:c0 + lw] * shifted[a:a + rc]
            y_s[pl.ds(r0, rc), c0:c0 + lw] = acc
        return carry
    lax.fori_loop(0, tm // rc, body, 0)

    o_ref[...] = _layernorm_silu(y_s[...], lg_ref[...], lb_ref[...]).astype(o_ref.dtype)


def _conv_ln_prompt(u3, dw, dwb, lg, lb, *, tm=256):
    bsz, t, d = u3.shape
    cw = dw.shape[0]
    halo = 32
    assert cw - 1 <= halo and t % tm == 0 and tm % halo == 0
    hb = tm // halo
    lw = min(d, 256)
    return pl.pallas_call(
        functools.partial(_conv_ln_prompt_kernel, tm=tm, halo=halo, cw=cw, rc=halo, lw=lw),
        grid=(bsz, t // tm),
        in_specs=[pl.BlockSpec((None, tm, d), lambda b, i: (b, i, 0)),
                  pl.BlockSpec((None, halo, d), lambda b, i: (b, jnp.maximum(i * hb - 1, 0), 0)),
                  pl.BlockSpec((cw, d), lambda b, i: (0, 0)),
                  pl.BlockSpec((1, d), lambda b, i: (0, 0)),
                  pl.BlockSpec((1, d), lambda b, i: (0, 0)),
                  pl.BlockSpec((1, d), lambda b, i: (0, 0))],
        out_specs=pl.BlockSpec((None, tm, d), lambda b, i: (b, i, 0)),
        out_shape=jax.ShapeDtypeStruct((bsz, t, d), BF16),
        scratch_shapes=[pltpu.VMEM((tm + halo, d), F32), pltpu.VMEM((tm, d), F32)],
        compiler_params=_cparams("arbitrary", "arbitrary"),
    )(u3, u3, dw, dwb.reshape(1, d), lg.reshape(1, d), lb.reshape(1, d))


def _conv_ln_sample_kernel(st_ref, u_ref, dw_ref, dwb_ref, lg_ref, lb_ref, o_ref, nst_ref, *, cw, lw):
    t_new, _, d = u_ref.shape
    n_st = cw - 1

    def row(idx, c0):
        if idx < n_st:
            return st_ref[idx, :, c0:c0 + lw]
        return u_ref[idx - n_st, :, c0:c0 + lw]

    for t in range(t_new):
        parts = []
        for c0 in range(0, d, lw):
            acc = dw_ref[0:1, c0:c0 + lw] * row(t, c0)
            for k in range(1, cw):
                acc = acc + dw_ref[k:k + 1, c0:c0 + lw] * row(t + k, c0)
            parts.append(acc + dwb_ref[:, c0:c0 + lw])
        y = jnp.concatenate(parts, axis=-1) if len(parts) > 1 else parts[0]
        o_ref[t] = _layernorm_silu(y, lg_ref[...], lb_ref[...]).astype(o_ref.dtype)
    for r in range(n_st):
        idx = t_new + r
        nst_ref[r] = st_ref[idx] if idx < n_st else u_ref[idx - n_st]


def _conv_ln_sample(st_tm, u_tm, dw, dwb, lg, lb):
    n_st, bsz, d = st_tm.shape
    t_new = u_tm.shape[0]
    cw = dw.shape[0]
    lw = min(d, 512)
    full = lambda shape: pl.BlockSpec(shape, lambda i: (0,) * len(shape))
    return pl.pallas_call(
        functools.partial(_conv_ln_sample_kernel, cw=cw, lw=lw),
        grid=(1,),
        in_specs=[full((n_st, bsz, d)), full((t_new, bsz, d)), full((cw, d)), full((1, d)), full((1, d)),
                  full((1, d))],
        out_specs=[full((t_new, bsz, d)), full((n_st, bsz, d))],
        out_shape=[jax.ShapeDtypeStruct((t_new, bsz, d), BF16),
                   jax.ShapeDtypeStruct((n_st, bsz, d), F32)],
        compiler_params=_cparams("arbitrary"),
    )(st_tm, u_tm, dw, dwb.reshape(1, d), lg.reshape(1, d), lb.reshape(1, d))


def _ffn_up_prompt_kernel(x_ref, wg_ref, wv_ref, dw_ref, dwb_ref, a_ref, st_ref, wg_s, wv_s, carry_s,
                          *, tiles_per_seq):
    i = pl.program_id(1)

    @pl.when(i == 0)
    def _():
        wg_s[...] = wg_ref[...].astype(BF16)
        wv_s[...] = wv_ref[...].astype(BF16)

    @pl.when(i % tiles_per_seq == 0)
    def _():
        carry_s[...] = jnp.zeros(carry_s.shape, F32)

    x = x_ref[...]
    g = _dot(x, wg_s[...])
    v = _dot(x, wv_s[...])
    tm = g.shape[0]
    row = lax.broadcasted_iota(jnp.int32, g.shape, 0)
    prev1 = carry_s[V7X_SUBLANES - 1:V7X_SUBLANES, :]
    prev2 = carry_s[V7X_SUBLANES - 2:V7X_SUBLANES - 1, :]
    g1 = jnp.where(row == 0, prev1, pltpu.roll(g, 1, 0))
    g2 = jnp.where(row == 0, prev2, jnp.where(row == 1, prev1, pltpu.roll(g, 2, 0)))
    gc = dw_ref[0:1, :] * g2 + dw_ref[1:2, :] * g1 + dw_ref[2:3, :] * g + dwb_ref[...]
    a_ref[...] = (_silu(gc) * v).astype(a_ref.dtype)
    carry_s[...] = g[tm - V7X_SUBLANES:tm, :]
    st_ref[...] = g[tm - 2:tm, :]


def _ffn_up_prompt(x, w_up, dw, dwb, layer, bsz, *, tm=512, tn=512):
    m, k = x.shape
    n = w_up.shape[2] // 2
    t = m // bsz
    tm, tn = min(tm, t), min(tn, n)
    assert t % tm == 0 and n % tn == 0 and dw.shape[1] == 3
    nj = n // tn
    tiles_per_seq = t // tm
    return pl.pallas_call(
        functools.partial(_ffn_up_prompt_kernel, tiles_per_seq=tiles_per_seq),
        grid=(nj, m // tm),
        in_specs=[pl.BlockSpec((tm, k), lambda j, i: (i, 0)),
                  pl.BlockSpec((None, k, tn), lambda j, i: (layer, 0, j)),
                  pl.BlockSpec((None, k, tn), lambda j, i: (layer, 0, nj + j)),
                  pl.BlockSpec((None, 3, tn), lambda j, i: (layer, 0, j)),
                  pl.BlockSpec((None, 1, tn), lambda j, i: (layer, 0, j))],
        out_specs=[pl.BlockSpec((tm, tn), lambda j, i: (i, j)),
                   pl.BlockSpec((None, 2, tn), lambda j, i: (i // tiles_per_seq, 0, j))],
        out_shape=[jax.ShapeDtypeStruct((m, n), BF16),
                   jax.ShapeDtypeStruct((bsz, 2, n), F32)],
        scratch_shapes=[pltpu.VMEM((k, tn), BF16), pltpu.VMEM((k, tn), BF16),
                        pltpu.VMEM((V7X_SUBLANES, tn), F32)],
        compiler_params=_cparams("arbitrary", "arbitrary"),
    )(x, w_up, w_up, dw, dwb.reshape(dwb.shape[0], 1, n))


def _ffn_up_sample_kernel(x_ref, wg_ref, wv_ref, dw_ref, dwb_ref, prev_ref, a_ref, st_ref, *, bsz):
    x = x_ref[...]
    g = _dot(x, wg_ref[...].astype(BF16))
    v = _dot(x, wv_ref[...].astype(BF16))
    rows = g.shape[0]
    full = jnp.concatenate([prev_ref[...], g], axis=0)
    gc = (dw_ref[0:1, :] * full[0:rows] + dw_ref[1:2, :] * full[bsz:bsz + rows]
          + dw_ref[2:3, :] * full[2 * bsz:2 * bsz + rows] + dwb_ref[...])
    a_ref[...] = (_silu(gc) * v).astype(a_ref.dtype)
    st_ref[...] = full[rows:rows + 2 * bsz]


def _ffn_up_sample(x, w_up, dw, dwb, prev_tm, layer, bsz, *, tn=512):
    m, k = x.shape
    n = w_up.shape[2] // 2
    tn = min(tn, n)
    assert n % tn == 0 and m >= 2 * bsz and dw.shape[1] == 3
    nj = n // tn
    return pl.pallas_call(
        functools.partial(_ffn_up_sample_kernel, bsz=bsz),
        grid=(nj,),
        in_specs=[pl.BlockSpec((m, k), lambda j: (0, 0)),
                  pl.BlockSpec((None, k, tn), lambda j: (layer, 0, j)),
                  pl.BlockSpec((None, k, tn), lambda j: (layer, 0, nj + j)),
                  pl.BlockSpec((None, 3, tn), lambda j: (layer, 0, j)),
                  pl.BlockSpec((None, 1, tn), lambda j: (layer, 0, j)),
                  pl.BlockSpec((2 * bsz, tn), lambda j: (0, j))],
        out_specs=[pl.BlockSpec((m, tn), lambda j: (0, j)),
                   pl.BlockSpec((2 * bsz, tn), lambda j: (0, j))],
        out_shape=[jax.ShapeDtypeStruct((m, n), BF16),
                   jax.ShapeDtypeStruct((2 * bsz, n), F32)],
        compiler_params=_cparams("arbitrary"),
    )(x, w_up, w_up, dw, dwb.reshape(dwb.shape[0], 1, n), prev_tm)


def _swa_prompt_kernel(sink_ref, q_ref, kc_ref, kp_ref, vc_ref, vp_ref, bias_ref, o_ref, *, n_heads, group, hd,
                       scale):
    n = pl.program_id(1)
    w = q_ref.shape[0]
    kcat = jnp.concatenate([kp_ref[...], kc_ref[...]], axis=0).astype(BF16)
    vcat = jnp.concatenate([vp_ref[...], vc_ref[...]], axis=0).astype(BF16)
    col = lax.broadcasted_iota(jnp.int32, (w, 2 * w), 1)
    keep = (col >= w) | (n > 0)
    for h in range(n_heads):
        kv = h // group
        q = q_ref[:, h * hd:(h + 1) * hd].astype(BF16)
        s = _dot_nt(q, kcat[:, kv * hd:(kv + 1) * hd]) * scale + bias_ref[h]
        s = jnp.where(keep, s, NEG)
        sink = sink_ref[h]
        m = jnp.maximum(jnp.max(s, axis=-1, keepdims=True), sink)
        p = jnp.exp(s - m)
        l = jnp.sum(p, axis=-1, keepdims=True) + jnp.exp(sink - m)
        o = _dot(p.astype(BF16), vcat[:, kv * hd:(kv + 1) * hd]) / l
        o_ref[:, h * hd:(h + 1) * hd] = o.astype(o_ref.dtype)


def _swa_prompt(qkv, sinks, bias, bsz, n_heads, n_kv, hd, window):
    m = qkv.shape[0]
    t = m // bsz
    nb = t // window
    qw, kw = n_heads * hd, n_kv * hd
    assert qw % kw == 0 and t % window == 0
    kcol, vcol = qw // kw, qw // kw + 1
    return pl.pallas_call(
        functools.partial(_swa_prompt_kernel, n_heads=n_heads, group=n_heads // n_kv, hd=hd, scale=hd ** -0.5),
        grid=(bsz, nb),
        in_specs=[pl.BlockSpec(memory_space=pltpu.SMEM),
                  pl.BlockSpec((window, qw), lambda b, n: (b * nb + n, 0)),
                  pl.BlockSpec((window, kw), lambda b, n: (b * nb + n, kcol)),
                  pl.BlockSpec((window, kw), lambda b, n: (b * nb + jnp.maximum(n - 1, 0), kcol)),
                  pl.BlockSpec((window, kw), lambda b, n: (b * nb + n, vcol)),
                  pl.BlockSpec((window, kw), lambda b, n: (b * nb + jnp.maximum(n - 1, 0), vcol)),
                  pl.BlockSpec((n_heads, window, 2 * window), lambda b, n: (0, 0, 0))],
        out_specs=pl.BlockSpec((window, qw), lambda b, n: (b * nb + n, 0)),
        out_shape=jax.ShapeDtypeStruct((m, qw), BF16),
        compiler_params=_cparams("arbitrary", "arbitrary"),
    )(sinks, qkv, qkv, qkv, qkv, qkv, bias)


def _swa_sample_kernel(sink_ref, qkv_ref, bk_ref, bv_ref, bias_ref, o_ref, *, n_heads, group, hd, scale, pad):
    w = bk_ref.shape[0]
    tp = qkv_ref.shape[0]
    qw = n_heads * hd
    kw = (n_heads // group) * hd
    zeros = jnp.zeros((pad - tp, kw), F32)
    kcat = jnp.concatenate([bk_ref[...], qkv_ref[:, qw:qw + kw], zeros], axis=0).astype(BF16)
    vcat = jnp.concatenate([bv_ref[...], qkv_ref[:, qw + kw:qw + 2 * kw], zeros], axis=0).astype(BF16)
    del w
    for h in range(n_heads):
        kv = h // group
        q = qkv_ref[:, h * hd:(h + 1) * hd].astype(BF16)
        s = _dot_nt(q, kcat[:, kv * hd:(kv + 1) * hd]) * scale + bias_ref[h]
        sink = sink_ref[h]
        m = jnp.maximum(jnp.max(s, axis=-1, keepdims=True), sink)
        p = jnp.exp(s - m)
        l = jnp.sum(p, axis=-1, keepdims=True) + jnp.exp(sink - m)
        o = _dot(p.astype(BF16), vcat[:, kv * hd:(kv + 1) * hd]) / l
        o_ref[:, h * hd:(h + 1) * hd] = o.astype(o_ref.dtype)


def _swa_sample(qkv_bm, buf_k, buf_v, sinks, bias, n_heads, n_kv, hd):
    dbs, tp, width = qkv_bm.shape
    window = buf_k.shape[1]
    qw, kw = n_heads * hd, n_kv * hd
    pad = bias.shape[2] - window
    return pl.pallas_call(
        functools.partial(_swa_sample_kernel, n_heads=n_heads, group=n_heads // n_kv, hd=hd, scale=hd ** -0.5,
                          pad=pad),
        grid=(dbs,),
        in_specs=[pl.BlockSpec(memory_space=pltpu.SMEM),
                  pl.BlockSpec((None, tp, width), lambda b: (b, 0, 0)),
                  pl.BlockSpec((None, window, kw), lambda b: (b, 0, 0)),
                  pl.BlockSpec((None, window, kw), lambda b: (b, 0, 0)),
                  pl.BlockSpec(bias.shape, lambda b: (0, 0, 0))],
        out_specs=pl.BlockSpec((None, tp, qw), lambda b: (b, 0, 0)),
        out_shape=jax.ShapeDtypeStruct((dbs, tp, qw), BF16),
        compiler_params=_cparams("arbitrary"),
    )(sinks, qkv_bm, buf_k, buf_v, bias)


def _block_means_kernel(k_ref, o_ref):
    o_ref[...] = jnp.mean(k_ref[...], axis=0, keepdims=True)


def _block_means(qkv, n_blocks_total, kcol, kw):
    return pl.pallas_call(
        _block_means_kernel,
        grid=(n_blocks_total,),
        in_specs=[pl.BlockSpec((MOBA_BLOCK, kw), lambda i: (i, kcol))],
        out_specs=pl.BlockSpec((None, 1, kw), lambda i: (i, 0, 0)),
        out_shape=jax.ShapeDtypeStruct((n_blocks_total, 1, kw), F32),
        compiler_params=_cparams("arbitrary"),
    )(qkv)


def _rank_rows(gate, blk, n_rows):
    rank = jnp.zeros(gate.shape, jnp.int32)
    for m in range(n_rows):
        gm = gate[m:m + 1, :]
        beats = (gm > gate) | ((gm == gate) & (blk > m))
        rank = rank + jnp.where(beats, 1, 0)
    return rank


def _moba_prompt_kernel(q_ref, k_ref, v_ref, means_ref, bias_ref, far_ref, o_ref, qs, sel_s, m_s, l_s, acc_s,
                        *, n_kv, group, hd, scale, n_near, nb):
    qb = pl.program_id(1)
    kb = pl.program_id(2)
    blk_rows = q_ref.shape[0]
    cols = group * blk_rows

    @pl.when(kb == 0)
    def _():
        blk = lax.broadcasted_iota(jnp.int32, (nb, cols), 0)
        for kv in range(n_kv):
            qg = jnp.concatenate([q_ref[:, (kv * group + g) * hd:(kv * group + g + 1) * hd]
                                  for g in range(group)], axis=0)
            qs[kv] = qg.astype(BF16)
            gate = _dot_nt(means_ref[:, kv * hd:(kv + 1) * hd], qg, precision=lax.Precision.HIGHEST)
            valid = blk < qb
            gate = jnp.where(valid, gate, -jnp.inf)
            rank = _rank_rows(gate, blk, nb)
            sel_s[kv] = jnp.where(valid & (rank < MOBA_TOPK), 1.0, 0.0)
            m_s[kv] = jnp.full((1, cols), NEG, F32)
            l_s[kv] = jnp.zeros((1, cols), F32)
            acc_s[kv] = jnp.zeros((hd, cols), F32)

    def attend(kv, add):
        k = k_ref[:, kv * hd:(kv + 1) * hd].astype(BF16)
        vt = v_ref[:, kv * hd:(kv + 1) * hd].T.astype(BF16)
        s = _dot_nt(k, qs[kv]) * scale + add
        m_prev = m_s[kv]
        m_new = jnp.maximum(m_prev, jnp.max(s, axis=0, keepdims=True))
        alpha = jnp.exp(m_prev - m_new)
        p = jnp.exp(s - m_new)
        l_s[kv] = alpha * l_s[kv] + jnp.sum(p, axis=0, keepdims=True)
        acc_s[kv] = alpha * acc_s[kv] + _dot(vt, p.astype(BF16))
        m_s[kv] = m_new

    delta = qb - kb

    @pl.when((kb <= qb) & (delta < n_near))
    def _():
        for kv in range(n_kv):
            chosen = (sel_s[kv, pl.ds(kb, 1), :] > 0.5) | (delta == 0)
            attend(kv, bias_ref[kv] + jnp.where(chosen, 0.0, NEG))

    @pl.when(delta >= n_near)
    def _():
        for kv in range(n_kv):
            chosen = sel_s[kv, pl.ds(kb, 1), :] > 0.5
            attend(kv, jnp.where(chosen, far_ref[kv], NEG))

    @pl.when(kb == qb)
    def _():
        for kv in range(n_kv):
            o = acc_s[kv] / l_s[kv]
            for g in range(group):
                h = kv * group + g
                o_ref[:, h * hd:(h + 1) * hd] = o[:, g * blk_rows:(g + 1) * blk_rows].T.astype(o_ref.dtype)


def _moba_prompt(qkv, means, bias_t, far_rows, bsz, n_heads, n_kv, hd):
    m = qkv.shape[0]
    t = m // bsz
    nb = t // MOBA_BLOCK
    group = n_heads // n_kv
    qw, kw = n_heads * hd, n_kv * hd
    kcol, vcol = qw // kw, qw // kw + 1
    n_near = bias_t.shape[0]
    cols = group * MOBA_BLOCK
    kv_idx = lambda b, qb, kb: b * nb + jnp.minimum(kb, qb)
    return pl.pallas_call(
        functools.partial(_moba_prompt_kernel, n_kv=n_kv, group=group, hd=hd, scale=hd ** -0.5,
                          n_near=n_near, nb=nb),
        grid=(bsz, nb, nb),
        in_specs=[pl.BlockSpec((MOBA_BLOCK, qw), lambda b, qb, kb: (b * nb + qb, 0)),
                  pl.BlockSpec((MOBA_BLOCK, kw), lambda b, qb, kb: (kv_idx(b, qb, kb), kcol)),
                  pl.BlockSpec((MOBA_BLOCK, kw), lambda b, qb, kb: (kv_idx(b, qb, kb), vcol)),
                  pl.BlockSpec((None, nb, kw), lambda b, qb, kb: (b, 0, 0)),
                  pl.BlockSpec((None, n_kv, MOBA_BLOCK, cols),
                               lambda b, qb, kb: (jnp.clip(qb - kb, 0, n_near - 1), 0, 0, 0)),
                  pl.BlockSpec((n_kv, 1, cols), lambda b, qb, kb: (0, 0, 0))],
        out_specs=pl.BlockSpec((MOBA_BLOCK, qw), lambda b, qb, kb: (b * nb + qb, 0)),
        out_shape=jax.ShapeDtypeStruct((m, qw), BF16),
        scratch_shapes=[pltpu.VMEM((n_kv, cols, hd), BF16),
                        pltpu.VMEM((n_kv, nb, cols), F32),
                        pltpu.VMEM((n_kv, 1, cols), F32),
                        pltpu.VMEM((n_kv, 1, cols), F32),
                        pltpu.VMEM((n_kv, hd, cols), F32)],
        compiler_params=_cparams("arbitrary", "arbitrary", "arbitrary"),
    )(qkv, qkv, qkv, means, bias_t, far_rows)


_MEAN_PAGES_PER_STEP = 16


def _pool_means_kernel(pt_ref, *refs, pages_per_block):
    del pt_ref
    o_ref = refs[-1]
    pages = refs[:-1]
    rows = []
    for blk in range(len(pages) // pages_per_block):
        tot = jnp.sum(pages[blk * pages_per_block][...], axis=0, keepdims=True)
        for p in range(1, pages_per_block):
            tot = tot + jnp.sum(pages[blk * pages_per_block + p][...], axis=0, keepdims=True)
        rows.append(tot)
    n_rows = pages_per_block * pages[0].shape[0]
    o_ref[...] = jnp.concatenate(rows, axis=0) * (1.0 / n_rows)


def _pool_means(pool, pt_flat, page_off, dbs, n_pages, page_size):
    kw = pool.shape[2]
    ppb = MOBA_BLOCK // page_size
    pps = min(_MEAN_PAGES_PER_STEP, n_pages)
    assert n_pages % pps == 0 and pps % ppb == 0 and (pps // ppb) % V7X_SUBLANES == 0
    steps = n_pages // pps

    def page_spec(i):
        return pl.BlockSpec((None, page_size, kw),
                            lambda b, s, pt: (pt[b * n_pages + s * pps + i] + page_off, 0, 0))

    grid_spec = pltpu.PrefetchScalarGridSpec(
        num_scalar_prefetch=1,
        grid=(dbs, steps),
        in_specs=[page_spec(i) for i in range(pps)],
        out_specs=pl.BlockSpec((None, pps // ppb, kw), lambda b, s, pt: (b, s, 0)),
    )
    return pl.pallas_call(
        functools.partial(_pool_means_kernel, pages_per_block=ppb),
        grid_spec=grid_spec,
        out_shape=jax.ShapeDtypeStruct((dbs, n_pages // ppb, kw), F32),
        compiler_params=_cparams("arbitrary", "arbitrary"),
    )(pt_flat, *([pool] * pps))


def _moba_topk_kernel(qkv_ref, means_ref, o_ref, *, n_heads, group, hd):
    tp = qkv_ref.shape[0]
    n_blk = means_ref.shape[0]
    cols = n_heads * tp
    qall = jnp.concatenate([qkv_ref[:, h * hd:(h + 1) * hd] for h in range(n_heads)], axis=0)
    col_kv = lax.broadcasted_iota(jnp.int32, (n_blk, cols), 1) // (group * tp)
    gate = jnp.zeros((n_blk, cols), F32)
    for kv in range(n_heads // group):
        g = _dot_nt(means_ref[:, kv * hd:(kv + 1) * hd], qall, precision=lax.Precision.HIGHEST)
        gate = jnp.where(col_kv == kv, g, gate)
    blk = lax.broadcasted_iota(jnp.int32, (n_blk, cols), 0)
    rank = _rank_rows(gate, blk, n_blk)
    blk_f = blk.astype(F32)
    rows = [jnp.sum(jnp.where(rank == k, blk_f, 0.0), axis=0, keepdims=True) for k in range(MOBA_TOPK)]
    rows.append(jnp.zeros((o_ref.shape[0] - MOBA_TOPK, cols), F32))
    o_ref[...] = jnp.concatenate(rows, axis=0).astype(jnp.int32)


def _moba_topk(qkv_bm, means, n_heads, n_kv, hd):
    dbs, tp, width = qkv_bm.shape
    n_blk, kw = means.shape[1], means.shape[2]
    cols = n_heads * tp
    return pl.pallas_call(
        functools.partial(_moba_topk_kernel, n_heads=n_heads, group=n_heads // n_kv, hd=hd),
        grid=(dbs,),
        in_specs=[pl.BlockSpec((None, tp, width), lambda b: (b, 0, 0)),
                  pl.BlockSpec((None, n_blk, kw), lambda b: (b, 0, 0))],
        out_specs=pl.BlockSpec((None, V7X_SUBLANES, cols), lambda b: (b, 0, 0)),
        out_shape=jax.ShapeDtypeStruct((dbs, V7X_SUBLANES, cols), jnp.int32),
        compiler_params=_cparams("arbitrary"),
    )(qkv_bm, means)


def _moba_sample_kernel(pt_ref, sel_ref, *refs, n_sel_pages, t_new, n_heads, n_blk, n_cls, ppb, scale, pad):
    del pt_ref
    b, t, h = pl.program_id(0), pl.program_id(1), pl.program_id(2)
    k_pages = refs[:n_sel_pages]
    v_pages = refs[n_sel_pages:2 * n_sel_pages]
    q_ref, kn_ref, vn_ref, bsel_ref, bown_ref, o_ref = refs[2 * n_sel_pages:]
    tp, hd = q_ref.shape
    zeros = jnp.zeros((pad - tp, hd), F32)
    kcat = jnp.concatenate([r[...] for r in k_pages] + [kn_ref[...], zeros], axis=0).astype(BF16)
    vcat = jnp.concatenate([r[...] for r in v_pages] + [vn_ref[...], zeros], axis=0).astype(BF16)
    q = jnp.broadcast_to(q_ref[pl.ds(t, 1), :], (tp, hd)).astype(BF16)
    bias_rows = []
    for s in range(MOBA_TOPK):
        sel = sel_ref[((b * MOBA_TOPK + s) * n_heads + h) * t_new + t]
        cls = jnp.minimum(n_blk - 1 - sel, n_cls - 1)
        bias_rows.append(bsel_ref[pl.ds(cls, 1), :])
    bias = jnp.concatenate(bias_rows + [bown_ref[...]], axis=-1)
    s = _dot_nt(q, kcat) * scale + bias
    m = jnp.max(s, axis=-1, keepdims=True)
    p = jnp.exp(s - m)
    l = jnp.sum(p, axis=-1, keepdims=True)
    o_ref[...] = _dot(p.astype(BF16), vcat) / l


def _moba_sample(qkv_bm, pool_k, pool_v, pt_flat, sel_flat, page_off, bias_sel, bias_own, *, t_new, n_heads, n_kv,
                 hd, n_pages, page_size):
    dbs, tp, _ = qkv_bm.shape
    group = n_heads // n_kv
    ppb = MOBA_BLOCK // page_size
    n_blk = n_pages // ppb
    n_sel_pages = MOBA_TOPK * ppb
    qw, kw = n_heads * hd, n_kv * hd
    kcol0, vcol0 = qw // hd, (qw + kw) // hd
    n_cls = bias_sel.shape[2]
    pad = bias_own.shape[3]

    def page_spec(i):
        s, p = divmod(i, ppb)

        def index(b, t, h, pt, sel):
            blk = sel[((b * MOBA_TOPK + s) * n_heads + h) * t_new + t]
            return (pt[b * n_pages + blk * ppb + p] + page_off, 0, h // group)
        return pl.BlockSpec((None, page_size, hd), index)

    in_specs = ([page_spec(i) for i in range(n_sel_pages)] * 2
                + [pl.BlockSpec((None, tp, hd), lambda b, t, h, pt, sel: (b, 0, h)),
                   pl.BlockSpec((None, tp, hd), lambda b, t, h, pt, sel: (b, 0, kcol0 + h // group)),
                   pl.BlockSpec((None, tp, hd), lambda b, t, h, pt, sel: (b, 0, vcol0 + h // group)),
                   pl.BlockSpec((None, None, n_cls, MOBA_BLOCK), lambda b, t, h, pt, sel: (h, t, 0, 0)),
                   pl.BlockSpec((None, None, 1, pad), lambda b, t, h, pt, sel: (h, t, 0, 0))])
    grid_spec = pltpu.PrefetchScalarGridSpec(
        num_scalar_prefetch=2,
        grid=(dbs, t_new, n_heads),
        in_specs=in_specs,
        out_specs=pl.BlockSpec((None, None, None, tp, hd), lambda b, t, h, pt, sel: (b, t, h, 0, 0)),
    )
    return pl.pallas_call(
        functools.partial(_moba_sample_kernel, n_sel_pages=n_sel_pages, t_new=t_new, n_heads=n_heads,
                          n_blk=n_blk, n_cls=n_cls, ppb=ppb, scale=hd ** -0.5, pad=pad),
        grid_spec=grid_spec,
        out_shape=jax.ShapeDtypeStruct((dbs, t_new, n_heads, tp, hd), F32),
        compiler_params=_cparams("arbitrary", "arbitrary", "arbitrary"),
    )(pt_flat, sel_flat, *([pool_k] * n_sel_pages), *([pool_v] * n_sel_pages), qkv_bm, qkv_bm, qkv_bm,
      bias_sel, bias_own)


def _swa_prompt_bias_idx(window, nbk):
    qi = np.arange(window)[:, None]
    kj = np.arange(2 * window)[None, :]
    dist = qi + window - kj
    return np.where((dist >= 0) & (dist < window), _bucket_np(dist, nbk), -1).astype(np.int32)


def _swa_sample_bias_idx(window, t_new, tp, pad, nbk):
    idx = np.full((tp, window + pad), -1, np.int32)
    for t in range(t_new):
        for j in range(window + t_new):
            dist = t + window - j
            if 0 <= dist < window:
                idx[t, j] = _bucket_np(np.array(dist), nbk)
    idx[t_new:, :] = 0
    return idx


def _moba_prompt_bias_idx(n_near, nbk):
    j = np.arange(MOBA_BLOCK)[:, None]
    qi = np.arange(MOBA_BLOCK)[None, :]
    tiles = []
    for delta in range(n_near):
        dist = delta * MOBA_BLOCK + qi - j
        tiles.append(np.where(dist >= 0, _bucket_np(dist, nbk), -1))
    return np.concatenate(tiles, axis=0).astype(np.int32)


def _moba_sample_bias_idx(t_new, n_cls_pad, n_near, nbk):
    idx = np.zeros((t_new, n_cls_pad, MOBA_BLOCK), np.int32)
    j = np.arange(MOBA_BLOCK)
    for t in range(t_new):
        for c in range(n_near):
            dist = (c + 1) * MOBA_BLOCK + t - j
            idx[t, c] = _bucket_np(dist, nbk)
        idx[t, n_near:] = nbk - 1
    return idx.reshape(t_new * n_cls_pad, MOBA_BLOCK)


def _moba_own_bias_idx(t_new, tp, pad, nbk):
    idx = np.full((t_new, pad), -1, np.int32)
    for t in range(t_new):
        for j in range(t + 1):
            idx[t, j] = _bucket_np(np.array(t - j), nbk)
    del tp
    return idx


def kernel(x_prompt, x_sample, c_prompt, c_sample, state_conv, cache_swa_k, cache_swa_v, cache_moba_k, cache_moba_v, page_table, state_ffn, ada_w, ada_b, norm_mix, norm_ffn, norm_final, rel_bias, conv_w1, conv_b1, conv_dw, conv_dw_b, conv_ln_g, conv_ln_b, conv_w2, conv_b2, swa_wqkv, swa_wo, swa_sinks, moba_wqkv, moba_wo, ffn_w_up, ffn_dw, ffn_dw_b, ffn_w_down):
    bsz, seq, d = x_prompt.shape
    dbs, t_new, _ = x_sample.shape
    depth = ada_w.shape[0]
    window, n_kv, hd = cache_swa_k.shape[2], cache_swa_k.shape[3], cache_swa_k.shape[4]
    nbk, n_heads = rel_bias.shape
    group = n_heads // n_kv
    qw, kw = n_heads * hd, n_kv * hd
    cw = conv_dw.shape[1]
    d_ff = ffn_dw.shape[2]
    n_pool, page_size = cache_moba_k.shape[1], cache_moba_k.shape[2]
    n_pages = page_table.shape[1]
    ppb = MOBA_BLOCK // page_size
    n_blk_past = n_pages // ppb
    tp = V7X_SUBLANES
    assert t_new <= tp and n_pages % ppb == 0 and n_blk_past >= MOBA_TOPK and seq % MOBA_BLOCK == 0
    n_mixers = 3

    n_c = bsz + dbs
    c_rows = -(-n_c // V7X_SUBLANES) * V7X_SUBLANES
    c_all = jnp.concatenate([c_prompt, c_sample, jnp.zeros((c_rows - n_c, d), F32)], axis=0)
    mod = _ada(c_all, ada_w, ada_b)

    def mods(layer, which):
        chunk = mod[layer, :, which * d:(which + 1) * d]
        return chunk[:bsz], chunk[bsz:bsz + dbs]

    d_sat = _saturation_distance(nbk)
    n_near = -(-(d_sat + MOBA_BLOCK - 1) // MOBA_BLOCK)
    swa_bias_p = swa_bias_s = moba_bias_p = moba_far = moba_bias_sel = moba_bias_own = None
    if depth > 1:
        swa_bias_p = _bias_expand(rel_bias, _swa_prompt_bias_idx(window, nbk), window,
                                  (n_heads, window, 2 * window), (None, window, 2 * window),
                                  lambda h: (h, 0, 0))
        pad_s = V7X_LANES
        swa_bias_s = _bias_expand(rel_bias, _swa_sample_bias_idx(window, t_new, tp, pad_s, nbk), tp,
                                  (n_heads, tp, window + pad_s), (None, tp, window + pad_s),
                                  lambda h: (h, 0, 0))
    if depth > 2:
        cols = group * MOBA_BLOCK
        moba_bias_p = _bias_expand(
            rel_bias, _moba_prompt_bias_idx(n_near, nbk), MOBA_BLOCK,
            (n_near * MOBA_BLOCK, n_kv * cols), (n_near * MOBA_BLOCK, MOBA_BLOCK),
            lambda h: (0, h)).reshape(n_near, MOBA_BLOCK, n_kv, cols).transpose(0, 2, 1, 3)
        moba_far = jnp.repeat(rel_bias[nbk - 1].reshape(n_kv, group, 1), MOBA_BLOCK, axis=2).reshape(n_kv, 1, cols)
        n_cls_pad = V7X_SUBLANES
        assert n_near + 1 <= n_cls_pad
        moba_bias_sel = _bias_expand(
            rel_bias, _moba_sample_bias_idx(t_new, n_cls_pad, n_near, nbk), n_cls_pad,
            (n_heads, t_new * n_cls_pad, MOBA_BLOCK), (None, t_new * n_cls_pad, MOBA_BLOCK),
            lambda h: (h, 0, 0)).reshape(n_heads, t_new, n_cls_pad, MOBA_BLOCK)
        pad_o = V7X_LANES
        moba_bias_own = _bias_expand(
            rel_bias, _moba_own_bias_idx(t_new, tp, pad_o, nbk), t_new,
            (n_heads, t_new, pad_o), (None, t_new, pad_o),
            lambda h: (h, 0, 0)).reshape(n_heads, t_new, 1, pad_o)

    def to_tm(a):
        return jnp.swapaxes(a, 0, 1).reshape((a.shape[1] * dbs,) + a.shape[2:])

    def to_bm(a, t):
        return jnp.swapaxes(a.reshape((t, dbs) + a.shape[1:]), 0, 1)

    xp = x_prompt.reshape(bsz * seq, d)
    xs = to_tm(x_sample)
    m_s = t_new * dbs

    conv_p, conv_s, swa_k_p, swa_v_p, swa_k_s, swa_v_s = [], [], [], [], [], []
    moba_k_p, moba_v_p, moba_k_s, moba_v_s, ffn_p, ffn_s = [], [], [], [], [], []

    def gates(layer, which):
        gp, gs = mods(layer, which)
        return gp.reshape(bsz, 1, d), jnp.tile(gs, (t_new, 1)).reshape(1, m_s, d)

    def normed(x_p, x_s, g, layer, which_sh):
        shp, shs = mods(layer, which_sh)
        scp, scs = mods(layer, which_sh + 1)
        hp = _norm_mod(x_p.reshape(bsz, seq, d), g, scp.reshape(bsz, 1, d), shp.reshape(bsz, 1, d), 512)
        hs = _norm_mod(x_s.reshape(t_new, dbs, d), g, scs.reshape(1, dbs, d), shs.reshape(1, dbs, d), dbs)
        return hp.reshape(bsz * seq, d), hs.reshape(m_s, d)

    def pad_time(a_bm):
        return jnp.concatenate([a_bm, jnp.zeros((dbs, tp - t_new, a_bm.shape[2]), a_bm.dtype)], axis=1)

    for layer in range(depth):
        kind, j = layer % n_mixers, layer // n_mixers
        hp, hs = normed(xp, xs, norm_mix[layer], layer, 0)
        g1p, g1s = gates(layer, 2)
        if kind == 0:
            up = _glu(hp, conv_w1, conv_b1, j)
            us = _glu(hs, conv_w1, conv_b1, j)
            yp = _conv_ln_prompt(up.reshape(bsz, seq, d), conv_dw[j], conv_dw_b[j], conv_ln_g[j], conv_ln_b[j])
            st_tm = jnp.swapaxes(state_conv[j], 0, 1)
            ys, nst_tm = _conv_ln_sample(st_tm, us.reshape(t_new, dbs, d), conv_dw[j], conv_dw_b[j],
                                         conv_ln_g[j], conv_ln_b[j])
            conv_p.append(up.reshape(bsz, seq, d)[:, seq - (cw - 1):])
            conv_s.append(jnp.swapaxes(nst_tm, 0, 1))
            xp = _mm(yp.reshape(bsz * seq, d), conv_w2, j, bias=conv_b2, res=xp, gate=g1p)
            xs = _mm(ys.reshape(m_s, d), conv_w2, j, bias=conv_b2, res=xs, gate=g1s)
        elif kind == 1:
            qkv_p = _mm(hp, swa_wqkv, j)
            qkv_s = _mm(hs, swa_wqkv, j)
            op = _swa_prompt(qkv_p, swa_sinks[j], swa_bias_p, bsz, n_heads, n_kv, hd, window)
            qkv_bm = to_bm(qkv_s, t_new)
            buf_k = cache_swa_k[j].reshape(dbs, window, kw)
            buf_v = cache_swa_v[j].reshape(dbs, window, kw)
            os_bm = _swa_sample(pad_time(qkv_bm), buf_k, buf_v, swa_sinks[j], swa_bias_s, n_heads, n_kv, hd)
            os_ = to_tm(os_bm[:, :t_new])
            k_p = qkv_p[:, qw:qw + kw].reshape(bsz, seq, n_kv, hd)
            v_p = qkv_p[:, qw + kw:].reshape(bsz, seq, n_kv, hd)
            swa_k_p.append(k_p[:, seq - window:])
            swa_v_p.append(v_p[:, seq - window:])
            k_s = qkv_bm[:, :, qw:qw + kw]
            v_s = qkv_bm[:, :, qw + kw:]
            swa_k_s.append(jnp.concatenate([buf_k, k_s], axis=1)[:, t_new:].reshape(dbs, window, n_kv, hd))
            swa_v_s.append(jnp.concatenate([buf_v, v_s], axis=1)[:, t_new:].reshape(dbs, window, n_kv, hd))
            xp = _mm(op, swa_wo, j, res=xp, gate=g1p)
            xs = _mm(os_, swa_wo, j, res=xs, gate=g1s)
        else:
            qkv_p = _mm(hp, moba_wqkv, j)
            qkv_s = _mm(hs, moba_wqkv, j)
            nb = seq // MOBA_BLOCK
            means_p = _block_means(qkv_p, bsz * nb, qw // kw, kw).reshape(bsz, nb, kw)
            op = _moba_prompt(qkv_p, means_p, moba_bias_p, moba_far, bsz, n_heads, n_kv, hd)
            moba_k_p.append(qkv_p[:, qw:qw + kw].reshape(bsz, seq // page_size, page_size, n_kv, hd))
            moba_v_p.append(qkv_p[:, qw + kw:].reshape(bsz, seq // page_size, page_size, n_kv, hd))

            qkv_bm = to_bm(qkv_s, t_new)
            qkv_pad = pad_time(qkv_bm)
            pool_k = cache_moba_k.reshape(cache_moba_k.shape[0] * n_pool, page_size, kw)
            pool_v = cache_moba_v.reshape(cache_moba_v.shape[0] * n_pool, page_size, kw)
            pt_flat = page_table.reshape(-1)
            means_s = _pool_means(pool_k, pt_flat, j * n_pool, dbs, n_pages, page_size)
            sel = _moba_topk(qkv_pad, means_s, n_heads, n_kv, hd)
            sel = sel[:, :MOBA_TOPK].reshape(dbs, MOBA_TOPK, n_heads, tp)[..., :t_new]
            o_all = _moba_sample(qkv_pad, pool_k, pool_v, pt_flat, sel.reshape(-1), j * n_pool, moba_bias_sel,
                                 moba_bias_own, t_new=t_new, n_heads=n_heads, n_kv=n_kv, hd=hd,
                                 n_pages=n_pages, page_size=page_size)
            os_bm = o_all[:, :, :, 0, :].reshape(dbs, t_new, qw).astype(BF16)
            moba_k_s.append(qkv_bm[:, :, qw:qw + kw].reshape(dbs, t_new, n_kv, hd))
            moba_v_s.append(qkv_bm[:, :, qw + kw:].reshape(dbs, t_new, n_kv, hd))
            xp = _mm(op, moba_wo, j, res=xp, gate=g1p)
            xs = _mm(to_tm(os_bm), moba_wo, j, res=xs, gate=g1s)

        hp, hs = normed(xp, xs, norm_ffn[layer], layer, 3)
        g2p, g2s = gates(layer, 5)
        ap, stp = _ffn_up_prompt(hp, ffn_w_up, ffn_dw, ffn_dw_b, layer, bsz)
        prev_tm = jnp.swapaxes(state_ffn[layer], 0, 1).reshape(2 * dbs, d_ff)
        as_, sts = _ffn_up_sample(hs, ffn_w_up, ffn_dw, ffn_dw_b, prev_tm, layer, dbs)
        ffn_p.append(stp)
        ffn_s.append(jnp.swapaxes(sts.reshape(2, dbs, d_ff), 0, 1))
        xp = _mm(ap, ffn_w_down, layer, res=xp, gate=g2p, tn=256)
        xs = _mm(as_, ffn_w_down, layer, res=xs, gate=g2s, tn=256)

    y_p = _norm_mod(xp.reshape(bsz, seq, d), norm_final, None, None, 512, out_dtype=F32)
    y_s = _norm_mod(xs.reshape(t_new, dbs, d), norm_final, None, None, dbs, out_dtype=F32)
    y_s = jnp.swapaxes(y_s, 0, 1)

    return (y_p, y_s, jnp.stack(conv_p), jnp.stack(conv_s), jnp.stack(swa_k_p), jnp.stack(swa_v_p),
            jnp.stack(swa_k_s), jnp.stack(swa_v_s), jnp.stack(moba_k_p), jnp.stack(moba_v_p),
            jnp.stack(moba_k_s), jnp.stack(moba_v_s), jnp.stack(ffn_p), jnp.stack(ffn_s))
```

```python
import functools
import math

import numpy as np
import jax
import jax.numpy as jnp
from jax import lax
from jax.experimental import pallas as pl
from jax.experimental.pallas import tpu as pltpu

MOBA_BLOCK = 256
MOBA_TOPK = 3
REL_MAX_DISTANCE = 1024
RMS_EPS = 1e-6
LN_EPS = 1e-5
N_MOD = 6

V7X_LANES = 128
V7X_SUBLANES = 8
V7X_VMEM_LIMIT_BYTES = 56 * 1024 * 1024

NEG = -1e30
BF16 = jnp.bfloat16
F32 = jnp.float32


def _cparams(*sem):
    return pltpu.CompilerParams(dimension_semantics=sem, vmem_limit_bytes=V7X_VMEM_LIMIT_BYTES)


def _dot(a, b):
    return jnp.dot(a, b, preferred_element_type=F32)


def _dot_nt(a, b, precision=None):
    return lax.dot_general(a, b, (((1,), (1,)), ((), ())), precision=precision,
                           preferred_element_type=F32)


def _silu(x):
    return x * jax.nn.sigmoid(x)


def _bucket_np(dist, num_buckets):
    n = np.maximum(dist, 0)
    max_exact = num_buckets // 2
    nf = np.maximum(n, 1).astype(np.float32)
    ratio = np.log(nf / np.float32(max_exact)) / np.float32(math.log(REL_MAX_DISTANCE / max_exact))
    large = max_exact + (ratio * np.float32(num_buckets - max_exact)).astype(np.int32)
    return np.where(n < max_exact, n, np.minimum(large, num_buckets - 1)).astype(np.int32)


def _saturation_distance(num_buckets):
    d = np.arange(0, 4 * REL_MAX_DISTANCE)
    b = _bucket_np(d, num_buckets)
    below = np.nonzero(b < num_buckets - 1)[0]
    return int(below.max()) + 1


def _bias_expand_kernel(tab_ref, idx_ref, o_ref, *, tiles):
    h = pl.program_id(0)
    for r0, nr, buckets in tiles:
        idx = idx_ref[r0:r0 + nr, :]
        acc = jnp.full(idx.shape, NEG, F32)
        for b in buckets:
            acc = jnp.where(idx == b, tab_ref[b, h], acc)
        o_ref[r0:r0 + nr, :] = acc


def _bias_expand(table, idx_np, tile_rows, out_shape, out_block, out_index):
    n_heads = table.shape[1]
    rows, cols = idx_np.shape
    tiles = []
    for r0 in range(0, rows, tile_rows):
        present = np.unique(idx_np[r0:r0 + tile_rows])
        tiles.append((r0, min(tile_rows, rows - r0), tuple(int(b) for b in present if b >= 0)))
    return pl.pallas_call(
        functools.partial(_bias_expand_kernel, tiles=tuple(tiles)),
        grid=(n_heads,),
        in_specs=[pl.BlockSpec(memory_space=pltpu.SMEM),
                  pl.BlockSpec((rows, cols), lambda h: (0, 0))],
        out_specs=pl.BlockSpec(out_block, out_index),
        out_shape=jax.ShapeDtypeStruct(out_shape, F32),
        compiler_params=_cparams("arbitrary"),
    )(table, jnp.asarray(idx_np))


def _ada_kernel(c_ref, w_ref, b_ref, o_ref):
    a = _silu(c_ref[...]).astype(BF16)
    o_ref[...] = _dot(a, w_ref[...].astype(BF16)) + b_ref[...]


def _ada(c_all, ada_w, ada_b):
    depth, d, n = ada_w.shape
    rows = c_all.shape[0]
    tn = min(n, 2048)
    return pl.pallas_call(
        _ada_kernel,
        grid=(depth, n // tn),
        in_specs=[pl.BlockSpec((rows, d), lambda l, j: (0, 0)),
                  pl.BlockSpec((None, d, tn), lambda l, j: (l, 0, j)),
                  pl.BlockSpec((None, 1, tn), lambda l, j: (l, 0, j))],
        out_specs=pl.BlockSpec((None, rows, tn), lambda l, j: (l, 0, j)),
        out_shape=jax.ShapeDtypeStruct((depth, rows, n), F32),
        compiler_params=_cparams("arbitrary", "arbitrary"),
    )(c_all, ada_w, ada_b.reshape(depth, 1, n))


def _norm_mod_kernel(x_ref, g_ref, sc_ref, sh_ref, o_ref):
    x = x_ref[...]
    ms = jnp.mean(x * x, axis=-1, keepdims=True)
    y = x * lax.rsqrt(ms + RMS_EPS) * g_ref[...]
    o_ref[...] = (y * (1.0 + sc_ref[...]) + sh_ref[...]).astype(o_ref.dtype)


def _rms_kernel(x_ref, g_ref, o_ref):
    x = x_ref[...]
    ms = jnp.mean(x * x, axis=-1, keepdims=True)
    o_ref[...] = (x * lax.rsqrt(ms + RMS_EPS) * g_ref[...]).astype(o_ref.dtype)


def _norm_mod(x3, g, sc3, sh3, rows_per_step, out_dtype=BF16):
    a, r, d = x3.shape
    tr = min(rows_per_step, r)
    g3 = g.reshape(1, 1, d)
    if sc3 is None:
        return pl.pallas_call(
            _rms_kernel,
            grid=(a, r // tr),
            in_specs=[pl.BlockSpec((1, tr, d), lambda i, j: (i, j, 0)),
                      pl.BlockSpec((1, 1, d), lambda i, j: (0, 0, 0))],
            out_specs=pl.BlockSpec((1, tr, d), lambda i, j: (i, j, 0)),
            out_shape=jax.ShapeDtypeStruct(x3.shape, out_dtype),
            compiler_params=_cparams("arbitrary", "arbitrary"),
        )(x3, g3)
    per_seq = sc3.shape[0] == a and sc3.shape[1] == 1
    if per_seq:
        mspec = pl.BlockSpec((1, 1, d), lambda i, j: (i, 0, 0))
    else:
        assert sc3.shape[0] == 1 and sc3.shape[1] == r and tr == r
        mspec = pl.BlockSpec((1, r, d), lambda i, j: (0, 0, 0))
    return pl.pallas_call(
        _norm_mod_kernel,
        grid=(a, r // tr),
        in_specs=[pl.BlockSpec((1, tr, d), lambda i, j: (i, j, 0)),
                  pl.BlockSpec((1, 1, d), lambda i, j: (0, 0, 0)),
                  mspec, mspec],
        out_specs=pl.BlockSpec((1, tr, d), lambda i, j: (i, j, 0)),
        out_shape=jax.ShapeDtypeStruct(x3.shape, out_dtype),
        compiler_params=_cparams("arbitrary", "arbitrary"),
    )(x3, g3, sc3, sh3)


def _mm_kernel(*refs, has_bias, has_res):
    x_ref, w_ref = refs[0], refs[1]
    pos = 2
    acc = _dot(x_ref[...], w_ref[...].astype(BF16))
    if has_bias:
        acc = acc + refs[pos][...]
        pos += 1
    if has_res:
        acc = refs[pos][...] + refs[pos + 1][...] * acc
        pos += 2
    o_ref = refs[pos]
    o_ref[...] = acc.astype(o_ref.dtype)


def _mm(x, w, layer, *, bias=None, res=None, gate=None, out_dtype=F32, tm=1024, tn=1024):
    m, k = x.shape
    n = w.shape[2]
    tm, tn = min(tm, m), min(tn, n)
    assert m % tm == 0 and n % tn == 0
    in_specs = [pl.BlockSpec((tm, k), lambda i, j: (i, 0)),
                pl.BlockSpec((None, k, tn), lambda i, j: (layer, 0, j))]
    args = [x, w]
    if bias is not None:
        in_specs.append(pl.BlockSpec((None, 1, tn), lambda i, j: (layer, 0, j)))
        args.append(bias.reshape(bias.shape[0], 1, n))
    if res is not None:
        gs, gr, _ = gate.shape
        assert gr in (1, tm) and (m // tm) % gs == 0
        tiles_per_gate = (m // tm) // gs
        in_specs.append(pl.BlockSpec((tm, tn), lambda i, j: (i, j)))
        in_specs.append(pl.BlockSpec((None, gr, tn), lambda i, j: (i // tiles_per_gate, 0, j)))
        args += [res, gate]
    return pl.pallas_call(
        functools.partial(_mm_kernel, has_bias=bias is not None, has_res=res is not None),
        grid=(m // tm, n // tn),
        in_specs=in_specs,
        out_specs=pl.BlockSpec((tm, tn), lambda i, j: (i, j)),
        out_shape=jax.ShapeDtypeStruct((m, n), out_dtype),
        compiler_params=_cparams("arbitrary", "arbitrary"),
    )(*args)


def _glu_kernel(x_ref, wa_ref, wg_ref, ba_ref, bg_ref, o_ref, wa_s, wg_s):
    @pl.when(pl.program_id(1) == 0)
    def _():
        wa_s[...] = wa_ref[...].astype(BF16)
        wg_s[...] = wg_ref[...].astype(BF16)

    x = x_ref[...]
    a = _dot(x, wa_s[...]) + ba_ref[...]
    g = _dot(x, wg_s[...]) + bg_ref[...]
    o_ref[...] = a * jax.nn.sigmoid(g)


def _glu(x, w1, b1, layer, *, tm=1024, tn=512):
    m, k = x.shape
    n = w1.shape[2] // 2
    tm, tn = min(tm, m), min(tn, n)
    nj = n // tn
    b3 = b1.reshape(b1.shape[0], 1, 2 * n)
    return pl.pallas_call(
        _glu_kernel,
        grid=(nj, m // tm),
        in_specs=[pl.BlockSpec((tm, k), lambda j, i: (i, 0)),
                  pl.BlockSpec((None, k, tn), lambda j, i: (layer, 0, j)),
                  pl.BlockSpec((None, k, tn), lambda j, i: (layer, 0, nj + j)),
                  pl.BlockSpec((None, 1, tn), lambda j, i: (layer, 0, j)),
                  pl.BlockSpec((None, 1, tn), lambda j, i: (layer, 0, nj + j))],
        out_specs=pl.BlockSpec((tm, tn), lambda j, i: (i, j)),
        out_shape=jax.ShapeDtypeStruct((m, n), F32),
        scratch_shapes=[pltpu.VMEM((k, tn), BF16), pltpu.VMEM((k, tn), BF16)],
        compiler_params=_cparams("arbitrary", "arbitrary"),
    )(x, w1, w1, b3, b3)


def _layernorm_silu(y, lg, lb):
    mu = jnp.mean(y, axis=-1, keepdims=True)
    yc = y - mu
    var = jnp.mean(yc * yc, axis=-1, keepdims=True)
    return _silu(yc * lax.rsqrt(var + LN_EPS) * lg + lb)


def _conv_ln_prompt_kernel(u_ref, halo_ref, dw_ref, dwb_ref, lg_ref, lb_ref, o_ref, full_s, y_s,
                           *, tm, halo, cw, rc, lw):
    i = pl.program_id(1)
    d = u_ref.shape[-1]
    full_s[halo:halo + tm, :] = u_ref[...]

    @pl.when(i == 0)
    def _():
        full_s[0:halo, :] = jnp.zeros((halo, d), F32)

    @pl.when(i > 0)
    def _():
        full_s[0:halo, :] = halo_ref[...]

    off = halo - (cw - 1)
    win = rc + halo
    sub = V7X_SUBLANES

    def body(r, carry):
        r0 = pl.multiple_of(r * rc, rc)
        for c0 in range(0, d, lw):
            window = full_s[pl.ds(r0, win), c0:c0 + lw]
            acc = jnp.zeros((rc, lw), F32) + dwb_ref[:, c0:c0 + lw]
            for s in range(sub):
                taps = [k for k in range(cw) if (off + k) % sub == s]
                if not taps:
                    continue
                shifted = window if s == 0 else pltpu.roll(window, win - s, 0)
                for k in taps:
                    a = off + k - s
                    acc = acc + dw_ref[k:k + 1, c0:c0 + lw] * shifted[a:a + rc]
            y_s[pl.ds(r0, rc), c0:c0 + lw] = acc
        return carry
    lax.fori_loop(0, tm // rc, body, 0)

    o_ref[...] = _layernorm_silu(y_s[...], lg_ref[...], lb_ref[...]).astype(o_ref.dtype)


def _conv_ln_prompt(u3, dw, dwb, lg, lb, *, tm=256):
    bsz, t, d = u3.shape
    cw = dw.shape[0]
    halo = 32
    assert cw - 1 <= halo and t % tm == 0 and tm % halo == 0
    hb = tm // halo
    lw = min(d, 256)
    return pl.pallas_call(
        functools.partial(_conv_ln_prompt_kernel, tm=tm, halo=halo, cw=cw, rc=halo, lw=lw),
        grid=(bsz, t // tm),
        in_specs=[pl.BlockSpec((None, tm, d), lambda b, i: (b, i, 0)),
                  pl.BlockSpec((None, halo, d), lambda b, i: (b, jnp.maximum(i * hb - 1, 0), 0)),
                  pl.BlockSpec((cw, d), lambda b, i: (0, 0)),
                  pl.BlockSpec((1, d), lambda b, i: (0, 0)),
                  pl.BlockSpec((1, d), lambda b, i: (0, 0)),
                  pl.BlockSpec((1, d), lambda b, i: (0, 0))],
        out_specs=pl.BlockSpec((None, tm, d), lambda b, i: (b, i, 0)),
        out_shape=jax.ShapeDtypeStruct((bsz, t, d), BF16),
        scratch_shapes=[pltpu.VMEM((tm + halo, d), F32), pltpu.VMEM((tm, d), F32)],
        compiler_params=_cparams("arbitrary", "arbitrary"),
    )(u3, u3, dw, dwb.reshape(1, d), lg.reshape(1, d), lb.reshape(1, d))


def _conv_ln_sample_kernel(st_ref, u_ref, dw_ref, dwb_ref, lg_ref, lb_ref, o_ref, nst_ref, *, cw, lw):
    t_new, _, d = u_ref.shape
    n_st = cw - 1

    def row(idx, c0):
        if idx < n_st:
            return st_ref[idx, :, c0:c0 + lw]
        return u_ref[idx - n_st, :, c0:c0 + lw]

    for t in range(t_new):
        parts = []
        for c0 in range(0, d, lw):
            acc = dw_ref[0:1, c0:c0 + lw] * row(t, c0)
            for k in range(1, cw):
                acc = acc + dw_ref[k:k + 1, c0:c0 + lw] * row(t + k, c0)
            parts.append(acc + dwb_ref[:, c0:c0 + lw])
        y = jnp.concatenate(parts, axis=-1) if len(parts) > 1 else parts[0]
        o_ref[t] = _layernorm_silu(y, lg_ref[...], lb_ref[...]).astype(o_ref.dtype)
    for r in range(n_st):
        idx = t_new + r
        nst_ref[r] = st_ref[idx] if idx < n_st else u_ref[idx - n_st]


def _conv_ln_sample(st_tm, u_tm, dw, dwb, lg, lb):
    n_st, bsz, d = st_tm.shape
    t_new = u_tm.shape[0]
    cw = dw.shape[0]
    lw = min(d, 512)
    full = lambda shape: pl.BlockSpec(shape, lambda i: (0,) * len(shape))
    return pl.pallas_call(
        functools.partial(_conv_ln_sample_kernel, cw=cw, lw=lw),
        grid=(1,),
        in_specs=[full((n_st, bsz, d)), full((t_new, bsz, d)), full((cw, d)), full((1, d)), full((1, d)),
                  full((1, d))],
        out_specs=[full((t_new, bsz, d)), full((n_st, bsz, d))],
        out_shape=[jax.ShapeDtypeStruct((t_new, bsz, d), BF16),
                   jax.ShapeDtypeStruct((n_st, bsz, d), F32)],
        compiler_params=_cparams("arbitrary"),
    )(st_tm, u_tm, dw, dwb.reshape(1, d), lg.reshape(1, d), lb.reshape(1, d))


def _ffn_up_prompt_kernel(x_ref, wg_ref, wv_ref, dw_ref, dwb_ref, a_ref, st_ref, wg_s, wv_s, carry_s,
                          *, tiles_per_seq):
    i = pl.program_id(1)

    @pl.when(i == 0)
    def _():
        wg_s[...] = wg_ref[...].astype(BF16)
        wv_s[...] = wv_ref[...].astype(BF16)

    @pl.when(i % tiles_per_seq == 0)
    def _():
        carry_s[...] = jnp.zeros(carry_s.shape, F32)

    x = x_ref[...]
    g = _dot(x, wg_s[...])
    v = _dot(x, wv_s[...])
    tm = g.shape[0]
    row = lax.broadcasted_iota(jnp.int32, g.shape, 0)
    prev1 = carry_s[V7X_SUBLANES - 1:V7X_SUBLANES, :]
    prev2 = carry_s[V7X_SUBLANES - 2:V7X_SUBLANES - 1, :]
    g1 = jnp.where(row == 0, prev1, pltpu.roll(g, 1, 0))
    g2 = jnp.where(row == 0, prev2, jnp.where(row == 1, prev1, pltpu.roll(g, 2, 0)))
    gc = dw_ref[0:1, :] * g2 + dw_ref[1:2, :] * g1 + dw_ref[2:3, :] * g + dwb_ref[...]
    a_ref[...] = (_silu(gc) * v).astype(a_ref.dtype)
    carry_s[...] = g[tm - V7X_SUBLANES:tm, :]
    st_ref[...] = g[tm - 2:tm, :]


def _ffn_up_prompt(x, w_up, dw, dwb, layer, bsz, *, tm=1024, tn=512):
    m, k = x.shape
    n = w_up.shape[2] // 2
    t = m // bsz
    tm, tn = min(tm, t), min(tn, n)
    assert t % tm == 0 and n % tn == 0 and dw.shape[1] == 3
    nj = n // tn
    tiles_per_seq = t // tm
    return pl.pallas_call(
        functools.partial(_ffn_up_prompt_kernel, tiles_per_seq=tiles_per_seq),
        grid=(nj, m // tm),
        in_specs=[pl.BlockSpec((tm, k), lambda j, i: (i, 0)),
                  pl.BlockSpec((None, k, tn), lambda j, i: (layer, 0, j)),
                  pl.BlockSpec((None, k, tn), lambda j, i: (layer, 0, nj + j)),
                  pl.BlockSpec((None, 3, tn), lambda j, i: (layer, 0, j)),
                  pl.BlockSpec((None, 1, tn), lambda j, i: (layer, 0, j))],
        out_specs=[pl.BlockSpec((tm, tn), lambda j, i: (i, j)),
                   pl.BlockSpec((None, 2, tn), lambda j, i: (i // tiles_per_seq, 0, j))],
        out_shape=[jax.ShapeDtypeStruct((m, n), BF16),
                   jax.ShapeDtypeStruct((bsz, 2, n), F32)],
        scratch_shapes=[pltpu.VMEM((k, tn), BF16), pltpu.VMEM((k, tn), BF16),
                        pltpu.VMEM((V7X_SUBLANES, tn), F32)],
        compiler_params=_cparams("arbitrary", "arbitrary"),
    )(x, w_up, w_up, dw, dwb.reshape(dwb.shape[0], 1, n))


def _ffn_up_sample_kernel(x_ref, wg_ref, wv_ref, dw_ref, dwb_ref, prev_ref, a_ref, st_ref, *, bsz):
    x = x_ref[...]
    g = _dot(x, wg_ref[...].astype(BF16))
    v = _dot(x, wv_ref[...].astype(BF16))
    rows = g.shape[0]
    full = jnp.concatenate([prev_ref[...], g], axis=0)
    gc = (dw_ref[0:1, :] * full[0:rows] + dw_ref[1:2, :] * full[bsz:bsz + rows]
          + dw_ref[2:3, :] * full[2 * bsz:2 * bsz + rows] + dwb_ref[...])
    a_ref[...] = (_silu(gc) * v).astype(a_ref.dtype)
    st_ref[...] = full[rows:rows + 2 * bsz]


def _ffn_up_sample(x, w_up, dw, dwb, prev_tm, layer, bsz, *, tn=512):
    m, k = x.shape
    n = w_up.shape[2] // 2
    tn = min(tn, n)
    assert n % tn == 0 and m >= 2 * bsz and dw.shape[1] == 3
    nj = n // tn
    return pl.pallas_call(
        functools.partial(_ffn_up_sample_kernel, bsz=bsz),
        grid=(nj,),
        in_specs=[pl.BlockSpec((m, k), lambda j: (0, 0)),
                  pl.BlockSpec((None, k, tn), lambda j: (layer, 0, j)),
                  pl.BlockSpec((None, k, tn), lambda j: (layer, 0, nj + j)),
                  pl.BlockSpec((None, 3, tn), lambda j: (layer, 0, j)),
                  pl.BlockSpec((None, 1, tn), lambda j: (layer, 0, j)),
                  pl.BlockSpec((2 * bsz, tn), lambda j: (0, j))],
        out_specs=[pl.BlockSpec((m, tn), lambda j: (0, j)),
                   pl.BlockSpec((2 * bsz, tn), lambda j: (0, j))],
        out_shape=[jax.ShapeDtypeStruct((m, n), BF16),
                   jax.ShapeDtypeStruct((2 * bsz, n), F32)],
        compiler_params=_cparams("arbitrary"),
    )(x, w_up, w_up, dw, dwb.reshape(dwb.shape[0], 1, n), prev_tm)


def _swa_prompt_kernel(sink_ref, q_ref, kc_ref, kp_ref, vc_ref, vp_ref, bias_ref, o_ref, *, n_heads, group, hd,
                       scale):
    n = pl.program_id(1)
    w = q_ref.shape[0]
    kcat = jnp.concatenate([kp_ref[...], kc_ref[...]], axis=0).astype(BF16)
    vcat = jnp.concatenate([vp_ref[...], vc_ref[...]], axis=0).astype(BF16)
    col = lax.broadcasted_iota(jnp.int32, (w, 2 * w), 1)
    keep = (col >= w) | (n > 0)
    for h in range(n_heads):
        kv = h // group
        q = q_ref[:, h * hd:(h + 1) * hd].astype(BF16)
        s = _dot_nt(q, kcat[:, kv * hd:(kv + 1) * hd]) * scale + bias_ref[h]
        s = jnp.where(keep, s, NEG)
        sink = sink_ref[h]
        m = jnp.maximum(jnp.max(s, axis=-1, keepdims=True), sink)
        p = jnp.exp(s - m)
        l = jnp.sum(p, axis=-1, keepdims=True) + jnp.exp(sink - m)
        o = _dot(p.astype(BF16), vcat[:, kv * hd:(kv + 1) * hd]) / l
        o_ref[:, h * hd:(h + 1) * hd] = o.astype(o_ref.dtype)


def _swa_prompt(qkv, sinks, bias, bsz, n_heads, n_kv, hd, window):
    m = qkv.shape[0]
    t = m // bsz
    nb = t // window
    qw, kw = n_heads * hd, n_kv * hd
    assert qw % kw == 0 and t % window == 0
    kcol, vcol = qw // kw, qw // kw + 1
    return pl.pallas_call(
        functools.partial(_swa_prompt_kernel, n_heads=n_heads, group=n_heads // n_kv, hd=hd, scale=hd ** -0.5),
        grid=(bsz, nb),
        in_specs=[pl.BlockSpec(memory_space=pltpu.SMEM),
                  pl.BlockSpec((window, qw), lambda b, n: (b * nb + n, 0)),
                  pl.BlockSpec((window, kw), lambda b, n: (b * nb + n, kcol)),
                  pl.BlockSpec((window, kw), lambda b, n: (b * nb + jnp.maximum(n - 1, 0), kcol)),
                  pl.BlockSpec((window, kw), lambda b, n: (b * nb + n, vcol)),
                  pl.BlockSpec((window, kw), lambda b, n: (b * nb + jnp.maximum(n - 1, 0), vcol)),
                  pl.BlockSpec((n_heads, window, 2 * window), lambda b, n: (0, 0, 0))],
        out_specs=pl.BlockSpec((window, qw), lambda b, n: (b * nb + n, 0)),
        out_shape=jax.ShapeDtypeStruct((m, qw), BF16),
        compiler_params=_cparams("arbitrary", "arbitrary"),
    )(sinks, qkv, qkv, qkv, qkv, qkv, bias)


def _swa_sample_kernel(sink_ref, qkv_ref, bk_ref, bv_ref, bias_ref, o_ref, *, n_heads, group, hd, scale, pad):
    w = bk_ref.shape[0]
    tp = qkv_ref.shape[0]
    qw = n_heads * hd
    kw = (n_heads // group) * hd
    zeros = jnp.zeros((pad - tp, kw), F32)
    kcat = jnp.concatenate([bk_ref[...], qkv_ref[:, qw:qw + kw], zeros], axis=0).astype(BF16)
    vcat = jnp.concatenate([bv_ref[...], qkv_ref[:, qw + kw:qw + 2 * kw], zeros], axis=0).astype(BF16)
    del w
    for h in range(n_heads):
        kv = h // group
        q = qkv_ref[:, h * hd:(h + 1) * hd].astype(BF16)
        s = _dot_nt(q, kcat[:, kv * hd:(kv + 1) * hd]) * scale + bias_ref[h]
        sink = sink_ref[h]
        m = jnp.maximum(jnp.max(s, axis=-1, keepdims=True), sink)
        p = jnp.exp(s - m)
        l = jnp.sum(p, axis=-1, keepdims=True) + jnp.exp(sink - m)
        o = _dot(p.astype(BF16), vcat[:, kv * hd:(kv + 1) * hd]) / l
        o_ref[:, h * hd:(h + 1) * hd] = o.astype(o_ref.dtype)


def _swa_sample(qkv_bm, buf_k, buf_v, sinks, bias, n_heads, n_kv, hd):
    dbs, tp, width = qkv_bm.shape
    window = buf_k.shape[1]
    qw, kw = n_heads * hd, n_kv * hd
    pad = bias.shape[2] - window
    return pl.pallas_call(
        functools.partial(_swa_sample_kernel, n_heads=n_heads, group=n_heads // n_kv, hd=hd, scale=hd ** -0.5,
                          pad=pad),
        grid=(dbs,),
        in_specs=[pl.BlockSpec(memory_space=pltpu.SMEM),
                  pl.BlockSpec((None, tp, width), lambda b: (b, 0, 0)),
                  pl.BlockSpec((None, window, kw), lambda b: (b, 0, 0)),
                  pl.BlockSpec((None, window, kw), lambda b: (b, 0, 0)),
                  pl.BlockSpec(bias.shape, lambda b: (0, 0, 0))],
        out_specs=pl.BlockSpec((None, tp, qw), lambda b: (b, 0, 0)),
        out_shape=jax.ShapeDtypeStruct((dbs, tp, qw), BF16),
        compiler_params=_cparams("arbitrary"),
    )(sinks, qkv_bm, buf_k, buf_v, bias)


def _block_means_kernel(k_ref, o_ref):
    o_ref[...] = jnp.mean(k_ref[...], axis=0, keepdims=True)


def _block_means(qkv, n_blocks_total, kcol, kw):
    return pl.pallas_call(
        _block_means_kernel,
        grid=(n_blocks_total,),
        in_specs=[pl.BlockSpec((MOBA_BLOCK, kw), lambda i: (i, kcol))],
        out_specs=pl.BlockSpec((None, 1, kw), lambda i: (i, 0, 0)),
        out_shape=jax.ShapeDtypeStruct((n_blocks_total, 1, kw), F32),
        compiler_params=_cparams("arbitrary"),
    )(qkv)


def _rank_rows(gate, blk, n_rows):
    rank = jnp.zeros(gate.shape, jnp.int32)
    for m in range(n_rows):
        gm = gate[m:m + 1, :]
        beats = (gm > gate) | ((gm == gate) & (blk > m))
        rank = rank + jnp.where(beats, 1, 0)
    return rank


def _moba_prompt_kernel(q_ref, k_ref, v_ref, means_ref, bias_ref, far_ref, o_ref, qs, sel_s, m_s, l_s, acc_s,
                        *, n_kv, group, hd, scale, n_near, nb):
    qb = pl.program_id(1)
    kb = pl.program_id(2)
    blk_rows = q_ref.shape[0]
    cols = group * blk_rows

    @pl.when(kb == 0)
    def _():
        blk = lax.broadcasted_iota(jnp.int32, (nb, cols), 0)
        for kv in range(n_kv):
            qg = jnp.concatenate([q_ref[:, (kv * group + g) * hd:(kv * group + g + 1) * hd]
                                  for g in range(group)], axis=0)
            qs[kv] = qg.astype(BF16)
            gate = _dot_nt(means_ref[:, kv * hd:(kv + 1) * hd], qg, precision=lax.Precision.HIGHEST)
            valid = blk < qb
            gate = jnp.where(valid, gate, -jnp.inf)
            rank = _rank_rows(gate, blk, nb)
            sel_s[kv] = jnp.where(valid & (rank < MOBA_TOPK), 1.0, 0.0)
            m_s[kv] = jnp.full((1, cols), NEG, F32)
            l_s[kv] = jnp.zeros((1, cols), F32)
            acc_s[kv] = jnp.zeros((hd, cols), F32)

    def attend(kv, add):
        k = k_ref[:, kv * hd:(kv + 1) * hd].astype(BF16)
        vt = v_ref[:, kv * hd:(kv + 1) * hd].T.astype(BF16)
        s = _dot_nt(k, qs[kv]) * scale + add
        m_prev = m_s[kv]
        m_new = jnp.maximum(m_prev, jnp.max(s, axis=0, keepdims=True))
        alpha = jnp.exp(m_prev - m_new)
        p = jnp.exp(s - m_new)
        l_s[kv] = alpha * l_s[kv] + jnp.sum(p, axis=0, keepdims=True)
        acc_s[kv] = alpha * acc_s[kv] + _dot(vt, p.astype(BF16))
        m_s[kv] = m_new

    delta = qb - kb

    @pl.when((kb <= qb) & (delta < n_near))
    def _():
        for kv in range(n_kv):
            chosen = (sel_s[kv, pl.ds(kb, 1), :] > 0.5) | (delta == 0)
            attend(kv, bias_ref[kv] + jnp.where(chosen, 0.0, NEG))

    @pl.when(delta >= n_near)
    def _():
        for kv in range(n_kv):
            chosen = sel_s[kv, pl.ds(kb, 1), :] > 0.5
            attend(kv, jnp.where(chosen, far_ref[kv], NEG))

    @pl.when(kb == qb)
    def _():
        for kv in range(n_kv):
            o = acc_s[kv] / l_s[kv]
            for g in range(group):
                h = kv * group + g
                o_ref[:, h * hd:(h + 1) * hd] = o[:, g * blk_rows:(g + 1) * blk_rows].T.astype(o_ref.dtype)


def _moba_prompt(qkv, means, bias_t, far_rows, bsz, n_heads, n_kv, hd):
    m = qkv.shape[0]
    t = m // bsz
    nb = t // MOBA_BLOCK
    group = n_heads // n_kv
    qw, kw = n_heads * hd, n_kv * hd
    kcol, vcol = qw // kw, qw // kw + 1
    n_near = bias_t.shape[0]
    cols = group * MOBA_BLOCK
    kv_idx = lambda b, qb, kb: b * nb + jnp.minimum(kb, qb)
    return pl.pallas_call(
        functools.partial(_moba_prompt_kernel, n_kv=n_kv, group=group, hd=hd, scale=hd ** -0.5,
                          n_near=n_near, nb=nb),
        grid=(bsz, nb, nb),
        in_specs=[pl.BlockSpec((MOBA_BLOCK, qw), lambda b, qb, kb: (b * nb + qb, 0)),
                  pl.BlockSpec((MOBA_BLOCK, kw), lambda b, qb, kb: (kv_idx(b, qb, kb), kcol)),
                  pl.BlockSpec((MOBA_BLOCK, kw), lambda b, qb, kb: (kv_idx(b, qb, kb), vcol)),
                  pl.BlockSpec((None, nb, kw), lambda b, qb, kb: (b, 0, 0)),
                  pl.BlockSpec((None, n_kv, MOBA_BLOCK, cols),
                               lambda b, qb, kb: (jnp.clip(qb - kb, 0, n_near - 1), 0, 0, 0)),
                  pl.BlockSpec((n_kv, 1, cols), lambda b, qb, kb: (0, 0, 0))],
        out_specs=pl.BlockSpec((MOBA_BLOCK, qw), lambda b, qb, kb: (b * nb + qb, 0)),
        out_shape=jax.ShapeDtypeStruct((m, qw), BF16),
        scratch_shapes=[pltpu.VMEM((n_kv, cols, hd), BF16),
                        pltpu.VMEM((n_kv, nb, cols), F32),
                        pltpu.VMEM((n_kv, 1, cols), F32),
                        pltpu.VMEM((n_kv, 1, cols), F32),
                        pltpu.VMEM((n_kv, hd, cols), F32)],
        compiler_params=_cparams("arbitrary", "arbitrary", "arbitrary"),
    )(qkv, qkv, qkv, means, bias_t, far_rows)


_MEAN_PAGES_PER_STEP = 16
_ATTN_PAGES_PER_STEP = 8


def _pool_means_kernel(pt_ref, *refs, ppb, n_kv):
    del pt_ref
    o_ref = refs[-1]
    pages = refs[:-1]
    hd = pages[0].shape[1]
    sub = V7X_SUBLANES
    reps = sub // n_kv
    folded = []
    for blk in range(len(pages) // ppb):
        tot = None
        for p in range(ppb):
            part = jnp.sum(pages[blk * ppb + p][...].reshape(-1, sub, hd), axis=0)
            tot = part if tot is None else tot + part
        full = tot
        for i in range(1, reps):
            full = full + pltpu.roll(tot, i * n_kv, 0)
        folded.append(full)
    row_grp = lax.broadcasted_iota(jnp.int32, (sub, hd), 0) // n_kv
    tiles = []
    for a in range(0, len(folded), reps):
        tile = folded[a]
        for i in range(1, reps):
            tile = jnp.where(row_grp == i, folded[a + i], tile)
        tiles.append(tile)
    n_keys = ppb * pages[0].shape[0] // n_kv
    o_ref[...] = jnp.concatenate(tiles, axis=0) * (1.0 / n_keys)


def _pool_means(pool, pt_flat, page_off, dbs, n_pages, n_kv):
    rows, hd = pool.shape[1], pool.shape[2]
    ppb = MOBA_BLOCK // (rows // n_kv)
    pps = min(_MEAN_PAGES_PER_STEP, n_pages)
    blocks = pps // ppb
    assert n_pages % pps == 0 and pps % ppb == 0 and V7X_SUBLANES % n_kv == 0
    assert (blocks * n_kv) % V7X_SUBLANES == 0

    def page_spec(i):
        return pl.BlockSpec((None, rows, hd),
                            lambda b, s, pt: (pt[b * n_pages + s * pps + i] + page_off, 0, 0))

    grid_spec = pltpu.PrefetchScalarGridSpec(
        num_scalar_prefetch=1,
        grid=(dbs, n_pages // pps),
        in_specs=[page_spec(i) for i in range(pps)],
        out_specs=pl.BlockSpec((None, blocks * n_kv, hd), lambda b, s, pt: (b, s, 0)),
    )
    return pl.pallas_call(
        functools.partial(_pool_means_kernel, ppb=ppb, n_kv=n_kv),
        grid_spec=grid_spec,
        out_shape=jax.ShapeDtypeStruct((dbs, (n_pages // ppb) * n_kv, hd), F32),
        compiler_params=_cparams("arbitrary", "arbitrary"),
    )(pt_flat, *([pool] * pps))


def _cols_expand_kernel(tab_ref, idx_ref, o_ref, *, buckets):
    idx = idx_ref[...]
    acc = jnp.full(idx.shape, NEG, F32)
    for b in buckets:
        acc = jnp.where(idx == b, tab_ref[b:b + 1, :], acc)
    o_ref[...] = acc


def _cols_expand(tab_cols, idx_np, tile_rows):
    rows, cols = idx_np.shape
    buckets = tuple(int(b) for b in np.unique(idx_np) if b >= 0)
    return pl.pallas_call(
        functools.partial(_cols_expand_kernel, buckets=buckets),
        grid=(rows // tile_rows,),
        in_specs=[pl.BlockSpec(tab_cols.shape, lambda i: (0, 0)),
                  pl.BlockSpec((tile_rows, cols), lambda i: (i, 0))],
        out_specs=pl.BlockSpec((tile_rows, cols), lambda i: (i, 0)),
        out_shape=jax.ShapeDtypeStruct((rows, cols), F32),
        compiler_params=_cparams("arbitrary"),
    )(tab_cols, jnp.asarray(idx_np))


def _moba_sample_kernel(pt_ref, *refs, pps, ppb, n_kv, group, tp, n_blk, n_pages, far_cls, scale):
    del pt_ref
    k_pages = refs[:pps]
    v_pages = refs[pps:2 * pps]
    q_ref, means_ref, bias_ref, own_ref, kn_ref, vn_ref, o_ref, sel_s, m_s, l_s, acc_s = refs[2 * pps:]
    step = pl.program_id(1)
    cols, hd = q_ref.shape

    @pl.when(step == 0)
    def _():
        q = q_ref[...]
        blk = lax.broadcasted_iota(jnp.int32, (n_blk, cols), 0)
        col_kv = lax.broadcasted_iota(jnp.int32, (n_blk, cols), 1) // (group * tp)
        gate = jnp.zeros((n_blk, cols), F32)
        for kv in range(n_kv):
            g = _dot_nt(means_ref[kv * n_blk:(kv + 1) * n_blk, :], q, precision=lax.Precision.HIGHEST)
            gate = jnp.where(col_kv == kv, g, gate)
        rank = _rank_rows(gate, blk, n_blk)
        sel_s[...] = jnp.where(rank < MOBA_TOPK, 0.0, NEG)
        m_s[...] = jnp.full(m_s.shape, NEG, F32)
        l_s[...] = jnp.zeros(l_s.shape, F32)
        acc_s[...] = jnp.zeros(acc_s.shape, F32)

    qb = q_ref[...].astype(BF16)

    def attend(keys, values, adds):
        scores = [_dot_nt(k.astype(BF16), qb) * scale + a for k, a in zip(keys, adds)]
        m_prev = m_s[...]
        m_new = m_prev
        for s in scores:
            m_new = jnp.maximum(m_new, jnp.max(s, axis=0, keepdims=True))
        alpha = jnp.exp(m_prev - m_new)
        l_new = alpha * l_s[...]
        acc = alpha * acc_s[...]
        for s, v in zip(scores, values):
            p = jnp.exp(s - m_new)
            l_new = l_new + jnp.sum(p, axis=0, keepdims=True)
            acc = acc + _dot(v.T.astype(BF16), p.astype(BF16))
        m_s[...] = m_new
        l_s[...] = l_new
        acc_s[...] = acc

    adds = []
    for p in range(pps):
        page = step * pps + p
        cls = jnp.minimum(n_pages - 1 - page, far_cls)
        adds.append(bias_ref[cls] + sel_s[pl.ds(page // ppb, 1), :])
    attend([r[...] for r in k_pages], [r[...] for r in v_pages], adds)

    @pl.when(step == pl.num_programs(1) - 1)
    def _():
        attend([kn_ref[...]], [vn_ref[...]], [own_ref[...]])
        o_ref[...] = (acc_s[...] / l_s[...]).T


def _moba_sample(q_cols, means_t, k_new, v_new, pool_k, pool_v, pt_flat, page_off, bias_pages, bias_own, *,
                 n_kv, group, tp, n_pages):
    dbs, cols, hd = q_cols.shape
    rows = pool_k.shape[1]
    ppb = MOBA_BLOCK // (rows // n_kv)
    n_blk = n_pages // ppb
    pps = min(_ATTN_PAGES_PER_STEP, n_pages)
    assert n_pages % pps == 0
    far_cls = bias_pages.shape[0] - 1
    n_own = k_new.shape[1]

    def page_spec(i):
        return pl.BlockSpec((None, rows, hd),
                            lambda b, s, pt: (pt[b * n_pages + s * pps + i] + page_off, 0, 0))

    per_seq = lambda r: pl.BlockSpec((None, r, hd), lambda b, s, pt: (b, 0, 0))
    in_specs = ([page_spec(i) for i in range(pps)] * 2
                + [per_seq(cols), per_seq(n_kv * n_blk),
                   pl.BlockSpec(bias_pages.shape, lambda b, s, pt: (0, 0, 0)),
                   pl.BlockSpec(bias_own.shape, lambda b, s, pt: (0, 0)),
                   per_seq(n_own), per_seq(n_own)])
    grid_spec = pltpu.PrefetchScalarGridSpec(
        num_scalar_prefetch=1,
        grid=(dbs, n_pages // pps),
        in_specs=in_specs,
        out_specs=pl.BlockSpec((None, cols, hd), lambda b, s, pt: (b, 0, 0)),
        scratch_shapes=[pltpu.VMEM((n_blk, cols), F32), pltpu.VMEM((1, cols), F32),
                        pltpu.VMEM((1, cols), F32), pltpu.VMEM((hd, cols), F32)],
    )
    return pl.pallas_call(
        functools.partial(_moba_sample_kernel, pps=pps, ppb=ppb, n_kv=n_kv, group=group, tp=tp, n_blk=n_blk,
                          n_pages=n_pages, far_cls=far_cls, scale=hd ** -0.5),
        grid_spec=grid_spec,
        out_shape=jax.ShapeDtypeStruct((dbs, cols, hd), F32),
        compiler_params=_cparams("arbitrary", "arbitrary"),
    )(pt_flat, *([pool_k] * pps), *([pool_v] * pps), q_cols, means_t, bias_pages, bias_own, k_new, v_new)


def _swa_prompt_bias_idx(window, nbk):
    qi = np.arange(window)[:, None]
    kj = np.arange(2 * window)[None, :]
    dist = qi + window - kj
    return np.where((dist >= 0) & (dist < window), _bucket_np(dist, nbk), -1).astype(np.int32)


def _swa_sample_bias_idx(window, t_new, tp, pad, nbk):
    idx = np.full((tp, window + pad), -1, np.int32)
    for t in range(t_new):
        for j in range(window + t_new):
            dist = t + window - j
            if 0 <= dist < window:
                idx[t, j] = _bucket_np(np.array(dist), nbk)
    idx[t_new:, :] = 0
    return idx


def _moba_prompt_bias_idx(n_near, nbk):
    j = np.arange(MOBA_BLOCK)[:, None]
    qi = np.arange(MOBA_BLOCK)[None, :]
    tiles = []
    for delta in range(n_near):
        dist = delta * MOBA_BLOCK + qi - j
        tiles.append(np.where(dist >= 0, _bucket_np(dist, nbk), -1))
    return np.concatenate(tiles, axis=0).astype(np.int32)


def _moba_sample_page_idx(far_cls, page, n_kv, n_heads, group, tp, t_new, nbk):
    cols = n_heads * tp
    h, t = np.arange(cols) // tp, np.arange(cols) % tp
    r, kvc = np.arange(page * n_kv) // n_kv, np.arange(page * n_kv) % n_kv
    match = kvc[:, None] == (h // group)[None, :]
    tiles = []
    for c in range(far_cls + 1):
        dist = (c + 1) * page + t[None, :] - r[:, None]
        b = _bucket_np(dist, nbk) if c < far_cls else np.full(dist.shape, nbk - 1, np.int32)
        b = np.where(t[None, :] < t_new, b, 0)
        tiles.append(np.where(match, b, -1))
    return np.concatenate(tiles, axis=0).astype(np.int32)


def _moba_sample_own_idx(n_kv, n_heads, group, tp, t_new, nbk):
    cols = n_heads * tp
    h, t = np.arange(cols) // tp, np.arange(cols) % tp
    tk, kvc = np.arange(tp * n_kv) // n_kv, np.arange(tp * n_kv) % n_kv
    ok = ((kvc[:, None] == (h // group)[None, :]) & (tk[:, None] <= t[None, :]) & (t[None, :] < t_new))
    return np.where(ok, _bucket_np(t[None, :] - tk[:, None], nbk), -1).astype(np.int32)


def kernel(x_prompt, x_sample, c_prompt, c_sample, state_conv, cache_swa_k, cache_swa_v, cache_moba_k, cache_moba_v, page_table, state_ffn, ada_w, ada_b, norm_mix, norm_ffn, norm_final, rel_bias, conv_w1, conv_b1, conv_dw, conv_dw_b, conv_ln_g, conv_ln_b, conv_w2, conv_b2, swa_wqkv, swa_wo, swa_sinks, moba_wqkv, moba_wo, ffn_w_up, ffn_dw, ffn_dw_b, ffn_w_down):
    bsz, seq, d = x_prompt.shape
    dbs, t_new, _ = x_sample.shape
    depth = ada_w.shape[0]
    window, n_kv, hd = cache_swa_k.shape[2], cache_swa_k.shape[3], cache_swa_k.shape[4]
    nbk, n_heads = rel_bias.shape
    group = n_heads // n_kv
    qw, kw = n_heads * hd, n_kv * hd
    cw = conv_dw.shape[1]
    d_ff = ffn_dw.shape[2]
    n_pool, page_size = cache_moba_k.shape[1], cache_moba_k.shape[2]
    n_pages = page_table.shape[1]
    ppb = MOBA_BLOCK // page_size
    n_blk_past = n_pages // ppb
    tp = V7X_SUBLANES
    assert t_new <= tp and n_pages % ppb == 0 and n_blk_past >= MOBA_TOPK and seq % MOBA_BLOCK == 0
    n_mixers = 3

    n_c = bsz + dbs
    c_rows = -(-n_c // V7X_SUBLANES) * V7X_SUBLANES
    c_all = jnp.concatenate([c_prompt, c_sample, jnp.zeros((c_rows - n_c, d), F32)], axis=0)
    mod = _ada(c_all, ada_w, ada_b)

    def mods(layer, which):
        chunk = mod[layer, :, which * d:(which + 1) * d]
        return chunk[:bsz], chunk[bsz:bsz + dbs]

    d_sat = _saturation_distance(nbk)
    n_near = -(-(d_sat + MOBA_BLOCK - 1) // MOBA_BLOCK)
    swa_bias_p = swa_bias_s = moba_bias_p = moba_far = moba_bias_sel = moba_bias_own = None
    if depth > 1:
        swa_bias_p = _bias_expand(rel_bias, _swa_prompt_bias_idx(window, nbk), window,
                                  (n_heads, window, 2 * window), (None, window, 2 * window),
                                  lambda h: (h, 0, 0))
        pad_s = V7X_LANES
        swa_bias_s = _bias_expand(rel_bias, _swa_sample_bias_idx(window, t_new, tp, pad_s, nbk), tp,
                                  (n_heads, tp, window + pad_s), (None, tp, window + pad_s),
                                  lambda h: (h, 0, 0))
    if depth > 2:
        cols = group * MOBA_BLOCK
        moba_bias_p = _bias_expand(
            rel_bias, _moba_prompt_bias_idx(n_near, nbk), MOBA_BLOCK,
            (n_near * MOBA_BLOCK, n_kv * cols), (n_near * MOBA_BLOCK, MOBA_BLOCK),
            lambda h: (0, h)).reshape(n_near, MOBA_BLOCK, n_kv, cols).transpose(0, 2, 1, 3)
        moba_far = jnp.repeat(rel_bias[nbk - 1].reshape(n_kv, group, 1), MOBA_BLOCK, axis=2).reshape(n_kv, 1, cols)
        far_cls = -(-(d_sat + page_size - 1) // page_size) - 1
        tab_cols = jnp.repeat(rel_bias, tp, axis=1)
        page_rows = page_size * n_kv
        moba_bias_sel = _cols_expand(
            tab_cols, _moba_sample_page_idx(far_cls, page_size, n_kv, n_heads, group, tp, t_new, nbk),
            page_rows).reshape(far_cls + 1, page_rows, n_heads * tp)
        moba_bias_own = _cols_expand(
            tab_cols, _moba_sample_own_idx(n_kv, n_heads, group, tp, t_new, nbk), tp * n_kv)

    def to_tm(a):
        return jnp.swapaxes(a, 0, 1).reshape((a.shape[1] * dbs,) + a.shape[2:])

    def to_bm(a, t):
        return jnp.swapaxes(a.reshape((t, dbs) + a.shape[1:]), 0, 1)

    xp = x_prompt.reshape(bsz * seq, d)
    xs = to_tm(x_sample)
    m_s = t_new * dbs

    conv_p, conv_s, swa_k_p, swa_v_p, swa_k_s, swa_v_s = [], [], [], [], [], []
    moba_k_p, moba_v_p, moba_k_s, moba_v_s, ffn_p, ffn_s = [], [], [], [], [], []

    def gates(layer, which):
        gp, gs = mods(layer, which)
        return gp.reshape(bsz, 1, d), jnp.tile(gs, (t_new, 1)).reshape(1, m_s, d)

    def normed(x_p, x_s, g, layer, which_sh):
        shp, shs = mods(layer, which_sh)
        scp, scs = mods(layer, which_sh + 1)
        hp = _norm_mod(x_p.reshape(bsz, seq, d), g, scp.reshape(bsz, 1, d), shp.reshape(bsz, 1, d), 512)
        hs = _norm_mod(x_s.reshape(t_new, dbs, d), g, scs.reshape(1, dbs, d), shs.reshape(1, dbs, d), dbs)
        return hp.reshape(bsz * seq, d), hs.reshape(m_s, d)

    def pad_time(a_bm):
        return jnp.concatenate([a_bm, jnp.zeros((dbs, tp - t_new, a_bm.shape[2]), a_bm.dtype)], axis=1)

    for layer in range(depth):
        kind, j = layer % n_mixers, layer // n_mixers
        hp, hs = normed(xp, xs, norm_mix[layer], layer, 0)
        g1p, g1s = gates(layer, 2)
        if kind == 0:
            up = _glu(hp, conv_w1, conv_b1, j)
            us = _glu(hs, conv_w1, conv_b1, j)
            yp = _conv_ln_prompt(up.reshape(bsz, seq, d), conv_dw[j], conv_dw_b[j], conv_ln_g[j], conv_ln_b[j])
            st_tm = jnp.swapaxes(state_conv[j], 0, 1)
            ys, nst_tm = _conv_ln_sample(st_tm, us.reshape(t_new, dbs, d), conv_dw[j], conv_dw_b[j],
                                         conv_ln_g[j], conv_ln_b[j])
            conv_p.append(up.reshape(bsz, seq, d)[:, seq - (cw - 1):])
            conv_s.append(jnp.swapaxes(nst_tm, 0, 1))
            xp = _mm(yp.reshape(bsz * seq, d), conv_w2, j, bias=conv_b2, res=xp, gate=g1p)
            xs = _mm(ys.reshape(m_s, d), conv_w2, j, bias=conv_b2, res=xs, gate=g1s)
        elif kind == 1:
            qkv_p = _mm(hp, swa_wqkv, j)
            qkv_s = _mm(hs, swa_wqkv, j)
            op = _swa_prompt(qkv_p, swa_sinks[j], swa_bias_p, bsz, n_heads, n_kv, hd, window)
            qkv_bm = to_bm(qkv_s, t_new)
            buf_k = cache_swa_k[j].reshape(dbs, window, kw)
            buf_v = cache_swa_v[j].reshape(dbs, window, kw)
            os_bm = _swa_sample(pad_time(qkv_bm), buf_k, buf_v, swa_sinks[j], swa_bias_s, n_heads, n_kv, hd)
            os_ = to_tm(os_bm[:, :t_new])
            k_p = qkv_p[:, qw:qw + kw].reshape(bsz, seq, n_kv, hd)
            v_p = qkv_p[:, qw + kw:].reshape(bsz, seq, n_kv, hd)
            swa_k_p.append(k_p[:, seq - window:])
            swa_v_p.append(v_p[:, seq - window:])
            k_s = qkv_bm[:, :, qw:qw + kw]
            v_s = qkv_bm[:, :, qw + kw:]
            swa_k_s.append(jnp.concatenate([buf_k, k_s], axis=1)[:, t_new:].reshape(dbs, window, n_kv, hd))
            swa_v_s.append(jnp.concatenate([buf_v, v_s], axis=1)[:, t_new:].reshape(dbs, window, n_kv, hd))
            xp = _mm(op, swa_wo, j, res=xp, gate=g1p)
            xs = _mm(os_, swa_wo, j, res=xs, gate=g1s)
        else:
            qkv_p = _mm(hp, moba_wqkv, j)
            qkv_s = _mm(hs, moba_wqkv, j)
            nb = seq // MOBA_BLOCK
            means_p = _block_means(qkv_p, bsz * nb, qw // kw, kw).reshape(bsz, nb, kw)
            op = _moba_prompt(qkv_p, means_p, moba_bias_p, moba_far, bsz, n_heads, n_kv, hd)
            moba_k_p.append(qkv_p[:, qw:qw + kw].reshape(bsz, seq // page_size, page_size, n_kv, hd))
            moba_v_p.append(qkv_p[:, qw + kw:].reshape(bsz, seq // page_size, page_size, n_kv, hd))

            qkv_bm = to_bm(qkv_s, t_new)
            qkv_pad = pad_time(qkv_bm)
            pool_k = cache_moba_k.reshape(cache_moba_k.shape[0] * n_pool, page_size * n_kv, hd)
            pool_v = cache_moba_v.reshape(cache_moba_v.shape[0] * n_pool, page_size * n_kv, hd)
            pt_flat = page_table.reshape(-1)
            means_s = _pool_means(pool_k, pt_flat, j * n_pool, dbs, n_pages, n_kv)
            means_t = means_s.reshape(dbs, n_blk_past, n_kv, hd).transpose(0, 2, 1, 3)
            means_t = means_t.reshape(dbs, n_kv * n_blk_past, hd)
            q_cols = qkv_pad[:, :, :qw].reshape(dbs, tp, n_heads, hd).transpose(0, 2, 1, 3)
            q_cols = q_cols.reshape(dbs, n_heads * tp, hd)
            k_new = qkv_pad[:, :, qw:qw + kw].reshape(dbs, tp * n_kv, hd)
            v_new = qkv_pad[:, :, qw + kw:].reshape(dbs, tp * n_kv, hd)
            o_cols = _moba_sample(q_cols, means_t, k_new, v_new, pool_k, pool_v, pt_flat, j * n_pool,
                                  moba_bias_sel, moba_bias_own, n_kv=n_kv, group=group, tp=tp, n_pages=n_pages)
            os_bm = o_cols.reshape(dbs, n_heads, tp, hd)[:, :, :t_new].transpose(0, 2, 1, 3)
            os_bm = os_bm.reshape(dbs, t_new, qw).astype(BF16)
            moba_k_s.append(qkv_bm[:, :, qw:qw + kw].reshape(dbs, t_new, n_kv, hd))
            moba_v_s.append(qkv_bm[:, :, qw + kw:].reshape(dbs, t_new, n_kv, hd))
            xp = _mm(op, moba_wo, j, res=xp, gate=g1p)
            xs = _mm(to_tm(os_bm), moba_wo, j, res=xs, gate=g1s)

        hp, hs = normed(xp, xs, norm_ffn[layer], layer, 3)
        g2p, g2s = gates(layer, 5)
        ap, stp = _ffn_up_prompt(hp, ffn_w_up, ffn_dw, ffn_dw_b, layer, bsz)
        prev_tm = jnp.swapaxes(state_ffn[layer], 0, 1).reshape(2 * dbs, d_ff)
        as_, sts = _ffn_up_sample(hs, ffn_w_up, ffn_dw, ffn_dw_b, prev_tm, layer, dbs)
        ffn_p.append(stp)
        ffn_s.append(jnp.swapaxes(sts.reshape(2, dbs, d_ff), 0, 1))
        xp = _mm(ap, ffn_w_down, layer, res=xp, gate=g2p, tn=256)
        xs = _mm(as_, ffn_w_down, layer, res=xs, gate=g2s, tn=256)

    y_p = _norm_mod(xp.reshape(bsz, seq, d), norm_final, None, None, 512, out_dtype=F32)
    y_s = _norm_mod(xs.reshape(t_new, dbs, d), norm_final, None, None, dbs, out_dtype=F32)
    y_s = jnp.swapaxes(y_s, 0, 1)

    return (y_p, y_s, jnp.stack(conv_p), jnp.stack(conv_s), jnp.stack(swa_k_p), jnp.stack(swa_v_p),
            jnp.stack(swa_k_s), jnp.stack(swa_v_s), jnp.stack(moba_k_p), jnp.stack(moba_v_p),
            jnp.stack(moba_k_s), jnp.stack(moba_v_s), jnp.stack(ffn_p), jnp.stack(ffn_s))
```

```python
import functools
import math

import numpy as np
import jax
import jax.numpy as jnp
from jax import lax
from jax.experimental import pallas as pl
from jax.experimental.pallas import tpu as pltpu

MOBA_BLOCK = 256
MOBA_TOPK = 3
REL_MAX_DISTANCE = 1024
RMS_EPS = 1e-6
LN_EPS = 1e-5
N_MOD = 6

V7X_LANES = 128
V7X_SUBLANES = 8
V7X_VMEM_LIMIT_BYTES = 56 * 1024 * 1024

NEG = -1e30
LOG2E = 1.4426950408889634
BF16 = jnp.bfloat16
F32 = jnp.float32


def _cparams(*sem):
    return pltpu.CompilerParams(dimension_semantics=sem, vmem_limit_bytes=V7X_VMEM_LIMIT_BYTES)


def _dot(a, b):
    return jnp.dot(a, b, preferred_element_type=F32)


def _dot_nt(a, b, precision=None):
    return lax.dot_general(a, b, (((1,), (1,)), ((), ())), precision=precision,
                           preferred_element_type=F32)


def _silu(x):
    return x * jax.nn.sigmoid(x)


def _bucket_np(dist, num_buckets):
    n = np.maximum(dist, 0)
    max_exact = num_buckets // 2
    nf = np.maximum(n, 1).astype(np.float32)
    ratio = np.log(nf / np.float32(max_exact)) / np.float32(math.log(REL_MAX_DISTANCE / max_exact))
    large = max_exact + (ratio * np.float32(num_buckets - max_exact)).astype(np.int32)
    return np.where(n < max_exact, n, np.minimum(large, num_buckets - 1)).astype(np.int32)


def _saturation_distance(num_buckets):
    d = np.arange(0, 4 * REL_MAX_DISTANCE)
    b = _bucket_np(d, num_buckets)
    below = np.nonzero(b < num_buckets - 1)[0]
    return int(below.max()) + 1


def _bias_expand_kernel(tab_ref, idx_ref, o_ref, *, tiles, mult):
    h = pl.program_id(0)
    for n, (r0, nr, buckets) in enumerate(tiles):
        idx = idx_ref[r0:r0 + nr, :]
        acc = jnp.full(idx.shape, NEG, F32)
        for b in buckets:
            acc = jnp.where(idx == b, tab_ref[b, h] * mult, acc)
        if len(o_ref.shape) == 3:
            o_ref[n] = acc
        else:
            o_ref[r0:r0 + nr, :] = acc


def _bias_expand(table, idx_np, tile_rows, out_shape, out_block, out_index, mult=1.0):
    n_heads = table.shape[1]
    rows, cols = idx_np.shape
    tiles = []
    for r0 in range(0, rows, tile_rows):
        present = np.unique(idx_np[r0:r0 + tile_rows])
        tiles.append((r0, min(tile_rows, rows - r0), tuple(int(b) for b in present if b >= 0)))
    return pl.pallas_call(
        functools.partial(_bias_expand_kernel, tiles=tuple(tiles), mult=mult),
        grid=(n_heads,),
        in_specs=[pl.BlockSpec(memory_space=pltpu.SMEM),
                  pl.BlockSpec((rows, cols), lambda h: (0, 0))],
        out_specs=pl.BlockSpec(out_block, out_index),
        out_shape=jax.ShapeDtypeStruct(out_shape, F32),
        compiler_params=_cparams("arbitrary"),
    )(table, jnp.asarray(idx_np))


def _ada_kernel(c_ref, w_ref, b_ref, o_ref):
    a = _silu(c_ref[...]).astype(BF16)
    o_ref[...] = _dot(a, w_ref[...].astype(BF16)) + b_ref[...]


def _ada(c_all, ada_w, ada_b):
    depth, d, n = ada_w.shape
    rows = c_all.shape[0]
    tn = min(n, 2048)
    return pl.pallas_call(
        _ada_kernel,
        grid=(depth, n // tn),
        in_specs=[pl.BlockSpec((rows, d), lambda l, j: (0, 0)),
                  pl.BlockSpec((None, d, tn), lambda l, j: (l, 0, j)),
                  pl.BlockSpec((None, 1, tn), lambda l, j: (l, 0, j))],
        out_specs=pl.BlockSpec((None, rows, tn), lambda l, j: (l, 0, j)),
        out_shape=jax.ShapeDtypeStruct((depth, rows, n), F32),
        compiler_params=_cparams("arbitrary", "arbitrary"),
    )(c_all, ada_w, ada_b.reshape(depth, 1, n))


def _norm_mod_kernel(x_ref, g_ref, sc_ref, sh_ref, o_ref):
    x = x_ref[...]
    ms = jnp.mean(x * x, axis=-1, keepdims=True)
    y = x * lax.rsqrt(ms + RMS_EPS) * g_ref[...]
    o_ref[...] = (y * (1.0 + sc_ref[...]) + sh_ref[...]).astype(o_ref.dtype)


def _rms_kernel(x_ref, g_ref, o_ref):
    x = x_ref[...]
    ms = jnp.mean(x * x, axis=-1, keepdims=True)
    o_ref[...] = (x * lax.rsqrt(ms + RMS_EPS) * g_ref[...]).astype(o_ref.dtype)


def _norm_mod(x3, g, sc3, sh3, rows_per_step, out_dtype=BF16):
    a, r, d = x3.shape
    tr = min(rows_per_step, r)
    g3 = g.reshape(1, 1, d)
    if sc3 is None:
        return pl.pallas_call(
            _rms_kernel,
            grid=(a, r // tr),
            in_specs=[pl.BlockSpec((1, tr, d), lambda i, j: (i, j, 0)),
                      pl.BlockSpec((1, 1, d), lambda i, j: (0, 0, 0))],
            out_specs=pl.BlockSpec((1, tr, d), lambda i, j: (i, j, 0)),
            out_shape=jax.ShapeDtypeStruct(x3.shape, out_dtype),
            compiler_params=_cparams("arbitrary", "arbitrary"),
        )(x3, g3)
    per_seq = sc3.shape[0] == a and sc3.shape[1] == 1
    if per_seq:
        mspec = pl.BlockSpec((1, 1, d), lambda i, j: (i, 0, 0))
    else:
        assert sc3.shape[0] == 1 and sc3.shape[1] == r and tr == r
        mspec = pl.BlockSpec((1, r, d), lambda i, j: (0, 0, 0))
    return pl.pallas_call(
        _norm_mod_kernel,
        grid=(a, r // tr),
        in_specs=[pl.BlockSpec((1, tr, d), lambda i, j: (i, j, 0)),
                  pl.BlockSpec((1, 1, d), lambda i, j: (0, 0, 0)),
                  mspec, mspec],
        out_specs=pl.BlockSpec((1, tr, d), lambda i, j: (i, j, 0)),
        out_shape=jax.ShapeDtypeStruct(x3.shape, out_dtype),
        compiler_params=_cparams("arbitrary", "arbitrary"),
    )(x3, g3, sc3, sh3)


def _mm_kernel(*refs, has_bias, has_res, cache_w, kv_heads):
    x_ref, w_ref = refs[0], refs[1]
    if cache_w:
        w_s = refs[-1]

        @pl.when(pl.program_id(1) == 0)
        def _():
            w_s[...] = w_ref[...].astype(BF16)
        w = w_s[...]
    else:
        w = w_ref[...].astype(BF16)
    pos = 2
    acc = _dot(x_ref[...], w)
    if has_bias:
        acc = acc + refs[pos][...]
        pos += 1
    if has_res:
        acc = refs[pos][...] + refs[pos + 1][...] * acc
        pos += 2
    o_ref = refs[pos]
    o_ref[...] = acc.astype(o_ref.dtype)
    if kv_heads:
        n_kv, hd = kv_heads
        k_ref, v_ref = refs[pos + 1], refs[pos + 2]
        rows = acc.shape[0]

        @pl.when(pl.program_id(0) == pl.num_programs(0) - 1)
        def _():
            for kv in range(n_kv):
                k_ref[pl.ds(kv, rows, stride=n_kv), :] = acc[:, kv * hd:(kv + 1) * hd]
                v_ref[pl.ds(kv, rows, stride=n_kv), :] = acc[:, (n_kv + kv) * hd:(n_kv + kv + 1) * hd]


def _mm(x, w, layer, *, bias=None, res=None, gate=None, out_dtype=F32, tm=1024, tn=1024, weights_outer=True,
        kv_heads=None):
    m, k = x.shape
    n = w.shape[2]
    tm, tn = min(tm, m), min(tn, n)
    assert m % tm == 0 and n % tn == 0
    if weights_outer:
        grid = (n // tn, m // tm)
        ij = lambda a, b: (b, a)
    else:
        grid = (m // tm, n // tn)
        ij = lambda a, b: (a, b)

    def spec(block, fn):
        return pl.BlockSpec(block, lambda a, b: fn(*ij(a, b)))

    in_specs = [spec((tm, k), lambda i, j: (i, 0)),
                spec((None, k, tn), lambda i, j: (layer, 0, j))]
    args = [x, w]
    if bias is not None:
        in_specs.append(spec((None, 1, tn), lambda i, j: (layer, 0, j)))
        args.append(bias.reshape(bias.shape[0], 1, n))
    if res is not None:
        gs, gr, _ = gate.shape
        assert gr in (1, tm) and (m // tm) % gs == 0
        tiles_per_gate = (m // tm) // gs
        in_specs.append(spec((tm, tn), lambda i, j: (i, j)))
        in_specs.append(spec((None, gr, tn), lambda i, j: (i // tiles_per_gate, 0, j)))
        args += [res, gate]
    out_specs = spec((tm, tn), lambda i, j: (i, j))
    out_shape = jax.ShapeDtypeStruct((m, n), out_dtype)
    if kv_heads:
        n_kv, hd = kv_heads
        nj = n // tn
        assert weights_outer and tn == 2 * n_kv * hd
        kv_spec = spec((tm * n_kv, hd), lambda i, j: (jnp.where(j == nj - 1, i, 0), 0))
        out_specs = [out_specs, kv_spec, kv_spec]
        out_shape = [out_shape] + [jax.ShapeDtypeStruct((m * n_kv, hd), F32)] * 2
    return pl.pallas_call(
        functools.partial(_mm_kernel, has_bias=bias is not None, has_res=res is not None,
                          cache_w=weights_outer, kv_heads=kv_heads),
        grid=grid,
        in_specs=in_specs,
        out_specs=out_specs,
        out_shape=out_shape,
        scratch_shapes=[pltpu.VMEM((k, tn), BF16)] if weights_outer else [],
        compiler_params=_cparams("arbitrary", "arbitrary"),
    )(*args)


def _glu_kernel(x_ref, wa_ref, wg_ref, ba_ref, bg_ref, o_ref, wa_s, wg_s):
    @pl.when(pl.program_id(1) == 0)
    def _():
        wa_s[...] = wa_ref[...].astype(BF16)
        wg_s[...] = wg_ref[...].astype(BF16)

    x = x_ref[...]
    a = _dot(x, wa_s[...]) + ba_ref[...]
    g = _dot(x, wg_s[...]) + bg_ref[...]
    o_ref[...] = a * jax.nn.sigmoid(g)


def _glu(x, w1, b1, layer, *, tm=1024, tn=512):
    m, k = x.shape
    n = w1.shape[2] // 2
    tm, tn = min(tm, m), min(tn, n)
    nj = n // tn
    b3 = b1.reshape(b1.shape[0], 1, 2 * n)
    return pl.pallas_call(
        _glu_kernel,
        grid=(nj, m // tm),
        in_specs=[pl.BlockSpec((tm, k), lambda j, i: (i, 0)),
                  pl.BlockSpec((None, k, tn), lambda j, i: (layer, 0, j)),
                  pl.BlockSpec((None, k, tn), lambda j, i: (layer, 0, nj + j)),
                  pl.BlockSpec((None, 1, tn), lambda j, i: (layer, 0, j)),
                  pl.BlockSpec((None, 1, tn), lambda j, i: (layer, 0, nj + j))],
        out_specs=pl.BlockSpec((tm, tn), lambda j, i: (i, j)),
        out_shape=jax.ShapeDtypeStruct((m, n), F32),
        scratch_shapes=[pltpu.VMEM((k, tn), BF16), pltpu.VMEM((k, tn), BF16)],
        compiler_params=_cparams("arbitrary", "arbitrary"),
    )(x, w1, w1, b3, b3)


def _layernorm_silu(y, lg, lb):
    mu = jnp.mean(y, axis=-1, keepdims=True)
    yc = y - mu
    var = jnp.mean(yc * yc, axis=-1, keepdims=True)
    return _silu(yc * lax.rsqrt(var + LN_EPS) * lg + lb)


def _conv_ln_prompt_kernel(u_ref, halo_ref, dw_ref, dwb_ref, lg_ref, lb_ref, o_ref, full_s, y_s,
                           *, tm, halo, cw, rc, lw):
    i = pl.program_id(1)
    d = u_ref.shape[-1]
    full_s[halo:halo + tm, :] = u_ref[...]

    @pl.when(i == 0)
    def _():
        full_s[0:halo, :] = jnp.zeros((halo, d), F32)

    @pl.when(i > 0)
    def _():
        full_s[0:halo, :] = halo_ref[...]

    off = halo - (cw - 1)
    win = rc + halo
    sub = V7X_SUBLANES

    def body(r, carry):
        r0 = pl.multiple_of(r * rc, rc)
        for c0 in range(0, d, lw):
            window = full_s[pl.ds(r0, win), c0:c0 + lw]
            acc = jnp.zeros((rc, lw), F32) + dwb_ref[:, c0:c0 + lw]
            for s in range(sub):
                taps = [k for k in range(cw) if (off + k) % sub == s]
                if not taps:
                    continue
                shifted = window if s == 0 else pltpu.roll(window, win - s, 0)
                for k in taps:
                    a = off + k - s
                    acc = acc + dw_ref[k:k + 1, c0:c0 + lw] * shifted[a:a + rc]
            y_s[pl.ds(r0, rc), c0:c0 + lw] = acc
        return carry
    lax.fori_loop(0, tm // rc, body, 0)

    o_ref[...] = _layernorm_silu(y_s[...], lg_ref[...], lb_ref[...]).astype(o_ref.dtype)


def _conv_ln_prompt(u3, dw, dwb, lg, lb, *, tm=256):
    bsz, t, d = u3.shape
    cw = dw.shape[0]
    halo = 32
    assert cw - 1 <= halo and t % tm == 0 and tm % halo == 0
    hb = tm // halo
    lw = min(d, 256)
    return pl.pallas_call(
        functools.partial(_conv_ln_prompt_kernel, tm=tm, halo=halo, cw=cw, rc=halo, lw=lw),
        grid=(bsz, t // tm),
        in_specs=[pl.BlockSpec((None, tm, d), lambda b, i: (b, i, 0)),
                  pl.BlockSpec((None, halo, d), lambda b, i: (b, jnp.maximum(i * hb - 1, 0), 0)),
                  pl.BlockSpec((cw, d), lambda b, i: (0, 0)),
                  pl.BlockSpec((1, d), lambda b, i: (0, 0)),
                  pl.BlockSpec((1, d), lambda b, i: (0, 0)),
                  pl.BlockSpec((1, d), lambda b, i: (0, 0))],
        out_specs=pl.BlockSpec((None, tm, d), lambda b, i: (b, i, 0)),
        out_shape=jax.ShapeDtypeStruct((bsz, t, d), BF16),
        scratch_shapes=[pltpu.VMEM((tm + halo, d), F32), pltpu.VMEM((tm, d), F32)],
        compiler_params=_cparams("arbitrary", "arbitrary"),
    )(u3, u3, dw, dwb.reshape(1, d), lg.reshape(1, d), lb.reshape(1, d))


def _conv_ln_sample_kernel(st_ref, u_ref, dw_ref, dwb_ref, lg_ref, lb_ref, o_ref, nst_ref, *, cw, lw):
    t_new, _, d = u_ref.shape
    n_st = cw - 1

    def row(idx, c0):
        if idx < n_st:
            return st_ref[idx, :, c0:c0 + lw]
        return u_ref[idx - n_st, :, c0:c0 + lw]

    for t in range(t_new):
        parts = []
        for c0 in range(0, d, lw):
            acc = dw_ref[0:1, c0:c0 + lw] * row(t, c0)
            for k in range(1, cw):
                acc = acc + dw_ref[k:k + 1, c0:c0 + lw] * row(t + k, c0)
            parts.append(acc + dwb_ref[:, c0:c0 + lw])
        y = jnp.concatenate(parts, axis=-1) if len(parts) > 1 else parts[0]
        o_ref[t] = _layernorm_silu(y, lg_ref[...], lb_ref[...]).astype(o_ref.dtype)
    for r in range(n_st):
        idx = t_new + r
        nst_ref[r] = st_ref[idx] if idx < n_st else u_ref[idx - n_st]


def _conv_ln_sample(st_tm, u_tm, dw, dwb, lg, lb):
    n_st, bsz, d = st_tm.shape
    t_new = u_tm.shape[0]
    cw = dw.shape[0]
    lw = min(d, 512)
    full = lambda shape: pl.BlockSpec(shape, lambda i: (0,) * len(shape))
    return pl.pallas_call(
        functools.partial(_conv_ln_sample_kernel, cw=cw, lw=lw),
        grid=(1,),
        in_specs=[full((n_st, bsz, d)), full((t_new, bsz, d)), full((cw, d)), full((1, d)), full((1, d)),
                  full((1, d))],
        out_specs=[full((t_new, bsz, d)), full((n_st, bsz, d))],
        out_shape=[jax.ShapeDtypeStruct((t_new, bsz, d), BF16),
                   jax.ShapeDtypeStruct((n_st, bsz, d), F32)],
        compiler_params=_cparams("arbitrary"),
    )(st_tm, u_tm, dw, dwb.reshape(1, d), lg.reshape(1, d), lb.reshape(1, d))


def _ffn_up_prompt_kernel(x_ref, wg_ref, wv_ref, dw_ref, dwb_ref, a_ref, st_ref, wg_s, wv_s, carry_s,
                          *, tiles_per_seq):
    i = pl.program_id(1)

    @pl.when(i == 0)
    def _():
        wg_s[...] = wg_ref[...].astype(BF16)
        wv_s[...] = wv_ref[...].astype(BF16)

    @pl.when(i % tiles_per_seq == 0)
    def _():
        carry_s[...] = jnp.zeros(carry_s.shape, F32)

    x = x_ref[...]
    g = _dot(x, wg_s[...])
    v = _dot(x, wv_s[...])
    tm = g.shape[0]
    row = lax.broadcasted_iota(jnp.int32, g.shape, 0)
    prev1 = carry_s[V7X_SUBLANES - 1:V7X_SUBLANES, :]
    prev2 = carry_s[V7X_SUBLANES - 2:V7X_SUBLANES - 1, :]
    g1 = jnp.where(row == 0, prev1, pltpu.roll(g, 1, 0))
    g2 = jnp.where(row == 0, prev2, jnp.where(row == 1, prev1, pltpu.roll(g, 2, 0)))
    gc = dw_ref[0:1, :] * g2 + dw_ref[1:2, :] * g1 + dw_ref[2:3, :] * g + dwb_ref[...]
    a_ref[...] = (_silu(gc) * v).astype(a_ref.dtype)
    carry_s[...] = g[tm - V7X_SUBLANES:tm, :]
    st_ref[...] = g[tm - 2:tm, :]


def _ffn_up_prompt(x, w_up, dw, dwb, layer, bsz, *, tm=1024, tn=512):
    m, k = x.shape
    n = w_up.shape[2] // 2
    t = m // bsz
    tm, tn = min(tm, t), min(tn, n)
    assert t % tm == 0 and n % tn == 0 and dw.shape[1] == 3
    nj = n // tn
    tiles_per_seq = t // tm
    return pl.pallas_call(
        functools.partial(_ffn_up_prompt_kernel, tiles_per_seq=tiles_per_seq),
        grid=(nj, m // tm),
        in_specs=[pl.BlockSpec((tm, k), lambda j, i: (i, 0)),
                  pl.BlockSpec((None, k, tn), lambda j, i: (layer, 0, j)),
                  pl.BlockSpec((None, k, tn), lambda j, i: (layer, 0, nj + j)),
                  pl.BlockSpec((None, 3, tn), lambda j, i: (layer, 0, j)),
                  pl.BlockSpec((None, 1, tn), lambda j, i: (layer, 0, j))],
        out_specs=[pl.BlockSpec((tm, tn), lambda j, i: (i, j)),
                   pl.BlockSpec((None, 2, tn), lambda j, i: (i // tiles_per_seq, 0, j))],
        out_shape=[jax.ShapeDtypeStruct((m, n), BF16),
                   jax.ShapeDtypeStruct((bsz, 2, n), F32)],
        scratch_shapes=[pltpu.VMEM((k, tn), BF16), pltpu.VMEM((k, tn), BF16),
                        pltpu.VMEM((V7X_SUBLANES, tn), F32)],
        compiler_params=_cparams("arbitrary", "arbitrary"),
    )(x, w_up, w_up, dw, dwb.reshape(dwb.shape[0], 1, n))


def _ffn_up_sample_kernel(x_ref, wg_ref, wv_ref, dw_ref, dwb_ref, prev_ref, a_ref, st_ref, *, bsz):
    x = x_ref[...]
    g = _dot(x, wg_ref[...].astype(BF16))
    v = _dot(x, wv_ref[...].astype(BF16))
    rows = g.shape[0]
    full = jnp.concatenate([prev_ref[...], g], axis=0)
    gc = (dw_ref[0:1, :] * full[0:rows] + dw_ref[1:2, :] * full[bsz:bsz + rows]
          + dw_ref[2:3, :] * full[2 * bsz:2 * bsz + rows] + dwb_ref[...])
    a_ref[...] = (_silu(gc) * v).astype(a_ref.dtype)
    st_ref[...] = full[rows:rows + 2 * bsz]


def _ffn_up_sample(x, w_up, dw, dwb, prev_tm, layer, bsz, *, tn=512):
    m, k = x.shape
    n = w_up.shape[2] // 2
    tn = min(tn, n)
    assert n % tn == 0 and m >= 2 * bsz and dw.shape[1] == 3
    nj = n // tn
    return pl.pallas_call(
        functools.partial(_ffn_up_sample_kernel, bsz=bsz),
        grid=(nj,),
        in_specs=[pl.BlockSpec((m, k), lambda j: (0, 0)),
                  pl.BlockSpec((None, k, tn), lambda j: (layer, 0, j)),
                  pl.BlockSpec((None, k, tn), lambda j: (layer, 0, nj + j)),
                  pl.BlockSpec((None, 3, tn), lambda j: (layer, 0, j)),
                  pl.BlockSpec((None, 1, tn), lambda j: (layer, 0, j)),
                  pl.BlockSpec((2 * bsz, tn), lambda j: (0, j))],
        out_specs=[pl.BlockSpec((m, tn), lambda j: (0, j)),
                   pl.BlockSpec((2 * bsz, tn), lambda j: (0, j))],
        out_shape=[jax.ShapeDtypeStruct((m, n), BF16),
                   jax.ShapeDtypeStruct((2 * bsz, n), F32)],
        compiler_params=_cparams("arbitrary"),
    )(x, w_up, w_up, dw, dwb.reshape(dwb.shape[0], 1, n), prev_tm)


def _swa_prompt_kernel(sink_ref, q_ref, kc_ref, kp_ref, vc_ref, vp_ref, bias_ref, o_ref, *, n_heads, group, hd,
                       scale):
    n = pl.program_id(1)
    w = q_ref.shape[0]
    kcat = jnp.concatenate([kp_ref[...], kc_ref[...]], axis=0).astype(BF16)
    vcat = jnp.concatenate([vp_ref[...], vc_ref[...]], axis=0).astype(BF16)
    col = lax.broadcasted_iota(jnp.int32, (w, 2 * w), 1)
    keep = (col >= w) | (n > 0)
    for h in range(n_heads):
        kv = h // group
        q = q_ref[:, h * hd:(h + 1) * hd].astype(BF16)
        s = _dot_nt(q, kcat[:, kv * hd:(kv + 1) * hd]) * scale + bias_ref[h]
        s = jnp.where(keep, s, NEG)
        sink = sink_ref[h]
        m = jnp.maximum(jnp.max(s, axis=-1, keepdims=True), sink)
        p = jnp.exp(s - m)
        l = jnp.sum(p, axis=-1, keepdims=True) + jnp.exp(sink - m)
        o = _dot(p.astype(BF16), vcat[:, kv * hd:(kv + 1) * hd]) / l
        o_ref[:, h * hd:(h + 1) * hd] = o.astype(o_ref.dtype)


def _swa_prompt(qkv, sinks, bias, bsz, n_heads, n_kv, hd, window):
    m = qkv.shape[0]
    t = m // bsz
    nb = t // window
    qw, kw = n_heads * hd, n_kv * hd
    assert qw % kw == 0 and t % window == 0
    kcol, vcol = qw // kw, qw // kw + 1
    return pl.pallas_call(
        functools.partial(_swa_prompt_kernel, n_heads=n_heads, group=n_heads // n_kv, hd=hd, scale=hd ** -0.5),
        grid=(bsz, nb),
        in_specs=[pl.BlockSpec(memory_space=pltpu.SMEM),
                  pl.BlockSpec((window, qw), lambda b, n: (b * nb + n, 0)),
                  pl.BlockSpec((window, kw), lambda b, n: (b * nb + n, kcol)),
                  pl.BlockSpec((window, kw), lambda b, n: (b * nb + jnp.maximum(n - 1, 0), kcol)),
                  pl.BlockSpec((window, kw), lambda b, n: (b * nb + n, vcol)),
                  pl.BlockSpec((window, kw), lambda b, n: (b * nb + jnp.maximum(n - 1, 0), vcol)),
                  pl.BlockSpec((n_heads, window, 2 * window), lambda b, n: (0, 0, 0))],
        out_specs=pl.BlockSpec((window, qw), lambda b, n: (b * nb + n, 0)),
        out_shape=jax.ShapeDtypeStruct((m, qw), BF16),
        compiler_params=_cparams("arbitrary", "arbitrary"),
    )(sinks, qkv, qkv, qkv, qkv, qkv, bias)


def _swa_sample_kernel(q_ref, bk_ref, bv_ref, kn_ref, vn_ref, bias_ref, own_ref, sink_ref, o_ref,
                       *, n_kv, group, tp, scale):
    cols, hd = q_ref.shape
    keys = bk_ref.shape[0] // n_kv
    qb = q_ref[...].astype(BF16)
    col_kv = lax.broadcasted_iota(jnp.int32, (keys, cols), 1) // (group * tp)

    def by_head(ref):
        return jnp.concatenate([ref[pl.ds(kv, keys, stride=n_kv), :] for kv in range(n_kv)], axis=0)

    full = _dot_nt(by_head(bk_ref).astype(BF16), qb)
    s = full[0:keys]
    for kv in range(1, n_kv):
        s = jnp.where(col_kv == kv, full[kv * keys:(kv + 1) * keys], s)
    s = s * scale + bias_ref[...]
    s_own = _dot_nt(kn_ref[...].astype(BF16), qb) * scale + own_ref[...]
    sink = sink_ref[...]
    m = jnp.maximum(jnp.maximum(jnp.max(s, axis=0, keepdims=True), jnp.max(s_own, axis=0, keepdims=True)), sink)
    p = jnp.exp(s - m)
    p_own = jnp.exp(s_own - m)
    l = jnp.sum(p, axis=0, keepdims=True) + jnp.sum(p_own, axis=0, keepdims=True) + jnp.exp(sink - m)
    spread = jnp.concatenate([jnp.where(col_kv == kv, p, 0.0) for kv in range(n_kv)], axis=0)
    acc = (_dot(by_head(bv_ref).T.astype(BF16), spread.astype(BF16))
           + _dot(vn_ref[...].T.astype(BF16), p_own.astype(BF16)))
    o_ref[...] = (acc / l).T.astype(o_ref.dtype)


def _swa_sample(q_cols, buf_k, buf_v, k_new, v_new, bias, own, sink_row, *, n_kv, group, tp):
    dbs, cols, hd = q_cols.shape
    per_seq = lambda a: pl.BlockSpec((None,) + a.shape[1:], lambda b: (b, 0, 0))
    shared = lambda a: pl.BlockSpec(a.shape, lambda b: (0, 0))
    return pl.pallas_call(
        functools.partial(_swa_sample_kernel, n_kv=n_kv, group=group, tp=tp, scale=hd ** -0.5),
        grid=(dbs,),
        in_specs=[per_seq(q_cols), per_seq(buf_k), per_seq(buf_v), per_seq(k_new), per_seq(v_new),
                  shared(bias), shared(own), shared(sink_row)],
        out_specs=pl.BlockSpec((None, cols, hd), lambda b: (b, 0, 0)),
        out_shape=jax.ShapeDtypeStruct((dbs, cols, hd), BF16),
        compiler_params=_cparams("arbitrary"),
    )(q_cols, buf_k, buf_v, k_new, v_new, bias, own, sink_row)


def _block_means_kernel(k_ref, o_ref):
    o_ref[...] = jnp.mean(k_ref[...], axis=0, keepdims=True)


def _block_means(qkv, n_blocks_total, kcol, kw):
    return pl.pallas_call(
        _block_means_kernel,
        grid=(n_blocks_total,),
        in_specs=[pl.BlockSpec((MOBA_BLOCK, kw), lambda i: (i, kcol))],
        out_specs=pl.BlockSpec((None, 1, kw), lambda i: (i, 0, 0)),
        out_shape=jax.ShapeDtypeStruct((n_blocks_total, 1, kw), F32),
        compiler_params=_cparams("arbitrary"),
    )(qkv)


def _rank_rows(gate, blk, n_rows):
    rank = jnp.zeros(gate.shape, jnp.int32)
    for m in range(n_rows):
        gm = gate[m:m + 1, :]
        beats = (gm > gate) | ((gm == gate) & (blk > m))
        rank = rank + jnp.where(beats, 1, 0)
    return rank


def _moba_prompt_kernel(q_ref, k_ref, v_ref, means_ref, bias_ref, far_ref, o_ref, qs, sel_s, m_s, l_s, acc_s,
                        *, n_kv, group, hd, scale, n_near, nb):
    qb = pl.program_id(1)
    kb = pl.program_id(2)
    blk_rows = q_ref.shape[0]
    cols = group * blk_rows

    @pl.when(kb == 0)
    def _():
        blk = lax.broadcasted_iota(jnp.int32, (nb, cols), 0)
        for kv in range(n_kv):
            qg = jnp.concatenate([q_ref[:, (kv * group + g) * hd:(kv * group + g + 1) * hd]
                                  for g in range(group)], axis=0)
            qs[kv] = (qg * (scale * LOG2E)).astype(BF16)
            gate = _dot_nt(means_ref[:, kv * hd:(kv + 1) * hd], qg, precision=lax.Precision.HIGHEST)
            valid = blk < qb
            gate = jnp.where(valid, gate, -jnp.inf)
            rank = _rank_rows(gate, blk, nb)
            sel_s[kv] = jnp.where(valid & (rank < MOBA_TOPK), 1.0, 0.0)
            m_s[kv] = jnp.full((1, cols), NEG, F32)
            l_s[kv] = jnp.zeros((1, cols), F32)
            acc_s[kv] = jnp.zeros((hd, cols), F32)

    def attend(kv, *adds):
        k = k_ref[:, kv * hd:(kv + 1) * hd].astype(BF16)
        vt = v_ref[:, kv * hd:(kv + 1) * hd].T.astype(BF16)
        s = _dot_nt(k, qs[kv])
        for a in adds:
            s = s + a
        m_prev = m_s[kv]
        m_new = jnp.maximum(m_prev, jnp.max(s, axis=0, keepdims=True))
        alpha = jnp.exp2(m_prev - m_new)
        p = jnp.exp2(s - m_new)
        l_s[kv] = alpha * l_s[kv] + jnp.sum(p, axis=0, keepdims=True)
        acc_s[kv] = alpha * acc_s[kv] + _dot(vt, p.astype(BF16))
        m_s[kv] = m_new

    delta = qb - kb

    @pl.when((kb <= qb) & (delta < n_near))
    def _():
        for kv in range(n_kv):
            chosen = (sel_s[kv, pl.ds(kb, 1), :] > 0.5) | (delta == 0)
            attend(kv, bias_ref[kv], jnp.where(chosen, 0.0, NEG))

    @pl.when(delta >= n_near)
    def _():
        for kv in range(n_kv):
            chosen = sel_s[kv, pl.ds(kb, 1), :] > 0.5
            attend(kv, jnp.where(chosen, far_ref[kv], NEG))

    @pl.when(kb == qb)
    def _():
        for kv in range(n_kv):
            o = acc_s[kv] / l_s[kv]
            for g in range(group):
                h = kv * group + g
                o_ref[:, h * hd:(h + 1) * hd] = o[:, g * blk_rows:(g + 1) * blk_rows].T.astype(o_ref.dtype)


def _moba_prompt(qkv, means, bias_t, far_rows, bsz, n_heads, n_kv, hd):
    m = qkv.shape[0]
    t = m // bsz
    nb = t // MOBA_BLOCK
    group = n_heads // n_kv
    qw, kw = n_heads * hd, n_kv * hd
    kcol, vcol = qw // kw, qw // kw + 1
    n_near = bias_t.shape[0]
    cols = group * MOBA_BLOCK
    kv_idx = lambda b, qb, kb: b * nb + jnp.minimum(kb, qb)
    return pl.pallas_call(
        functools.partial(_moba_prompt_kernel, n_kv=n_kv, group=group, hd=hd, scale=hd ** -0.5,
                          n_near=n_near, nb=nb),
        grid=(bsz, nb, nb),
        in_specs=[pl.BlockSpec((MOBA_BLOCK, qw), lambda b, qb, kb: (b * nb + qb, 0)),
                  pl.BlockSpec((MOBA_BLOCK, kw), lambda b, qb, kb: (kv_idx(b, qb, kb), kcol)),
                  pl.BlockSpec((MOBA_BLOCK, kw), lambda b, qb, kb: (kv_idx(b, qb, kb), vcol)),
                  pl.BlockSpec((None, nb, kw), lambda b, qb, kb: (b, 0, 0)),
                  pl.BlockSpec((None, n_kv, MOBA_BLOCK, cols),
                               lambda b, qb, kb: (jnp.clip(qb - kb, 0, n_near - 1), 0, 0, 0)),
                  pl.BlockSpec((n_kv, 1, cols), lambda b, qb, kb: (0, 0, 0))],
        out_specs=pl.BlockSpec((MOBA_BLOCK, qw), lambda b, qb, kb: (b * nb + qb, 0)),
        out_shape=jax.ShapeDtypeStruct((m, qw), BF16),
        scratch_shapes=[pltpu.VMEM((n_kv, cols, hd), BF16),
                        pltpu.VMEM((n_kv, nb, cols), F32),
                        pltpu.VMEM((n_kv, 1, cols), F32),
                        pltpu.VMEM((n_kv, 1, cols), F32),
                        pltpu.VMEM((n_kv, hd, cols), F32)],
        compiler_params=_cparams("arbitrary", "arbitrary", "arbitrary"),
    )(qkv, qkv, qkv, means, bias_t, far_rows)


_MEAN_PAGES_PER_STEP = 16
_ATTN_PAGES_PER_STEP = 8


def _pool_means_kernel(pt_ref, *refs, ppb, n_kv):
    del pt_ref
    o_ref = refs[-1]
    pages = refs[:-1]
    hd = pages[0].shape[1]
    sub = V7X_SUBLANES
    reps = sub // n_kv
    folded = []
    for blk in range(len(pages) // ppb):
        tot = None
        for p in range(ppb):
            part = jnp.sum(pages[blk * ppb + p][...].reshape(-1, sub, hd), axis=0)
            tot = part if tot is None else tot + part
        full = tot
        for i in range(1, reps):
            full = full + pltpu.roll(tot, i * n_kv, 0)
        folded.append(full)
    row_grp = lax.broadcasted_iota(jnp.int32, (sub, hd), 0) // n_kv
    tiles = []
    for a in range(0, len(folded), reps):
        tile = folded[a]
        for i in range(1, reps):
            tile = jnp.where(row_grp == i, folded[a + i], tile)
        tiles.append(tile)
    n_keys = ppb * pages[0].shape[0] // n_kv
    o_ref[...] = jnp.concatenate(tiles, axis=0) * (1.0 / n_keys)


def _pool_means(pool, pt_flat, page_off, dbs, n_pages, n_kv):
    rows, hd = pool.shape[1], pool.shape[2]
    ppb = MOBA_BLOCK // (rows // n_kv)
    pps = min(_MEAN_PAGES_PER_STEP, n_pages)
    blocks = pps // ppb
    assert n_pages % pps == 0 and pps % ppb == 0 and V7X_SUBLANES % n_kv == 0
    assert (blocks * n_kv) % V7X_SUBLANES == 0

    def page_spec(i):
        return pl.BlockSpec((None, rows, hd),
                            lambda b, s, pt: (pt[b * n_pages + s * pps + i] + page_off, 0, 0))

    grid_spec = pltpu.PrefetchScalarGridSpec(
        num_scalar_prefetch=1,
        grid=(dbs, n_pages // pps),
        in_specs=[page_spec(i) for i in range(pps)],
        out_specs=pl.BlockSpec((None, blocks * n_kv, hd), lambda b, s, pt: (b, s, 0)),
    )
    return pl.pallas_call(
        functools.partial(_pool_means_kernel, ppb=ppb, n_kv=n_kv),
        grid_spec=grid_spec,
        out_shape=jax.ShapeDtypeStruct((dbs, (n_pages // ppb) * n_kv, hd), F32),
        compiler_params=_cparams("arbitrary", "arbitrary"),
    )(pt_flat, *([pool] * pps))


def _cols_expand_kernel(tab_ref, idx_ref, o_ref, *, buckets, mult):
    idx = idx_ref[...]
    acc = jnp.full(idx.shape, NEG, F32)
    for b in buckets:
        acc = jnp.where(idx == b, tab_ref[b:b + 1, :] * mult, acc)
    o_ref[...] = acc


def _cols_expand(tab_cols, idx_np, tile_rows, mult=1.0):
    rows, cols = idx_np.shape
    buckets = tuple(int(b) for b in np.unique(idx_np) if b >= 0)
    return pl.pallas_call(
        functools.partial(_cols_expand_kernel, buckets=buckets, mult=mult),
        grid=(rows // tile_rows,),
        in_specs=[pl.BlockSpec(tab_cols.shape, lambda i: (0, 0)),
                  pl.BlockSpec((tile_rows, cols), lambda i: (i, 0))],
        out_specs=pl.BlockSpec((tile_rows, cols), lambda i: (i, 0)),
        out_shape=jax.ShapeDtypeStruct((rows, cols), F32),
        compiler_params=_cparams("arbitrary"),
    )(tab_cols, jnp.asarray(idx_np))


def _moba_sample_kernel(pt_ref, *refs, pps, ppb, n_kv, group, tp, n_blk, n_pages, far_cls, scale):
    del pt_ref
    k_pages = refs[:pps]
    v_pages = refs[pps:2 * pps]
    q_ref, means_ref, bias_ref, own_ref, kn_ref, vn_ref, o_ref, sel_s, m_s, l_s, acc_s = refs[2 * pps:]
    step = pl.program_id(1)
    cols, hd = q_ref.shape

    @pl.when(step == 0)
    def _():
        q = q_ref[...]
        blk = lax.broadcasted_iota(jnp.int32, (n_blk, cols), 0)
        col_kv = lax.broadcasted_iota(jnp.int32, (n_blk, cols), 1) // (group * tp)
        gate = jnp.zeros((n_blk, cols), F32)
        for kv in range(n_kv):
            g = _dot_nt(means_ref[kv * n_blk:(kv + 1) * n_blk, :], q, precision=lax.Precision.HIGHEST)
            gate = jnp.where(col_kv == kv, g, gate)
        rank = _rank_rows(gate, blk, n_blk)
        sel_s[...] = jnp.where(rank < MOBA_TOPK, 0.0, NEG)
        m_s[...] = jnp.full(m_s.shape, NEG, F32)
        l_s[...] = jnp.zeros(l_s.shape, F32)
        acc_s[...] = jnp.zeros(acc_s.shape, F32)

    qb = (q_ref[...] * (scale * LOG2E)).astype(BF16)
    page_keys = k_pages[0].shape[0] // n_kv
    col_kv = lax.broadcasted_iota(jnp.int32, (page_keys, cols), 1) // (group * tp)

    def accumulate(scores, values, expand):
        m_prev = m_s[...]
        m_new = m_prev
        for s in scores:
            m_new = jnp.maximum(m_new, jnp.max(s, axis=0, keepdims=True))
        alpha = jnp.exp2(m_prev - m_new)
        l_new = alpha * l_s[...]
        acc = alpha * acc_s[...]
        for s, v in zip(scores, values):
            p = jnp.exp2(s - m_new)
            l_new = l_new + jnp.sum(p, axis=0, keepdims=True)
            acc = acc + _dot(v.T.astype(BF16), expand(p).astype(BF16))
        m_s[...] = m_new
        l_s[...] = l_new
        acc_s[...] = acc

    def by_head(ref):
        return jnp.concatenate([ref[pl.ds(kv, page_keys, stride=n_kv), :] for kv in range(n_kv)], axis=0)

    def spread(p):
        return jnp.concatenate([jnp.where(col_kv == kv, p, 0.0) for kv in range(n_kv)], axis=0)

    scores, values = [], []
    for i in range(pps):
        page = step * pps + i
        cls = jnp.minimum(n_pages - 1 - page, far_cls)
        full = _dot_nt(by_head(k_pages[i]).astype(BF16), qb)
        s = full[0:page_keys]
        for kv in range(1, n_kv):
            s = jnp.where(col_kv == kv, full[kv * page_keys:(kv + 1) * page_keys], s)
        scores.append(s + bias_ref[cls] + sel_s[pl.ds(page // ppb, 1), :])
        values.append(by_head(v_pages[i]))
    accumulate(scores, values, spread)

    @pl.when(step == pl.num_programs(1) - 1)
    def _():
        s_own = _dot_nt(kn_ref[...].astype(BF16), qb) + own_ref[...]
        accumulate([s_own], [vn_ref[...]], lambda p: p)
        o_ref[...] = (acc_s[...] / l_s[...]).T


def _moba_sample(q_cols, means_t, k_new, v_new, pool_k, pool_v, pt_flat, page_off, bias_pages, bias_own, *,
                 n_kv, group, tp, n_pages):
    dbs, cols, hd = q_cols.shape
    rows = pool_k.shape[1]
    ppb = MOBA_BLOCK // (rows // n_kv)
    n_blk = n_pages // ppb
    pps = min(_ATTN_PAGES_PER_STEP, n_pages)
    assert n_pages % pps == 0
    far_cls = bias_pages.shape[0] - 1
    n_own = k_new.shape[1]

    def page_spec(i):
        return pl.BlockSpec((None, rows, hd),
                            lambda b, s, pt: (pt[b * n_pages + s * pps + i] + page_off, 0, 0))

    per_seq = lambda r: pl.BlockSpec((None, r, hd), lambda b, s, pt: (b, 0, 0))
    in_specs = ([page_spec(i) for i in range(pps)] * 2
                + [per_seq(cols), per_seq(n_kv * n_blk),
                   pl.BlockSpec(bias_pages.shape, lambda b, s, pt: (0, 0, 0)),
                   pl.BlockSpec(bias_own.shape, lambda b, s, pt: (0, 0)),
                   per_seq(n_own), per_seq(n_own)])
    grid_spec = pltpu.PrefetchScalarGridSpec(
        num_scalar_prefetch=1,
        grid=(dbs, n_pages // pps),
        in_specs=in_specs,
        out_specs=pl.BlockSpec((None, cols, hd), lambda b, s, pt: (b, 0, 0)),
        scratch_shapes=[pltpu.VMEM((n_blk, cols), F32), pltpu.VMEM((1, cols), F32),
                        pltpu.VMEM((1, cols), F32), pltpu.VMEM((hd, cols), F32)],
    )
    return pl.pallas_call(
        functools.partial(_moba_sample_kernel, pps=pps, ppb=ppb, n_kv=n_kv, group=group, tp=tp, n_blk=n_blk,
                          n_pages=n_pages, far_cls=far_cls, scale=hd ** -0.5),
        grid_spec=grid_spec,
        out_shape=jax.ShapeDtypeStruct((dbs, cols, hd), F32),
        compiler_params=_cparams("arbitrary", "arbitrary"),
    )(pt_flat, *([pool_k] * pps), *([pool_v] * pps), q_cols, means_t, bias_pages, bias_own, k_new, v_new)


def _swa_prompt_bias_idx(window, nbk):
    qi = np.arange(window)[:, None]
    kj = np.arange(2 * window)[None, :]
    dist = qi + window - kj
    return np.where((dist >= 0) & (dist < window), _bucket_np(dist, nbk), -1).astype(np.int32)


def _swa_sample_bias_idx(window, n_heads, tp, t_new, nbk):
    t = np.arange(n_heads * tp) % tp
    r = np.arange(window)
    dist = t[None, :] + window - r[:, None]
    idx = np.where((dist >= 0) & (dist < window), _bucket_np(dist, nbk), -1)
    return np.where(t[None, :] < t_new, idx, 0).astype(np.int32)


def _moba_prompt_bias_idx(n_near, nbk):
    j = np.arange(MOBA_BLOCK)[:, None]
    qi = np.arange(MOBA_BLOCK)[None, :]
    tiles = []
    for delta in range(n_near):
        dist = delta * MOBA_BLOCK + qi - j
        tiles.append(np.where(dist >= 0, _bucket_np(dist, nbk), -1))
    return np.concatenate(tiles, axis=0).astype(np.int32)


def _moba_sample_page_idx(far_cls, page, n_heads, tp, t_new, nbk):
    cols = n_heads * tp
    t = np.arange(cols) % tp
    r = np.arange(page)
    tiles = []
    for c in range(far_cls + 1):
        dist = (c + 1) * page + t[None, :] - r[:, None]
        b = _bucket_np(dist, nbk) if c < far_cls else np.full(dist.shape, nbk - 1, np.int32)
        tiles.append(np.where(t[None, :] < t_new, b, 0))
    return np.concatenate(tiles, axis=0).astype(np.int32)


def _moba_sample_own_idx(n_kv, n_heads, group, tp, t_new, nbk):
    cols = n_heads * tp
    h, t = np.arange(cols) // tp, np.arange(cols) % tp
    tk, kvc = np.arange(tp * n_kv) // n_kv, np.arange(tp * n_kv) % n_kv
    ok = ((kvc[:, None] == (h // group)[None, :]) & (tk[:, None] <= t[None, :]) & (t[None, :] < t_new))
    return np.where(ok, _bucket_np(t[None, :] - tk[:, None], nbk), -1).astype(np.int32)


def kernel(x_prompt, x_sample, c_prompt, c_sample, state_conv, cache_swa_k, cache_swa_v, cache_moba_k, cache_moba_v, page_table, state_ffn, ada_w, ada_b, norm_mix, norm_ffn, norm_final, rel_bias, conv_w1, conv_b1, conv_dw, conv_dw_b, conv_ln_g, conv_ln_b, conv_w2, conv_b2, swa_wqkv, swa_wo, swa_sinks, moba_wqkv, moba_wo, ffn_w_up, ffn_dw, ffn_dw_b, ffn_w_down):
    bsz, seq, d = x_prompt.shape
    dbs, t_new, _ = x_sample.shape
    depth = ada_w.shape[0]
    window, n_kv, hd = cache_swa_k.shape[2], cache_swa_k.shape[3], cache_swa_k.shape[4]
    nbk, n_heads = rel_bias.shape
    group = n_heads // n_kv
    qw, kw = n_heads * hd, n_kv * hd
    cw = conv_dw.shape[1]
    d_ff = ffn_dw.shape[2]
    n_pool, page_size = cache_moba_k.shape[1], cache_moba_k.shape[2]
    n_pages = page_table.shape[1]
    ppb = MOBA_BLOCK // page_size
    n_blk_past = n_pages // ppb
    tp = V7X_SUBLANES
    assert t_new <= tp and n_pages % ppb == 0 and n_blk_past >= MOBA_TOPK and seq % MOBA_BLOCK == 0
    n_mixers = 3

    n_c = bsz + dbs
    c_rows = -(-n_c // V7X_SUBLANES) * V7X_SUBLANES
    c_all = jnp.concatenate([c_prompt, c_sample, jnp.zeros((c_rows - n_c, d), F32)], axis=0)
    mod = _ada(c_all, ada_w, ada_b)

    def mods(layer, which):
        chunk = mod[layer, :, which * d:(which + 1) * d]
        return chunk[:bsz], chunk[bsz:bsz + dbs]

    d_sat = _saturation_distance(nbk)
    n_near = -(-(d_sat + MOBA_BLOCK - 1) // MOBA_BLOCK)
    swa_bias_p = swa_bias_s = swa_bias_own = moba_bias_p = moba_far = moba_bias_sel = moba_bias_own = None
    tab_cols = jnp.repeat(rel_bias, tp, axis=1)
    own_idx = _moba_sample_own_idx(n_kv, n_heads, group, tp, t_new, nbk)
    if depth > 1:
        swa_bias_p = _bias_expand(rel_bias, _swa_prompt_bias_idx(window, nbk), window,
                                  (n_heads, window, 2 * window), (None, window, 2 * window),
                                  lambda h: (h, 0, 0))
        swa_bias_s = _cols_expand(tab_cols, _swa_sample_bias_idx(window, n_heads, tp, t_new, nbk), window)
        swa_bias_own = _cols_expand(tab_cols, own_idx, tp * n_kv)
    if depth > 2:
        cols = group * MOBA_BLOCK
        moba_bias_p = _bias_expand(
            rel_bias, _moba_prompt_bias_idx(n_near, nbk), MOBA_BLOCK,
            (n_near, n_kv, MOBA_BLOCK, cols), (n_near, None, MOBA_BLOCK, MOBA_BLOCK),
            lambda h: (0, h // group, 0, h % group), mult=LOG2E)
        moba_far = jnp.repeat(rel_bias[nbk - 1].reshape(n_kv, group, 1), MOBA_BLOCK, axis=2).reshape(n_kv, 1, cols)
        moba_far = moba_far * LOG2E
        far_cls = -(-(d_sat + page_size - 1) // page_size) - 1
        moba_bias_sel = _cols_expand(
            tab_cols, _moba_sample_page_idx(far_cls, page_size, n_heads, tp, t_new, nbk),
            page_size, mult=LOG2E).reshape(far_cls + 1, page_size, n_heads * tp)
        moba_bias_own = _cols_expand(tab_cols, own_idx, tp * n_kv, mult=LOG2E)

    def to_tm(a):
        return jnp.swapaxes(a, 0, 1).reshape((a.shape[1] * dbs,) + a.shape[2:])

    def to_bm(a, t):
        return jnp.swapaxes(a.reshape((t, dbs) + a.shape[1:]), 0, 1)

    xp = x_prompt.reshape(bsz * seq, d)
    xs = to_tm(x_sample)
    m_s = t_new * dbs

    conv_p, conv_s, swa_k_p, swa_v_p, swa_k_s, swa_v_s = [], [], [], [], [], []
    moba_k_p, moba_v_p, moba_k_s, moba_v_s, ffn_p, ffn_s = [], [], [], [], [], []

    def gates(layer, which):
        gp, gs = mods(layer, which)
        return gp.reshape(bsz, 1, d), jnp.tile(gs, (t_new, 1)).reshape(1, m_s, d)

    def normed(x_p, x_s, g, layer, which_sh):
        shp, shs = mods(layer, which_sh)
        scp, scs = mods(layer, which_sh + 1)
        hp = _norm_mod(x_p.reshape(bsz, seq, d), g, scp.reshape(bsz, 1, d), shp.reshape(bsz, 1, d), 512)
        hs = _norm_mod(x_s.reshape(t_new, dbs, d), g, scs.reshape(1, dbs, d), shs.reshape(1, dbs, d), dbs)
        return hp.reshape(bsz * seq, d), hs.reshape(m_s, d)

    def sample_views(qkv_bm):
        pad = jnp.concatenate([qkv_bm, jnp.zeros((dbs, tp - t_new, qkv_bm.shape[2]), qkv_bm.dtype)], axis=1)
        q_cols = pad[:, :, :qw].reshape(dbs, tp, n_heads, hd).transpose(0, 2, 1, 3).reshape(dbs, n_heads * tp, hd)
        return (q_cols, pad[:, :, qw:qw + kw].reshape(dbs, tp * n_kv, hd),
                pad[:, :, qw + kw:].reshape(dbs, tp * n_kv, hd))

    def cols_to_bm(o_cols):
        o = o_cols.reshape(dbs, n_heads, tp, hd)[:, :, :t_new].transpose(0, 2, 1, 3)
        return o.reshape(dbs, t_new, qw)

    for layer in range(depth):
        kind, j = layer % n_mixers, layer // n_mixers
        hp, hs = normed(xp, xs, norm_mix[layer], layer, 0)
        g1p, g1s = gates(layer, 2)
        if kind == 0:
            up = _glu(hp, conv_w1, conv_b1, j)
            us = _glu(hs, conv_w1, conv_b1, j)
            yp = _conv_ln_prompt(up.reshape(bsz, seq, d), conv_dw[j], conv_dw_b[j], conv_ln_g[j], conv_ln_b[j])
            st_tm = jnp.swapaxes(state_conv[j], 0, 1)
            ys, nst_tm = _conv_ln_sample(st_tm, us.reshape(t_new, dbs, d), conv_dw[j], conv_dw_b[j],
                                         conv_ln_g[j], conv_ln_b[j])
            conv_p.append(up.reshape(bsz, seq, d)[:, seq - (cw - 1):])
            conv_s.append(jnp.swapaxes(nst_tm, 0, 1))
            xp = _mm(yp.reshape(bsz * seq, d), conv_w2, j, bias=conv_b2, res=xp, gate=g1p)
            xs = _mm(ys.reshape(m_s, d), conv_w2, j, bias=conv_b2, res=xs, gate=g1s)
        elif kind == 1:
            qkv_p = _mm(hp, swa_wqkv, j)
            qkv_s = _mm(hs, swa_wqkv, j)
            op = _swa_prompt(qkv_p, swa_sinks[j], swa_bias_p, bsz, n_heads, n_kv, hd, window)
            q_cols, k_new, v_new = sample_views(to_bm(qkv_s, t_new))
            buf_k = cache_swa_k[j].reshape(dbs, window * n_kv, hd)
            buf_v = cache_swa_v[j].reshape(dbs, window * n_kv, hd)
            sink_row = jnp.repeat(swa_sinks[j], tp).reshape(1, n_heads * tp)
            o_cols = _swa_sample(q_cols, buf_k, buf_v, k_new, v_new, swa_bias_s, swa_bias_own, sink_row,
                                 n_kv=n_kv, group=group, tp=tp)
            os_ = to_tm(cols_to_bm(o_cols))
            tail = qkv_p.reshape(bsz, seq, qw + 2 * kw)[:, seq - window:]
            swa_k_p.append(tail[:, :, qw:qw + kw].reshape(bsz, window, n_kv, hd))
            swa_v_p.append(tail[:, :, qw + kw:].reshape(bsz, window, n_kv, hd))
            keep = t_new * n_kv
            swa_k_s.append(jnp.concatenate([buf_k[:, keep:], k_new[:, :keep]], axis=1)
                           .reshape(dbs, window, n_kv, hd))
            swa_v_s.append(jnp.concatenate([buf_v[:, keep:], v_new[:, :keep]], axis=1)
                           .reshape(dbs, window, n_kv, hd))
            xp = _mm(op, swa_wo, j, res=xp, gate=g1p)
            xs = _mm(os_, swa_wo, j, res=xs, gate=g1s)
        else:
            qkv_p, k_rows, v_rows = _mm(hp, moba_wqkv, j, kv_heads=(n_kv, hd))
            qkv_s = _mm(hs, moba_wqkv, j)
            nb = seq // MOBA_BLOCK
            means_p = _block_means(qkv_p, bsz * nb, qw // kw, kw).reshape(bsz, nb, kw)
            op = _moba_prompt(qkv_p, means_p, moba_bias_p, moba_far, bsz, n_heads, n_kv, hd)
            moba_k_p.append(k_rows.reshape(bsz, seq // page_size, page_size, n_kv, hd))
            moba_v_p.append(v_rows.reshape(bsz, seq // page_size, page_size, n_kv, hd))

            qkv_bm = to_bm(qkv_s, t_new)
            q_cols, k_new, v_new = sample_views(qkv_bm)
            pool_k = cache_moba_k.reshape(cache_moba_k.shape[0] * n_pool, page_size * n_kv, hd)
            pool_v = cache_moba_v.reshape(cache_moba_v.shape[0] * n_pool, page_size * n_kv, hd)
            pt_flat = page_table.reshape(-1)
            means_s = _pool_means(pool_k, pt_flat, j * n_pool, dbs, n_pages, n_kv)
            means_t = means_s.reshape(dbs, n_blk_past, n_kv, hd).transpose(0, 2, 1, 3)
            means_t = means_t.reshape(dbs, n_kv * n_blk_past, hd)
            o_cols = _moba_sample(q_cols, means_t, k_new, v_new, pool_k, pool_v, pt_flat, j * n_pool,
                                  moba_bias_sel, moba_bias_own, n_kv=n_kv, group=group, tp=tp, n_pages=n_pages)
            os_bm = cols_to_bm(o_cols).astype(BF16)
            moba_k_s.append(qkv_bm[:, :, qw:qw + kw].reshape(dbs, t_new, n_kv, hd))
            moba_v_s.append(qkv_bm[:, :, qw + kw:].reshape(dbs, t_new, n_kv, hd))
            xp = _mm(op, moba_wo, j, res=xp, gate=g1p)
            xs = _mm(to_tm(os_bm), moba_wo, j, res=xs, gate=g1s)

        hp, hs = normed(xp, xs, norm_ffn[layer], layer, 3)
        g2p, g2s = gates(layer, 5)
        ap, stp = _ffn_up_prompt(hp, ffn_w_up, ffn_dw, ffn_dw_b, layer, bsz)
        prev_tm = jnp.swapaxes(state_ffn[layer], 0, 1).reshape(2 * dbs, d_ff)
        as_, sts = _ffn_up_sample(hs, ffn_w_up, ffn_dw, ffn_dw_b, prev_tm, layer, dbs)
        ffn_p.append(stp)
        ffn_s.append(jnp.swapaxes(sts.reshape(2, dbs, d_ff), 0, 1))
        xp = _mm(ap, ffn_w_down, layer, res=xp, gate=g2p, tn=256, weights_outer=False)
        xs = _mm(as_, ffn_w_down, layer, res=xs, gate=g2s, tn=256, weights_outer=False)

    y_p = _norm_mod(xp.reshape(bsz, seq, d), norm_final, None, None, 512, out_dtype=F32)
    y_s = _norm_mod(xs.reshape(t_new, dbs, d), norm_final, None, None, dbs, out_dtype=F32)
    y_s = jnp.swapaxes(y_s, 0, 1)

    return (y_p, y_s, jnp.stack(conv_p), jnp.stack(conv_s), jnp.stack(swa_k_p), jnp.stack(swa_v_p),
            jnp.stack(swa_k_s), jnp.stack(swa_v_s), jnp.stack(moba_k_p), jnp.stack(moba_v_p),
            jnp.stack(moba_k_s), jnp.stack(moba_v_s), jnp.stack(ffn_p), jnp.stack(ffn_s))
```

```python
import functools
import math

import numpy as np
import jax
import jax.numpy as jnp
from jax import lax
from jax.experimental import pallas as pl
from jax.experimental.pallas import tpu as pltpu

MOBA_BLOCK = 256
MOBA_TOPK = 3
REL_MAX_DISTANCE = 1024
RMS_EPS = 1e-6
LN_EPS = 1e-5
N_MOD = 6

V7X_LANES = 128
V7X_SUBLANES = 8
V7X_VMEM_LIMIT_BYTES = 56 * 1024 * 1024

NEG = -1e30
LOG2E = 1.4426950408889634
BF16 = jnp.bfloat16
F32 = jnp.float32


def _cparams(*sem):
    return pltpu.CompilerParams(dimension_semantics=sem, vmem_limit_bytes=V7X_VMEM_LIMIT_BYTES)


def _dot(a, b):
    return jnp.dot(a, b, preferred_element_type=F32)


def _dot_nt(a, b, precision=None):
    return lax.dot_general(a, b, (((1,), (1,)), ((), ())), precision=precision,
                           preferred_element_type=F32)


def _silu(x):
    return x * jax.nn.sigmoid(x)


def _bucket_np(dist, num_buckets):
    n = np.maximum(dist, 0)
    max_exact = num_buckets // 2
    nf = np.maximum(n, 1).astype(np.float32)
    ratio = np.log(nf / np.float32(max_exact)) / np.float32(math.log(REL_MAX_DISTANCE / max_exact))
    large = max_exact + (ratio * np.float32(num_buckets - max_exact)).astype(np.int32)
    return np.where(n < max_exact, n, np.minimum(large, num_buckets - 1)).astype(np.int32)


def _saturation_distance(num_buckets):
    d = np.arange(0, 4 * REL_MAX_DISTANCE)
    b = _bucket_np(d, num_buckets)
    below = np.nonzero(b < num_buckets - 1)[0]
    return int(below.max()) + 1


def _bias_expand_kernel(tab_ref, idx_ref, o_ref, *, tiles, mult):
    h = pl.program_id(0)
    for n, (r0, nr, buckets) in enumerate(tiles):
        idx = idx_ref[r0:r0 + nr, :]
        acc = jnp.full(idx.shape, NEG, F32)
        for b in buckets:
            acc = jnp.where(idx == b, tab_ref[b, h] * mult, acc)
        if len(o_ref.shape) == 3:
            o_ref[n] = acc
        else:
            o_ref[r0:r0 + nr, :] = acc


def _bias_expand(table, idx_np, tile_rows, out_shape, out_block, out_index, mult=1.0):
    n_heads = table.shape[1]
    rows, cols = idx_np.shape
    tiles = []
    for r0 in range(0, rows, tile_rows):
        present = np.unique(idx_np[r0:r0 + tile_rows])
        tiles.append((r0, min(tile_rows, rows - r0), tuple(int(b) for b in present if b >= 0)))
    return pl.pallas_call(
        functools.partial(_bias_expand_kernel, tiles=tuple(tiles), mult=mult),
        grid=(n_heads,),
        in_specs=[pl.BlockSpec(memory_space=pltpu.SMEM),
                  pl.BlockSpec((rows, cols), lambda h: (0, 0))],
        out_specs=pl.BlockSpec(out_block, out_index),
        out_shape=jax.ShapeDtypeStruct(out_shape, F32),
        compiler_params=_cparams("arbitrary"),
    )(table, jnp.asarray(idx_np))


def _ada_kernel(c_ref, w_ref, b_ref, o_ref):
    a = _silu(c_ref[...]).astype(BF16)
    o_ref[...] = _dot(a, w_ref[...].astype(BF16)) + b_ref[...]


def _ada(c_all, ada_w, ada_b):
    depth, d, n = ada_w.shape
    rows = c_all.shape[0]
    tn = min(n, 2048)
    return pl.pallas_call(
        _ada_kernel,
        grid=(depth, n // tn),
        in_specs=[pl.BlockSpec((rows, d), lambda l, j: (0, 0)),
                  pl.BlockSpec((None, d, tn), lambda l, j: (l, 0, j)),
                  pl.BlockSpec((None, 1, tn), lambda l, j: (l, 0, j))],
        out_specs=pl.BlockSpec((None, rows, tn), lambda l, j: (l, 0, j)),
        out_shape=jax.ShapeDtypeStruct((depth, rows, n), F32),
        compiler_params=_cparams("arbitrary", "arbitrary"),
    )(c_all, ada_w, ada_b.reshape(depth, 1, n))


def _norm_mod_kernel(x_ref, g_ref, sc_ref, sh_ref, o_ref):
    x = x_ref[...]
    ms = jnp.mean(x * x, axis=-1, keepdims=True)
    y = x * lax.rsqrt(ms + RMS_EPS) * g_ref[...]
    o_ref[...] = (y * (1.0 + sc_ref[...]) + sh_ref[...]).astype(o_ref.dtype)


def _rms_kernel(x_ref, g_ref, o_ref):
    x = x_ref[...]
    ms = jnp.mean(x * x, axis=-1, keepdims=True)
    o_ref[...] = (x * lax.rsqrt(ms + RMS_EPS) * g_ref[...]).astype(o_ref.dtype)


def _norm_mod(x3, g, sc3, sh3, rows_per_step, out_dtype=BF16):
    a, r, d = x3.shape
    tr = min(rows_per_step, r)
    g3 = g.reshape(1, 1, d)
    if sc3 is None:
        return pl.pallas_call(
            _rms_kernel,
            grid=(a, r // tr),
            in_specs=[pl.BlockSpec((1, tr, d), lambda i, j: (i, j, 0)),
                      pl.BlockSpec((1, 1, d), lambda i, j: (0, 0, 0))],
            out_specs=pl.BlockSpec((1, tr, d), lambda i, j: (i, j, 0)),
            out_shape=jax.ShapeDtypeStruct(x3.shape, out_dtype),
            compiler_params=_cparams("arbitrary", "arbitrary"),
        )(x3, g3)
    per_seq = sc3.shape[0] == a and sc3.shape[1] == 1
    if per_seq:
        mspec = pl.BlockSpec((1, 1, d), lambda i, j: (i, 0, 0))
    else:
        assert sc3.shape[0] == 1 and sc3.shape[1] == r and tr == r
        mspec = pl.BlockSpec((1, r, d), lambda i, j: (0, 0, 0))
    return pl.pallas_call(
        _norm_mod_kernel,
        grid=(a, r // tr),
        in_specs=[pl.BlockSpec((1, tr, d), lambda i, j: (i, j, 0)),
                  pl.BlockSpec((1, 1, d), lambda i, j: (0, 0, 0)),
                  mspec, mspec],
        out_specs=pl.BlockSpec((1, tr, d), lambda i, j: (i, j, 0)),
        out_shape=jax.ShapeDtypeStruct(x3.shape, out_dtype),
        compiler_params=_cparams("arbitrary", "arbitrary"),
    )(x3, g3, sc3, sh3)


def _mm_kernel(*refs, names, cache_w, kv_heads):
    r = dict(zip(names, refs))
    if cache_w:
        @pl.when(pl.program_id(1) == 0)
        def _():
            r["w_s"][...] = r["w"][...].astype(BF16)
        w = r["w_s"][...]
    else:
        w = r["w"][...].astype(BF16)

    def product(x_name, res_name, gate_name):
        acc = _dot(r[x_name][...], w)
        if "bias" in r:
            acc = acc + r["bias"][...]
        if res_name in r:
            acc = r[res_name][...] + r[gate_name][...] * acc
        return acc

    acc = product("x", "res", "gate")
    r["o"][...] = acc.astype(r["o"].dtype)
    if kv_heads:
        n_kv, hd = kv_heads
        rows = acc.shape[0]

        @pl.when(pl.program_id(0) == pl.num_programs(0) - 1)
        def _():
            for kv in range(n_kv):
                r["k"][pl.ds(kv, rows, stride=n_kv), :] = acc[:, kv * hd:(kv + 1) * hd]
                r["v"][pl.ds(kv, rows, stride=n_kv), :] = acc[:, (n_kv + kv) * hd:(n_kv + kv + 1) * hd]
    if "x2" in r:
        row_tile = pl.program_id(1) if cache_w else pl.program_id(0)
        last = (pl.num_programs(1) - 1) if cache_w else 0

        @pl.when(row_tile == last)
        def _():
            r["o2"][...] = product("x2", "res2", "gate2").astype(r["o2"].dtype)


def _mm(x, w, layer, *, bias=None, res=None, gate=None, out_dtype=F32, tm=1024, tn=1024, weights_outer=True,
        kv_heads=None, side=None):
    m, k = x.shape
    n = w.shape[2]
    tm, tn = min(tm, m), min(tn, n)
    assert m % tm == 0 and n % tn == 0
    if weights_outer:
        grid = (n // tn, m // tm)
        ij = lambda a, b: (b, a)
    else:
        grid = (m // tm, n // tn)
        ij = lambda a, b: (a, b)

    def spec(block, fn):
        return pl.BlockSpec(block, lambda a, b: fn(*ij(a, b)))

    names = ["x", "w"]
    in_specs = [spec((tm, k), lambda i, j: (i, 0)),
                spec((None, k, tn), lambda i, j: (layer, 0, j))]
    args = [x, w]
    if bias is not None:
        names.append("bias")
        in_specs.append(spec((None, 1, tn), lambda i, j: (layer, 0, j)))
        args.append(bias.reshape(bias.shape[0], 1, n))
    if res is not None:
        gs, gr, _ = gate.shape
        assert gr in (1, tm) and (m // tm) % gs == 0
        tiles_per_gate = (m // tm) // gs
        names += ["res", "gate"]
        in_specs.append(spec((tm, tn), lambda i, j: (i, j)))
        in_specs.append(spec((None, gr, tn), lambda i, j: (i // tiles_per_gate, 0, j)))
        args += [res, gate]
    if side is not None:
        x2, res2, gate2 = side
        m2 = x2.shape[0]
        nj_side = n // tn
        if weights_outer:
            side_col = lambda i, j: j
        else:
            side_col = lambda i, j: jnp.where(i == 0, j, nj_side - 1)
        names.append("x2")
        in_specs.append(spec((m2, k), lambda i, j: (0, 0)))
        args.append(x2)
        if res2 is not None:
            names += ["res2", "gate2"]
            in_specs += [spec((m2, tn), lambda i, j: (0, side_col(i, j)))] * 2
            args += [res2, gate2]
    names.append("o")
    out_specs = [spec((tm, tn), lambda i, j: (i, j))]
    out_shape = [jax.ShapeDtypeStruct((m, n), out_dtype)]
    if kv_heads:
        n_kv, hd = kv_heads
        nj = n // tn
        assert weights_outer and tn == 2 * n_kv * hd
        kv_spec = spec((tm * n_kv, hd), lambda i, j: (jnp.where(j == nj - 1, i, 0), 0))
        names += ["k", "v"]
        out_specs += [kv_spec, kv_spec]
        out_shape += [jax.ShapeDtypeStruct((m * n_kv, hd), F32)] * 2
    if side is not None:
        names.append("o2")
        out_specs.append(spec((m2, tn), lambda i, j: (0, side_col(i, j))))
        out_shape.append(jax.ShapeDtypeStruct((m2, n), out_dtype))
    if weights_outer:
        names.append("w_s")
    outs = pl.pallas_call(
        functools.partial(_mm_kernel, names=tuple(names), cache_w=weights_outer, kv_heads=kv_heads),
        grid=grid,
        in_specs=in_specs,
        out_specs=out_specs,
        out_shape=out_shape,
        scratch_shapes=[pltpu.VMEM((k, tn), BF16)] if weights_outer else [],
        compiler_params=_cparams("arbitrary", "arbitrary"),
    )(*args)
    return outs[0] if len(outs) == 1 else tuple(outs)


def _glu_kernel(x_ref, x2_ref, wa_ref, wg_ref, ba_ref, bg_ref, o_ref, o2_ref, wa_s, wg_s):
    @pl.when(pl.program_id(1) == 0)
    def _():
        wa_s[...] = wa_ref[...].astype(BF16)
        wg_s[...] = wg_ref[...].astype(BF16)

    def glu(x):
        a = _dot(x, wa_s[...]) + ba_ref[...]
        g = _dot(x, wg_s[...]) + bg_ref[...]
        return a * jax.nn.sigmoid(g)

    o_ref[...] = glu(x_ref[...])

    @pl.when(pl.program_id(1) == pl.num_programs(1) - 1)
    def _():
        o2_ref[...] = glu(x2_ref[...])


def _glu(x, x2, w1, b1, layer, *, tm=1024, tn=512):
    m, k = x.shape
    m2 = x2.shape[0]
    n = w1.shape[2] // 2
    tm, tn = min(tm, m), min(tn, n)
    nj = n // tn
    b3 = b1.reshape(b1.shape[0], 1, 2 * n)
    return pl.pallas_call(
        _glu_kernel,
        grid=(nj, m // tm),
        in_specs=[pl.BlockSpec((tm, k), lambda j, i: (i, 0)),
                  pl.BlockSpec((m2, k), lambda j, i: (0, 0)),
                  pl.BlockSpec((None, k, tn), lambda j, i: (layer, 0, j)),
                  pl.BlockSpec((None, k, tn), lambda j, i: (layer, 0, nj + j)),
                  pl.BlockSpec((None, 1, tn), lambda j, i: (layer, 0, j)),
                  pl.BlockSpec((None, 1, tn), lambda j, i: (layer, 0, nj + j))],
        out_specs=[pl.BlockSpec((tm, tn), lambda j, i: (i, j)),
                   pl.BlockSpec((m2, tn), lambda j, i: (0, j))],
        out_shape=[jax.ShapeDtypeStruct((m, n), F32), jax.ShapeDtypeStruct((m2, n), F32)],
        scratch_shapes=[pltpu.VMEM((k, tn), BF16), pltpu.VMEM((k, tn), BF16)],
        compiler_params=_cparams("arbitrary", "arbitrary"),
    )(x, x2, w1, w1, b3, b3)


def _layernorm_silu(y, lg, lb):
    mu = jnp.mean(y, axis=-1, keepdims=True)
    yc = y - mu
    var = jnp.mean(yc * yc, axis=-1, keepdims=True)
    return _silu(yc * lax.rsqrt(var + LN_EPS) * lg + lb)


def _conv_ln_prompt_kernel(u_ref, halo_ref, dw_ref, dwb_ref, lg_ref, lb_ref, o_ref, full_s, y_s,
                           *, tm, halo, cw, rc, lw):
    i = pl.program_id(1)
    d = u_ref.shape[-1]
    full_s[halo:halo + tm, :] = u_ref[...]

    @pl.when(i == 0)
    def _():
        full_s[0:halo, :] = jnp.zeros((halo, d), F32)

    @pl.when(i > 0)
    def _():
        full_s[0:halo, :] = halo_ref[...]

    off = halo - (cw - 1)
    win = rc + halo
    sub = V7X_SUBLANES

    def body(r, carry):
        r0 = pl.multiple_of(r * rc, rc)
        for c0 in range(0, d, lw):
            window = full_s[pl.ds(r0, win), c0:c0 + lw]
            acc = jnp.zeros((rc, lw), F32) + dwb_ref[:, c0:c0 + lw]
            for s in range(sub):
                taps = [k for k in range(cw) if (off + k) % sub == s]
                if not taps:
                    continue
                shifted = window if s == 0 else pltpu.roll(window, win - s, 0)
                for k in taps:
                    a = off + k - s
                    acc = acc + dw_ref[k:k + 1, c0:c0 + lw] * shifted[a:a + rc]
            y_s[pl.ds(r0, rc), c0:c0 + lw] = acc
        return carry
    lax.fori_loop(0, tm // rc, body, 0)

    o_ref[...] = _layernorm_silu(y_s[...], lg_ref[...], lb_ref[...]).astype(o_ref.dtype)


def _conv_ln_prompt(u3, dw, dwb, lg, lb, *, tm=256):
    bsz, t, d = u3.shape
    cw = dw.shape[0]
    halo = 32
    assert cw - 1 <= halo and t % tm == 0 and tm % halo == 0
    hb = tm // halo
    lw = min(d, V7X_LANES)
    rc = min(tm, 4 * halo)
    return pl.pallas_call(
        functools.partial(_conv_ln_prompt_kernel, tm=tm, halo=halo, cw=cw, rc=rc, lw=lw),
        grid=(bsz, t // tm),
        in_specs=[pl.BlockSpec((None, tm, d), lambda b, i: (b, i, 0)),
                  pl.BlockSpec((None, halo, d), lambda b, i: (b, jnp.maximum(i * hb - 1, 0), 0)),
                  pl.BlockSpec((cw, d), lambda b, i: (0, 0)),
                  pl.BlockSpec((1, d), lambda b, i: (0, 0)),
                  pl.BlockSpec((1, d), lambda b, i: (0, 0)),
                  pl.BlockSpec((1, d), lambda b, i: (0, 0))],
        out_specs=pl.BlockSpec((None, tm, d), lambda b, i: (b, i, 0)),
        out_shape=jax.ShapeDtypeStruct((bsz, t, d), BF16),
        scratch_shapes=[pltpu.VMEM((tm + halo, d), F32), pltpu.VMEM((tm, d), F32)],
        compiler_params=_cparams("arbitrary", "arbitrary"),
    )(u3, u3, dw, dwb.reshape(1, d), lg.reshape(1, d), lb.reshape(1, d))


def _conv_ln_sample_kernel(st_ref, u_ref, dw_ref, dwb_ref, lg_ref, lb_ref, o_ref, nst_ref, *, cw, lw):
    t_new, _, d = u_ref.shape
    n_st = cw - 1

    def row(idx, c0):
        if idx < n_st:
            return st_ref[idx, :, c0:c0 + lw]
        return u_ref[idx - n_st, :, c0:c0 + lw]

    for t in range(t_new):
        parts = []
        for c0 in range(0, d, lw):
            acc = dw_ref[0:1, c0:c0 + lw] * row(t, c0)
            for k in range(1, cw):
                acc = acc + dw_ref[k:k + 1, c0:c0 + lw] * row(t + k, c0)
            parts.append(acc + dwb_ref[:, c0:c0 + lw])
        y = jnp.concatenate(parts, axis=-1) if len(parts) > 1 else parts[0]
        o_ref[t] = _layernorm_silu(y, lg_ref[...], lb_ref[...]).astype(o_ref.dtype)
    for r in range(n_st):
        idx = t_new + r
        nst_ref[r] = st_ref[idx] if idx < n_st else u_ref[idx - n_st]


def _conv_ln_sample(st_tm, u_tm, dw, dwb, lg, lb):
    n_st, bsz, d = st_tm.shape
    t_new = u_tm.shape[0]
    cw = dw.shape[0]
    lw = min(d, 512)
    full = lambda shape: pl.BlockSpec(shape, lambda i: (0,) * len(shape))
    return pl.pallas_call(
        functools.partial(_conv_ln_sample_kernel, cw=cw, lw=lw),
        grid=(1,),
        in_specs=[full((n_st, bsz, d)), full((t_new, bsz, d)), full((cw, d)), full((1, d)), full((1, d)),
                  full((1, d))],
        out_specs=[full((t_new, bsz, d)), full((n_st, bsz, d))],
        out_shape=[jax.ShapeDtypeStruct((t_new, bsz, d), BF16),
                   jax.ShapeDtypeStruct((n_st, bsz, d), F32)],
        compiler_params=_cparams("arbitrary"),
    )(st_tm, u_tm, dw, dwb.reshape(1, d), lg.reshape(1, d), lb.reshape(1, d))


def _ffn_up_kernel(x_ref, x2_ref, prev2_ref, wg_ref, wv_ref, dw_ref, dwb_ref, a_ref, st_ref, a2_ref, st2_ref,
                   wg_s, wv_s, carry_s, *, tiles_per_seq, bsz2):
    i = pl.program_id(1)

    @pl.when(i == 0)
    def _():
        wg_s[...] = wg_ref[...].astype(BF16)
        wv_s[...] = wv_ref[...].astype(BF16)

    @pl.when(i % tiles_per_seq == 0)
    def _():
        carry_s[...] = jnp.zeros(carry_s.shape, F32)

    x = x_ref[...]
    g = _dot(x, wg_s[...])
    v = _dot(x, wv_s[...])
    tm = g.shape[0]
    row = lax.broadcasted_iota(jnp.int32, g.shape, 0)
    prev1 = carry_s[V7X_SUBLANES - 1:V7X_SUBLANES, :]
    prev2 = carry_s[V7X_SUBLANES - 2:V7X_SUBLANES - 1, :]
    g1 = jnp.where(row == 0, prev1, pltpu.roll(g, 1, 0))
    g2 = jnp.where(row == 0, prev2, jnp.where(row == 1, prev1, pltpu.roll(g, 2, 0)))
    gc = dw_ref[0:1, :] * g2 + dw_ref[1:2, :] * g1 + dw_ref[2:3, :] * g + dwb_ref[...]
    a_ref[...] = (_silu(gc) * v).astype(a_ref.dtype)
    carry_s[...] = g[tm - V7X_SUBLANES:tm, :]
    st_ref[...] = g[tm - 2:tm, :]

    @pl.when(i == pl.num_programs(1) - 1)
    def _():
        x2 = x2_ref[...]
        g_s = _dot(x2, wg_s[...])
        v_s = _dot(x2, wv_s[...])
        rows = g_s.shape[0]
        full = jnp.concatenate([prev2_ref[...], g_s], axis=0)
        gc_s = (dw_ref[0:1, :] * full[0:rows] + dw_ref[1:2, :] * full[bsz2:bsz2 + rows]
                + dw_ref[2:3, :] * full[2 * bsz2:2 * bsz2 + rows] + dwb_ref[...])
        a2_ref[...] = (_silu(gc_s) * v_s).astype(a2_ref.dtype)
        st2_ref[...] = full[rows:rows + 2 * bsz2]


def _ffn_up(x, x2, prev2_tm, w_up, dw, dwb, layer, bsz, bsz2, *, tm=1024, tn=512):
    m, k = x.shape
    m2 = x2.shape[0]
    n = w_up.shape[2] // 2
    t = m // bsz
    tm, tn = min(tm, t), min(tn, n)
    assert t % tm == 0 and n % tn == 0 and dw.shape[1] == 3 and m2 >= 2 * bsz2
    nj = n // tn
    tiles_per_seq = t // tm
    return pl.pallas_call(
        functools.partial(_ffn_up_kernel, tiles_per_seq=tiles_per_seq, bsz2=bsz2),
        grid=(nj, m // tm),
        in_specs=[pl.BlockSpec((tm, k), lambda j, i: (i, 0)),
                  pl.BlockSpec((m2, k), lambda j, i: (0, 0)),
                  pl.BlockSpec((2 * bsz2, tn), lambda j, i: (0, j)),
                  pl.BlockSpec((None, k, tn), lambda j, i: (layer, 0, j)),
                  pl.BlockSpec((None, k, tn), lambda j, i: (layer, 0, nj + j)),
                  pl.BlockSpec((None, 3, tn), lambda j, i: (layer, 0, j)),
                  pl.BlockSpec((None, 1, tn), lambda j, i: (layer, 0, j))],
        out_specs=[pl.BlockSpec((tm, tn), lambda j, i: (i, j)),
                   pl.BlockSpec((None, 2, tn), lambda j, i: (i // tiles_per_seq, 0, j)),
                   pl.BlockSpec((m2, tn), lambda j, i: (0, j)),
                   pl.BlockSpec((2 * bsz2, tn), lambda j, i: (0, j))],
        out_shape=[jax.ShapeDtypeStruct((m, n), BF16),
                   jax.ShapeDtypeStruct((bsz, 2, n), F32),
                   jax.ShapeDtypeStruct((m2, n), BF16),
                   jax.ShapeDtypeStruct((2 * bsz2, n), F32)],
        scratch_shapes=[pltpu.VMEM((k, tn), BF16), pltpu.VMEM((k, tn), BF16),
                        pltpu.VMEM((V7X_SUBLANES, tn), F32)],
        compiler_params=_cparams("arbitrary", "arbitrary"),
    )(x, x2, prev2_tm, w_up, w_up, dw, dwb.reshape(dwb.shape[0], 1, n))


def _swa_prompt_kernel(sink_ref, q_ref, kc_ref, kp_ref, vc_ref, vp_ref, bias_ref, o_ref, *, n_heads, group, hd,
                       scale):
    n = pl.program_id(1)
    w = q_ref.shape[0]
    kcat = jnp.concatenate([kp_ref[...], kc_ref[...]], axis=0).astype(BF16)
    vcat = jnp.concatenate([vp_ref[...], vc_ref[...]], axis=0).astype(BF16)
    col = lax.broadcasted_iota(jnp.int32, (w, 2 * w), 1)
    keep = (col >= w) | (n > 0)
    for h in range(n_heads):
        kv = h // group
        q = q_ref[:, h * hd:(h + 1) * hd].astype(BF16)
        s = _dot_nt(q, kcat[:, kv * hd:(kv + 1) * hd]) * scale + bias_ref[h]
        s = jnp.where(keep, s, NEG)
        sink = sink_ref[h]
        m = jnp.maximum(jnp.max(s, axis=-1, keepdims=True), sink)
        p = jnp.exp(s - m)
        l = jnp.sum(p, axis=-1, keepdims=True) + jnp.exp(sink - m)
        o = _dot(p.astype(BF16), vcat[:, kv * hd:(kv + 1) * hd]) / l
        o_ref[:, h * hd:(h + 1) * hd] = o.astype(o_ref.dtype)


def _swa_prompt(qkv, sinks, bias, bsz, n_heads, n_kv, hd, window):
    m = qkv.shape[0]
    t = m // bsz
    nb = t // window
    qw, kw = n_heads * hd, n_kv * hd
    assert qw % kw == 0 and t % window == 0
    kcol, vcol = qw // kw, qw // kw + 1
    return pl.pallas_call(
        functools.partial(_swa_prompt_kernel, n_heads=n_heads, group=n_heads // n_kv, hd=hd, scale=hd ** -0.5),
        grid=(bsz, nb),
        in_specs=[pl.BlockSpec(memory_space=pltpu.SMEM),
                  pl.BlockSpec((window, qw), lambda b, n: (b * nb + n, 0)),
                  pl.BlockSpec((window, kw), lambda b, n: (b * nb + n, kcol)),
                  pl.BlockSpec((window, kw), lambda b, n: (b * nb + jnp.maximum(n - 1, 0), kcol)),
                  pl.BlockSpec((window, kw), lambda b, n: (b * nb + n, vcol)),
                  pl.BlockSpec((window, kw), lambda b, n: (b * nb + jnp.maximum(n - 1, 0), vcol)),
                  pl.BlockSpec((n_heads, window, 2 * window), lambda b, n: (0, 0, 0))],
        out_specs=pl.BlockSpec((window, qw), lambda b, n: (b * nb + n, 0)),
        out_shape=jax.ShapeDtypeStruct((m, qw), BF16),
        compiler_params=_cparams("arbitrary", "arbitrary"),
    )(sinks, qkv, qkv, qkv, qkv, qkv, bias)


def _swa_sample_kernel(q_ref, bk_ref, bv_ref, kn_ref, vn_ref, bias_ref, own_ref, sink_ref, o_ref,
                       *, n_kv, group, tp, scale):
    cols, hd = q_ref.shape
    keys = bk_ref.shape[0] // n_kv
    qb = q_ref[...].astype(BF16)
    col_kv = lax.broadcasted_iota(jnp.int32, (keys, cols), 1) // (group * tp)

    def by_head(ref):
        return jnp.concatenate([ref[pl.ds(kv, keys, stride=n_kv), :] for kv in range(n_kv)], axis=0)

    full = _dot_nt(by_head(bk_ref).astype(BF16), qb)
    s = full[0:keys]
    for kv in range(1, n_kv):
        s = jnp.where(col_kv == kv, full[kv * keys:(kv + 1) * keys], s)
    s = s * scale + bias_ref[...]
    s_own = _dot_nt(kn_ref[...].astype(BF16), qb) * scale + own_ref[...]
    sink = sink_ref[...]
    m = jnp.maximum(jnp.maximum(jnp.max(s, axis=0, keepdims=True), jnp.max(s_own, axis=0, keepdims=True)), sink)
    p = jnp.exp(s - m)
    p_own = jnp.exp(s_own - m)
    l = jnp.sum(p, axis=0, keepdims=True) + jnp.sum(p_own, axis=0, keepdims=True) + jnp.exp(sink - m)
    spread = jnp.concatenate([jnp.where(col_kv == kv, p, 0.0) for kv in range(n_kv)], axis=0)
    acc = (_dot(by_head(bv_ref).T.astype(BF16), spread.astype(BF16))
           + _dot(vn_ref[...].T.astype(BF16), p_own.astype(BF16)))
    o_ref[...] = (acc / l).T.astype(o_ref.dtype)


def _swa_sample(q_cols, buf_k, buf_v, k_new, v_new, bias, own, sink_row, *, n_kv, group, tp):
    dbs, cols, hd = q_cols.shape
    per_seq = lambda a: pl.BlockSpec((None,) + a.shape[1:], lambda b: (b, 0, 0))
    shared = lambda a: pl.BlockSpec(a.shape, lambda b: (0, 0))
    return pl.pallas_call(
        functools.partial(_swa_sample_kernel, n_kv=n_kv, group=group, tp=tp, scale=hd ** -0.5),
        grid=(dbs,),
        in_specs=[per_seq(q_cols), per_seq(buf_k), per_seq(buf_v), per_seq(k_new), per_seq(v_new),
                  shared(bias), shared(own), shared(sink_row)],
        out_specs=pl.BlockSpec((None, cols, hd), lambda b: (b, 0, 0)),
        out_shape=jax.ShapeDtypeStruct((dbs, cols, hd), BF16),
        compiler_params=_cparams("arbitrary"),
    )(q_cols, buf_k, buf_v, k_new, v_new, bias, own, sink_row)


def _block_means_kernel(k_ref, o_ref):
    o_ref[...] = jnp.mean(k_ref[...], axis=0, keepdims=True)


def _block_means(qkv, n_blocks_total, kcol, kw):
    return pl.pallas_call(
        _block_means_kernel,
        grid=(n_blocks_total,),
        in_specs=[pl.BlockSpec((MOBA_BLOCK, kw), lambda i: (i, kcol))],
        out_specs=pl.BlockSpec((None, 1, kw), lambda i: (i, 0, 0)),
        out_shape=jax.ShapeDtypeStruct((n_blocks_total, 1, kw), F32),
        compiler_params=_cparams("arbitrary"),
    )(qkv)


def _rank_rows(gate, blk, n_rows):
    rank = jnp.zeros(gate.shape, jnp.int32)
    for m in range(n_rows):
        gm = gate[m:m + 1, :]
        beats = (gm > gate) | ((gm == gate) & (blk > m))
        rank = rank + jnp.where(beats, 1, 0)
    return rank


def _moba_prompt_kernel(q_ref, k_ref, v_ref, means_ref, bias_ref, far_ref, o_ref, qs, sel_s, m_s, l_s, acc_s,
                        *, n_kv, group, hd, scale, n_near, nb):
    qb = pl.program_id(1)
    kb = pl.program_id(2)
    blk_rows = q_ref.shape[0]
    cols = group * blk_rows

    @pl.when(kb == 0)
    def _():
        blk = lax.broadcasted_iota(jnp.int32, (nb, cols), 0)
        for kv in range(n_kv):
            qg = jnp.concatenate([q_ref[:, (kv * group + g) * hd:(kv * group + g + 1) * hd]
                                  for g in range(group)], axis=0)
            qs[kv] = (qg * (scale * LOG2E)).astype(BF16)
            gate = _dot_nt(means_ref[:, kv * hd:(kv + 1) * hd], qg, precision=lax.Precision.HIGHEST)
            valid = blk < qb
            gate = jnp.where(valid, gate, -jnp.inf)
            rank = _rank_rows(gate, blk, nb)
            sel_s[kv] = jnp.where(valid & (rank < MOBA_TOPK), 1.0, 0.0)
            m_s[kv] = jnp.full((1, cols), NEG, F32)
            l_s[kv] = jnp.zeros((1, cols), F32)
            acc_s[kv] = jnp.zeros((hd, cols), F32)

    def attend_all(adds_of):
        chunks = [(kv, g) for kv in range(n_kv) for g in range(group)]
        col = lambda g: slice(g * blk_rows, (g + 1) * blk_rows)
        scores = {}
        for kv in range(n_kv):
            k = k_ref[:, kv * hd:(kv + 1) * hd].astype(BF16)
            adds = adds_of(kv)
            for g in range(group):
                s = _dot_nt(k, qs[kv, col(g), :])
                for a in adds:
                    s = s + a[:, col(g)]
                scores[kv, g] = s
        alphas, probs = {}, {}
        for kv, g in chunks:
            m_prev = m_s[kv, :, col(g)]
            m_new = jnp.maximum(m_prev, jnp.max(scores[kv, g], axis=0, keepdims=True))
            alpha = jnp.exp2(m_prev - m_new)
            p = jnp.exp2(scores[kv, g] - m_new)
            l_s[kv, :, col(g)] = alpha * l_s[kv, :, col(g)] + jnp.sum(p, axis=0, keepdims=True)
            m_s[kv, :, col(g)] = m_new
            alphas[kv, g], probs[kv, g] = alpha, p.astype(BF16)
        for kv in range(n_kv):
            vt = v_ref[:, kv * hd:(kv + 1) * hd].T.astype(BF16)
            for g in range(group):
                acc_s[kv, :, col(g)] = alphas[kv, g] * acc_s[kv, :, col(g)] + _dot(vt, probs[kv, g])

    delta = qb - kb

    @pl.when((kb <= qb) & (delta < n_near))
    def _():
        def adds_of(kv):
            chosen = (sel_s[kv, pl.ds(kb, 1), :] > 0.5) | (delta == 0)
            return bias_ref[kv], jnp.where(chosen, 0.0, NEG)
        attend_all(adds_of)

    @pl.when(delta >= n_near)
    def _():
        def adds_of(kv):
            chosen = sel_s[kv, pl.ds(kb, 1), :] > 0.5
            return (jnp.where(chosen, far_ref[kv], NEG),)
        attend_all(adds_of)

    @pl.when(kb == qb)
    def _():
        for kv in range(n_kv):
            o = acc_s[kv] / l_s[kv]
            for g in range(group):
                h = kv * group + g
                o_ref[:, h * hd:(h + 1) * hd] = o[:, g * blk_rows:(g + 1) * blk_rows].T.astype(o_ref.dtype)


def _moba_prompt(qkv, means, bias_t, far_rows, bsz, n_heads, n_kv, hd):
    m = qkv.shape[0]
    t = m // bsz
    nb = t // MOBA_BLOCK
    group = n_heads // n_kv
    qw, kw = n_heads * hd, n_kv * hd
    kcol, vcol = qw // kw, qw // kw + 1
    n_near = bias_t.shape[0]
    cols = group * MOBA_BLOCK
    kv_idx = lambda b, qb, kb: b * nb + jnp.minimum(kb, qb)
    return pl.pallas_call(
        functools.partial(_moba_prompt_kernel, n_kv=n_kv, group=group, hd=hd, scale=hd ** -0.5,
                          n_near=n_near, nb=nb),
        grid=(bsz, nb, nb),
        in_specs=[pl.BlockSpec((MOBA_BLOCK, qw), lambda b, qb, kb: (b * nb + qb, 0)),
                  pl.BlockSpec((MOBA_BLOCK, kw), lambda b, qb, kb: (kv_idx(b, qb, kb), kcol)),
                  pl.BlockSpec((MOBA_BLOCK, kw), lambda b, qb, kb: (kv_idx(b, qb, kb), vcol)),
                  pl.BlockSpec((None, nb, kw), lambda b, qb, kb: (b, 0, 0)),
                  pl.BlockSpec((None, n_kv, MOBA_BLOCK, cols),
                               lambda b, qb, kb: (jnp.clip(qb - kb, 0, n_near - 1), 0, 0, 0)),
                  pl.BlockSpec((n_kv, 1, cols), lambda b, qb, kb: (0, 0, 0))],
        out_specs=pl.BlockSpec((MOBA_BLOCK, qw), lambda b, qb, kb: (b * nb + qb, 0)),
        out_shape=jax.ShapeDtypeStruct((m, qw), BF16),
        scratch_shapes=[pltpu.VMEM((n_kv, cols, hd), BF16),
                        pltpu.VMEM((n_kv, nb, cols), F32),
                        pltpu.VMEM((n_kv, 1, cols), F32),
                        pltpu.VMEM((n_kv, 1, cols), F32),
                        pltpu.VMEM((n_kv, hd, cols), F32)],
        compiler_params=_cparams("arbitrary", "arbitrary", "arbitrary"),
    )(qkv, qkv, qkv, means, bias_t, far_rows)


_MEAN_PAGES_PER_STEP = 16
_ATTN_PAGES_PER_STEP = 8


def _pool_means_kernel(pt_ref, *refs, ppb, n_kv):
    del pt_ref
    o_ref = refs[-1]
    pages = refs[:-1]
    hd = pages[0].shape[1]
    sub = V7X_SUBLANES
    reps = sub // n_kv
    folded = []
    for blk in range(len(pages) // ppb):
        tot = None
        for p in range(ppb):
            part = jnp.sum(pages[blk * ppb + p][...].reshape(-1, sub, hd), axis=0)
            tot = part if tot is None else tot + part
        full = tot
        for i in range(1, reps):
            full = full + pltpu.roll(tot, i * n_kv, 0)
        folded.append(full)
    row_grp = lax.broadcasted_iota(jnp.int32, (sub, hd), 0) // n_kv
    tiles = []
    for a in range(0, len(folded), reps):
        tile = folded[a]
        for i in range(1, reps):
            tile = jnp.where(row_grp == i, folded[a + i], tile)
        tiles.append(tile)
    n_keys = ppb * pages[0].shape[0] // n_kv
    o_ref[...] = jnp.concatenate(tiles, axis=0) * (1.0 / n_keys)


def _pool_means(pool, pt_flat, page_off, dbs, n_pages, n_kv):
    rows, hd = pool.shape[1], pool.shape[2]
    ppb = MOBA_BLOCK // (rows // n_kv)
    pps = min(_MEAN_PAGES_PER_STEP, n_pages)
    blocks = pps // ppb
    assert n_pages % pps == 0 and pps % ppb == 0 and V7X_SUBLANES % n_kv == 0
    assert (blocks * n_kv) % V7X_SUBLANES == 0

    def page_spec(i):
        return pl.BlockSpec((None, rows, hd),
                            lambda b, s, pt: (pt[b * n_pages + s * pps + i] + page_off, 0, 0))

    grid_spec = pltpu.PrefetchScalarGridSpec(
        num_scalar_prefetch=1,
        grid=(dbs, n_pages // pps),
        in_specs=[page_spec(i) for i in range(pps)],
        out_specs=pl.BlockSpec((None, blocks * n_kv, hd), lambda b, s, pt: (b, s, 0)),
    )
    return pl.pallas_call(
        functools.partial(_pool_means_kernel, ppb=ppb, n_kv=n_kv),
        grid_spec=grid_spec,
        out_shape=jax.ShapeDtypeStruct((dbs, (n_pages // ppb) * n_kv, hd), F32),
        compiler_params=_cparams("arbitrary", "arbitrary"),
    )(pt_flat, *([pool] * pps))


def _cols_expand_kernel(tab_ref, idx_ref, o_ref, *, buckets, mult):
    idx = idx_ref[...]
    acc = jnp.full(idx.shape, NEG, F32)
    for b in buckets:
        acc = jnp.where(idx == b, tab_ref[b:b + 1, :] * mult, acc)
    o_ref[...] = acc


def _cols_expand(tab_cols, idx_np, tile_rows, mult=1.0):
    rows, cols = idx_np.shape
    buckets = tuple(int(b) for b in np.unique(idx_np) if b >= 0)
    return pl.pallas_call(
        functools.partial(_cols_expand_kernel, buckets=buckets, mult=mult),
        grid=(rows // tile_rows,),
        in_specs=[pl.BlockSpec(tab_cols.shape, lambda i: (0, 0)),
                  pl.BlockSpec((tile_rows, cols), lambda i: (i, 0))],
        out_specs=pl.BlockSpec((tile_rows, cols), lambda i: (i, 0)),
        out_shape=jax.ShapeDtypeStruct((rows, cols), F32),
        compiler_params=_cparams("arbitrary"),
    )(tab_cols, jnp.asarray(idx_np))


def _moba_sample_kernel(pt_ref, *refs, pps, ppb, n_kv, group, tp, n_blk, n_pages, far_cls, scale):
    del pt_ref
    k_pages = refs[:pps]
    v_pages = refs[pps:2 * pps]
    q_ref, means_ref, bias_ref, own_ref, kn_ref, vn_ref, o_ref, sel_s, m_s, l_s, acc_s = refs[2 * pps:]
    step = pl.program_id(1)
    cols, hd = q_ref.shape

    @pl.when(step == 0)
    def _():
        q = q_ref[...]
        blk = lax.broadcasted_iota(jnp.int32, (n_blk, cols), 0)
        col_kv = lax.broadcasted_iota(jnp.int32, (n_blk, cols), 1) // (group * tp)
        gate = jnp.zeros((n_blk, cols), F32)
        for kv in range(n_kv):
            g = _dot_nt(means_ref[kv * n_blk:(kv + 1) * n_blk, :], q, precision=lax.Precision.HIGHEST)
            gate = jnp.where(col_kv == kv, g, gate)
        rank = _rank_rows(gate, blk, n_blk)
        sel_s[...] = jnp.where(rank < MOBA_TOPK, 0.0, NEG)
        m_s[...] = jnp.full(m_s.shape, NEG, F32)
        l_s[...] = jnp.zeros(l_s.shape, F32)
        acc_s[...] = jnp.zeros(acc_s.shape, F32)

    qb = (q_ref[...] * (scale * LOG2E)).astype(BF16)
    page_keys = k_pages[0].shape[0] // n_kv
    col_kv = lax.broadcasted_iota(jnp.int32, (page_keys, cols), 1) // (group * tp)

    def accumulate(scores, values, expand):
        m_prev = m_s[...]
        m_new = m_prev
        for s in scores:
            m_new = jnp.maximum(m_new, jnp.max(s, axis=0, keepdims=True))
        alpha = jnp.exp2(m_prev - m_new)
        l_new = alpha * l_s[...]
        acc = alpha * acc_s[...]
        for s, v in zip(scores, values):
            p = jnp.exp2(s - m_new)
            l_new = l_new + jnp.sum(p, axis=0, keepdims=True)
            acc = acc + _dot(v.T.astype(BF16), expand(p).astype(BF16))
        m_s[...] = m_new
        l_s[...] = l_new
        acc_s[...] = acc

    def by_head(ref):
        return jnp.concatenate([ref[pl.ds(kv, page_keys, stride=n_kv), :] for kv in range(n_kv)], axis=0)

    def spread(p):
        return jnp.concatenate([jnp.where(col_kv == kv, p, 0.0) for kv in range(n_kv)], axis=0)

    scores, values = [], []
    for i in range(pps):
        page = step * pps + i
        cls = jnp.minimum(n_pages - 1 - page, far_cls)
        full = _dot_nt(by_head(k_pages[i]).astype(BF16), qb)
        s = full[0:page_keys]
        for kv in range(1, n_kv):
            s = jnp.where(col_kv == kv, full[kv * page_keys:(kv + 1) * page_keys], s)
        scores.append(s + bias_ref[cls] + sel_s[pl.ds(page // ppb, 1), :])
        values.append(by_head(v_pages[i]))
    accumulate(scores, values, spread)

    @pl.when(step == pl.num_programs(1) - 1)
    def _():
        s_own = _dot_nt(kn_ref[...].astype(BF16), qb) + own_ref[...]
        accumulate([s_own], [vn_ref[...]], lambda p: p)
        o_ref[...] = (acc_s[...] / l_s[...]).T


def _moba_sample(q_cols, means_t, k_new, v_new, pool_k, pool_v, pt_flat, page_off, bias_pages, bias_own, *,
                 n_kv, group, tp, n_pages):
    dbs, cols, hd = q_cols.shape
    rows = pool_k.shape[1]
    ppb = MOBA_BLOCK // (rows // n_kv)
    n_blk = n_pages // ppb
    pps = min(_ATTN_PAGES_PER_STEP, n_pages)
    assert n_pages % pps == 0
    far_cls = bias_pages.shape[0] - 1
    n_own = k_new.shape[1]

    def page_spec(i):
        return pl.BlockSpec((None, rows, hd),
                            lambda b, s, pt: (pt[b * n_pages + s * pps + i] + page_off, 0, 0))

    per_seq = lambda r: pl.BlockSpec((None, r, hd), lambda b, s, pt: (b, 0, 0))
    in_specs = ([page_spec(i) for i in range(pps)] * 2
                + [per_seq(cols), per_seq(n_kv * n_blk),
                   pl.BlockSpec(bias_pages.shape, lambda b, s, pt: (0, 0, 0)),
                   pl.BlockSpec(bias_own.shape, lambda b, s, pt: (0, 0)),
                   per_seq(n_own), per_seq(n_own)])
    grid_spec = pltpu.PrefetchScalarGridSpec(
        num_scalar_prefetch=1,
        grid=(dbs, n_pages // pps),
        in_specs=in_specs,
        out_specs=pl.BlockSpec((None, cols, hd), lambda b, s, pt: (b, 0, 0)),
        scratch_shapes=[pltpu.VMEM((n_blk, cols), F32), pltpu.VMEM((1, cols), F32),
                        pltpu.VMEM((1, cols), F32), pltpu.VMEM((hd, cols), F32)],
    )
    return pl.pallas_call(
        functools.partial(_moba_sample_kernel, pps=pps, ppb=ppb, n_kv=n_kv, group=group, tp=tp, n_blk=n_blk,
                          n_pages=n_pages, far_cls=far_cls, scale=hd ** -0.5),
        grid_spec=grid_spec,
        out_shape=jax.ShapeDtypeStruct((dbs, cols, hd), F32),
        compiler_params=_cparams("arbitrary", "arbitrary"),
    )(pt_flat, *([pool_k] * pps), *([pool_v] * pps), q_cols, means_t, bias_pages, bias_own, k_new, v_new)


def _swa_prompt_bias_idx(window, nbk):
    qi = np.arange(window)[:, None]
    kj = np.arange(2 * window)[None, :]
    dist = qi + window - kj
    return np.where((dist >= 0) & (dist < window), _bucket_np(dist, nbk), -1).astype(np.int32)


def _swa_sample_bias_idx(window, n_heads, tp, t_new, nbk):
    t = np.arange(n_heads * tp) % tp
    r = np.arange(window)
    dist = t[None, :] + window - r[:, None]
    idx = np.where((dist >= 0) & (dist < window), _bucket_np(dist, nbk), -1)
    return np.where(t[None, :] < t_new, idx, 0).astype(np.int32)


def _moba_prompt_bias_idx(n_near, nbk):
    j = np.arange(MOBA_BLOCK)[:, None]
    qi = np.arange(MOBA_BLOCK)[None, :]
    tiles = []
    for delta in range(n_near):
        dist = delta * MOBA_BLOCK + qi - j
        tiles.append(np.where(dist >= 0, _bucket_np(dist, nbk), -1))
    return np.concatenate(tiles, axis=0).astype(np.int32)


def _moba_sample_page_idx(far_cls, page, n_heads, tp, t_new, nbk):
    cols = n_heads * tp
    t = np.arange(cols) % tp
    r = np.arange(page)
    tiles = []
    for c in range(far_cls + 1):
        dist = (c + 1) * page + t[None, :] - r[:, None]
        b = _bucket_np(dist, nbk) if c < far_cls else np.full(dist.shape, nbk - 1, np.int32)
        tiles.append(np.where(t[None, :] < t_new, b, 0))
    return np.concatenate(tiles, axis=0).astype(np.int32)


def _moba_sample_own_idx(n_kv, n_heads, group, tp, t_new, nbk):
    cols = n_heads * tp
    h, t = np.arange(cols) // tp, np.arange(cols) % tp
    tk, kvc = np.arange(tp * n_kv) // n_kv, np.arange(tp * n_kv) % n_kv
    ok = ((kvc[:, None] == (h // group)[None, :]) & (tk[:, None] <= t[None, :]) & (t[None, :] < t_new))
    return np.where(ok, _bucket_np(t[None, :] - tk[:, None], nbk), -1).astype(np.int32)


def kernel(x_prompt, x_sample, c_prompt, c_sample, state_conv, cache_swa_k, cache_swa_v, cache_moba_k, cache_moba_v, page_table, state_ffn, ada_w, ada_b, norm_mix, norm_ffn, norm_final, rel_bias, conv_w1, conv_b1, conv_dw, conv_dw_b, conv_ln_g, conv_ln_b, conv_w2, conv_b2, swa_wqkv, swa_wo, swa_sinks, moba_wqkv, moba_wo, ffn_w_up, ffn_dw, ffn_dw_b, ffn_w_down):
    bsz, seq, d = x_prompt.shape
    dbs, t_new, _ = x_sample.shape
    depth = ada_w.shape[0]
    window, n_kv, hd = cache_swa_k.shape[2], cache_swa_k.shape[3], cache_swa_k.shape[4]
    nbk, n_heads = rel_bias.shape
    group = n_heads // n_kv
    qw, kw = n_heads * hd, n_kv * hd
    cw = conv_dw.shape[1]
    d_ff = ffn_dw.shape[2]
    n_pool, page_size = cache_moba_k.shape[1], cache_moba_k.shape[2]
    n_pages = page_table.shape[1]
    ppb = MOBA_BLOCK // page_size
    n_blk_past = n_pages // ppb
    tp = V7X_SUBLANES
    assert t_new <= tp and n_pages % ppb == 0 and n_blk_past >= MOBA_TOPK and seq % MOBA_BLOCK == 0
    n_mixers = 3

    n_c = bsz + dbs
    c_rows = -(-n_c // V7X_SUBLANES) * V7X_SUBLANES
    c_all = jnp.concatenate([c_prompt, c_sample, jnp.zeros((c_rows - n_c, d), F32)], axis=0)
    mod = _ada(c_all, ada_w, ada_b)

    def mods(layer, which):
        chunk = mod[layer, :, which * d:(which + 1) * d]
        return chunk[:bsz], chunk[bsz:bsz + dbs]

    d_sat = _saturation_distance(nbk)
    n_near = -(-(d_sat + MOBA_BLOCK - 1) // MOBA_BLOCK)
    swa_bias_p = swa_bias_s = swa_bias_own = moba_bias_p = moba_far = moba_bias_sel = moba_bias_own = None
    tab_cols = jnp.repeat(rel_bias, tp, axis=1)
    own_idx = _moba_sample_own_idx(n_kv, n_heads, group, tp, t_new, nbk)
    if depth > 1:
        swa_bias_p = _bias_expand(rel_bias, _swa_prompt_bias_idx(window, nbk), window,
                                  (n_heads, window, 2 * window), (None, window, 2 * window),
                                  lambda h: (h, 0, 0))
        swa_bias_s = _cols_expand(tab_cols, _swa_sample_bias_idx(window, n_heads, tp, t_new, nbk), window)
        swa_bias_own = _cols_expand(tab_cols, own_idx, tp * n_kv)
    if depth > 2:
        cols = group * MOBA_BLOCK
        moba_bias_p = _bias_expand(
            rel_bias, _moba_prompt_bias_idx(n_near, nbk), MOBA_BLOCK,
            (n_near, n_kv, MOBA_BLOCK, cols), (n_near, None, MOBA_BLOCK, MOBA_BLOCK),
            lambda h: (0, h // group, 0, h % group), mult=LOG2E)
        moba_far = jnp.repeat(rel_bias[nbk - 1].reshape(n_kv, group, 1), MOBA_BLOCK, axis=2).reshape(n_kv, 1, cols)
        moba_far = moba_far * LOG2E
        far_cls = -(-(d_sat + page_size - 1) // page_size) - 1
        moba_bias_sel = _cols_expand(
            tab_cols, _moba_sample_page_idx(far_cls, page_size, n_heads, tp, t_new, nbk),
            page_size, mult=LOG2E).reshape(far_cls + 1, page_size, n_heads * tp)
        moba_bias_own = _cols_expand(tab_cols, own_idx, tp * n_kv, mult=LOG2E)

    def to_tm(a):
        return jnp.swapaxes(a, 0, 1).reshape((a.shape[1] * dbs,) + a.shape[2:])

    def to_bm(a, t):
        return jnp.swapaxes(a.reshape((t, dbs) + a.shape[1:]), 0, 1)

    xp = x_prompt.reshape(bsz * seq, d)
    xs = to_tm(x_sample)
    m_s = t_new * dbs

    conv_p, conv_s, swa_k_p, swa_v_p, swa_k_s, swa_v_s = [], [], [], [], [], []
    moba_k_p, moba_v_p, moba_k_s, moba_v_s, ffn_p, ffn_s = [], [], [], [], [], []

    def gates(layer, which):
        gp, gs = mods(layer, which)
        return gp.reshape(bsz, 1, d), jnp.tile(gs, (t_new, 1)).reshape(1, m_s, d)

    def normed(x_p, x_s, g, layer, which_sh):
        shp, shs = mods(layer, which_sh)
        scp, scs = mods(layer, which_sh + 1)
        hp = _norm_mod(x_p.reshape(bsz, seq, d), g, scp.reshape(bsz, 1, d), shp.reshape(bsz, 1, d), 512)
        hs = _norm_mod(x_s.reshape(t_new, dbs, d), g, scs.reshape(1, dbs, d), shs.reshape(1, dbs, d), dbs)
        return hp.reshape(bsz * seq, d), hs.reshape(m_s, d)

    def sample_views(qkv_bm):
        pad = jnp.concatenate([qkv_bm, jnp.zeros((dbs, tp - t_new, qkv_bm.shape[2]), qkv_bm.dtype)], axis=1)
        q_cols = pad[:, :, :qw].reshape(dbs, tp, n_heads, hd).transpose(0, 2, 1, 3).reshape(dbs, n_heads * tp, hd)
        return (q_cols, pad[:, :, qw:qw + kw].reshape(dbs, tp * n_kv, hd),
                pad[:, :, qw + kw:].reshape(dbs, tp * n_kv, hd))

    def cols_to_bm(o_cols):
        o = o_cols.reshape(dbs, n_heads, tp, hd)[:, :, :t_new].transpose(0, 2, 1, 3)
        return o.reshape(dbs, t_new, qw)

    for layer in range(depth):
        kind, j = layer % n_mixers, layer // n_mixers
        hp, hs = normed(xp, xs, norm_mix[layer], layer, 0)
        g1p, g1s = gates(layer, 2)
        if kind == 0:
            up, us = _glu(hp, hs, conv_w1, conv_b1, j)
            yp = _conv_ln_prompt(up.reshape(bsz, seq, d), conv_dw[j], conv_dw_b[j], conv_ln_g[j], conv_ln_b[j])
            st_tm = jnp.swapaxes(state_conv[j], 0, 1)
            ys, nst_tm = _conv_ln_sample(st_tm, us.reshape(t_new, dbs, d), conv_dw[j], conv_dw_b[j],
                                         conv_ln_g[j], conv_ln_b[j])
            conv_p.append(up.reshape(bsz, seq, d)[:, seq - (cw - 1):])
            conv_s.append(jnp.swapaxes(nst_tm, 0, 1))
            xp, xs = _mm(yp.reshape(bsz * seq, d), conv_w2, j, bias=conv_b2, res=xp, gate=g1p,
                         side=(ys.reshape(m_s, d), xs, g1s[0]))
        elif kind == 1:
            qkv_p, qkv_s = _mm(hp, swa_wqkv, j, side=(hs, None, None))
            op = _swa_prompt(qkv_p, swa_sinks[j], swa_bias_p, bsz, n_heads, n_kv, hd, window)
            q_cols, k_new, v_new = sample_views(to_bm(qkv_s, t_new))
            buf_k = cache_swa_k[j].reshape(dbs, window * n_kv, hd)
            buf_v = cache_swa_v[j].reshape(dbs, window * n_kv, hd)
            sink_row = jnp.repeat(swa_sinks[j], tp).reshape(1, n_heads * tp)
            o_cols = _swa_sample(q_cols, buf_k, buf_v, k_new, v_new, swa_bias_s, swa_bias_own, sink_row,
                                 n_kv=n_kv, group=group, tp=tp)
            os_ = to_tm(cols_to_bm(o_cols))
            tail = qkv_p.reshape(bsz, seq, qw + 2 * kw)[:, seq - window:]
            swa_k_p.append(tail[:, :, qw:qw + kw].reshape(bsz, window, n_kv, hd))
            swa_v_p.append(tail[:, :, qw + kw:].reshape(bsz, window, n_kv, hd))
            keep = t_new * n_kv
            swa_k_s.append(jnp.concatenate([buf_k[:, keep:], k_new[:, :keep]], axis=1)
                           .reshape(dbs, window, n_kv, hd))
            swa_v_s.append(jnp.concatenate([buf_v[:, keep:], v_new[:, :keep]], axis=1)
                           .reshape(dbs, window, n_kv, hd))
            xp, xs = _mm(op, swa_wo, j, res=xp, gate=g1p, side=(os_, xs, g1s[0]))
        else:
            qkv_p, k_rows, v_rows, qkv_s = _mm(hp, moba_wqkv, j, kv_heads=(n_kv, hd), side=(hs, None, None))
            nb = seq // MOBA_BLOCK
            means_p = _block_means(qkv_p, bsz * nb, qw // kw, kw).reshape(bsz, nb, kw)
            op = _moba_prompt(qkv_p, means_p, moba_bias_p, moba_far, bsz, n_heads, n_kv, hd)
            moba_k_p.append(k_rows.reshape(bsz, seq // page_size, page_size, n_kv, hd))
            moba_v_p.append(v_rows.reshape(bsz, seq // page_size, page_size, n_kv, hd))

            qkv_bm = to_bm(qkv_s, t_new)
            q_cols, k_new, v_new = sample_views(qkv_bm)
            pool_k = cache_moba_k.reshape(cache_moba_k.shape[0] * n_pool, page_size * n_kv, hd)
            pool_v = cache_moba_v.reshape(cache_moba_v.shape[0] * n_pool, page_size * n_kv, hd)
            pt_flat = page_table.reshape(-1)
            means_s = _pool_means(pool_k, pt_flat, j * n_pool, dbs, n_pages, n_kv)
            means_t = means_s.reshape(dbs, n_blk_past, n_kv, hd).transpose(0, 2, 1, 3)
            means_t = means_t.reshape(dbs, n_kv * n_blk_past, hd)
            o_cols = _moba_sample(q_cols, means_t, k_new, v_new, pool_k, pool_v, pt_flat, j * n_pool,
                                  moba_bias_sel, moba_bias_own, n_kv=n_kv, group=group, tp=tp, n_pages=n_pages)
            os_bm = cols_to_bm(o_cols).astype(BF16)
            moba_k_s.append(qkv_bm[:, :, qw:qw + kw].reshape(dbs, t_new, n_kv, hd))
            moba_v_s.append(qkv_bm[:, :, qw + kw:].reshape(dbs, t_new, n_kv, hd))
            xp, xs = _mm(op, moba_wo, j, res=xp, gate=g1p, side=(to_tm(os_bm), xs, g1s[0]))

        hp, hs = normed(xp, xs, norm_ffn[layer], layer, 3)
        g2p, g2s = gates(layer, 5)
        prev_tm = jnp.swapaxes(state_ffn[layer], 0, 1).reshape(2 * dbs, d_ff)
        ap, stp, as_, sts = _ffn_up(hp, hs, prev_tm, ffn_w_up, ffn_dw, ffn_dw_b, layer, bsz, dbs)
        ffn_p.append(stp)
        ffn_s.append(jnp.swapaxes(sts.reshape(2, dbs, d_ff), 0, 1))
        xp, xs = _mm(ap, ffn_w_down, layer, res=xp, gate=g2p, tn=256, weights_outer=False,
                     side=(as_, xs, g2s[0]))

    y_p = _norm_mod(xp.reshape(bsz, seq, d), norm_final, None, None, 512, out_dtype=F32)
    y_s = _norm_mod(xs.reshape(t_new, dbs, d), norm_final, None, None, dbs, out_dtype=F32)
    y_s = jnp.swapaxes(y_s, 0, 1)

    return (y_p, y_s, jnp.stack(conv_p), jnp.stack(conv_s), jnp.stack(swa_k_p), jnp.stack(swa_v_p),
            jnp.stack(swa_k_s), jnp.stack(swa_v_s), jnp.stack(moba_k_p), jnp.stack(moba_v_p),
            jnp.stack(moba_k_s), jnp.stack(moba_v_s), jnp.stack(ffn_p), jnp.stack(ffn_s))
```

```python
import functools
import math

import numpy as np
import jax
import jax.numpy as jnp
from jax import lax
from jax.experimental import pallas as pl
from jax.experimental.pallas import tpu as pltpu

MOBA_BLOCK = 256
MOBA_TOPK = 3
REL_MAX_DISTANCE = 1024
RMS_EPS = 1e-6
LN_EPS = 1e-5
N_MOD = 6

V7X_LANES = 128
V7X_SUBLANES = 8
V7X_VMEM_LIMIT_BYTES = 56 * 1024 * 1024

NEG = -1e30
LOG2E = 1.4426950408889634
BF16 = jnp.bfloat16
F32 = jnp.float32


def _cparams(*sem):
    return pltpu.CompilerParams(dimension_semantics=sem, vmem_limit_bytes=V7X_VMEM_LIMIT_BYTES)


def _dot(a, b):
    return jnp.dot(a, b, preferred_element_type=F32)


def _dot_nt(a, b, precision=None):
    return lax.dot_general(a, b, (((1,), (1,)), ((), ())), precision=precision,
                           preferred_element_type=F32)


def _silu(x):
    return x * jax.nn.sigmoid(x)


def _bucket_np(dist, num_buckets):
    n = np.maximum(dist, 0)
    max_exact = num_buckets // 2
    nf = np.maximum(n, 1).astype(np.float32)
    ratio = np.log(nf / np.float32(max_exact)) / np.float32(math.log(REL_MAX_DISTANCE / max_exact))
    large = max_exact + (ratio * np.float32(num_buckets - max_exact)).astype(np.int32)
    return np.where(n < max_exact, n, np.minimum(large, num_buckets - 1)).astype(np.int32)


def _saturation_distance(num_buckets):
    d = np.arange(0, 4 * REL_MAX_DISTANCE)
    b = _bucket_np(d, num_buckets)
    below = np.nonzero(b < num_buckets - 1)[0]
    return int(below.max()) + 1


def _bias_expand_kernel(tab_ref, idx_ref, o_ref, *, tiles, mult):
    h = pl.program_id(0)
    for n, (r0, nr, buckets) in enumerate(tiles):
        idx = idx_ref[r0:r0 + nr, :]
        acc = jnp.full(idx.shape, NEG, F32)
        for b in buckets:
            acc = jnp.where(idx == b, tab_ref[b, h] * mult, acc)
        if len(o_ref.shape) == 3:
            o_ref[n] = acc
        else:
            o_ref[r0:r0 + nr, :] = acc


def _bias_expand(table, idx_np, tile_rows, out_shape, out_block, out_index, mult=1.0):
    n_heads = table.shape[1]
    rows, cols = idx_np.shape
    tiles = []
    for r0 in range(0, rows, tile_rows):
        present = np.unique(idx_np[r0:r0 + tile_rows])
        tiles.append((r0, min(tile_rows, rows - r0), tuple(int(b) for b in present if b >= 0)))
    return pl.pallas_call(
        functools.partial(_bias_expand_kernel, tiles=tuple(tiles), mult=mult),
        grid=(n_heads,),
        in_specs=[pl.BlockSpec(memory_space=pltpu.SMEM),
                  pl.BlockSpec((rows, cols), lambda h: (0, 0))],
        out_specs=pl.BlockSpec(out_block, out_index),
        out_shape=jax.ShapeDtypeStruct(out_shape, F32),
        compiler_params=_cparams("arbitrary"),
    )(table, jnp.asarray(idx_np))


def _ada_kernel(c_ref, w_ref, b_ref, o_ref):
    a = _silu(c_ref[...]).astype(BF16)
    o_ref[...] = _dot(a, w_ref[...].astype(BF16)) + b_ref[...]


def _ada(c_all, ada_w, ada_b):
    depth, d, n = ada_w.shape
    rows = c_all.shape[0]
    tn = min(n, 2048)
    return pl.pallas_call(
        _ada_kernel,
        grid=(depth, n // tn),
        in_specs=[pl.BlockSpec((rows, d), lambda l, j: (0, 0)),
                  pl.BlockSpec((None, d, tn), lambda l, j: (l, 0, j)),
                  pl.BlockSpec((None, 1, tn), lambda l, j: (l, 0, j))],
        out_specs=pl.BlockSpec((None, rows, tn), lambda l, j: (l, 0, j)),
        out_shape=jax.ShapeDtypeStruct((depth, rows, n), F32),
        compiler_params=_cparams("arbitrary", "arbitrary"),
    )(c_all, ada_w, ada_b.reshape(depth, 1, n))


def _norm_mod_kernel(x_ref, g_ref, sc_ref, sh_ref, o_ref):
    x = x_ref[...]
    ms = jnp.mean(x * x, axis=-1, keepdims=True)
    y = x * lax.rsqrt(ms + RMS_EPS) * g_ref[...]
    o_ref[...] = (y * (1.0 + sc_ref[...]) + sh_ref[...]).astype(o_ref.dtype)


def _rms_kernel(x_ref, g_ref, o_ref):
    x = x_ref[...]
    ms = jnp.mean(x * x, axis=-1, keepdims=True)
    o_ref[...] = (x * lax.rsqrt(ms + RMS_EPS) * g_ref[...]).astype(o_ref.dtype)


def _norm_mod(x3, g, sc3, sh3, rows_per_step, out_dtype=BF16):
    a, r, d = x3.shape
    tr = min(rows_per_step, r)
    g3 = g.reshape(1, 1, d)
    if sc3 is None:
        return pl.pallas_call(
            _rms_kernel,
            grid=(a, r // tr),
            in_specs=[pl.BlockSpec((1, tr, d), lambda i, j: (i, j, 0)),
                      pl.BlockSpec((1, 1, d), lambda i, j: (0, 0, 0))],
            out_specs=pl.BlockSpec((1, tr, d), lambda i, j: (i, j, 0)),
            out_shape=jax.ShapeDtypeStruct(x3.shape, out_dtype),
            compiler_params=_cparams("arbitrary", "arbitrary"),
        )(x3, g3)
    per_seq = sc3.shape[0] == a and sc3.shape[1] == 1
    if per_seq:
        mspec = pl.BlockSpec((1, 1, d), lambda i, j: (i, 0, 0))
    else:
        assert sc3.shape[0] == 1 and sc3.shape[1] == r and tr == r
        mspec = pl.BlockSpec((1, r, d), lambda i, j: (0, 0, 0))
    return pl.pallas_call(
        _norm_mod_kernel,
        grid=(a, r // tr),
        in_specs=[pl.BlockSpec((1, tr, d), lambda i, j: (i, j, 0)),
                  pl.BlockSpec((1, 1, d), lambda i, j: (0, 0, 0)),
                  mspec, mspec],
        out_specs=pl.BlockSpec((1, tr, d), lambda i, j: (i, j, 0)),
        out_shape=jax.ShapeDtypeStruct(x3.shape, out_dtype),
        compiler_params=_cparams("arbitrary", "arbitrary"),
    )(x3, g3, sc3, sh3)


def _mm_kernel(*refs, names, cache_w, kv_heads):
    r = dict(zip(names, refs))
    if cache_w:
        @pl.when(pl.program_id(1) == 0)
        def _():
            r["w_s"][...] = r["w"][...].astype(BF16)
        w = r["w_s"][...]
    else:
        w = r["w"][...].astype(BF16)

    def product(x_name, res_name, gate_name):
        acc = _dot(r[x_name][...], w)
        if "bias" in r:
            acc = acc + r["bias"][...]
        if res_name in r:
            acc = r[res_name][...] + r[gate_name][...] * acc
        return acc

    acc = product("x", "res", "gate")
    r["o"][...] = acc.astype(r["o"].dtype)
    if kv_heads:
        n_kv, hd = kv_heads
        rows = acc.shape[0]

        @pl.when(pl.program_id(0) == pl.num_programs(0) - 1)
        def _():
            for kv in range(n_kv):
                r["k"][pl.ds(kv, rows, stride=n_kv), :] = acc[:, kv * hd:(kv + 1) * hd]
                r["v"][pl.ds(kv, rows, stride=n_kv), :] = acc[:, (n_kv + kv) * hd:(n_kv + kv + 1) * hd]
    if "x2" in r:
        row_tile = pl.program_id(1) if cache_w else pl.program_id(0)
        last = (pl.num_programs(1) - 1) if cache_w else 0

        @pl.when(row_tile == last)
        def _():
            r["o2"][...] = product("x2", "res2", "gate2").astype(r["o2"].dtype)


def _mm(x, w, layer, *, bias=None, res=None, gate=None, out_dtype=F32, tm=1024, tn=1024, weights_outer=True,
        kv_heads=None, side=None):
    m, k = x.shape
    n = w.shape[2]
    tm, tn = min(tm, m), min(tn, n)
    assert m % tm == 0 and n % tn == 0
    if weights_outer:
        grid = (n // tn, m // tm)
        ij = lambda a, b: (b, a)
    else:
        grid = (m // tm, n // tn)
        ij = lambda a, b: (a, b)

    def spec(block, fn):
        return pl.BlockSpec(block, lambda a, b: fn(*ij(a, b)))

    names = ["x", "w"]
    in_specs = [spec((tm, k), lambda i, j: (i, 0)),
                spec((None, k, tn), lambda i, j: (layer, 0, j))]
    args = [x, w]
    if bias is not None:
        names.append("bias")
        in_specs.append(spec((None, 1, tn), lambda i, j: (layer, 0, j)))
        args.append(bias.reshape(bias.shape[0], 1, n))
    if res is not None:
        gs, gr, _ = gate.shape
        assert gr in (1, tm) and (m // tm) % gs == 0
        tiles_per_gate = (m // tm) // gs
        names += ["res", "gate"]
        in_specs.append(spec((tm, tn), lambda i, j: (i, j)))
        in_specs.append(spec((None, gr, tn), lambda i, j: (i // tiles_per_gate, 0, j)))
        args += [res, gate]
    if side is not None:
        x2, res2, gate2 = side
        m2 = x2.shape[0]
        nj_side = n // tn
        if weights_outer:
            side_col = lambda i, j: j
        else:
            side_col = lambda i, j: jnp.where(i == 0, j, nj_side - 1)
        names.append("x2")
        in_specs.append(spec((m2, k), lambda i, j: (0, 0)))
        args.append(x2)
        if res2 is not None:
            names += ["res2", "gate2"]
            in_specs += [spec((m2, tn), lambda i, j: (0, side_col(i, j)))] * 2
            args += [res2, gate2]
    names.append("o")
    out_specs = [spec((tm, tn), lambda i, j: (i, j))]
    out_shape = [jax.ShapeDtypeStruct((m, n), out_dtype)]
    if kv_heads:
        n_kv, hd = kv_heads
        nj = n // tn
        assert weights_outer and tn == 2 * n_kv * hd
        kv_spec = spec((tm * n_kv, hd), lambda i, j: (jnp.where(j == nj - 1, i, 0), 0))
        names += ["k", "v"]
        out_specs += [kv_spec, kv_spec]
        out_shape += [jax.ShapeDtypeStruct((m * n_kv, hd), F32)] * 2
    if side is not None:
        names.append("o2")
        out_specs.append(spec((m2, tn), lambda i, j: (0, side_col(i, j))))
        out_shape.append(jax.ShapeDtypeStruct((m2, n), out_dtype))
    if weights_outer:
        names.append("w_s")
    outs = pl.pallas_call(
        functools.partial(_mm_kernel, names=tuple(names), cache_w=weights_outer, kv_heads=kv_heads),
        grid=grid,
        in_specs=in_specs,
        out_specs=out_specs,
        out_shape=out_shape,
        scratch_shapes=[pltpu.VMEM((k, tn), BF16)] if weights_outer else [],
        compiler_params=_cparams("arbitrary", "arbitrary"),
    )(*args)
    return outs[0] if len(outs) == 1 else tuple(outs)


def _glu_kernel(x_ref, x2_ref, wa_ref, wg_ref, ba_ref, bg_ref, o_ref, o2_ref, wa_s, wg_s):
    @pl.when(pl.program_id(1) == 0)
    def _():
        wa_s[...] = wa_ref[...].astype(BF16)
        wg_s[...] = wg_ref[...].astype(BF16)

    def glu(x):
        a = _dot(x, wa_s[...]) + ba_ref[...]
        g = _dot(x, wg_s[...]) + bg_ref[...]
        return a * jax.nn.sigmoid(g)

    o_ref[...] = glu(x_ref[...])

    @pl.when(pl.program_id(1) == pl.num_programs(1) - 1)
    def _():
        o2_ref[...] = glu(x2_ref[...])


def _glu(x, x2, w1, b1, layer, *, tm=1024, tn=512):
    m, k = x.shape
    m2 = x2.shape[0]
    n = w1.shape[2] // 2
    tm, tn = min(tm, m), min(tn, n)
    nj = n // tn
    b3 = b1.reshape(b1.shape[0], 1, 2 * n)
    return pl.pallas_call(
        _glu_kernel,
        grid=(nj, m // tm),
        in_specs=[pl.BlockSpec((tm, k), lambda j, i: (i, 0)),
                  pl.BlockSpec((m2, k), lambda j, i: (0, 0)),
                  pl.BlockSpec((None, k, tn), lambda j, i: (layer, 0, j)),
                  pl.BlockSpec((None, k, tn), lambda j, i: (layer, 0, nj + j)),
                  pl.BlockSpec((None, 1, tn), lambda j, i: (layer, 0, j)),
                  pl.BlockSpec((None, 1, tn), lambda j, i: (layer, 0, nj + j))],
        out_specs=[pl.BlockSpec((tm, tn), lambda j, i: (i, j)),
                   pl.BlockSpec((m2, tn), lambda j, i: (0, j))],
        out_shape=[jax.ShapeDtypeStruct((m, n), F32), jax.ShapeDtypeStruct((m2, n), F32)],
        scratch_shapes=[pltpu.VMEM((k, tn), BF16), pltpu.VMEM((k, tn), BF16)],
        compiler_params=_cparams("arbitrary", "arbitrary"),
    )(x, x2, w1, w1, b3, b3)


def _layernorm_silu(y, lg, lb):
    mu = jnp.mean(y, axis=-1, keepdims=True)
    yc = y - mu
    var = jnp.mean(yc * yc, axis=-1, keepdims=True)
    return _silu(yc * lax.rsqrt(var + LN_EPS) * lg + lb)


def _conv_ln_prompt_kernel(u_ref, halo_ref, dw_ref, dwb_ref, lg_ref, lb_ref, o_ref, full_s, y_s,
                           *, tm, halo, cw, rc, lw):
    i = pl.program_id(1)
    d = u_ref.shape[-1]
    full_s[halo:halo + tm, :] = u_ref[...]

    @pl.when(i == 0)
    def _():
        full_s[0:halo, :] = jnp.zeros((halo, d), F32)

    @pl.when(i > 0)
    def _():
        full_s[0:halo, :] = halo_ref[...]

    off = halo - (cw - 1)
    win = rc + halo
    sub = V7X_SUBLANES

    def body(r, carry):
        r0 = pl.multiple_of(r * rc, rc)
        for c0 in range(0, d, lw):
            window = full_s[pl.ds(r0, win), c0:c0 + lw]
            acc = jnp.zeros((rc, lw), F32) + dwb_ref[:, c0:c0 + lw]
            for s in range(sub):
                taps = [k for k in range(cw) if (off + k) % sub == s]
                if not taps:
                    continue
                shifted = window if s == 0 else pltpu.roll(window, win - s, 0)
                for k in taps:
                    a = off + k - s
                    acc = acc + dw_ref[k:k + 1, c0:c0 + lw] * shifted[a:a + rc]
            y_s[pl.ds(r0, rc), c0:c0 + lw] = acc
        return carry
    lax.fori_loop(0, tm // rc, body, 0)

    o_ref[...] = _layernorm_silu(y_s[...], lg_ref[...], lb_ref[...]).astype(o_ref.dtype)


def _conv_ln_prompt(u3, dw, dwb, lg, lb, *, tm=256):
    bsz, t, d = u3.shape
    cw = dw.shape[0]
    halo = 32
    assert cw - 1 <= halo and t % tm == 0 and tm % halo == 0
    hb = tm // halo
    lw = min(d, V7X_LANES)
    rc = min(tm, 4 * halo)
    return pl.pallas_call(
        functools.partial(_conv_ln_prompt_kernel, tm=tm, halo=halo, cw=cw, rc=rc, lw=lw),
        grid=(bsz, t // tm),
        in_specs=[pl.BlockSpec((None, tm, d), lambda b, i: (b, i, 0)),
                  pl.BlockSpec((None, halo, d), lambda b, i: (b, jnp.maximum(i * hb - 1, 0), 0)),
                  pl.BlockSpec((cw, d), lambda b, i: (0, 0)),
                  pl.BlockSpec((1, d), lambda b, i: (0, 0)),
                  pl.BlockSpec((1, d), lambda b, i: (0, 0)),
                  pl.BlockSpec((1, d), lambda b, i: (0, 0))],
        out_specs=pl.BlockSpec((None, tm, d), lambda b, i: (b, i, 0)),
        out_shape=jax.ShapeDtypeStruct((bsz, t, d), BF16),
        scratch_shapes=[pltpu.VMEM((tm + halo, d), F32), pltpu.VMEM((tm, d), F32)],
        compiler_params=_cparams("arbitrary", "arbitrary"),
    )(u3, u3, dw, dwb.reshape(1, d), lg.reshape(1, d), lb.reshape(1, d))


def _conv_ln_sample_kernel(st_ref, u_ref, dw_ref, dwb_ref, lg_ref, lb_ref, o_ref, nst_ref, *, cw, lw):
    t_new, _, d = u_ref.shape
    n_st = cw - 1

    def row(idx, c0):
        if idx < n_st:
            return st_ref[idx, :, c0:c0 + lw]
        return u_ref[idx - n_st, :, c0:c0 + lw]

    for t in range(t_new):
        parts = []
        for c0 in range(0, d, lw):
            acc = dw_ref[0:1, c0:c0 + lw] * row(t, c0)
            for k in range(1, cw):
                acc = acc + dw_ref[k:k + 1, c0:c0 + lw] * row(t + k, c0)
            parts.append(acc + dwb_ref[:, c0:c0 + lw])
        y = jnp.concatenate(parts, axis=-1) if len(parts) > 1 else parts[0]
        o_ref[t] = _layernorm_silu(y, lg_ref[...], lb_ref[...]).astype(o_ref.dtype)
    for r in range(n_st):
        idx = t_new + r
        nst_ref[r] = st_ref[idx] if idx < n_st else u_ref[idx - n_st]


def _conv_ln_sample(st_tm, u_tm, dw, dwb, lg, lb):
    n_st, bsz, d = st_tm.shape
    t_new = u_tm.shape[0]
    cw = dw.shape[0]
    lw = min(d, 512)
    full = lambda shape: pl.BlockSpec(shape, lambda i: (0,) * len(shape))
    return pl.pallas_call(
        functools.partial(_conv_ln_sample_kernel, cw=cw, lw=lw),
        grid=(1,),
        in_specs=[full((n_st, bsz, d)), full((t_new, bsz, d)), full((cw, d)), full((1, d)), full((1, d)),
                  full((1, d))],
        out_specs=[full((t_new, bsz, d)), full((n_st, bsz, d))],
        out_shape=[jax.ShapeDtypeStruct((t_new, bsz, d), BF16),
                   jax.ShapeDtypeStruct((n_st, bsz, d), F32)],
        compiler_params=_cparams("arbitrary"),
    )(st_tm, u_tm, dw, dwb.reshape(1, d), lg.reshape(1, d), lb.reshape(1, d))


def _ffn_up_kernel(x_ref, x2_ref, prev2_ref, wg_ref, wv_ref, dw_ref, dwb_ref, a_ref, st_ref, a2_ref, st2_ref,
                   wg_s, wv_s, carry_s, *, tiles_per_seq, bsz2):
    i = pl.program_id(1)

    @pl.when(i == 0)
    def _():
        wg_s[...] = wg_ref[...].astype(BF16)
        wv_s[...] = wv_ref[...].astype(BF16)

    @pl.when(i % tiles_per_seq == 0)
    def _():
        carry_s[...] = jnp.zeros(carry_s.shape, F32)

    x = x_ref[...]
    g = _dot(x, wg_s[...])
    v = _dot(x, wv_s[...])
    tm = g.shape[0]
    row = lax.broadcasted_iota(jnp.int32, g.shape, 0)
    prev1 = carry_s[V7X_SUBLANES - 1:V7X_SUBLANES, :]
    prev2 = carry_s[V7X_SUBLANES - 2:V7X_SUBLANES - 1, :]
    g1 = jnp.where(row == 0, prev1, pltpu.roll(g, 1, 0))
    g2 = jnp.where(row == 0, prev2, jnp.where(row == 1, prev1, pltpu.roll(g, 2, 0)))
    gc = dw_ref[0:1, :] * g2 + dw_ref[1:2, :] * g1 + dw_ref[2:3, :] * g + dwb_ref[...]
    a_ref[...] = (_silu(gc) * v).astype(a_ref.dtype)
    carry_s[...] = g[tm - V7X_SUBLANES:tm, :]
    st_ref[...] = g[tm - 2:tm, :]

    @pl.when(i == pl.num_programs(1) - 1)
    def _():
        x2 = x2_ref[...]
        g_s = _dot(x2, wg_s[...])
        v_s = _dot(x2, wv_s[...])
        rows = g_s.shape[0]
        full = jnp.concatenate([prev2_ref[...], g_s], axis=0)
        gc_s = (dw_ref[0:1, :] * full[0:rows] + dw_ref[1:2, :] * full[bsz2:bsz2 + rows]
                + dw_ref[2:3, :] * full[2 * bsz2:2 * bsz2 + rows] + dwb_ref[...])
        a2_ref[...] = (_silu(gc_s) * v_s).astype(a2_ref.dtype)
        st2_ref[...] = full[rows:rows + 2 * bsz2]


def _ffn_up(x, x2, prev2_tm, w_up, dw, dwb, layer, bsz, bsz2, *, tm=1024, tn=512):
    m, k = x.shape
    m2 = x2.shape[0]
    n = w_up.shape[2] // 2
    t = m // bsz
    tm, tn = min(tm, t), min(tn, n)
    assert t % tm == 0 and n % tn == 0 and dw.shape[1] == 3 and m2 >= 2 * bsz2
    nj = n // tn
    tiles_per_seq = t // tm
    return pl.pallas_call(
        functools.partial(_ffn_up_kernel, tiles_per_seq=tiles_per_seq, bsz2=bsz2),
        grid=(nj, m // tm),
        in_specs=[pl.BlockSpec((tm, k), lambda j, i: (i, 0)),
                  pl.BlockSpec((m2, k), lambda j, i: (0, 0)),
                  pl.BlockSpec((2 * bsz2, tn), lambda j, i: (0, j)),
                  pl.BlockSpec((None, k, tn), lambda j, i: (layer, 0, j)),
                  pl.BlockSpec((None, k, tn), lambda j, i: (layer, 0, nj + j)),
                  pl.BlockSpec((None, 3, tn), lambda j, i: (layer, 0, j)),
                  pl.BlockSpec((None, 1, tn), lambda j, i: (layer, 0, j))],
        out_specs=[pl.BlockSpec((tm, tn), lambda j, i: (i, j)),
                   pl.BlockSpec((None, 2, tn), lambda j, i: (i // tiles_per_seq, 0, j)),
                   pl.BlockSpec((m2, tn), lambda j, i: (0, j)),
                   pl.BlockSpec((2 * bsz2, tn), lambda j, i: (0, j))],
        out_shape=[jax.ShapeDtypeStruct((m, n), BF16),
                   jax.ShapeDtypeStruct((bsz, 2, n), F32),
                   jax.ShapeDtypeStruct((m2, n), BF16),
                   jax.ShapeDtypeStruct((2 * bsz2, n), F32)],
        scratch_shapes=[pltpu.VMEM((k, tn), BF16), pltpu.VMEM((k, tn), BF16),
                        pltpu.VMEM((V7X_SUBLANES, tn), F32)],
        compiler_params=_cparams("arbitrary", "arbitrary"),
    )(x, x2, prev2_tm, w_up, w_up, dw, dwb.reshape(dwb.shape[0], 1, n))


def _swa_prompt_kernel(sink_ref, q_ref, kc_ref, kp_ref, vc_ref, vp_ref, bias_ref, o_ref, *, n_heads, group, hd,
                       scale):
    n = pl.program_id(1)
    w = q_ref.shape[0]
    kcat = jnp.concatenate([kp_ref[...], kc_ref[...]], axis=0).astype(BF16)
    vcat = jnp.concatenate([vp_ref[...], vc_ref[...]], axis=0).astype(BF16)
    col = lax.broadcasted_iota(jnp.int32, (w, 2 * w), 1)
    keep = (col >= w) | (n > 0)
    for h in range(n_heads):
        kv = h // group
        q = q_ref[:, h * hd:(h + 1) * hd].astype(BF16)
        s = _dot_nt(q, kcat[:, kv * hd:(kv + 1) * hd]) * scale + bias_ref[h]
        s = jnp.where(keep, s, NEG)
        sink = sink_ref[h]
        m = jnp.maximum(jnp.max(s, axis=-1, keepdims=True), sink)
        p = jnp.exp(s - m)
        l = jnp.sum(p, axis=-1, keepdims=True) + jnp.exp(sink - m)
        o = _dot(p.astype(BF16), vcat[:, kv * hd:(kv + 1) * hd]) / l
        o_ref[:, h * hd:(h + 1) * hd] = o.astype(o_ref.dtype)


def _swa_prompt(qkv, sinks, bias, bsz, n_heads, n_kv, hd, window):
    m = qkv.shape[0]
    t = m // bsz
    nb = t // window
    qw, kw = n_heads * hd, n_kv * hd
    assert qw % kw == 0 and t % window == 0
    kcol, vcol = qw // kw, qw // kw + 1
    return pl.pallas_call(
        functools.partial(_swa_prompt_kernel, n_heads=n_heads, group=n_heads // n_kv, hd=hd, scale=hd ** -0.5),
        grid=(bsz, nb),
        in_specs=[pl.BlockSpec(memory_space=pltpu.SMEM),
                  pl.BlockSpec((window, qw), lambda b, n: (b * nb + n, 0)),
                  pl.BlockSpec((window, kw), lambda b, n: (b * nb + n, kcol)),
                  pl.BlockSpec((window, kw), lambda b, n: (b * nb + jnp.maximum(n - 1, 0), kcol)),
                  pl.BlockSpec((window, kw), lambda b, n: (b * nb + n, vcol)),
                  pl.BlockSpec((window, kw), lambda b, n: (b * nb + jnp.maximum(n - 1, 0), vcol)),
                  pl.BlockSpec((n_heads, window, 2 * window), lambda b, n: (0, 0, 0))],
        out_specs=pl.BlockSpec((window, qw), lambda b, n: (b * nb + n, 0)),
        out_shape=jax.ShapeDtypeStruct((m, qw), BF16),
        compiler_params=_cparams("arbitrary", "arbitrary"),
    )(sinks, qkv, qkv, qkv, qkv, qkv, bias)


def _swa_sample_kernel(q_ref, bk_ref, bv_ref, kn_ref, vn_ref, bias_ref, own_ref, sink_ref, o_ref,
                       *, n_kv, group, tp, scale):
    cols, hd = q_ref.shape
    keys = bk_ref.shape[0] // n_kv
    qb = q_ref[...].astype(BF16)
    col_kv = lax.broadcasted_iota(jnp.int32, (keys, cols), 1) // (group * tp)

    def by_head(ref):
        return jnp.concatenate([ref[pl.ds(kv, keys, stride=n_kv), :] for kv in range(n_kv)], axis=0)

    full = _dot_nt(by_head(bk_ref).astype(BF16), qb)
    s = full[0:keys]
    for kv in range(1, n_kv):
        s = jnp.where(col_kv == kv, full[kv * keys:(kv + 1) * keys], s)
    s = s * scale + bias_ref[...]
    s_own = _dot_nt(kn_ref[...].astype(BF16), qb) * scale + own_ref[...]
    sink = sink_ref[...]
    m = jnp.maximum(jnp.maximum(jnp.max(s, axis=0, keepdims=True), jnp.max(s_own, axis=0, keepdims=True)), sink)
    p = jnp.exp(s - m)
    p_own = jnp.exp(s_own - m)
    l = jnp.sum(p, axis=0, keepdims=True) + jnp.sum(p_own, axis=0, keepdims=True) + jnp.exp(sink - m)
    spread = jnp.concatenate([jnp.where(col_kv == kv, p, 0.0) for kv in range(n_kv)], axis=0)
    acc = (_dot(by_head(bv_ref).T.astype(BF16), spread.astype(BF16))
           + _dot(vn_ref[...].T.astype(BF16), p_own.astype(BF16)))
    o_ref[...] = (acc / l).T.astype(o_ref.dtype)


def _swa_sample(q_cols, buf_k, buf_v, k_new, v_new, bias, own, sink_row, *, n_kv, group, tp):
    dbs, cols, hd = q_cols.shape
    per_seq = lambda a: pl.BlockSpec((None,) + a.shape[1:], lambda b: (b, 0, 0))
    shared = lambda a: pl.BlockSpec(a.shape, lambda b: (0, 0))
    return pl.pallas_call(
        functools.partial(_swa_sample_kernel, n_kv=n_kv, group=group, tp=tp, scale=hd ** -0.5),
        grid=(dbs,),
        in_specs=[per_seq(q_cols), per_seq(buf_k), per_seq(buf_v), per_seq(k_new), per_seq(v_new),
                  shared(bias), shared(own), shared(sink_row)],
        out_specs=pl.BlockSpec((None, cols, hd), lambda b: (b, 0, 0)),
        out_shape=jax.ShapeDtypeStruct((dbs, cols, hd), BF16),
        compiler_params=_cparams("arbitrary"),
    )(q_cols, buf_k, buf_v, k_new, v_new, bias, own, sink_row)


def _block_means_kernel(k_ref, o_ref):
    o_ref[...] = jnp.mean(k_ref[...], axis=0, keepdims=True)


def _block_means(qkv, n_blocks_total, kcol, kw):
    return pl.pallas_call(
        _block_means_kernel,
        grid=(n_blocks_total,),
        in_specs=[pl.BlockSpec((MOBA_BLOCK, kw), lambda i: (i, kcol))],
        out_specs=pl.BlockSpec((None, 1, kw), lambda i: (i, 0, 0)),
        out_shape=jax.ShapeDtypeStruct((n_blocks_total, 1, kw), F32),
        compiler_params=_cparams("arbitrary"),
    )(qkv)


def _rank_rows(gate, blk, n_rows):
    rank = jnp.zeros(gate.shape, jnp.int32)
    for m in range(n_rows):
        gm = gate[m:m + 1, :]
        beats = (gm > gate) | ((gm == gate) & (blk > m))
        rank = rank + jnp.where(beats, 1, 0)
    return rank


def _moba_prompt_kernel(qb_ref, kb_ref, q_ref, k_ref, v_ref, means_ref, diag_ref, far_ref, o_ref,
                        qs, sel_s, m_s, l_s, acc_s, *, n_kv, group, hd, scale, n_near, nb):
    qb = qb_ref[pl.program_id(1)]
    kb = kb_ref[pl.program_id(1)]
    blk_rows = q_ref.shape[0]
    cols = group * blk_rows

    @pl.when(kb == 0)
    def _():
        blk = lax.broadcasted_iota(jnp.int32, (nb, cols), 0)
        for kv in range(n_kv):
            qg = jnp.concatenate([q_ref[:, (kv * group + g) * hd:(kv * group + g + 1) * hd]
                                  for g in range(group)], axis=0)
            qs[kv] = (qg * (scale * LOG2E)).astype(BF16)
            gate = _dot_nt(means_ref[:, kv * hd:(kv + 1) * hd], qg, precision=lax.Precision.HIGHEST)
            valid = blk < qb
            gate = jnp.where(valid, gate, -jnp.inf)
            rank = _rank_rows(gate, blk, nb)
            sel_s[kv] = jnp.where(valid & (rank < MOBA_TOPK), 1.0, 0.0)
            m_s[kv] = jnp.full((1, cols), NEG, F32)
            l_s[kv] = jnp.zeros((1, cols), F32)
            acc_s[kv] = jnp.zeros((hd, cols), F32)

    def attend_all(adds_of):
        chunks = [(kv, g) for kv in range(n_kv) for g in range(group)]
        col = lambda g: slice(g * blk_rows, (g + 1) * blk_rows)
        scores = {}
        for kv in range(n_kv):
            k = k_ref[:, kv * hd:(kv + 1) * hd].astype(BF16)
            for g in range(group):
                s = _dot_nt(k, qs[kv, col(g), :])
                for a in adds_of(kv, g):
                    s = s + a
                scores[kv, g] = s
        alphas, probs = {}, {}
        for kv, g in chunks:
            m_prev = m_s[kv, :, col(g)]
            m_new = jnp.maximum(m_prev, jnp.max(scores[kv, g], axis=0, keepdims=True))
            alpha = jnp.exp2(m_prev - m_new)
            p = jnp.exp2(scores[kv, g] - m_new)
            l_s[kv, :, col(g)] = alpha * l_s[kv, :, col(g)] + jnp.sum(p, axis=0, keepdims=True)
            m_s[kv, :, col(g)] = m_new
            alphas[kv, g], probs[kv, g] = alpha, p.astype(BF16)
        for kv in range(n_kv):
            vt = v_ref[:, kv * hd:(kv + 1) * hd].T.astype(BF16)
            for g in range(group):
                acc_s[kv, :, col(g)] = alphas[kv, g] * acc_s[kv, :, col(g)] + _dot(vt, probs[kv, g])

    delta = qb - kb
    cs = lambda g: slice(g * blk_rows, (g + 1) * blk_rows)

    @pl.when(delta < n_near)
    def _():
        def adds_of(kv, g):
            diag = diag_ref[kv * group + g, pl.ds(delta, 1), :]
            tile = pltpu.roll(jnp.broadcast_to(diag, (blk_rows, 2 * blk_rows)), 0, 1, stride=1, stride_axis=0)
            chosen = (sel_s[kv, pl.ds(kb, 1), cs(g)] > 0.5) | (delta == 0)
            return tile[:, blk_rows:], jnp.where(chosen, 0.0, NEG)
        attend_all(adds_of)

    @pl.when(delta >= n_near)
    def _():
        def adds_of(kv, g):
            chosen = sel_s[kv, pl.ds(kb, 1), cs(g)] > 0.5
            return (jnp.where(chosen, far_ref[kv, :, cs(g)], NEG),)
        attend_all(adds_of)

    @pl.when(kb == qb)
    def _():
        for kv in range(n_kv):
            o = acc_s[kv] / l_s[kv]
            for g in range(group):
                h = kv * group + g
                o_ref[:, h * hd:(h + 1) * hd] = o[:, g * blk_rows:(g + 1) * blk_rows].T.astype(o_ref.dtype)


def _moba_prompt(qkv, means, diag, far_rows, bsz, n_heads, n_kv, hd):
    m = qkv.shape[0]
    t = m // bsz
    nb = t // MOBA_BLOCK
    group = n_heads // n_kv
    qw, kw = n_heads * hd, n_kv * hd
    kcol, vcol = qw // kw, qw // kw + 1
    n_near = diag.shape[1]
    cols = group * MOBA_BLOCK
    pairs = [(qb, kb) for qb in range(nb) for kb in range(qb + 1)]
    qb_tab = jnp.asarray([p[0] for p in pairs], jnp.int32)
    kb_tab = jnp.asarray([p[1] for p in pairs], jnp.int32)
    grid_spec = pltpu.PrefetchScalarGridSpec(
        num_scalar_prefetch=2,
        grid=(bsz, len(pairs)),
        in_specs=[pl.BlockSpec((MOBA_BLOCK, qw), lambda b, s, qt, kt: (b * nb + qt[s], 0)),
                  pl.BlockSpec((MOBA_BLOCK, kw), lambda b, s, qt, kt: (b * nb + kt[s], kcol)),
                  pl.BlockSpec((MOBA_BLOCK, kw), lambda b, s, qt, kt: (b * nb + kt[s], vcol)),
                  pl.BlockSpec((None, nb, kw), lambda b, s, qt, kt: (b, 0, 0)),
                  pl.BlockSpec(diag.shape, lambda b, s, qt, kt: (0, 0, 0)),
                  pl.BlockSpec((n_kv, 1, cols), lambda b, s, qt, kt: (0, 0, 0))],
        out_specs=pl.BlockSpec((MOBA_BLOCK, qw), lambda b, s, qt, kt: (b * nb + qt[s], 0)),
        scratch_shapes=[pltpu.VMEM((n_kv, cols, hd), BF16),
                        pltpu.VMEM((n_kv, nb, cols), F32),
                        pltpu.VMEM((n_kv, 1, cols), F32),
                        pltpu.VMEM((n_kv, 1, cols), F32),
                        pltpu.VMEM((n_kv, hd, cols), F32)],
    )
    return pl.pallas_call(
        functools.partial(_moba_prompt_kernel, n_kv=n_kv, group=group, hd=hd, scale=hd ** -0.5,
                          n_near=n_near, nb=nb),
        grid_spec=grid_spec,
        out_shape=jax.ShapeDtypeStruct((m, qw), BF16),
        compiler_params=_cparams("arbitrary", "arbitrary"),
    )(qb_tab, kb_tab, qkv, qkv, qkv, means, diag, far_rows)


_MEAN_PAGES_PER_STEP = 32
_ATTN_PAGES_PER_STEP = 16


def _pool_means_kernel(pt_ref, *refs, ppb, n_kv):
    del pt_ref
    o_ref = refs[-1]
    pages = refs[:-1]
    hd = pages[0].shape[1]
    sub = V7X_SUBLANES
    reps = sub // n_kv
    folded = []
    for blk in range(len(pages) // ppb):
        tot = None
        for p in range(ppb):
            part = jnp.sum(pages[blk * ppb + p][...].reshape(-1, sub, hd), axis=0)
            tot = part if tot is None else tot + part
        full = tot
        for i in range(1, reps):
            full = full + pltpu.roll(tot, i * n_kv, 0)
        folded.append(full)
    row_grp = lax.broadcasted_iota(jnp.int32, (sub, hd), 0) // n_kv
    tiles = []
    for a in range(0, len(folded), reps):
        tile = folded[a]
        for i in range(1, reps):
            tile = jnp.where(row_grp == i, folded[a + i], tile)
        tiles.append(tile)
    n_keys = ppb * pages[0].shape[0] // n_kv
    o_ref[...] = jnp.concatenate(tiles, axis=0) * (1.0 / n_keys)


def _pool_means(pool, pt_flat, page_off, dbs, n_pages, n_kv):
    rows, hd = pool.shape[1], pool.shape[2]
    ppb = MOBA_BLOCK // (rows // n_kv)
    pps = min(_MEAN_PAGES_PER_STEP, n_pages)
    blocks = pps // ppb
    assert n_pages % pps == 0 and pps % ppb == 0 and V7X_SUBLANES % n_kv == 0
    assert (blocks * n_kv) % V7X_SUBLANES == 0

    def page_spec(i):
        return pl.BlockSpec((None, rows, hd),
                            lambda b, s, pt: (pt[b * n_pages + s * pps + i] + page_off, 0, 0))

    grid_spec = pltpu.PrefetchScalarGridSpec(
        num_scalar_prefetch=1,
        grid=(dbs, n_pages // pps),
        in_specs=[page_spec(i) for i in range(pps)],
        out_specs=pl.BlockSpec((None, blocks * n_kv, hd), lambda b, s, pt: (b, s, 0)),
    )
    return pl.pallas_call(
        functools.partial(_pool_means_kernel, ppb=ppb, n_kv=n_kv),
        grid_spec=grid_spec,
        out_shape=jax.ShapeDtypeStruct((dbs, (n_pages // ppb) * n_kv, hd), F32),
        compiler_params=_cparams("arbitrary", "arbitrary"),
    )(pt_flat, *([pool] * pps))


def _cols_expand_kernel(tab_ref, idx_ref, o_ref, *, buckets, mult):
    idx = idx_ref[...]
    acc = jnp.full(idx.shape, NEG, F32)
    for b in buckets:
        acc = jnp.where(idx == b, tab_ref[b:b + 1, :] * mult, acc)
    o_ref[...] = acc


def _cols_expand(tab_cols, idx_np, tile_rows, mult=1.0):
    rows, cols = idx_np.shape
    buckets = tuple(int(b) for b in np.unique(idx_np) if b >= 0)
    return pl.pallas_call(
        functools.partial(_cols_expand_kernel, buckets=buckets, mult=mult),
        grid=(rows // tile_rows,),
        in_specs=[pl.BlockSpec(tab_cols.shape, lambda i: (0, 0)),
                  pl.BlockSpec((tile_rows, cols), lambda i: (i, 0))],
        out_specs=pl.BlockSpec((tile_rows, cols), lambda i: (i, 0)),
        out_shape=jax.ShapeDtypeStruct((rows, cols), F32),
        compiler_params=_cparams("arbitrary"),
    )(tab_cols, jnp.asarray(idx_np))


def _moba_sample_kernel(pt_ref, *refs, pps, ppb, n_kv, group, tp, n_blk, n_pages, far_cls, scale):
    del pt_ref
    k_pages = refs[:pps]
    v_pages = refs[pps:2 * pps]
    q_ref, means_ref, bias_ref, own_ref, kn_ref, vn_ref, o_ref, sel_s, m_s, l_s, acc_s = refs[2 * pps:]
    step = pl.program_id(1)
    cols, hd = q_ref.shape

    @pl.when(step == 0)
    def _():
        q = q_ref[...]
        blk = lax.broadcasted_iota(jnp.int32, (n_blk, cols), 0)
        col_kv = lax.broadcasted_iota(jnp.int32, (n_blk, cols), 1) // (group * tp)
        gate = jnp.zeros((n_blk, cols), F32)
        for kv in range(n_kv):
            g = _dot_nt(means_ref[kv * n_blk:(kv + 1) * n_blk, :], q, precision=lax.Precision.HIGHEST)
            gate = jnp.where(col_kv == kv, g, gate)
        rank = _rank_rows(gate, blk, n_blk)
        sel_s[...] = jnp.where(rank < MOBA_TOPK, 0.0, NEG)
        m_s[...] = jnp.full(m_s.shape, NEG, F32)
        l_s[...] = jnp.zeros(l_s.shape, F32)
        acc_s[...] = jnp.zeros(acc_s.shape, F32)

    qb = (q_ref[...] * (scale * LOG2E)).astype(BF16)
    page_keys = k_pages[0].shape[0] // n_kv
    col_kv = lax.broadcasted_iota(jnp.int32, (page_keys, cols), 1) // (group * tp)

    def accumulate(scores, values, expand):
        m_prev = m_s[...]
        m_new = m_prev
        for s in scores:
            m_new = jnp.maximum(m_new, jnp.max(s, axis=0, keepdims=True))
        alpha = jnp.exp2(m_prev - m_new)
        l_new = alpha * l_s[...]
        acc = alpha * acc_s[...]
        for s, v in zip(scores, values):
            p = jnp.exp2(s - m_new)
            l_new = l_new + jnp.sum(p, axis=0, keepdims=True)
            acc = acc + _dot(v.T.astype(BF16), expand(p).astype(BF16))
        m_s[...] = m_new
        l_s[...] = l_new
        acc_s[...] = acc

    def by_head(ref):
        return jnp.concatenate([ref[pl.ds(kv, page_keys, stride=n_kv), :] for kv in range(n_kv)], axis=0)

    def spread(p):
        return jnp.concatenate([jnp.where(col_kv == kv, p, 0.0) for kv in range(n_kv)], axis=0)

    scores, values = [], []
    for i in range(pps):
        page = step * pps + i
        cls = jnp.minimum(n_pages - 1 - page, far_cls)
        full = _dot_nt(by_head(k_pages[i]).astype(BF16), qb)
        s = full[0:page_keys]
        for kv in range(1, n_kv):
            s = jnp.where(col_kv == kv, full[kv * page_keys:(kv + 1) * page_keys], s)
        scores.append(s + bias_ref[cls] + sel_s[pl.ds(page // ppb, 1), :])
        values.append(by_head(v_pages[i]))
    accumulate(scores, values, spread)

    @pl.when(step == pl.num_programs(1) - 1)
    def _():
        s_own = _dot_nt(kn_ref[...].astype(BF16), qb) + own_ref[...]
        accumulate([s_own], [vn_ref[...]], lambda p: p)
        o_ref[...] = (acc_s[...] / l_s[...]).T


def _moba_sample(q_cols, means_t, k_new, v_new, pool_k, pool_v, pt_flat, page_off, bias_pages, bias_own, *,
                 n_kv, group, tp, n_pages):
    dbs, cols, hd = q_cols.shape
    rows = pool_k.shape[1]
    ppb = MOBA_BLOCK // (rows // n_kv)
    n_blk = n_pages // ppb
    pps = min(_ATTN_PAGES_PER_STEP, n_pages)
    assert n_pages % pps == 0
    far_cls = bias_pages.shape[0] - 1
    n_own = k_new.shape[1]

    def page_spec(i):
        return pl.BlockSpec((None, rows, hd),
                            lambda b, s, pt: (pt[b * n_pages + s * pps + i] + page_off, 0, 0))

    per_seq = lambda r: pl.BlockSpec((None, r, hd), lambda b, s, pt: (b, 0, 0))
    in_specs = ([page_spec(i) for i in range(pps)] * 2
                + [per_seq(cols), per_seq(n_kv * n_blk),
                   pl.BlockSpec(bias_pages.shape, lambda b, s, pt: (0, 0, 0)),
                   pl.BlockSpec(bias_own.shape, lambda b, s, pt: (0, 0)),
                   per_seq(n_own), per_seq(n_own)])
    grid_spec = pltpu.PrefetchScalarGridSpec(
        num_scalar_prefetch=1,
        grid=(dbs, n_pages // pps),
        in_specs=in_specs,
        out_specs=pl.BlockSpec((None, cols, hd), lambda b, s, pt: (b, 0, 0)),
        scratch_shapes=[pltpu.VMEM((n_blk, cols), F32), pltpu.VMEM((1, cols), F32),
                        pltpu.VMEM((1, cols), F32), pltpu.VMEM((hd, cols), F32)],
    )
    return pl.pallas_call(
        functools.partial(_moba_sample_kernel, pps=pps, ppb=ppb, n_kv=n_kv, group=group, tp=tp, n_blk=n_blk,
                          n_pages=n_pages, far_cls=far_cls, scale=hd ** -0.5),
        grid_spec=grid_spec,
        out_shape=jax.ShapeDtypeStruct((dbs, cols, hd), F32),
        compiler_params=_cparams("arbitrary", "arbitrary"),
    )(pt_flat, *([pool_k] * pps), *([pool_v] * pps), q_cols, means_t, bias_pages, bias_own, k_new, v_new)


def _swa_prompt_bias_idx(window, nbk):
    qi = np.arange(window)[:, None]
    kj = np.arange(2 * window)[None, :]
    dist = qi + window - kj
    return np.where((dist >= 0) & (dist < window), _bucket_np(dist, nbk), -1).astype(np.int32)


def _swa_sample_bias_idx(window, n_heads, tp, t_new, nbk):
    t = np.arange(n_heads * tp) % tp
    r = np.arange(window)
    dist = t[None, :] + window - r[:, None]
    idx = np.where((dist >= 0) & (dist < window), _bucket_np(dist, nbk), -1)
    return np.where(t[None, :] < t_new, idx, 0).astype(np.int32)


def _moba_prompt_diag_idx(n_near, nbk):
    dist = np.arange(n_near)[:, None] * MOBA_BLOCK + np.arange(2 * MOBA_BLOCK)[None, :] - MOBA_BLOCK
    return np.where(dist >= 0, _bucket_np(dist, nbk), -1).astype(np.int32)


def _moba_sample_page_idx(far_cls, page, n_heads, tp, t_new, nbk):
    cols = n_heads * tp
    t = np.arange(cols) % tp
    r = np.arange(page)
    tiles = []
    for c in range(far_cls + 1):
        dist = (c + 1) * page + t[None, :] - r[:, None]
        b = _bucket_np(dist, nbk) if c < far_cls else np.full(dist.shape, nbk - 1, np.int32)
        tiles.append(np.where(t[None, :] < t_new, b, 0))
    return np.concatenate(tiles, axis=0).astype(np.int32)


def _moba_sample_own_idx(n_kv, n_heads, group, tp, t_new, nbk):
    cols = n_heads * tp
    h, t = np.arange(cols) // tp, np.arange(cols) % tp
    tk, kvc = np.arange(tp * n_kv) // n_kv, np.arange(tp * n_kv) % n_kv
    ok = ((kvc[:, None] == (h // group)[None, :]) & (tk[:, None] <= t[None, :]) & (t[None, :] < t_new))
    return np.where(ok, _bucket_np(t[None, :] - tk[:, None], nbk), -1).astype(np.int32)


def kernel(x_prompt, x_sample, c_prompt, c_sample, state_conv, cache_swa_k, cache_swa_v, cache_moba_k, cache_moba_v, page_table, state_ffn, ada_w, ada_b, norm_mix, norm_ffn, norm_final, rel_bias, conv_w1, conv_b1, conv_dw, conv_dw_b, conv_ln_g, conv_ln_b, conv_w2, conv_b2, swa_wqkv, swa_wo, swa_sinks, moba_wqkv, moba_wo, ffn_w_up, ffn_dw, ffn_dw_b, ffn_w_down):
    bsz, seq, d = x_prompt.shape
    dbs, t_new, _ = x_sample.shape
    depth = ada_w.shape[0]
    window, n_kv, hd = cache_swa_k.shape[2], cache_swa_k.shape[3], cache_swa_k.shape[4]
    nbk, n_heads = rel_bias.shape
    group = n_heads // n_kv
    qw, kw = n_heads * hd, n_kv * hd
    cw = conv_dw.shape[1]
    d_ff = ffn_dw.shape[2]
    n_pool, page_size = cache_moba_k.shape[1], cache_moba_k.shape[2]
    n_pages = page_table.shape[1]
    ppb = MOBA_BLOCK // page_size
    n_blk_past = n_pages // ppb
    tp = V7X_SUBLANES
    assert t_new <= tp and n_pages % ppb == 0 and n_blk_past >= MOBA_TOPK and seq % MOBA_BLOCK == 0
    n_mixers = 3

    n_c = bsz + dbs
    c_rows = -(-n_c // V7X_SUBLANES) * V7X_SUBLANES
    c_all = jnp.concatenate([c_prompt, c_sample, jnp.zeros((c_rows - n_c, d), F32)], axis=0)
    mod = _ada(c_all, ada_w, ada_b)

    def mods(layer, which):
        chunk = mod[layer, :, which * d:(which + 1) * d]
        return chunk[:bsz], chunk[bsz:bsz + dbs]

    d_sat = _saturation_distance(nbk)
    n_near = -(-(d_sat + MOBA_BLOCK - 1) // MOBA_BLOCK)
    swa_bias_p = swa_bias_s = swa_bias_own = moba_bias_p = moba_far = moba_bias_sel = moba_bias_own = None
    tab_cols = jnp.repeat(rel_bias, tp, axis=1)
    own_idx = _moba_sample_own_idx(n_kv, n_heads, group, tp, t_new, nbk)
    if depth > 1:
        swa_bias_p = _bias_expand(rel_bias, _swa_prompt_bias_idx(window, nbk), window,
                                  (n_heads, window, 2 * window), (None, window, 2 * window),
                                  lambda h: (h, 0, 0))
        swa_bias_s = _cols_expand(tab_cols, _swa_sample_bias_idx(window, n_heads, tp, t_new, nbk), window)
        swa_bias_own = _cols_expand(tab_cols, own_idx, tp * n_kv)
    if depth > 2:
        cols = group * MOBA_BLOCK
        moba_bias_p = _bias_expand(
            rel_bias, _moba_prompt_diag_idx(n_near, nbk), n_near,
            (n_heads, n_near, 2 * MOBA_BLOCK), (None, n_near, 2 * MOBA_BLOCK),
            lambda h: (h, 0, 0), mult=LOG2E)
        moba_far = jnp.repeat(rel_bias[nbk - 1].reshape(n_kv, group, 1), MOBA_BLOCK, axis=2).reshape(n_kv, 1, cols)
        moba_far = moba_far * LOG2E
        far_cls = -(-(d_sat + page_size - 1) // page_size) - 1
        moba_bias_sel = _cols_expand(
            tab_cols, _moba_sample_page_idx(far_cls, page_size, n_heads, tp, t_new, nbk),
            page_size, mult=LOG2E).reshape(far_cls + 1, page_size, n_heads * tp)
        moba_bias_own = _cols_expand(tab_cols, own_idx, tp * n_kv, mult=LOG2E)

    def to_tm(a):
        return jnp.swapaxes(a, 0, 1).reshape((a.shape[1] * dbs,) + a.shape[2:])

    def to_bm(a, t):
        return jnp.swapaxes(a.reshape((t, dbs) + a.shape[1:]), 0, 1)

    xp = x_prompt.reshape(bsz * seq, d)
    xs = to_tm(x_sample)
    m_s = t_new * dbs

    conv_p, conv_s, swa_k_p, swa_v_p, swa_k_s, swa_v_s = [], [], [], [], [], []
    moba_k_p, moba_v_p, moba_k_s, moba_v_s, ffn_p, ffn_s = [], [], [], [], [], []

    def gates(layer, which):
        gp, gs = mods(layer, which)
        return gp.reshape(bsz, 1, d), jnp.tile(gs, (t_new, 1)).reshape(1, m_s, d)

    def normed(x_p, x_s, g, layer, which_sh):
        shp, shs = mods(layer, which_sh)
        scp, scs = mods(layer, which_sh + 1)
        hp = _norm_mod(x_p.reshape(bsz, seq, d), g, scp.reshape(bsz, 1, d), shp.reshape(bsz, 1, d), 512)
        hs = _norm_mod(x_s.reshape(t_new, dbs, d), g, scs.reshape(1, dbs, d), shs.reshape(1, dbs, d), dbs)
        return hp.reshape(bsz * seq, d), hs.reshape(m_s, d)

    def sample_views(qkv_bm):
        pad = jnp.concatenate([qkv_bm, jnp.zeros((dbs, tp - t_new, qkv_bm.shape[2]), qkv_bm.dtype)], axis=1)
        q_cols = pad[:, :, :qw].reshape(dbs, tp, n_heads, hd).transpose(0, 2, 1, 3).reshape(dbs, n_heads * tp, hd)
        return (q_cols, pad[:, :, qw:qw + kw].reshape(dbs, tp * n_kv, hd),
                pad[:, :, qw + kw:].reshape(dbs, tp * n_kv, hd))

    def cols_to_bm(o_cols):
        o = o_cols.reshape(dbs, n_heads, tp, hd)[:, :, :t_new].transpose(0, 2, 1, 3)
        return o.reshape(dbs, t_new, qw)

    for layer in range(depth):
        kind, j = layer % n_mixers, layer // n_mixers
        hp, hs = normed(xp, xs, norm_mix[layer], layer, 0)
        g1p, g1s = gates(layer, 2)
        if kind == 0:
            up, us = _glu(hp, hs, conv_w1, conv_b1, j)
            yp = _conv_ln_prompt(up.reshape(bsz, seq, d), conv_dw[j], conv_dw_b[j], conv_ln_g[j], conv_ln_b[j])
            st_tm = jnp.swapaxes(state_conv[j], 0, 1)
            ys, nst_tm = _conv_ln_sample(st_tm, us.reshape(t_new, dbs, d), conv_dw[j], conv_dw_b[j],
                                         conv_ln_g[j], conv_ln_b[j])
            conv_p.append(up.reshape(bsz, seq, d)[:, seq - (cw - 1):])
            conv_s.append(jnp.swapaxes(nst_tm, 0, 1))
            xp, xs = _mm(yp.reshape(bsz * seq, d), conv_w2, j, bias=conv_b2, res=xp, gate=g1p,
                         side=(ys.reshape(m_s, d), xs, g1s[0]))
        elif kind == 1:
            qkv_p, qkv_s = _mm(hp, swa_wqkv, j, side=(hs, None, None))
            op = _swa_prompt(qkv_p, swa_sinks[j], swa_bias_p, bsz, n_heads, n_kv, hd, window)
            q_cols, k_new, v_new = sample_views(to_bm(qkv_s, t_new))
            buf_k = cache_swa_k[j].reshape(dbs, window * n_kv, hd)
            buf_v = cache_swa_v[j].reshape(dbs, window * n_kv, hd)
            sink_row = jnp.repeat(swa_sinks[j], tp).reshape(1, n_heads * tp)
            o_cols = _swa_sample(q_cols, buf_k, buf_v, k_new, v_new, swa_bias_s, swa_bias_own, sink_row,
                                 n_kv=n_kv, group=group, tp=tp)
            os_ = to_tm(cols_to_bm(o_cols))
            tail = qkv_p.reshape(bsz, seq, qw + 2 * kw)[:, seq - window:]
            swa_k_p.append(tail[:, :, qw:qw + kw].reshape(bsz, window, n_kv, hd))
            swa_v_p.append(tail[:, :, qw + kw:].reshape(bsz, window, n_kv, hd))
            keep = t_new * n_kv
            swa_k_s.append(jnp.concatenate([buf_k[:, keep:], k_new[:, :keep]], axis=1)
                           .reshape(dbs, window, n_kv, hd))
            swa_v_s.append(jnp.concatenate([buf_v[:, keep:], v_new[:, :keep]], axis=1)
                           .reshape(dbs, window, n_kv, hd))
            xp, xs = _mm(op, swa_wo, j, res=xp, gate=g1p, side=(os_, xs, g1s[0]))
        else:
            qkv_p, k_rows, v_rows, qkv_s = _mm(hp, moba_wqkv, j, kv_heads=(n_kv, hd), side=(hs, None, None))
            nb = seq // MOBA_BLOCK
            means_p = _block_means(qkv_p, bsz * nb, qw // kw, kw).reshape(bsz, nb, kw)
            op = _moba_prompt(qkv_p, means_p, moba_bias_p, moba_far, bsz, n_heads, n_kv, hd)
            moba_k_p.append(k_rows.reshape(bsz, seq // page_size, page_size, n_kv, hd))
            moba_v_p.append(v_rows.reshape(bsz, seq // page_size, page_size, n_kv, hd))

            qkv_bm = to_bm(qkv_s, t_new)
            q_cols, k_new, v_new = sample_views(qkv_bm)
            pool_k = cache_moba_k.reshape(cache_moba_k.shape[0] * n_pool, page_size * n_kv, hd)
            pool_v = cache_moba_v.reshape(cache_moba_v.shape[0] * n_pool, page_size * n_kv, hd)
            pt_flat = page_table.reshape(-1)
            means_s = _pool_means(pool_k, pt_flat, j * n_pool, dbs, n_pages, n_kv)
            means_t = means_s.reshape(dbs, n_blk_past, n_kv, hd).transpose(0, 2, 1, 3)
            means_t = means_t.reshape(dbs, n_kv * n_blk_past, hd)
            o_cols = _moba_sample(q_cols, means_t, k_new, v_new, pool_k, pool_v, pt_flat, j * n_pool,
                                  moba_bias_sel, moba_bias_own, n_kv=n_kv, group=group, tp=tp, n_pages=n_pages)
            os_bm = cols_to_bm(o_cols).astype(BF16)
            moba_k_s.append(qkv_bm[:, :, qw:qw + kw].reshape(dbs, t_new, n_kv, hd))
            moba_v_s.append(qkv_bm[:, :, qw + kw:].reshape(dbs, t_new, n_kv, hd))
            xp, xs = _mm(op, moba_wo, j, res=xp, gate=g1p, side=(to_tm(os_bm), xs, g1s[0]))

        hp, hs = normed(xp, xs, norm_ffn[layer], layer, 3)
        g2p, g2s = gates(layer, 5)
        prev_tm = jnp.swapaxes(state_ffn[layer], 0, 1).reshape(2 * dbs, d_ff)
        ap, stp, as_, sts = _ffn_up(hp, hs, prev_tm, ffn_w_up, ffn_dw, ffn_dw_b, layer, bsz, dbs)
        ffn_p.append(stp)
        ffn_s.append(jnp.swapaxes(sts.reshape(2, dbs, d_ff), 0, 1))
        xp, xs = _mm(ap, ffn_w_down, layer, res=xp, gate=g2p, tn=256, weights_outer=False,
                     side=(as_, xs, g2s[0]))

    y_p = _norm_mod(xp.reshape(bsz, seq, d), norm_final, None, None, 512, out_dtype=F32)
    y_s = _norm_mod(xs.reshape(t_new, dbs, d), norm_final, None, None, dbs, out_dtype=F32)
    y_s = jnp.swapaxes(y_s, 0, 1)

    return (y_p, y_s, jnp.stack(conv_p), jnp.stack(conv_s), jnp.stack(swa_k_p), jnp.stack(swa_v_p),
            jnp.stack(swa_k_s), jnp.stack(swa_v_s), jnp.stack(moba_k_p), jnp.stack(moba_v_p),
            jnp.stack(moba_k_s), jnp.stack(moba_v_s), jnp.stack(ffn_p), jnp.stack(ffn_s))
```

```python
import functools
import math

import numpy as np
import jax
import jax.numpy as jnp
from jax import lax
from jax.experimental import pallas as pl
from jax.experimental.pallas import tpu as pltpu

MOBA_BLOCK = 256
MOBA_TOPK = 3
REL_MAX_DISTANCE = 1024
RMS_EPS = 1e-6
LN_EPS = 1e-5
N_MOD = 6

V7X_LANES = 128
V7X_SUBLANES = 8
V7X_VMEM_LIMIT_BYTES = 56 * 1024 * 1024

NEG = -1e30
LOG2E = 1.4426950408889634
BF16 = jnp.bfloat16
F32 = jnp.float32


def _cparams(*sem):
    return pltpu.CompilerParams(dimension_semantics=sem, vmem_limit_bytes=V7X_VMEM_LIMIT_BYTES)


def _dot(a, b):
    return jnp.dot(a, b, preferred_element_type=F32)


def _dot_nt(a, b, precision=None):
    return lax.dot_general(a, b, (((1,), (1,)), ((), ())), precision=precision,
                           preferred_element_type=F32)


def _silu(x):
    return x * jax.nn.sigmoid(x)


def _bucket_np(dist, num_buckets):
    n = np.maximum(dist, 0)
    max_exact = num_buckets // 2
    nf = np.maximum(n, 1).astype(np.float32)
    ratio = np.log(nf / np.float32(max_exact)) / np.float32(math.log(REL_MAX_DISTANCE / max_exact))
    large = max_exact + (ratio * np.float32(num_buckets - max_exact)).astype(np.int32)
    return np.where(n < max_exact, n, np.minimum(large, num_buckets - 1)).astype(np.int32)


def _saturation_distance(num_buckets):
    d = np.arange(0, 4 * REL_MAX_DISTANCE)
    b = _bucket_np(d, num_buckets)
    below = np.nonzero(b < num_buckets - 1)[0]
    return int(below.max()) + 1


def _bias_expand_kernel(tab_ref, idx_ref, o_ref, *, tiles, mult):
    h = pl.program_id(0)
    for n, (r0, nr, buckets) in enumerate(tiles):
        idx = idx_ref[r0:r0 + nr, :]
        acc = jnp.full(idx.shape, NEG, F32)
        for b in buckets:
            acc = jnp.where(idx == b, tab_ref[b, h] * mult, acc)
        if len(o_ref.shape) == 3:
            o_ref[n] = acc
        else:
            o_ref[r0:r0 + nr, :] = acc


def _bias_expand(table, idx_np, tile_rows, out_shape, out_block, out_index, mult=1.0):
    n_heads = table.shape[1]
    rows, cols = idx_np.shape
    tiles = []
    for r0 in range(0, rows, tile_rows):
        present = np.unique(idx_np[r0:r0 + tile_rows])
        tiles.append((r0, min(tile_rows, rows - r0), tuple(int(b) for b in present if b >= 0)))
    return pl.pallas_call(
        functools.partial(_bias_expand_kernel, tiles=tuple(tiles), mult=mult),
        grid=(n_heads,),
        in_specs=[pl.BlockSpec(memory_space=pltpu.SMEM),
                  pl.BlockSpec((rows, cols), lambda h: (0, 0))],
        out_specs=pl.BlockSpec(out_block, out_index),
        out_shape=jax.ShapeDtypeStruct(out_shape, F32),
        compiler_params=_cparams("arbitrary"),
    )(table, jnp.asarray(idx_np))


def _ada_kernel(c_ref, w_ref, b_ref, o_ref):
    a = _silu(c_ref[...]).astype(BF16)
    o_ref[...] = _dot(a, w_ref[...].astype(BF16)) + b_ref[...]


def _ada(c_all, ada_w, ada_b):
    depth, d, n = ada_w.shape
    rows = c_all.shape[0]
    tn = min(n, 2048)
    return pl.pallas_call(
        _ada_kernel,
        grid=(depth, n // tn),
        in_specs=[pl.BlockSpec((rows, d), lambda l, j: (0, 0)),
                  pl.BlockSpec((None, d, tn), lambda l, j: (l, 0, j)),
                  pl.BlockSpec((None, 1, tn), lambda l, j: (l, 0, j))],
        out_specs=pl.BlockSpec((None, rows, tn), lambda l, j: (l, 0, j)),
        out_shape=jax.ShapeDtypeStruct((depth, rows, n), F32),
        compiler_params=_cparams("arbitrary", "arbitrary"),
    )(c_all, ada_w, ada_b.reshape(depth, 1, n))


def _norm_mod_kernel(x_ref, g_ref, sc_ref, sh_ref, o_ref):
    x = x_ref[...]
    ms = jnp.mean(x * x, axis=-1, keepdims=True)
    y = x * lax.rsqrt(ms + RMS_EPS) * g_ref[...]
    o_ref[...] = (y * (1.0 + sc_ref[...]) + sh_ref[...]).astype(o_ref.dtype)


def _rms_kernel(x_ref, g_ref, o_ref):
    x = x_ref[...]
    ms = jnp.mean(x * x, axis=-1, keepdims=True)
    o_ref[...] = (x * lax.rsqrt(ms + RMS_EPS) * g_ref[...]).astype(o_ref.dtype)


def _norm_mod(x3, g, sc3, sh3, rows_per_step, out_dtype=BF16):
    a, r, d = x3.shape
    tr = min(rows_per_step, r)
    g3 = g.reshape(1, 1, d)
    if sc3 is None:
        return pl.pallas_call(
            _rms_kernel,
            grid=(a, r // tr),
            in_specs=[pl.BlockSpec((1, tr, d), lambda i, j: (i, j, 0)),
                      pl.BlockSpec((1, 1, d), lambda i, j: (0, 0, 0))],
            out_specs=pl.BlockSpec((1, tr, d), lambda i, j: (i, j, 0)),
            out_shape=jax.ShapeDtypeStruct(x3.shape, out_dtype),
            compiler_params=_cparams("arbitrary", "arbitrary"),
        )(x3, g3)
    per_seq = sc3.shape[0] == a and sc3.shape[1] == 1
    if per_seq:
        mspec = pl.BlockSpec((1, 1, d), lambda i, j: (i, 0, 0))
    else:
        assert sc3.shape[0] == 1 and sc3.shape[1] == r and tr == r
        mspec = pl.BlockSpec((1, r, d), lambda i, j: (0, 0, 0))
    return pl.pallas_call(
        _norm_mod_kernel,
        grid=(a, r // tr),
        in_specs=[pl.BlockSpec((1, tr, d), lambda i, j: (i, j, 0)),
                  pl.BlockSpec((1, 1, d), lambda i, j: (0, 0, 0)),
                  mspec, mspec],
        out_specs=pl.BlockSpec((1, tr, d), lambda i, j: (i, j, 0)),
        out_shape=jax.ShapeDtypeStruct(x3.shape, out_dtype),
        compiler_params=_cparams("arbitrary", "arbitrary"),
    )(x3, g3, sc3, sh3)


def _cast_kernel(x_ref, o_ref):
    o_ref[...] = x_ref[...].astype(o_ref.dtype)


def _to_bf16(w3, row_tiles=8):
    l, k, n = w3.shape
    tk = k // row_tiles if k % (row_tiles * 2 * V7X_SUBLANES) == 0 else k
    return pl.pallas_call(
        _cast_kernel,
        grid=(l, k // tk),
        in_specs=[pl.BlockSpec((None, tk, n), lambda a, b: (a, b, 0))],
        out_specs=pl.BlockSpec((None, tk, n), lambda a, b: (a, b, 0)),
        out_shape=jax.ShapeDtypeStruct(w3.shape, BF16),
        compiler_params=_cparams("arbitrary", "arbitrary"),
    )(w3)


def _mm_kernel(*refs, names, cache_w, kv_heads):
    r = dict(zip(names, refs))
    if cache_w:
        @pl.when(pl.program_id(1) == 0)
        def _():
            r["w_s"][...] = r["w"][...].astype(BF16)
        w = r["w_s"][...]
    else:
        w = r["w"][...].astype(BF16)

    def product(x_name, res_name, gate_name):
        acc = _dot(r[x_name][...], w)
        if "bias" in r:
            acc = acc + r["bias"][...]
        if res_name in r:
            acc = r[res_name][...] + r[gate_name][...] * acc
        return acc

    acc = product("x", "res", "gate")
    r["o"][...] = acc.astype(r["o"].dtype)
    if kv_heads:
        n_kv, hd = kv_heads
        rows = acc.shape[0]

        @pl.when(pl.program_id(0) == pl.num_programs(0) - 1)
        def _():
            for kv in range(n_kv):
                r["k"][pl.ds(kv, rows, stride=n_kv), :] = acc[:, kv * hd:(kv + 1) * hd]
                r["v"][pl.ds(kv, rows, stride=n_kv), :] = acc[:, (n_kv + kv) * hd:(n_kv + kv + 1) * hd]
    if "x2" in r:
        row_tile = pl.program_id(1) if cache_w else pl.program_id(0)
        last = (pl.num_programs(1) - 1) if cache_w else 0

        @pl.when(row_tile == last)
        def _():
            r["o2"][...] = product("x2", "res2", "gate2").astype(r["o2"].dtype)


def _mm(x, w, layer, *, bias=None, res=None, gate=None, out_dtype=F32, tm=1024, tn=1024, weights_outer=True,
        kv_heads=None, side=None):
    m, k = x.shape
    n = w.shape[2]
    tm, tn = min(tm, m), min(tn, n)
    assert m % tm == 0 and n % tn == 0
    if weights_outer:
        grid = (n // tn, m // tm)
        ij = lambda a, b: (b, a)
    else:
        grid = (m // tm, n // tn)
        ij = lambda a, b: (a, b)

    def spec(block, fn):
        return pl.BlockSpec(block, lambda a, b: fn(*ij(a, b)))

    names = ["x", "w"]
    in_specs = [spec((tm, k), lambda i, j: (i, 0)),
                spec((None, k, tn), lambda i, j: (layer, 0, j))]
    args = [x, w]
    if bias is not None:
        names.append("bias")
        in_specs.append(spec((None, 1, tn), lambda i, j: (layer, 0, j)))
        args.append(bias.reshape(bias.shape[0], 1, n))
    if res is not None:
        gs, gr, _ = gate.shape
        assert gr in (1, tm) and (m // tm) % gs == 0
        tiles_per_gate = (m // tm) // gs
        names += ["res", "gate"]
        in_specs.append(spec((tm, tn), lambda i, j: (i, j)))
        in_specs.append(spec((None, gr, tn), lambda i, j: (i // tiles_per_gate, 0, j)))
        args += [res, gate]
    if side is not None:
        x2, res2, gate2 = side
        m2 = x2.shape[0]
        nj_side = n // tn
        if weights_outer:
            side_col = lambda i, j: j
        else:
            side_col = lambda i, j: jnp.where(i == 0, j, nj_side - 1)
        names.append("x2")
        in_specs.append(spec((m2, k), lambda i, j: (0, 0)))
        args.append(x2)
        if res2 is not None:
            names += ["res2", "gate2"]
            in_specs += [spec((m2, tn), lambda i, j: (0, side_col(i, j)))] * 2
            args += [res2, gate2]
    names.append("o")
    out_specs = [spec((tm, tn), lambda i, j: (i, j))]
    out_shape = [jax.ShapeDtypeStruct((m, n), out_dtype)]
    if kv_heads:
        n_kv, hd = kv_heads
        nj = n // tn
        assert weights_outer and tn == 2 * n_kv * hd
        kv_spec = spec((tm * n_kv, hd), lambda i, j: (jnp.where(j == nj - 1, i, 0), 0))
        names += ["k", "v"]
        out_specs += [kv_spec, kv_spec]
        out_shape += [jax.ShapeDtypeStruct((m * n_kv, hd), F32)] * 2
    if side is not None:
        names.append("o2")
        out_specs.append(spec((m2, tn), lambda i, j: (0, side_col(i, j))))
        out_shape.append(jax.ShapeDtypeStruct((m2, n), out_dtype))
    if weights_outer:
        names.append("w_s")
    outs = pl.pallas_call(
        functools.partial(_mm_kernel, names=tuple(names), cache_w=weights_outer, kv_heads=kv_heads),
        grid=grid,
        in_specs=in_specs,
        out_specs=out_specs,
        out_shape=out_shape,
        scratch_shapes=[pltpu.VMEM((k, tn), BF16)] if weights_outer else [],
        compiler_params=_cparams("arbitrary", "arbitrary"),
    )(*args)
    return outs[0] if len(outs) == 1 else tuple(outs)


def _glu_kernel(x_ref, x2_ref, wa_ref, wg_ref, ba_ref, bg_ref, o_ref, o2_ref, wa_s, wg_s):
    @pl.when(pl.program_id(1) == 0)
    def _():
        wa_s[...] = wa_ref[...].astype(BF16)
        wg_s[...] = wg_ref[...].astype(BF16)

    def glu(x):
        a = _dot(x, wa_s[...]) + ba_ref[...]
        g = _dot(x, wg_s[...]) + bg_ref[...]
        return a * jax.nn.sigmoid(g)

    o_ref[...] = glu(x_ref[...])

    @pl.when(pl.program_id(1) == pl.num_programs(1) - 1)
    def _():
        o2_ref[...] = glu(x2_ref[...])


def _glu(x, x2, w1, b1, layer, *, tm=1024, tn=512):
    m, k = x.shape
    m2 = x2.shape[0]
    n = w1.shape[2] // 2
    tm, tn = min(tm, m), min(tn, n)
    nj = n // tn
    b3 = b1.reshape(b1.shape[0], 1, 2 * n)
    return pl.pallas_call(
        _glu_kernel,
        grid=(nj, m // tm),
        in_specs=[pl.BlockSpec((tm, k), lambda j, i: (i, 0)),
                  pl.BlockSpec((m2, k), lambda j, i: (0, 0)),
                  pl.BlockSpec((None, k, tn), lambda j, i: (layer, 0, j)),
                  pl.BlockSpec((None, k, tn), lambda j, i: (layer, 0, nj + j)),
                  pl.BlockSpec((None, 1, tn), lambda j, i: (layer, 0, j)),
                  pl.BlockSpec((None, 1, tn), lambda j, i: (layer, 0, nj + j))],
        out_specs=[pl.BlockSpec((tm, tn), lambda j, i: (i, j)),
                   pl.BlockSpec((m2, tn), lambda j, i: (0, j))],
        out_shape=[jax.ShapeDtypeStruct((m, n), F32), jax.ShapeDtypeStruct((m2, n), F32)],
        scratch_shapes=[pltpu.VMEM((k, tn), BF16), pltpu.VMEM((k, tn), BF16)],
        compiler_params=_cparams("arbitrary", "arbitrary"),
    )(x, x2, w1, w1, b3, b3)


def _layernorm_silu(y, lg, lb):
    mu = jnp.mean(y, axis=-1, keepdims=True)
    yc = y - mu
    var = jnp.mean(yc * yc, axis=-1, keepdims=True)
    return _silu(yc * lax.rsqrt(var + LN_EPS) * lg + lb)


def _conv_ln_prompt_kernel(u_ref, halo_ref, dw_ref, dwb_ref, lg_ref, lb_ref, o_ref, full_s, y_s,
                           *, tm, halo, cw, rc, lw):
    i = pl.program_id(1)
    d = u_ref.shape[-1]
    full_s[halo:halo + tm, :] = u_ref[...]

    @pl.when(i == 0)
    def _():
        full_s[0:halo, :] = jnp.zeros((halo, d), F32)

    @pl.when(i > 0)
    def _():
        full_s[0:halo, :] = halo_ref[...]

    off = halo - (cw - 1)
    win = rc + halo
    sub = V7X_SUBLANES

    def body(r, carry):
        r0 = pl.multiple_of(r * rc, rc)
        for c0 in range(0, d, lw):
            window = full_s[pl.ds(r0, win), c0:c0 + lw]
            acc = jnp.zeros((rc, lw), F32) + dwb_ref[:, c0:c0 + lw]
            for s in range(sub):
                taps = [k for k in range(cw) if (off + k) % sub == s]
                if not taps:
                    continue
                shifted = window if s == 0 else pltpu.roll(window, win - s, 0)
                for k in taps:
                    a = off + k - s
                    acc = acc + dw_ref[k:k + 1, c0:c0 + lw] * shifted[a:a + rc]
            y_s[pl.ds(r0, rc), c0:c0 + lw] = acc
        return carry
    lax.fori_loop(0, tm // rc, body, 0)

    o_ref[...] = _layernorm_silu(y_s[...], lg_ref[...], lb_ref[...]).astype(o_ref.dtype)


def _conv_ln_prompt(u3, dw, dwb, lg, lb, *, tm=256):
    bsz, t, d = u3.shape
    cw = dw.shape[0]
    halo = 32
    assert cw - 1 <= halo and t % tm == 0 and tm % halo == 0
    hb = tm // halo
    lw = min(d, V7X_LANES)
    rc = min(tm, 4 * halo)
    return pl.pallas_call(
        functools.partial(_conv_ln_prompt_kernel, tm=tm, halo=halo, cw=cw, rc=rc, lw=lw),
        grid=(bsz, t // tm),
        in_specs=[pl.BlockSpec((None, tm, d), lambda b, i: (b, i, 0)),
                  pl.BlockSpec((None, halo, d), lambda b, i: (b, jnp.maximum(i * hb - 1, 0), 0)),
                  pl.BlockSpec((cw, d), lambda b, i: (0, 0)),
                  pl.BlockSpec((1, d), lambda b, i: (0, 0)),
                  pl.BlockSpec((1, d), lambda b, i: (0, 0)),
                  pl.BlockSpec((1, d), lambda b, i: (0, 0))],
        out_specs=pl.BlockSpec((None, tm, d), lambda b, i: (b, i, 0)),
        out_shape=jax.ShapeDtypeStruct((bsz, t, d), BF16),
        scratch_shapes=[pltpu.VMEM((tm + halo, d), F32), pltpu.VMEM((tm, d), F32)],
        compiler_params=_cparams("arbitrary", "arbitrary"),
    )(u3, u3, dw, dwb.reshape(1, d), lg.reshape(1, d), lb.reshape(1, d))


def _conv_ln_sample_kernel(st_ref, u_ref, dw_ref, dwb_ref, lg_ref, lb_ref, o_ref, nst_ref, *, cw, lw):
    t_new, _, d = u_ref.shape
    n_st = cw - 1

    def row(idx, c0):
        if idx < n_st:
            return st_ref[idx, :, c0:c0 + lw]
        return u_ref[idx - n_st, :, c0:c0 + lw]

    for t in range(t_new):
        parts = []
        for c0 in range(0, d, lw):
            acc = dw_ref[0:1, c0:c0 + lw] * row(t, c0)
            for k in range(1, cw):
                acc = acc + dw_ref[k:k + 1, c0:c0 + lw] * row(t + k, c0)
            parts.append(acc + dwb_ref[:, c0:c0 + lw])
        y = jnp.concatenate(parts, axis=-1) if len(parts) > 1 else parts[0]
        o_ref[t] = _layernorm_silu(y, lg_ref[...], lb_ref[...]).astype(o_ref.dtype)
    for r in range(n_st):
        idx = t_new + r
        nst_ref[r] = st_ref[idx] if idx < n_st else u_ref[idx - n_st]


def _conv_ln_sample(st_tm, u_tm, dw, dwb, lg, lb):
    n_st, bsz, d = st_tm.shape
    t_new = u_tm.shape[0]
    cw = dw.shape[0]
    lw = min(d, 512)
    full = lambda shape: pl.BlockSpec(shape, lambda i: (0,) * len(shape))
    return pl.pallas_call(
        functools.partial(_conv_ln_sample_kernel, cw=cw, lw=lw),
        grid=(1,),
        in_specs=[full((n_st, bsz, d)), full((t_new, bsz, d)), full((cw, d)), full((1, d)), full((1, d)),
                  full((1, d))],
        out_specs=[full((t_new, bsz, d)), full((n_st, bsz, d))],
        out_shape=[jax.ShapeDtypeStruct((t_new, bsz, d), BF16),
                   jax.ShapeDtypeStruct((n_st, bsz, d), F32)],
        compiler_params=_cparams("arbitrary"),
    )(st_tm, u_tm, dw, dwb.reshape(1, d), lg.reshape(1, d), lb.reshape(1, d))


def _ffn_up_kernel(x_ref, x2_ref, prev2_ref, wg_ref, wv_ref, dw_ref, dwb_ref, a_ref, st_ref, a2_ref, st2_ref,
                   wg_s, wv_s, carry_s, *, tiles_per_seq, bsz2):
    i = pl.program_id(1)

    @pl.when(i == 0)
    def _():
        wg_s[...] = wg_ref[...].astype(BF16)
        wv_s[...] = wv_ref[...].astype(BF16)

    @pl.when(i % tiles_per_seq == 0)
    def _():
        carry_s[...] = jnp.zeros(carry_s.shape, F32)

    x = x_ref[...]
    g = _dot(x, wg_s[...])
    v = _dot(x, wv_s[...])
    tm = g.shape[0]
    row = lax.broadcasted_iota(jnp.int32, g.shape, 0)
    prev1 = carry_s[V7X_SUBLANES - 1:V7X_SUBLANES, :]
    prev2 = carry_s[V7X_SUBLANES - 2:V7X_SUBLANES - 1, :]
    g1 = jnp.where(row == 0, prev1, pltpu.roll(g, 1, 0))
    g2 = jnp.where(row == 0, prev2, jnp.where(row == 1, prev1, pltpu.roll(g, 2, 0)))
    gc = dw_ref[0:1, :] * g2 + dw_ref[1:2, :] * g1 + dw_ref[2:3, :] * g + dwb_ref[...]
    a_ref[...] = (_silu(gc) * v).astype(a_ref.dtype)
    carry_s[...] = g[tm - V7X_SUBLANES:tm, :]
    st_ref[...] = g[tm - 2:tm, :]

    @pl.when(i == pl.num_programs(1) - 1)
    def _():
        x2 = x2_ref[...]
        g_s = _dot(x2, wg_s[...])
        v_s = _dot(x2, wv_s[...])
        rows = g_s.shape[0]
        full = jnp.concatenate([prev2_ref[...], g_s], axis=0)
        gc_s = (dw_ref[0:1, :] * full[0:rows] + dw_ref[1:2, :] * full[bsz2:bsz2 + rows]
                + dw_ref[2:3, :] * full[2 * bsz2:2 * bsz2 + rows] + dwb_ref[...])
        a2_ref[...] = (_silu(gc_s) * v_s).astype(a2_ref.dtype)
        st2_ref[...] = full[rows:rows + 2 * bsz2]


def _ffn_up(x, x2, prev2_tm, w_up, dw, dwb, layer, bsz, bsz2, *, tm=1024, tn=512):
    m, k = x.shape
    m2 = x2.shape[0]
    n = w_up.shape[2] // 2
    t = m // bsz
    tm, tn = min(tm, t), min(tn, n)
    assert t % tm == 0 and n % tn == 0 and dw.shape[1] == 3 and m2 >= 2 * bsz2
    nj = n // tn
    tiles_per_seq = t // tm
    return pl.pallas_call(
        functools.partial(_ffn_up_kernel, tiles_per_seq=tiles_per_seq, bsz2=bsz2),
        grid=(nj, m // tm),
        in_specs=[pl.BlockSpec((tm, k), lambda j, i: (i, 0)),
                  pl.BlockSpec((m2, k), lambda j, i: (0, 0)),
                  pl.BlockSpec((2 * bsz2, tn), lambda j, i: (0, j)),
                  pl.BlockSpec((None, k, tn), lambda j, i: (layer, 0, j)),
                  pl.BlockSpec((None, k, tn), lambda j, i: (layer, 0, nj + j)),
                  pl.BlockSpec((None, 3, tn), lambda j, i: (layer, 0, j)),
                  pl.BlockSpec((None, 1, tn), lambda j, i: (layer, 0, j))],
        out_specs=[pl.BlockSpec((tm, tn), lambda j, i: (i, j)),
                   pl.BlockSpec((None, 2, tn), lambda j, i: (i // tiles_per_seq, 0, j)),
                   pl.BlockSpec((m2, tn), lambda j, i: (0, j)),
                   pl.BlockSpec((2 * bsz2, tn), lambda j, i: (0, j))],
        out_shape=[jax.ShapeDtypeStruct((m, n), BF16),
                   jax.ShapeDtypeStruct((bsz, 2, n), F32),
                   jax.ShapeDtypeStruct((m2, n), BF16),
                   jax.ShapeDtypeStruct((2 * bsz2, n), F32)],
        scratch_shapes=[pltpu.VMEM((k, tn), BF16), pltpu.VMEM((k, tn), BF16),
                        pltpu.VMEM((V7X_SUBLANES, tn), F32)],
        compiler_params=_cparams("arbitrary", "arbitrary"),
    )(x, x2, prev2_tm, w_up, w_up, dw, dwb.reshape(dwb.shape[0], 1, n))


def _swa_prompt_kernel(sink_ref, q_ref, kc_ref, kp_ref, vc_ref, vp_ref, bias_ref, o_ref, *, n_heads, group, hd,
                       scale):
    n = pl.program_id(1)
    w = q_ref.shape[0]
    kcat = jnp.concatenate([kp_ref[...], kc_ref[...]], axis=0).astype(BF16)
    vcat = jnp.concatenate([vp_ref[...], vc_ref[...]], axis=0).astype(BF16)
    col = lax.broadcasted_iota(jnp.int32, (w, 2 * w), 1)
    keep = (col >= w) | (n > 0)
    for h in range(n_heads):
        kv = h // group
        q = q_ref[:, h * hd:(h + 1) * hd].astype(BF16)
        s = _dot_nt(q, kcat[:, kv * hd:(kv + 1) * hd]) * scale + bias_ref[h]
        s = jnp.where(keep, s, NEG)
        sink = sink_ref[h]
        m = jnp.maximum(jnp.max(s, axis=-1, keepdims=True), sink)
        p = jnp.exp(s - m)
        l = jnp.sum(p, axis=-1, keepdims=True) + jnp.exp(sink - m)
        o = _dot(p.astype(BF16), vcat[:, kv * hd:(kv + 1) * hd]) / l
        o_ref[:, h * hd:(h + 1) * hd] = o.astype(o_ref.dtype)


def _swa_prompt(qkv, sinks, bias, bsz, n_heads, n_kv, hd, window):
    m = qkv.shape[0]
    t = m // bsz
    nb = t // window
    qw, kw = n_heads * hd, n_kv * hd
    assert qw % kw == 0 and t % window == 0
    kcol, vcol = qw // kw, qw // kw + 1
    return pl.pallas_call(
        functools.partial(_swa_prompt_kernel, n_heads=n_heads, group=n_heads // n_kv, hd=hd, scale=hd ** -0.5),
        grid=(bsz, nb),
        in_specs=[pl.BlockSpec(memory_space=pltpu.SMEM),
                  pl.BlockSpec((window, qw), lambda b, n: (b * nb + n, 0)),
                  pl.BlockSpec((window, kw), lambda b, n: (b * nb + n, kcol)),
                  pl.BlockSpec((window, kw), lambda b, n: (b * nb + jnp.maximum(n - 1, 0), kcol)),
                  pl.BlockSpec((window, kw), lambda b, n: (b * nb + n, vcol)),
                  pl.BlockSpec((window, kw), lambda b, n: (b * nb + jnp.maximum(n - 1, 0), vcol)),
                  pl.BlockSpec((n_heads, window, 2 * window), lambda b, n: (0, 0, 0))],
        out_specs=pl.BlockSpec((window, qw), lambda b, n: (b * nb + n, 0)),
        out_shape=jax.ShapeDtypeStruct((m, qw), BF16),
        compiler_params=_cparams("arbitrary", "arbitrary"),
    )(sinks, qkv, qkv, qkv, qkv, qkv, bias)


def _swa_sample_kernel(q_ref, bk_ref, bv_ref, kn_ref, vn_ref, bias_ref, own_ref, sink_ref, o_ref,
                       *, n_kv, group, tp, scale):
    cols, hd = q_ref.shape
    keys = bk_ref.shape[0] // n_kv
    qb = q_ref[...].astype(BF16)
    col_kv = lax.broadcasted_iota(jnp.int32, (keys, cols), 1) // (group * tp)

    def by_head(ref):
        return jnp.concatenate([ref[pl.ds(kv, keys, stride=n_kv), :] for kv in range(n_kv)], axis=0)

    full = _dot_nt(by_head(bk_ref).astype(BF16), qb)
    s = full[0:keys]
    for kv in range(1, n_kv):
        s = jnp.where(col_kv == kv, full[kv * keys:(kv + 1) * keys], s)
    s = s * scale + bias_ref[...]
    s_own = _dot_nt(kn_ref[...].astype(BF16), qb) * scale + own_ref[...]
    sink = sink_ref[...]
    m = jnp.maximum(jnp.maximum(jnp.max(s, axis=0, keepdims=True), jnp.max(s_own, axis=0, keepdims=True)), sink)
    p = jnp.exp(s - m)
    p_own = jnp.exp(s_own - m)
    l = jnp.sum(p, axis=0, keepdims=True) + jnp.sum(p_own, axis=0, keepdims=True) + jnp.exp(sink - m)
    spread = jnp.concatenate([jnp.where(col_kv == kv, p, 0.0) for kv in range(n_kv)], axis=0)
    acc = (_dot(by_head(bv_ref).T.astype(BF16), spread.astype(BF16))
           + _dot(vn_ref[...].T.astype(BF16), p_own.astype(BF16)))
    o_ref[...] = (acc / l).T.astype(o_ref.dtype)


def _swa_sample(q_cols, buf_k, buf_v, k_new, v_new, bias, own, sink_row, *, n_kv, group, tp):
    dbs, cols, hd = q_cols.shape
    per_seq = lambda a: pl.BlockSpec((None,) + a.shape[1:], lambda b: (b, 0, 0))
    shared = lambda a: pl.BlockSpec(a.shape, lambda b: (0, 0))
    return pl.pallas_call(
        functools.partial(_swa_sample_kernel, n_kv=n_kv, group=group, tp=tp, scale=hd ** -0.5),
        grid=(dbs,),
        in_specs=[per_seq(q_cols), per_seq(buf_k), per_seq(buf_v), per_seq(k_new), per_seq(v_new),
                  shared(bias), shared(own), shared(sink_row)],
        out_specs=pl.BlockSpec((None, cols, hd), lambda b: (b, 0, 0)),
        out_shape=jax.ShapeDtypeStruct((dbs, cols, hd), BF16),
        compiler_params=_cparams("arbitrary"),
    )(q_cols, buf_k, buf_v, k_new, v_new, bias, own, sink_row)


def _block_means_kernel(k_ref, o_ref):
    o_ref[...] = jnp.mean(k_ref[...], axis=0, keepdims=True)


def _block_means(qkv, n_blocks_total, kcol, kw):
    return pl.pallas_call(
        _block_means_kernel,
        grid=(n_blocks_total,),
        in_specs=[pl.BlockSpec((MOBA_BLOCK, kw), lambda i: (i, kcol))],
        out_specs=pl.BlockSpec((None, 1, kw), lambda i: (i, 0, 0)),
        out_shape=jax.ShapeDtypeStruct((n_blocks_total, 1, kw), F32),
        compiler_params=_cparams("arbitrary"),
    )(qkv)


def _rank_rows(gate, blk, n_rows):
    rank = jnp.zeros(gate.shape, jnp.int32)
    for m in range(n_rows):
        gm = gate[m:m + 1, :]
        beats = (gm > gate) | ((gm == gate) & (blk > m))
        rank = rank + jnp.where(beats, 1, 0)
    return rank


def _moba_prompt_kernel(qb_ref, kb_ref, q_ref, k_ref, v_ref, means_ref, diag_ref, far_ref, o_ref,
                        qs, sel_s, m_s, l_s, acc_s, *, n_kv, group, hd, scale, n_near, nb):
    qb = qb_ref[pl.program_id(1)]
    kb = kb_ref[pl.program_id(1)]
    blk_rows = q_ref.shape[0]
    cols = group * blk_rows

    @pl.when(kb == 0)
    def _():
        blk = lax.broadcasted_iota(jnp.int32, (nb, cols), 0)
        for kv in range(n_kv):
            qg = jnp.concatenate([q_ref[:, (kv * group + g) * hd:(kv * group + g + 1) * hd]
                                  for g in range(group)], axis=0)
            qs[kv] = (qg * (scale * LOG2E)).astype(BF16)
            gate = _dot_nt(means_ref[:, kv * hd:(kv + 1) * hd], qg, precision=lax.Precision.HIGHEST)
            valid = blk < qb
            gate = jnp.where(valid, gate, -jnp.inf)
            rank = _rank_rows(gate, blk, nb)
            sel_s[kv] = jnp.where(valid & (rank < MOBA_TOPK), 1.0, 0.0)
            m_s[kv] = jnp.full((1, cols), NEG, F32)
            l_s[kv] = jnp.zeros((1, cols), F32)
            acc_s[kv] = jnp.zeros((hd, cols), F32)

    def attend_all(adds_of):
        chunks = [(kv, g) for kv in range(n_kv) for g in range(group)]
        col = lambda g: slice(g * blk_rows, (g + 1) * blk_rows)
        scores = {}
        for kv in range(n_kv):
            k = k_ref[:, kv * hd:(kv + 1) * hd].astype(BF16)
            for g in range(group):
                s = _dot_nt(k, qs[kv, col(g), :])
                for a in adds_of(kv, g):
                    s = s + a
                scores[kv, g] = s
        alphas, probs = {}, {}
        for kv, g in chunks:
            m_prev = m_s[kv, :, col(g)]
            m_new = jnp.maximum(m_prev, jnp.max(scores[kv, g], axis=0, keepdims=True))
            alpha = jnp.exp2(m_prev - m_new)
            p = jnp.exp2(scores[kv, g] - m_new)
            l_s[kv, :, col(g)] = alpha * l_s[kv, :, col(g)] + jnp.sum(p, axis=0, keepdims=True)
            m_s[kv, :, col(g)] = m_new
            alphas[kv, g], probs[kv, g] = alpha, p.astype(BF16)
        for kv in range(n_kv):
            vt = v_ref[:, kv * hd:(kv + 1) * hd].T.astype(BF16)
            for g in range(group):
                acc_s[kv, :, col(g)] = alphas[kv, g] * acc_s[kv, :, col(g)] + _dot(vt, probs[kv, g])

    delta = qb - kb
    cs = lambda g: slice(g * blk_rows, (g + 1) * blk_rows)

    @pl.when(delta < n_near)
    def _():
        def adds_of(kv, g):
            diag = diag_ref[kv * group + g, pl.ds(delta, 1), :]
            tile = pltpu.roll(jnp.broadcast_to(diag, (blk_rows, 2 * blk_rows)), 0, 1, stride=1, stride_axis=0)
            chosen = (sel_s[kv, pl.ds(kb, 1), cs(g)] > 0.5) | (delta == 0)
            return tile[:, blk_rows:], jnp.where(chosen, 0.0, NEG)
        attend_all(adds_of)

    @pl.when(delta >= n_near)
    def _():
        def adds_of(kv, g):
            chosen = sel_s[kv, pl.ds(kb, 1), cs(g)] > 0.5
            return (jnp.where(chosen, far_ref[kv, :, cs(g)], NEG),)
        attend_all(adds_of)

    @pl.when(kb == qb)
    def _():
        for kv in range(n_kv):
            o = acc_s[kv] / l_s[kv]
            for g in range(group):
                h = kv * group + g
                o_ref[:, h * hd:(h + 1) * hd] = o[:, g * blk_rows:(g + 1) * blk_rows].T.astype(o_ref.dtype)


def _moba_prompt(qkv, means, diag, far_rows, bsz, n_heads, n_kv, hd):
    m = qkv.shape[0]
    t = m // bsz
    nb = t // MOBA_BLOCK
    group = n_heads // n_kv
    qw, kw = n_heads * hd, n_kv * hd
    kcol, vcol = qw // kw, qw // kw + 1
    n_near = diag.shape[1]
    cols = group * MOBA_BLOCK
    pairs = [(qb, kb) for qb in range(nb) for kb in range(qb + 1)]
    qb_tab = jnp.asarray([p[0] for p in pairs], jnp.int32)
    kb_tab = jnp.asarray([p[1] for p in pairs], jnp.int32)
    grid_spec = pltpu.PrefetchScalarGridSpec(
        num_scalar_prefetch=2,
        grid=(bsz, len(pairs)),
        in_specs=[pl.BlockSpec((MOBA_BLOCK, qw), lambda b, s, qt, kt: (b * nb + qt[s], 0)),
                  pl.BlockSpec((MOBA_BLOCK, kw), lambda b, s, qt, kt: (b * nb + kt[s], kcol)),
                  pl.BlockSpec((MOBA_BLOCK, kw), lambda b, s, qt, kt: (b * nb + kt[s], vcol)),
                  pl.BlockSpec((None, nb, kw), lambda b, s, qt, kt: (b, 0, 0)),
                  pl.BlockSpec(diag.shape, lambda b, s, qt, kt: (0, 0, 0)),
                  pl.BlockSpec((n_kv, 1, cols), lambda b, s, qt, kt: (0, 0, 0))],
        out_specs=pl.BlockSpec((MOBA_BLOCK, qw), lambda b, s, qt, kt: (b * nb + qt[s], 0)),
        scratch_shapes=[pltpu.VMEM((n_kv, cols, hd), BF16),
                        pltpu.VMEM((n_kv, nb, cols), F32),
                        pltpu.VMEM((n_kv, 1, cols), F32),
                        pltpu.VMEM((n_kv, 1, cols), F32),
                        pltpu.VMEM((n_kv, hd, cols), F32)],
    )
    return pl.pallas_call(
        functools.partial(_moba_prompt_kernel, n_kv=n_kv, group=group, hd=hd, scale=hd ** -0.5,
                          n_near=n_near, nb=nb),
        grid_spec=grid_spec,
        out_shape=jax.ShapeDtypeStruct((m, qw), BF16),
        compiler_params=_cparams("arbitrary", "arbitrary"),
    )(qb_tab, kb_tab, qkv, qkv, qkv, means, diag, far_rows)


_MEAN_PAGES_PER_STEP = 64
_ATTN_PAGES_PER_STEP = 32


def _pool_means_kernel(pt_ref, *refs, ppb, n_kv):
    del pt_ref
    o_ref = refs[-1]
    pages = refs[:-1]
    hd = pages[0].shape[1]
    sub = V7X_SUBLANES
    reps = sub // n_kv
    folded = []
    for blk in range(len(pages) // ppb):
        tot = None
        for p in range(ppb):
            part = jnp.sum(pages[blk * ppb + p][...].reshape(-1, sub, hd), axis=0)
            tot = part if tot is None else tot + part
        full = tot
        for i in range(1, reps):
            full = full + pltpu.roll(tot, i * n_kv, 0)
        folded.append(full)
    row_grp = lax.broadcasted_iota(jnp.int32, (sub, hd), 0) // n_kv
    tiles = []
    for a in range(0, len(folded), reps):
        tile = folded[a]
        for i in range(1, reps):
            tile = jnp.where(row_grp == i, folded[a + i], tile)
        tiles.append(tile)
    n_keys = ppb * pages[0].shape[0] // n_kv
    o_ref[...] = jnp.concatenate(tiles, axis=0) * (1.0 / n_keys)


def _pool_means(pool, pt_flat, page_off, dbs, n_pages, n_kv):
    rows, hd = pool.shape[1], pool.shape[2]
    ppb = MOBA_BLOCK // (rows // n_kv)
    pps = min(_MEAN_PAGES_PER_STEP, n_pages)
    blocks = pps // ppb
    assert n_pages % pps == 0 and pps % ppb == 0 and V7X_SUBLANES % n_kv == 0
    assert (blocks * n_kv) % V7X_SUBLANES == 0

    def page_spec(i):
        return pl.BlockSpec((None, rows, hd),
                            lambda b, s, pt: (pt[b * n_pages + s * pps + i] + page_off, 0, 0))

    grid_spec = pltpu.PrefetchScalarGridSpec(
        num_scalar_prefetch=1,
        grid=(dbs, n_pages // pps),
        in_specs=[page_spec(i) for i in range(pps)],
        out_specs=pl.BlockSpec((None, blocks * n_kv, hd), lambda b, s, pt: (b, s, 0)),
    )
    return pl.pallas_call(
        functools.partial(_pool_means_kernel, ppb=ppb, n_kv=n_kv),
        grid_spec=grid_spec,
        out_shape=jax.ShapeDtypeStruct((dbs, (n_pages // ppb) * n_kv, hd), F32),
        compiler_params=_cparams("arbitrary", "arbitrary"),
    )(pt_flat, *([pool] * pps))


def _cols_expand_kernel(tab_ref, idx_ref, o_ref, *, buckets, mult):
    idx = idx_ref[...]
    acc = jnp.full(idx.shape, NEG, F32)
    for b in buckets:
        acc = jnp.where(idx == b, tab_ref[b:b + 1, :] * mult, acc)
    o_ref[...] = acc


def _cols_expand(tab_cols, idx_np, tile_rows, mult=1.0):
    rows, cols = idx_np.shape
    buckets = tuple(int(b) for b in np.unique(idx_np) if b >= 0)
    return pl.pallas_call(
        functools.partial(_cols_expand_kernel, buckets=buckets, mult=mult),
        grid=(rows // tile_rows,),
        in_specs=[pl.BlockSpec(tab_cols.shape, lambda i: (0, 0)),
                  pl.BlockSpec((tile_rows, cols), lambda i: (i, 0))],
        out_specs=pl.BlockSpec((tile_rows, cols), lambda i: (i, 0)),
        out_shape=jax.ShapeDtypeStruct((rows, cols), F32),
        compiler_params=_cparams("arbitrary"),
    )(tab_cols, jnp.asarray(idx_np))


def _moba_sample_kernel(pt_ref, *refs, pps, ppb, n_kv, group, tp, n_blk, n_pages, far_cls, scale):
    del pt_ref
    k_pages = refs[:pps]
    v_pages = refs[pps:2 * pps]
    q_ref, means_ref, bias_ref, own_ref, kn_ref, vn_ref, o_ref, sel_s, m_s, l_s, acc_s = refs[2 * pps:]
    step = pl.program_id(1)
    cols, hd = q_ref.shape

    @pl.when(step == 0)
    def _():
        q = q_ref[...]
        blk = lax.broadcasted_iota(jnp.int32, (n_blk, cols), 0)
        col_kv = lax.broadcasted_iota(jnp.int32, (n_blk, cols), 1) // (group * tp)
        gate = jnp.zeros((n_blk, cols), F32)
        for kv in range(n_kv):
            g = _dot_nt(means_ref[kv * n_blk:(kv + 1) * n_blk, :], q, precision=lax.Precision.HIGHEST)
            gate = jnp.where(col_kv == kv, g, gate)
        rank = _rank_rows(gate, blk, n_blk)
        sel_s[...] = jnp.where(rank < MOBA_TOPK, 0.0, NEG)
        m_s[...] = jnp.full(m_s.shape, NEG, F32)
        l_s[...] = jnp.zeros(l_s.shape, F32)
        acc_s[...] = jnp.zeros(acc_s.shape, F32)

    qb = (q_ref[...] * (scale * LOG2E)).astype(BF16)
    page_keys = k_pages[0].shape[0] // n_kv
    col_kv = lax.broadcasted_iota(jnp.int32, (page_keys, cols), 1) // (group * tp)

    def accumulate(scores, values, expand):
        m_prev = m_s[...]
        m_new = m_prev
        for s in scores:
            m_new = jnp.maximum(m_new, jnp.max(s, axis=0, keepdims=True))
        alpha = jnp.exp2(m_prev - m_new)
        l_new = alpha * l_s[...]
        acc = alpha * acc_s[...]
        for s, v in zip(scores, values):
            p = jnp.exp2(s - m_new)
            l_new = l_new + jnp.sum(p, axis=0, keepdims=True)
            acc = acc + _dot(v.T.astype(BF16), expand(p).astype(BF16))
        m_s[...] = m_new
        l_s[...] = l_new
        acc_s[...] = acc

    def by_head(ref):
        return jnp.concatenate([ref[pl.ds(kv, page_keys, stride=n_kv), :] for kv in range(n_kv)], axis=0)

    def spread(p):
        return jnp.concatenate([jnp.where(col_kv == kv, p, 0.0) for kv in range(n_kv)], axis=0)

    scores, values = [], []
    for i in range(pps):
        page = step * pps + i
        cls = jnp.minimum(n_pages - 1 - page, far_cls)
        full = _dot_nt(by_head(k_pages[i]).astype(BF16), qb)
        s = full[0:page_keys]
        for kv in range(1, n_kv):
            s = jnp.where(col_kv == kv, full[kv * page_keys:(kv + 1) * page_keys], s)
        scores.append(s + bias_ref[cls] + sel_s[pl.ds(page // ppb, 1), :])
        values.append(by_head(v_pages[i]))
    accumulate(scores, values, spread)

    @pl.when(step == pl.num_programs(1) - 1)
    def _():
        s_own = _dot_nt(kn_ref[...].astype(BF16), qb) + own_ref[...]
        accumulate([s_own], [vn_ref[...]], lambda p: p)
        o_ref[...] = (acc_s[...] / l_s[...]).T


def _moba_sample(q_cols, means_t, k_new, v_new, pool_k, pool_v, pt_flat, page_off, bias_pages, bias_own, *,
                 n_kv, group, tp, n_pages):
    dbs, cols, hd = q_cols.shape
    rows = pool_k.shape[1]
    ppb = MOBA_BLOCK // (rows // n_kv)
    n_blk = n_pages // ppb
    pps = min(_ATTN_PAGES_PER_STEP, n_pages)
    assert n_pages % pps == 0
    far_cls = bias_pages.shape[0] - 1
    n_own = k_new.shape[1]

    def page_spec(i):
        return pl.BlockSpec((None, rows, hd),
                            lambda b, s, pt: (pt[b * n_pages + s * pps + i] + page_off, 0, 0))

    per_seq = lambda r: pl.BlockSpec((None, r, hd), lambda b, s, pt: (b, 0, 0))
    in_specs = ([page_spec(i) for i in range(pps)] * 2
                + [per_seq(cols), per_seq(n_kv * n_blk),
                   pl.BlockSpec(bias_pages.shape, lambda b, s, pt: (0, 0, 0)),
                   pl.BlockSpec(bias_own.shape, lambda b, s, pt: (0, 0)),
                   per_seq(n_own), per_seq(n_own)])
    grid_spec = pltpu.PrefetchScalarGridSpec(
        num_scalar_prefetch=1,
        grid=(dbs, n_pages // pps),
        in_specs=in_specs,
        out_specs=pl.BlockSpec((None, cols, hd), lambda b, s, pt: (b, 0, 0)),
        scratch_shapes=[pltpu.VMEM((n_blk, cols), F32), pltpu.VMEM((1, cols), F32),
                        pltpu.VMEM((1, cols), F32), pltpu.VMEM((hd, cols), F32)],
    )
    return pl.pallas_call(
        functools.partial(_moba_sample_kernel, pps=pps, ppb=ppb, n_kv=n_kv, group=group, tp=tp, n_blk=n_blk,
                          n_pages=n_pages, far_cls=far_cls, scale=hd ** -0.5),
        grid_spec=grid_spec,
        out_shape=jax.ShapeDtypeStruct((dbs, cols, hd), F32),
        compiler_params=_cparams("arbitrary", "arbitrary"),
    )(pt_flat, *([pool_k] * pps), *([pool_v] * pps), q_cols, means_t, bias_pages, bias_own, k_new, v_new)


def _swa_prompt_bias_idx(window, nbk):
    qi = np.arange(window)[:, None]
    kj = np.arange(2 * window)[None, :]
    dist = qi + window - kj
    return np.where((dist >= 0) & (dist < window), _bucket_np(dist, nbk), -1).astype(np.int32)


def _swa_sample_bias_idx(window, n_heads, tp, t_new, nbk):
    t = np.arange(n_heads * tp) % tp
    r = np.arange(window)
    dist = t[None, :] + window - r[:, None]
    idx = np.where((dist >= 0) & (dist < window), _bucket_np(dist, nbk), -1)
    return np.where(t[None, :] < t_new, idx, 0).astype(np.int32)


def _moba_prompt_diag_idx(n_near, nbk):
    dist = np.arange(n_near)[:, None] * MOBA_BLOCK + np.arange(2 * MOBA_BLOCK)[None, :] - MOBA_BLOCK
    return np.where(dist >= 0, _bucket_np(dist, nbk), -1).astype(np.int32)


def _moba_sample_page_idx(far_cls, page, n_heads, tp, t_new, nbk):
    cols = n_heads * tp
    t = np.arange(cols) % tp
    r = np.arange(page)
    tiles = []
    for c in range(far_cls + 1):
        dist = (c + 1) * page + t[None, :] - r[:, None]
        b = _bucket_np(dist, nbk) if c < far_cls else np.full(dist.shape, nbk - 1, np.int32)
        tiles.append(np.where(t[None, :] < t_new, b, 0))
    return np.concatenate(tiles, axis=0).astype(np.int32)


def _moba_sample_own_idx(n_kv, n_heads, group, tp, t_new, nbk):
    cols = n_heads * tp
    h, t = np.arange(cols) // tp, np.arange(cols) % tp
    tk, kvc = np.arange(tp * n_kv) // n_kv, np.arange(tp * n_kv) % n_kv
    ok = ((kvc[:, None] == (h // group)[None, :]) & (tk[:, None] <= t[None, :]) & (t[None, :] < t_new))
    return np.where(ok, _bucket_np(t[None, :] - tk[:, None], nbk), -1).astype(np.int32)


def kernel(x_prompt, x_sample, c_prompt, c_sample, state_conv, cache_swa_k, cache_swa_v, cache_moba_k, cache_moba_v, page_table, state_ffn, ada_w, ada_b, norm_mix, norm_ffn, norm_final, rel_bias, conv_w1, conv_b1, conv_dw, conv_dw_b, conv_ln_g, conv_ln_b, conv_w2, conv_b2, swa_wqkv, swa_wo, swa_sinks, moba_wqkv, moba_wo, ffn_w_up, ffn_dw, ffn_dw_b, ffn_w_down):
    bsz, seq, d = x_prompt.shape
    dbs, t_new, _ = x_sample.shape
    depth = ada_w.shape[0]
    window, n_kv, hd = cache_swa_k.shape[2], cache_swa_k.shape[3], cache_swa_k.shape[4]
    nbk, n_heads = rel_bias.shape
    group = n_heads // n_kv
    qw, kw = n_heads * hd, n_kv * hd
    cw = conv_dw.shape[1]
    d_ff = ffn_dw.shape[2]
    n_pool, page_size = cache_moba_k.shape[1], cache_moba_k.shape[2]
    n_pages = page_table.shape[1]
    ppb = MOBA_BLOCK // page_size
    n_blk_past = n_pages // ppb
    tp = V7X_SUBLANES
    assert t_new <= tp and n_pages % ppb == 0 and n_blk_past >= MOBA_TOPK and seq % MOBA_BLOCK == 0
    n_mixers = 3

    n_c = bsz + dbs
    c_rows = -(-n_c // V7X_SUBLANES) * V7X_SUBLANES
    c_all = jnp.concatenate([c_prompt, c_sample, jnp.zeros((c_rows - n_c, d), F32)], axis=0)
    mod = _ada(c_all, ada_w, ada_b)

    def mods(layer, which):
        chunk = mod[layer, :, which * d:(which + 1) * d]
        return chunk[:bsz], chunk[bsz:bsz + dbs]

    d_sat = _saturation_distance(nbk)
    n_near = -(-(d_sat + MOBA_BLOCK - 1) // MOBA_BLOCK)
    swa_bias_p = swa_bias_s = swa_bias_own = moba_bias_p = moba_far = moba_bias_sel = moba_bias_own = None
    tab_cols = jnp.repeat(rel_bias, tp, axis=1)
    own_idx = _moba_sample_own_idx(n_kv, n_heads, group, tp, t_new, nbk)
    if depth > 1:
        swa_bias_p = _bias_expand(rel_bias, _swa_prompt_bias_idx(window, nbk), window,
                                  (n_heads, window, 2 * window), (None, window, 2 * window),
                                  lambda h: (h, 0, 0))
        swa_bias_s = _cols_expand(tab_cols, _swa_sample_bias_idx(window, n_heads, tp, t_new, nbk), window)
        swa_bias_own = _cols_expand(tab_cols, own_idx, tp * n_kv)
    if depth > 2:
        cols = group * MOBA_BLOCK
        moba_bias_p = _bias_expand(
            rel_bias, _moba_prompt_diag_idx(n_near, nbk), n_near,
            (n_heads, n_near, 2 * MOBA_BLOCK), (None, n_near, 2 * MOBA_BLOCK),
            lambda h: (h, 0, 0), mult=LOG2E)
        moba_far = jnp.repeat(rel_bias[nbk - 1].reshape(n_kv, group, 1), MOBA_BLOCK, axis=2).reshape(n_kv, 1, cols)
        moba_far = moba_far * LOG2E
        far_cls = -(-(d_sat + page_size - 1) // page_size) - 1
        moba_bias_sel = _cols_expand(
            tab_cols, _moba_sample_page_idx(far_cls, page_size, n_heads, tp, t_new, nbk),
            page_size, mult=LOG2E).reshape(far_cls + 1, page_size, n_heads * tp)
        moba_bias_own = _cols_expand(tab_cols, own_idx, tp * n_kv, mult=LOG2E)

    def to_tm(a):
        return jnp.swapaxes(a, 0, 1).reshape((a.shape[1] * dbs,) + a.shape[2:])

    def to_bm(a, t):
        return jnp.swapaxes(a.reshape((t, dbs) + a.shape[1:]), 0, 1)

    w_down_bf16 = _to_bf16(ffn_w_down)

    xp = x_prompt.reshape(bsz * seq, d)
    xs = to_tm(x_sample)
    m_s = t_new * dbs

    conv_p, conv_s, swa_k_p, swa_v_p, swa_k_s, swa_v_s = [], [], [], [], [], []
    moba_k_p, moba_v_p, moba_k_s, moba_v_s, ffn_p, ffn_s = [], [], [], [], [], []

    def gates(layer, which):
        gp, gs = mods(layer, which)
        return gp.reshape(bsz, 1, d), jnp.tile(gs, (t_new, 1)).reshape(1, m_s, d)

    def normed(x_p, x_s, g, layer, which_sh):
        shp, shs = mods(layer, which_sh)
        scp, scs = mods(layer, which_sh + 1)
        hp = _norm_mod(x_p.reshape(bsz, seq, d), g, scp.reshape(bsz, 1, d), shp.reshape(bsz, 1, d), 512)
        hs = _norm_mod(x_s.reshape(t_new, dbs, d), g, scs.reshape(1, dbs, d), shs.reshape(1, dbs, d), dbs)
        return hp.reshape(bsz * seq, d), hs.reshape(m_s, d)

    def sample_views(qkv_bm):
        pad = jnp.concatenate([qkv_bm, jnp.zeros((dbs, tp - t_new, qkv_bm.shape[2]), qkv_bm.dtype)], axis=1)
        q_cols = pad[:, :, :qw].reshape(dbs, tp, n_heads, hd).transpose(0, 2, 1, 3).reshape(dbs, n_heads * tp, hd)
        return (q_cols, pad[:, :, qw:qw + kw].reshape(dbs, tp * n_kv, hd),
                pad[:, :, qw + kw:].reshape(dbs, tp * n_kv, hd))

    def cols_to_bm(o_cols):
        o = o_cols.reshape(dbs, n_heads, tp, hd)[:, :, :t_new].transpose(0, 2, 1, 3)
        return o.reshape(dbs, t_new, qw)

    for layer in range(depth):
        kind, j = layer % n_mixers, layer // n_mixers
        hp, hs = normed(xp, xs, norm_mix[layer], layer, 0)
        g1p, g1s = gates(layer, 2)
        if kind == 0:
            up, us = _glu(hp, hs, conv_w1, conv_b1, j)
            yp = _conv_ln_prompt(up.reshape(bsz, seq, d), conv_dw[j], conv_dw_b[j], conv_ln_g[j], conv_ln_b[j])
            st_tm = jnp.swapaxes(state_conv[j], 0, 1)
            ys, nst_tm = _conv_ln_sample(st_tm, us.reshape(t_new, dbs, d), conv_dw[j], conv_dw_b[j],
                                         conv_ln_g[j], conv_ln_b[j])
            conv_p.append(up.reshape(bsz, seq, d)[:, seq - (cw - 1):])
            conv_s.append(jnp.swapaxes(nst_tm, 0, 1))
            xp, xs = _mm(yp.reshape(bsz * seq, d), conv_w2, j, bias=conv_b2, res=xp, gate=g1p,
                         side=(ys.reshape(m_s, d), xs, g1s[0]))
        elif kind == 1:
            qkv_p, qkv_s = _mm(hp, swa_wqkv, j, side=(hs, None, None))
            op = _swa_prompt(qkv_p, swa_sinks[j], swa_bias_p, bsz, n_heads, n_kv, hd, window)
            q_cols, k_new, v_new = sample_views(to_bm(qkv_s, t_new))
            buf_k = cache_swa_k[j].reshape(dbs, window * n_kv, hd)
            buf_v = cache_swa_v[j].reshape(dbs, window * n_kv, hd)
            sink_row = jnp.repeat(swa_sinks[j], tp).reshape(1, n_heads * tp)
            o_cols = _swa_sample(q_cols, buf_k, buf_v, k_new, v_new, swa_bias_s, swa_bias_own, sink_row,
                                 n_kv=n_kv, group=group, tp=tp)
            os_ = to_tm(cols_to_bm(o_cols))
            tail = qkv_p.reshape(bsz, seq, qw + 2 * kw)[:, seq - window:]
            swa_k_p.append(tail[:, :, qw:qw + kw].reshape(bsz, window, n_kv, hd))
            swa_v_p.append(tail[:, :, qw + kw:].reshape(bsz, window, n_kv, hd))
            keep = t_new * n_kv
            swa_k_s.append(jnp.concatenate([buf_k[:, keep:], k_new[:, :keep]], axis=1)
                           .reshape(dbs, window, n_kv, hd))
            swa_v_s.append(jnp.concatenate([buf_v[:, keep:], v_new[:, :keep]], axis=1)
                           .reshape(dbs, window, n_kv, hd))
            xp, xs = _mm(op, swa_wo, j, res=xp, gate=g1p, side=(os_, xs, g1s[0]))
        else:
            qkv_p, k_rows, v_rows, qkv_s = _mm(hp, moba_wqkv, j, kv_heads=(n_kv, hd), side=(hs, None, None))
            nb = seq // MOBA_BLOCK
            means_p = _block_means(qkv_p, bsz * nb, qw // kw, kw).reshape(bsz, nb, kw)
            op = _moba_prompt(qkv_p, means_p, moba_bias_p, moba_far, bsz, n_heads, n_kv, hd)
            moba_k_p.append(k_rows.reshape(bsz, seq // page_size, page_size, n_kv, hd))
            moba_v_p.append(v_rows.reshape(bsz, seq // page_size, page_size, n_kv, hd))

            qkv_bm = to_bm(qkv_s, t_new)
            q_cols, k_new, v_new = sample_views(qkv_bm)
            pool_k = cache_moba_k.reshape(cache_moba_k.shape[0] * n_pool, page_size * n_kv, hd)
            pool_v = cache_moba_v.reshape(cache_moba_v.shape[0] * n_pool, page_size * n_kv, hd)
            pt_flat = page_table.reshape(-1)
            means_s = _pool_means(pool_k, pt_flat, j * n_pool, dbs, n_pages, n_kv)
            means_t = means_s.reshape(dbs, n_blk_past, n_kv, hd).transpose(0, 2, 1, 3)
            means_t = means_t.reshape(dbs, n_kv * n_blk_past, hd)
            o_cols = _moba_sample(q_cols, means_t, k_new, v_new, pool_k, pool_v, pt_flat, j * n_pool,
                                  moba_bias_sel, moba_bias_own, n_kv=n_kv, group=group, tp=tp, n_pages=n_pages)
            os_bm = cols_to_bm(o_cols).astype(BF16)
            moba_k_s.append(qkv_bm[:, :, qw:qw + kw].reshape(dbs, t_new, n_kv, hd))
            moba_v_s.append(qkv_bm[:, :, qw + kw:].reshape(dbs, t_new, n_kv, hd))
            xp, xs = _mm(op, moba_wo, j, res=xp, gate=g1p, side=(to_tm(os_bm), xs, g1s[0]))

        hp, hs = normed(xp, xs, norm_ffn[layer], layer, 3)
        g2p, g2s = gates(layer, 5)
        prev_tm = jnp.swapaxes(state_ffn[layer], 0, 1).reshape(2 * dbs, d_ff)
        ap, stp, as_, sts = _ffn_up(hp, hs, prev_tm, ffn_w_up, ffn_dw, ffn_dw_b, layer, bsz, dbs)
        ffn_p.append(stp)
        ffn_s.append(jnp.swapaxes(sts.reshape(2, dbs, d_ff), 0, 1))
        xp, xs = _mm(ap, w_down_bf16, layer, res=xp, gate=g2p, tn=512, weights_outer=False,
                     side=(as_, xs, g2s[0]))

    y_p = _norm_mod(xp.reshape(bsz, seq, d), norm_final, None, None, 512, out_dtype=F32)
    y_s = _norm_mod(xs.reshape(t_new, dbs, d), norm_final, None, None, dbs, out_dtype=F32)
    y_s = jnp.swapaxes(y_s, 0, 1)

    return (y_p, y_s, jnp.stack(conv_p), jnp.stack(conv_s), jnp.stack(swa_k_p), jnp.stack(swa_v_p),
            jnp.stack(swa_k_s), jnp.stack(swa_v_s), jnp.stack(moba_k_p), jnp.stack(moba_v_p),
            jnp.stack(moba_k_s), jnp.stack(moba_v_s), jnp.stack(ffn_p), jnp.stack(ffn_s))
```

```python
import functools
import math

import numpy as np
import jax
import jax.numpy as jnp
from jax import lax
from jax.experimental import pallas as pl
from jax.experimental.pallas import tpu as pltpu

MOBA_BLOCK = 256
MOBA_TOPK = 3
REL_MAX_DISTANCE = 1024
RMS_EPS = 1e-6
LN_EPS = 1e-5
N_MOD = 6

V7X_LANES = 128
V7X_SUBLANES = 8
V7X_VMEM_LIMIT_BYTES = 56 * 1024 * 1024

NEG = -1e30
LOG2E = 1.4426950408889634
BF16 = jnp.bfloat16
F32 = jnp.float32


def _cparams(*sem):
    return pltpu.CompilerParams(dimension_semantics=sem, vmem_limit_bytes=V7X_VMEM_LIMIT_BYTES)


def _dot(a, b):
    return jnp.dot(a, b, preferred_element_type=F32)


def _dot_nt(a, b, precision=None):
    return lax.dot_general(a, b, (((1,), (1,)), ((), ())), precision=precision,
                           preferred_element_type=F32)


def _silu(x):
    return x * jax.nn.sigmoid(x)


def _bucket_np(dist, num_buckets):
    n = np.maximum(dist, 0)
    max_exact = num_buckets // 2
    nf = np.maximum(n, 1).astype(np.float32)
    ratio = np.log(nf / np.float32(max_exact)) / np.float32(math.log(REL_MAX_DISTANCE / max_exact))
    large = max_exact + (ratio * np.float32(num_buckets - max_exact)).astype(np.int32)
    return np.where(n < max_exact, n, np.minimum(large, num_buckets - 1)).astype(np.int32)


def _saturation_distance(num_buckets):
    d = np.arange(0, 4 * REL_MAX_DISTANCE)
    b = _bucket_np(d, num_buckets)
    below = np.nonzero(b < num_buckets - 1)[0]
    return int(below.max()) + 1


def _bias_expand_kernel(tab_ref, idx_ref, o_ref, *, tiles, mult):
    h = pl.program_id(0)
    for n, (r0, nr, buckets) in enumerate(tiles):
        idx = idx_ref[r0:r0 + nr, :]
        acc = jnp.full(idx.shape, NEG, F32)
        for b in buckets:
            acc = jnp.where(idx == b, tab_ref[b, h] * mult, acc)
        if len(o_ref.shape) == 3:
            o_ref[n] = acc
        else:
            o_ref[r0:r0 + nr, :] = acc


def _bias_expand(table, idx_np, tile_rows, out_shape, out_block, out_index, mult=1.0):
    n_heads = table.shape[1]
    rows, cols = idx_np.shape
    tiles = []
    for r0 in range(0, rows, tile_rows):
        present = np.unique(idx_np[r0:r0 + tile_rows])
        tiles.append((r0, min(tile_rows, rows - r0), tuple(int(b) for b in present if b >= 0)))
    return pl.pallas_call(
        functools.partial(_bias_expand_kernel, tiles=tuple(tiles), mult=mult),
        grid=(n_heads,),
        in_specs=[pl.BlockSpec(memory_space=pltpu.SMEM),
                  pl.BlockSpec((rows, cols), lambda h: (0, 0))],
        out_specs=pl.BlockSpec(out_block, out_index),
        out_shape=jax.ShapeDtypeStruct(out_shape, F32),
        compiler_params=_cparams("arbitrary"),
    )(table, jnp.asarray(idx_np))


def _ada_kernel(c_ref, w_ref, b_ref, o_ref):
    a = _silu(c_ref[...]).astype(BF16)
    o_ref[...] = _dot(a, w_ref[...].astype(BF16)) + b_ref[...]


def _ada(c_all, ada_w, ada_b):
    depth, d, n = ada_w.shape
    rows = c_all.shape[0]
    tn = min(n, 2048)
    return pl.pallas_call(
        _ada_kernel,
        grid=(depth, n // tn),
        in_specs=[pl.BlockSpec((rows, d), lambda l, j: (0, 0)),
                  pl.BlockSpec((None, d, tn), lambda l, j: (l, 0, j)),
                  pl.BlockSpec((None, 1, tn), lambda l, j: (l, 0, j))],
        out_specs=pl.BlockSpec((None, rows, tn), lambda l, j: (l, 0, j)),
        out_shape=jax.ShapeDtypeStruct((depth, rows, n), F32),
        compiler_params=_cparams("arbitrary", "arbitrary"),
    )(c_all, ada_w, ada_b.reshape(depth, 1, n))


def _norm_mod_kernel(x_ref, g_ref, sc_ref, sh_ref, o_ref):
    x = x_ref[...]
    ms = jnp.mean(x * x, axis=-1, keepdims=True)
    y = x * lax.rsqrt(ms + RMS_EPS) * g_ref[...]
    o_ref[...] = (y * (1.0 + sc_ref[...]) + sh_ref[...]).astype(o_ref.dtype)


def _rms_kernel(x_ref, g_ref, o_ref):
    x = x_ref[...]
    ms = jnp.mean(x * x, axis=-1, keepdims=True)
    o_ref[...] = (x * lax.rsqrt(ms + RMS_EPS) * g_ref[...]).astype(o_ref.dtype)


def _norm_mod(x3, g, sc3, sh3, rows_per_step, out_dtype=BF16):
    a, r, d = x3.shape
    tr = min(rows_per_step, r)
    g3 = g.reshape(1, 1, d)
    if sc3 is None:
        return pl.pallas_call(
            _rms_kernel,
            grid=(a, r // tr),
            in_specs=[pl.BlockSpec((1, tr, d), lambda i, j: (i, j, 0)),
                      pl.BlockSpec((1, 1, d), lambda i, j: (0, 0, 0))],
            out_specs=pl.BlockSpec((1, tr, d), lambda i, j: (i, j, 0)),
            out_shape=jax.ShapeDtypeStruct(x3.shape, out_dtype),
            compiler_params=_cparams("arbitrary", "arbitrary"),
        )(x3, g3)
    per_seq = sc3.shape[0] == a and sc3.shape[1] == 1
    if per_seq:
        mspec = pl.BlockSpec((1, 1, d), lambda i, j: (i, 0, 0))
    else:
        assert sc3.shape[0] == 1 and sc3.shape[1] == r and tr == r
        mspec = pl.BlockSpec((1, r, d), lambda i, j: (0, 0, 0))
    return pl.pallas_call(
        _norm_mod_kernel,
        grid=(a, r // tr),
        in_specs=[pl.BlockSpec((1, tr, d), lambda i, j: (i, j, 0)),
                  pl.BlockSpec((1, 1, d), lambda i, j: (0, 0, 0)),
                  mspec, mspec],
        out_specs=pl.BlockSpec((1, tr, d), lambda i, j: (i, j, 0)),
        out_shape=jax.ShapeDtypeStruct(x3.shape, out_dtype),
        compiler_params=_cparams("arbitrary", "arbitrary"),
    )(x3, g3, sc3, sh3)


def _mm_kernel(*refs, names, cache_w, kv_heads):
    r = dict(zip(names, refs))
    if cache_w:
        @pl.when(pl.program_id(1) == 0)
        def _():
            r["w_s"][...] = r["w"][...].astype(BF16)
        w = r["w_s"][...]
    else:
        w = r["w"][...].astype(BF16)

    def product(x_name, res_name, gate_name):
        acc = _dot(r[x_name][...], w)
        if "bias" in r:
            acc = acc + r["bias"][...]
        if res_name in r:
            acc = r[res_name][...] + r[gate_name][...] * acc
        return acc

    acc = product("x", "res", "gate")
    r["o"][...] = acc.astype(r["o"].dtype)
    if kv_heads:
        n_kv, hd = kv_heads
        rows = acc.shape[0]

        @pl.when(pl.program_id(0) == pl.num_programs(0) - 1)
        def _():
            for kv in range(n_kv):
                r["k"][pl.ds(kv, rows, stride=n_kv), :] = acc[:, kv * hd:(kv + 1) * hd]
                r["v"][pl.ds(kv, rows, stride=n_kv), :] = acc[:, (n_kv + kv) * hd:(n_kv + kv + 1) * hd]
    if "x2" in r:
        row_tile = pl.program_id(1) if cache_w else pl.program_id(0)
        last = (pl.num_programs(1) - 1) if cache_w else 0

        @pl.when(row_tile == last)
        def _():
            r["o2"][...] = product("x2", "res2", "gate2").astype(r["o2"].dtype)


def _mm(x, w, layer, *, bias=None, res=None, gate=None, out_dtype=F32, tm=1024, tn=1024, weights_outer=True,
        kv_heads=None, side=None):
    m, k = x.shape
    n = w.shape[2]
    tm, tn = min(tm, m), min(tn, n)
    assert m % tm == 0 and n % tn == 0
    if weights_outer:
        grid = (n // tn, m // tm)
        ij = lambda a, b: (b, a)
    else:
        grid = (m // tm, n // tn)
        ij = lambda a, b: (a, b)

    def spec(block, fn):
        return pl.BlockSpec(block, lambda a, b: fn(*ij(a, b)))

    names = ["x", "w"]
    in_specs = [spec((tm, k), lambda i, j: (i, 0)),
                spec((None, k, tn), lambda i, j: (layer, 0, j))]
    args = [x, w]
    if bias is not None:
        names.append("bias")
        in_specs.append(spec((None, 1, tn), lambda i, j: (layer, 0, j)))
        args.append(bias.reshape(bias.shape[0], 1, n))
    if res is not None:
        gs, gr, _ = gate.shape
        assert gr in (1, tm) and (m // tm) % gs == 0
        tiles_per_gate = (m // tm) // gs
        names += ["res", "gate"]
        in_specs.append(spec((tm, tn), lambda i, j: (i, j)))
        in_specs.append(spec((None, gr, tn), lambda i, j: (i // tiles_per_gate, 0, j)))
        args += [res, gate]
    if side is not None:
        x2, res2, gate2 = side
        m2 = x2.shape[0]
        nj_side = n // tn
        if weights_outer:
            side_col = lambda i, j: j
        else:
            side_col = lambda i, j: jnp.where(i == 0, j, nj_side - 1)
        names.append("x2")
        in_specs.append(spec((m2, k), lambda i, j: (0, 0)))
        args.append(x2)
        if res2 is not None:
            names += ["res2", "gate2"]
            in_specs += [spec((m2, tn), lambda i, j: (0, side_col(i, j)))] * 2
            args += [res2, gate2]
    names.append("o")
    out_specs = [spec((tm, tn), lambda i, j: (i, j))]
    out_shape = [jax.ShapeDtypeStruct((m, n), out_dtype)]
    if kv_heads:
        n_kv, hd = kv_heads
        nj = n // tn
        assert weights_outer and tn == 2 * n_kv * hd
        kv_spec = spec((tm * n_kv, hd), lambda i, j: (jnp.where(j == nj - 1, i, 0), 0))
        names += ["k", "v"]
        out_specs += [kv_spec, kv_spec]
        out_shape += [jax.ShapeDtypeStruct((m * n_kv, hd), F32)] * 2
    if side is not None:
        names.append("o2")
        out_specs.append(spec((m2, tn), lambda i, j: (0, side_col(i, j))))
        out_shape.append(jax.ShapeDtypeStruct((m2, n), out_dtype))
    if weights_outer:
        names.append("w_s")
    outs = pl.pallas_call(
        functools.partial(_mm_kernel, names=tuple(names), cache_w=weights_outer, kv_heads=kv_heads),
        grid=grid,
        in_specs=in_specs,
        out_specs=out_specs,
        out_shape=out_shape,
        scratch_shapes=[pltpu.VMEM((k, tn), BF16)] if weights_outer else [],
        compiler_params=_cparams("arbitrary", "arbitrary"),
    )(*args)
    return outs[0] if len(outs) == 1 else tuple(outs)


def _glu_kernel(x_ref, x2_ref, wa_ref, wg_ref, ba_ref, bg_ref, o_ref, o2_ref, wa_s, wg_s):
    @pl.when(pl.program_id(1) == 0)
    def _():
        wa_s[...] = wa_ref[...].astype(BF16)
        wg_s[...] = wg_ref[...].astype(BF16)

    def glu(x):
        a = _dot(x, wa_s[...]) + ba_ref[...]
        g = _dot(x, wg_s[...]) + bg_ref[...]
        return a * jax.nn.sigmoid(g)

    o_ref[...] = glu(x_ref[...])

    @pl.when(pl.program_id(1) == pl.num_programs(1) - 1)
    def _():
        o2_ref[...] = glu(x2_ref[...])


def _glu(x, x2, w1, b1, layer, *, tm=1024, tn=512):
    m, k = x.shape
    m2 = x2.shape[0]
    n = w1.shape[2] // 2
    tm, tn = min(tm, m), min(tn, n)
    nj = n // tn
    b3 = b1.reshape(b1.shape[0], 1, 2 * n)
    return pl.pallas_call(
        _glu_kernel,
        grid=(nj, m // tm),
        in_specs=[pl.BlockSpec((tm, k), lambda j, i: (i, 0)),
                  pl.BlockSpec((m2, k), lambda j, i: (0, 0)),
                  pl.BlockSpec((None, k, tn), lambda j, i: (layer, 0, j)),
                  pl.BlockSpec((None, k, tn), lambda j, i: (layer, 0, nj + j)),
                  pl.BlockSpec((None, 1, tn), lambda j, i: (layer, 0, j)),
                  pl.BlockSpec((None, 1, tn), lambda j, i: (layer, 0, nj + j))],
        out_specs=[pl.BlockSpec((tm, tn), lambda j, i: (i, j)),
                   pl.BlockSpec((m2, tn), lambda j, i: (0, j))],
        out_shape=[jax.ShapeDtypeStruct((m, n), F32), jax.ShapeDtypeStruct((m2, n), F32)],
        scratch_shapes=[pltpu.VMEM((k, tn), BF16), pltpu.VMEM((k, tn), BF16)],
        compiler_params=_cparams("arbitrary", "arbitrary"),
    )(x, x2, w1, w1, b3, b3)


def _layernorm_silu(y, lg, lb):
    mu = jnp.mean(y, axis=-1, keepdims=True)
    yc = y - mu
    var = jnp.mean(yc * yc, axis=-1, keepdims=True)
    return _silu(yc * lax.rsqrt(var + LN_EPS) * lg + lb)


def _conv_ln_prompt_kernel(u_ref, halo_ref, dw_ref, dwb_ref, lg_ref, lb_ref, o_ref, full_s, y_s,
                           *, tm, halo, cw, rc, lw):
    i = pl.program_id(1)
    d = u_ref.shape[-1]
    full_s[halo:halo + tm, :] = u_ref[...]

    @pl.when(i == 0)
    def _():
        full_s[0:halo, :] = jnp.zeros((halo, d), F32)

    @pl.when(i > 0)
    def _():
        full_s[0:halo, :] = halo_ref[...]

    off = halo - (cw - 1)
    win = rc + halo
    sub = V7X_SUBLANES

    def body(r, carry):
        r0 = pl.multiple_of(r * rc, rc)
        for c0 in range(0, d, lw):
            window = full_s[pl.ds(r0, win), c0:c0 + lw]
            acc = jnp.zeros((rc, lw), F32) + dwb_ref[:, c0:c0 + lw]
            for s in range(sub):
                taps = [k for k in range(cw) if (off + k) % sub == s]
                if not taps:
                    continue
                shifted = window if s == 0 else pltpu.roll(window, win - s, 0)
                for k in taps:
                    a = off + k - s
                    acc = acc + dw_ref[k:k + 1, c0:c0 + lw] * shifted[a:a + rc]
            y_s[pl.ds(r0, rc), c0:c0 + lw] = acc
        return carry
    lax.fori_loop(0, tm // rc, body, 0)

    o_ref[...] = _layernorm_silu(y_s[...], lg_ref[...], lb_ref[...]).astype(o_ref.dtype)


def _conv_ln_prompt(u3, dw, dwb, lg, lb, *, tm=256):
    bsz, t, d = u3.shape
    cw = dw.shape[0]
    halo = 32
    assert cw - 1 <= halo and t % tm == 0 and tm % halo == 0
    hb = tm // halo
    lw = min(d, V7X_LANES)
    rc = min(tm, 4 * halo)
    return pl.pallas_call(
        functools.partial(_conv_ln_prompt_kernel, tm=tm, halo=halo, cw=cw, rc=rc, lw=lw),
        grid=(bsz, t // tm),
        in_specs=[pl.BlockSpec((None, tm, d), lambda b, i: (b, i, 0)),
                  pl.BlockSpec((None, halo, d), lambda b, i: (b, jnp.maximum(i * hb - 1, 0), 0)),
                  pl.BlockSpec((cw, d), lambda b, i: (0, 0)),
                  pl.BlockSpec((1, d), lambda b, i: (0, 0)),
                  pl.BlockSpec((1, d), lambda b, i: (0, 0)),
                  pl.BlockSpec((1, d), lambda b, i: (0, 0))],
        out_specs=pl.BlockSpec((None, tm, d), lambda b, i: (b, i, 0)),
        out_shape=jax.ShapeDtypeStruct((bsz, t, d), BF16),
        scratch_shapes=[pltpu.VMEM((tm + halo, d), F32), pltpu.VMEM((tm, d), F32)],
        compiler_params=_cparams("arbitrary", "arbitrary"),
    )(u3, u3, dw, dwb.reshape(1, d), lg.reshape(1, d), lb.reshape(1, d))


def _conv_ln_sample_kernel(st_ref, u_ref, dw_ref, dwb_ref, lg_ref, lb_ref, o_ref, nst_ref, *, cw, lw):
    t_new, _, d = u_ref.shape
    n_st = cw - 1

    def row(idx, c0):
        if idx < n_st:
            return st_ref[idx, :, c0:c0 + lw]
        return u_ref[idx - n_st, :, c0:c0 + lw]

    for t in range(t_new):
        parts = []
        for c0 in range(0, d, lw):
            acc = dw_ref[0:1, c0:c0 + lw] * row(t, c0)
            for k in range(1, cw):
                acc = acc + dw_ref[k:k + 1, c0:c0 + lw] * row(t + k, c0)
            parts.append(acc + dwb_ref[:, c0:c0 + lw])
        y = jnp.concatenate(parts, axis=-1) if len(parts) > 1 else parts[0]
        o_ref[t] = _layernorm_silu(y, lg_ref[...], lb_ref[...]).astype(o_ref.dtype)
    for r in range(n_st):
        idx = t_new + r
        nst_ref[r] = st_ref[idx] if idx < n_st else u_ref[idx - n_st]


def _conv_ln_sample(st_tm, u_tm, dw, dwb, lg, lb):
    n_st, bsz, d = st_tm.shape
    t_new = u_tm.shape[0]
    cw = dw.shape[0]
    lw = min(d, 512)
    full = lambda shape: pl.BlockSpec(shape, lambda i: (0,) * len(shape))
    return pl.pallas_call(
        functools.partial(_conv_ln_sample_kernel, cw=cw, lw=lw),
        grid=(1,),
        in_specs=[full((n_st, bsz, d)), full((t_new, bsz, d)), full((cw, d)), full((1, d)), full((1, d)),
                  full((1, d))],
        out_specs=[full((t_new, bsz, d)), full((n_st, bsz, d))],
        out_shape=[jax.ShapeDtypeStruct((t_new, bsz, d), BF16),
                   jax.ShapeDtypeStruct((n_st, bsz, d), F32)],
        compiler_params=_cparams("arbitrary"),
    )(st_tm, u_tm, dw, dwb.reshape(1, d), lg.reshape(1, d), lb.reshape(1, d))


def _ffn_up_kernel(x_ref, x2_ref, prev2_ref, wg_ref, wv_ref, dw_ref, dwb_ref, wd_ref, a_ref, st_ref, a2_ref,
                   st2_ref, wd_bf_ref, wg_s, wv_s, carry_s, *, tiles_per_seq, bsz2):
    i = pl.program_id(1)
    wd_bf_ref[...] = wd_ref[...].astype(BF16)

    @pl.when(i == 0)
    def _():
        wg_s[...] = wg_ref[...].astype(BF16)
        wv_s[...] = wv_ref[...].astype(BF16)

    @pl.when(i % tiles_per_seq == 0)
    def _():
        carry_s[...] = jnp.zeros(carry_s.shape, F32)

    x = x_ref[...]
    g = _dot(x, wg_s[...])
    v = _dot(x, wv_s[...])
    tm = g.shape[0]
    row = lax.broadcasted_iota(jnp.int32, g.shape, 0)
    prev1 = carry_s[V7X_SUBLANES - 1:V7X_SUBLANES, :]
    prev2 = carry_s[V7X_SUBLANES - 2:V7X_SUBLANES - 1, :]
    g1 = jnp.where(row == 0, prev1, pltpu.roll(g, 1, 0))
    g2 = jnp.where(row == 0, prev2, jnp.where(row == 1, prev1, pltpu.roll(g, 2, 0)))
    gc = dw_ref[0:1, :] * g2 + dw_ref[1:2, :] * g1 + dw_ref[2:3, :] * g + dwb_ref[...]
    a_ref[...] = (_silu(gc) * v).astype(a_ref.dtype)
    carry_s[...] = g[tm - V7X_SUBLANES:tm, :]
    st_ref[...] = g[tm - 2:tm, :]

    @pl.when(i == pl.num_programs(1) - 1)
    def _():
        x2 = x2_ref[...]
        g_s = _dot(x2, wg_s[...])
        v_s = _dot(x2, wv_s[...])
        rows = g_s.shape[0]
        full = jnp.concatenate([prev2_ref[...], g_s], axis=0)
        gc_s = (dw_ref[0:1, :] * full[0:rows] + dw_ref[1:2, :] * full[bsz2:bsz2 + rows]
                + dw_ref[2:3, :] * full[2 * bsz2:2 * bsz2 + rows] + dwb_ref[...])
        a2_ref[...] = (_silu(gc_s) * v_s).astype(a2_ref.dtype)
        st2_ref[...] = full[rows:rows + 2 * bsz2]


def _ffn_up(x, x2, prev2_tm, w_up, dw, dwb, w_down, layer, bsz, bsz2, *, tm=1024, tn=512):
    m, k = x.shape
    m2 = x2.shape[0]
    n = w_up.shape[2] // 2
    t = m // bsz
    tm, tn = min(tm, t), min(tn, n)
    assert t % tm == 0 and n % tn == 0 and dw.shape[1] == 3 and m2 >= 2 * bsz2
    nj, ni = n // tn, m // tm
    tiles_per_seq = t // tm
    kd, nd = w_down.shape[1], w_down.shape[2]
    slab = kd // (nj * ni)
    assert slab * nj * ni == kd and slab % (2 * V7X_SUBLANES) == 0
    return pl.pallas_call(
        functools.partial(_ffn_up_kernel, tiles_per_seq=tiles_per_seq, bsz2=bsz2),
        grid=(nj, ni),
        in_specs=[pl.BlockSpec((tm, k), lambda j, i: (i, 0)),
                  pl.BlockSpec((m2, k), lambda j, i: (0, 0)),
                  pl.BlockSpec((2 * bsz2, tn), lambda j, i: (0, j)),
                  pl.BlockSpec((None, k, tn), lambda j, i: (layer, 0, j)),
                  pl.BlockSpec((None, k, tn), lambda j, i: (layer, 0, nj + j)),
                  pl.BlockSpec((None, 3, tn), lambda j, i: (layer, 0, j)),
                  pl.BlockSpec((None, 1, tn), lambda j, i: (layer, 0, j)),
                  pl.BlockSpec((None, slab, nd), lambda j, i: (layer, j * ni + i, 0))],
        out_specs=[pl.BlockSpec((tm, tn), lambda j, i: (i, j)),
                   pl.BlockSpec((None, 2, tn), lambda j, i: (i // tiles_per_seq, 0, j)),
                   pl.BlockSpec((m2, tn), lambda j, i: (0, j)),
                   pl.BlockSpec((2 * bsz2, tn), lambda j, i: (0, j)),
                   pl.BlockSpec((None, slab, nd), lambda j, i: (0, j * ni + i, 0))],
        out_shape=[jax.ShapeDtypeStruct((m, n), BF16),
                   jax.ShapeDtypeStruct((bsz, 2, n), F32),
                   jax.ShapeDtypeStruct((m2, n), BF16),
                   jax.ShapeDtypeStruct((2 * bsz2, n), F32),
                   jax.ShapeDtypeStruct((1, kd, nd), BF16)],
        scratch_shapes=[pltpu.VMEM((k, tn), BF16), pltpu.VMEM((k, tn), BF16),
                        pltpu.VMEM((V7X_SUBLANES, tn), F32)],
        compiler_params=_cparams("arbitrary", "arbitrary"),
    )(x, x2, prev2_tm, w_up, w_up, dw, dwb.reshape(dwb.shape[0], 1, n), w_down)


def _swa_prompt_kernel(sink_ref, q_ref, kc_ref, kp_ref, vc_ref, vp_ref, bias_ref, o_ref, *, n_heads, group, hd,
                       scale):
    n = pl.program_id(1)
    w = q_ref.shape[0]
    kcat = jnp.concatenate([kp_ref[...], kc_ref[...]], axis=0).astype(BF16)
    vcat = jnp.concatenate([vp_ref[...], vc_ref[...]], axis=0).astype(BF16)
    col = lax.broadcasted_iota(jnp.int32, (w, 2 * w), 1)
    keep = (col >= w) | (n > 0)
    scores = []
    for h in range(n_heads):
        kv = h // group
        q = q_ref[:, h * hd:(h + 1) * hd].astype(BF16)
        s = _dot_nt(q, kcat[:, kv * hd:(kv + 1) * hd]) * scale + bias_ref[h]
        scores.append(jnp.where(keep, s, NEG))
    probs, sums = [], []
    for h in range(n_heads):
        sink = sink_ref[h]
        m = jnp.maximum(jnp.max(scores[h], axis=-1, keepdims=True), sink)
        p = jnp.exp(scores[h] - m)
        sums.append(jnp.sum(p, axis=-1, keepdims=True) + jnp.exp(sink - m))
        probs.append(p.astype(BF16))
    for h in range(n_heads):
        kv = h // group
        o = _dot(probs[h], vcat[:, kv * hd:(kv + 1) * hd]) / sums[h]
        o_ref[:, h * hd:(h + 1) * hd] = o.astype(o_ref.dtype)


def _swa_prompt(qkv, sinks, bias, bsz, n_heads, n_kv, hd, window):
    m = qkv.shape[0]
    t = m // bsz
    nb = t // window
    qw, kw = n_heads * hd, n_kv * hd
    assert qw % kw == 0 and t % window == 0
    kcol, vcol = qw // kw, qw // kw + 1
    return pl.pallas_call(
        functools.partial(_swa_prompt_kernel, n_heads=n_heads, group=n_heads // n_kv, hd=hd, scale=hd ** -0.5),
        grid=(bsz, nb),
        in_specs=[pl.BlockSpec(memory_space=pltpu.SMEM),
                  pl.BlockSpec((window, qw), lambda b, n: (b * nb + n, 0)),
                  pl.BlockSpec((window, kw), lambda b, n: (b * nb + n, kcol)),
                  pl.BlockSpec((window, kw), lambda b, n: (b * nb + jnp.maximum(n - 1, 0), kcol)),
                  pl.BlockSpec((window, kw), lambda b, n: (b * nb + n, vcol)),
                  pl.BlockSpec((window, kw), lambda b, n: (b * nb + jnp.maximum(n - 1, 0), vcol)),
                  pl.BlockSpec((n_heads, window, 2 * window), lambda b, n: (0, 0, 0))],
        out_specs=pl.BlockSpec((window, qw), lambda b, n: (b * nb + n, 0)),
        out_shape=jax.ShapeDtypeStruct((m, qw), BF16),
        compiler_params=_cparams("arbitrary", "arbitrary"),
    )(sinks, qkv, qkv, qkv, qkv, qkv, bias)


def _swa_sample_kernel(q_ref, bk_ref, bv_ref, kn_ref, vn_ref, bias_ref, own_ref, sink_ref, o_ref,
                       *, n_kv, group, tp, scale):
    cols, hd = q_ref.shape
    keys = bk_ref.shape[0] // n_kv
    qb = q_ref[...].astype(BF16)
    col_kv = lax.broadcasted_iota(jnp.int32, (keys, cols), 1) // (group * tp)

    def by_head(ref):
        return jnp.concatenate([ref[pl.ds(kv, keys, stride=n_kv), :] for kv in range(n_kv)], axis=0)

    full = _dot_nt(by_head(bk_ref).astype(BF16), qb)
    s = full[0:keys]
    for kv in range(1, n_kv):
        s = jnp.where(col_kv == kv, full[kv * keys:(kv + 1) * keys], s)
    s = s * scale + bias_ref[...]
    s_own = _dot_nt(kn_ref[...].astype(BF16), qb) * scale + own_ref[...]
    sink = sink_ref[...]
    m = jnp.maximum(jnp.maximum(jnp.max(s, axis=0, keepdims=True), jnp.max(s_own, axis=0, keepdims=True)), sink)
    p = jnp.exp(s - m)
    p_own = jnp.exp(s_own - m)
    l = jnp.sum(p, axis=0, keepdims=True) + jnp.sum(p_own, axis=0, keepdims=True) + jnp.exp(sink - m)
    spread = jnp.concatenate([jnp.where(col_kv == kv, p, 0.0) for kv in range(n_kv)], axis=0)
    acc = (_dot(by_head(bv_ref).T.astype(BF16), spread.astype(BF16))
           + _dot(vn_ref[...].T.astype(BF16), p_own.astype(BF16)))
    o_ref[...] = (acc / l).T.astype(o_ref.dtype)


def _swa_sample(q_cols, buf_k, buf_v, k_new, v_new, bias, own, sink_row, *, n_kv, group, tp):
    dbs, cols, hd = q_cols.shape
    per_seq = lambda a: pl.BlockSpec((None,) + a.shape[1:], lambda b: (b, 0, 0))
    shared = lambda a: pl.BlockSpec(a.shape, lambda b: (0, 0))
    return pl.pallas_call(
        functools.partial(_swa_sample_kernel, n_kv=n_kv, group=group, tp=tp, scale=hd ** -0.5),
        grid=(dbs,),
        in_specs=[per_seq(q_cols), per_seq(buf_k), per_seq(buf_v), per_seq(k_new), per_seq(v_new),
                  shared(bias), shared(own), shared(sink_row)],
        out_specs=pl.BlockSpec((None, cols, hd), lambda b: (b, 0, 0)),
        out_shape=jax.ShapeDtypeStruct((dbs, cols, hd), BF16),
        compiler_params=_cparams("arbitrary"),
    )(q_cols, buf_k, buf_v, k_new, v_new, bias, own, sink_row)


def _block_means_kernel(k_ref, o_ref):
    o_ref[...] = jnp.mean(k_ref[...], axis=0, keepdims=True)


def _block_means(qkv, n_blocks_total, kcol, kw):
    return pl.pallas_call(
        _block_means_kernel,
        grid=(n_blocks_total,),
        in_specs=[pl.BlockSpec((MOBA_BLOCK, kw), lambda i: (i, kcol))],
        out_specs=pl.BlockSpec((None, 1, kw), lambda i: (i, 0, 0)),
        out_shape=jax.ShapeDtypeStruct((n_blocks_total, 1, kw), F32),
        compiler_params=_cparams("arbitrary"),
    )(qkv)


def _rank_rows(gate, blk, n_rows):
    rank = jnp.zeros(gate.shape, jnp.int32)
    for m in range(n_rows):
        gm = gate[m:m + 1, :]
        beats = (gm > gate) | ((gm == gate) & (blk > m))
        rank = rank + jnp.where(beats, 1, 0)
    return rank


def _moba_prompt_kernel(qb_ref, kb_ref, q_ref, k_ref, v_ref, means_ref, diag_ref, far_ref, o_ref,
                        qs, sel_s, m_s, l_s, acc_s, *, n_kv, group, hd, scale, n_near, nb):
    qb = qb_ref[pl.program_id(1)]
    kb = kb_ref[pl.program_id(1)]
    blk_rows = q_ref.shape[0]
    cols = group * blk_rows

    @pl.when(kb == 0)
    def _():
        blk = lax.broadcasted_iota(jnp.int32, (nb, cols), 0)
        for kv in range(n_kv):
            qg = jnp.concatenate([q_ref[:, (kv * group + g) * hd:(kv * group + g + 1) * hd]
                                  for g in range(group)], axis=0)
            qs[kv] = (qg * (scale * LOG2E)).astype(BF16)
            gate = _dot_nt(means_ref[:, kv * hd:(kv + 1) * hd], qg, precision=lax.Precision.HIGHEST)
            valid = blk < qb
            gate = jnp.where(valid, gate, -jnp.inf)
            rank = _rank_rows(gate, blk, nb)
            sel_s[kv] = jnp.where(valid & (rank < MOBA_TOPK), 1.0, 0.0)
            m_s[kv] = jnp.full((1, cols), NEG, F32)
            l_s[kv] = jnp.zeros((1, cols), F32)
            acc_s[kv] = jnp.zeros((hd, cols), F32)

    def attend_all(adds_of):
        chunks = [(kv, g) for kv in range(n_kv) for g in range(group)]
        col = lambda g: slice(g * blk_rows, (g + 1) * blk_rows)
        scores = {}
        for kv in range(n_kv):
            k = k_ref[:, kv * hd:(kv + 1) * hd].astype(BF16)
            for g in range(group):
                s = _dot_nt(k, qs[kv, col(g), :])
                for a in adds_of(kv, g):
                    s = s + a
                scores[kv, g] = s
        alphas, probs = {}, {}
        for kv, g in chunks:
            m_prev = m_s[kv, :, col(g)]
            m_new = jnp.maximum(m_prev, jnp.max(scores[kv, g], axis=0, keepdims=True))
            alpha = jnp.exp2(m_prev - m_new)
            p = jnp.exp2(scores[kv, g] - m_new)
            l_s[kv, :, col(g)] = alpha * l_s[kv, :, col(g)] + jnp.sum(p, axis=0, keepdims=True)
            m_s[kv, :, col(g)] = m_new
            alphas[kv, g], probs[kv, g] = alpha, p.astype(BF16)
        for kv in range(n_kv):
            vt = v_ref[:, kv * hd:(kv + 1) * hd].T.astype(BF16)
            for g in range(group):
                acc_s[kv, :, col(g)] = alphas[kv, g] * acc_s[kv, :, col(g)] + _dot(vt, probs[kv, g])

    delta = qb - kb
    cs = lambda g: slice(g * blk_rows, (g + 1) * blk_rows)

    @pl.when(delta < n_near)
    def _():
        def adds_of(kv, g):
            diag = diag_ref[kv * group + g, pl.ds(delta, 1), :]
            tile = pltpu.roll(jnp.broadcast_to(diag, (blk_rows, 2 * blk_rows)), 0, 1, stride=1, stride_axis=0)
            chosen = (sel_s[kv, pl.ds(kb, 1), cs(g)] > 0.5) | (delta == 0)
            return tile[:, blk_rows:], jnp.where(chosen, 0.0, NEG)
        attend_all(adds_of)

    @pl.when(delta >= n_near)
    def _():
        def adds_of(kv, g):
            chosen = sel_s[kv, pl.ds(kb, 1), cs(g)] > 0.5
            return (jnp.where(chosen, far_ref[kv, :, cs(g)], NEG),)
        attend_all(adds_of)

    @pl.when(kb == qb)
    def _():
        for kv in range(n_kv):
            o = acc_s[kv] / l_s[kv]
            for g in range(group):
                h = kv * group + g
                o_ref[:, h * hd:(h + 1) * hd] = o[:, g * blk_rows:(g + 1) * blk_rows].T.astype(o_ref.dtype)


def _moba_prompt(qkv, means, diag, far_rows, bsz, n_heads, n_kv, hd):
    m = qkv.shape[0]
    t = m // bsz
    nb = t // MOBA_BLOCK
    group = n_heads // n_kv
    qw, kw = n_heads * hd, n_kv * hd
    kcol, vcol = qw // kw, qw // kw + 1
    n_near = diag.shape[1]
    cols = group * MOBA_BLOCK
    pairs = [(qb, kb) for qb in range(nb) for kb in range(qb + 1)]
    qb_tab = jnp.asarray([p[0] for p in pairs], jnp.int32)
    kb_tab = jnp.asarray([p[1] for p in pairs], jnp.int32)
    grid_spec = pltpu.PrefetchScalarGridSpec(
        num_scalar_prefetch=2,
        grid=(bsz, len(pairs)),
        in_specs=[pl.BlockSpec((MOBA_BLOCK, qw), lambda b, s, qt, kt: (b * nb + qt[s], 0)),
                  pl.BlockSpec((MOBA_BLOCK, kw), lambda b, s, qt, kt: (b * nb + kt[s], kcol)),
                  pl.BlockSpec((MOBA_BLOCK, kw), lambda b, s, qt, kt: (b * nb + kt[s], vcol)),
                  pl.BlockSpec((None, nb, kw), lambda b, s, qt, kt: (b, 0, 0)),
                  pl.BlockSpec(diag.shape, lambda b, s, qt, kt: (0, 0, 0)),
                  pl.BlockSpec((n_kv, 1, cols), lambda b, s, qt, kt: (0, 0, 0))],
        out_specs=pl.BlockSpec((MOBA_BLOCK, qw), lambda b, s, qt, kt: (b * nb + qt[s], 0)),
        scratch_shapes=[pltpu.VMEM((n_kv, cols, hd), BF16),
                        pltpu.VMEM((n_kv, nb, cols), F32),
                        pltpu.VMEM((n_kv, 1, cols), F32),
                        pltpu.VMEM((n_kv, 1, cols), F32),
                        pltpu.VMEM((n_kv, hd, cols), F32)],
    )
    return pl.pallas_call(
        functools.partial(_moba_prompt_kernel, n_kv=n_kv, group=group, hd=hd, scale=hd ** -0.5,
                          n_near=n_near, nb=nb),
        grid_spec=grid_spec,
        out_shape=jax.ShapeDtypeStruct((m, qw), BF16),
        compiler_params=_cparams("arbitrary", "arbitrary"),
    )(qb_tab, kb_tab, qkv, qkv, qkv, means, diag, far_rows)


_MEAN_PAGES_PER_STEP = 64
_ATTN_PAGES_PER_STEP = 32


def _pool_means_kernel(pt_ref, *refs, ppb, n_kv):
    del pt_ref
    o_ref = refs[-1]
    pages = refs[:-1]
    hd = pages[0].shape[1]
    sub = V7X_SUBLANES
    reps = sub // n_kv
    folded = []
    for blk in range(len(pages) // ppb):
        tot = None
        for p in range(ppb):
            part = jnp.sum(pages[blk * ppb + p][...].reshape(-1, sub, hd), axis=0)
            tot = part if tot is None else tot + part
        full = tot
        for i in range(1, reps):
            full = full + pltpu.roll(tot, i * n_kv, 0)
        folded.append(full)
    row_grp = lax.broadcasted_iota(jnp.int32, (sub, hd), 0) // n_kv
    tiles = []
    for a in range(0, len(folded), reps):
        tile = folded[a]
        for i in range(1, reps):
            tile = jnp.where(row_grp == i, folded[a + i], tile)
        tiles.append(tile)
    n_keys = ppb * pages[0].shape[0] // n_kv
    o_ref[...] = jnp.concatenate(tiles, axis=0) * (1.0 / n_keys)


def _pool_means(pool, pt_flat, page_off, dbs, n_pages, n_kv):
    rows, hd = pool.shape[1], pool.shape[2]
    ppb = MOBA_BLOCK // (rows // n_kv)
    pps = min(_MEAN_PAGES_PER_STEP, n_pages)
    blocks = pps // ppb
    assert n_pages % pps == 0 and pps % ppb == 0 and V7X_SUBLANES % n_kv == 0
    assert (blocks * n_kv) % V7X_SUBLANES == 0

    def page_spec(i):
        return pl.BlockSpec((None, rows, hd),
                            lambda b, s, pt: (pt[b * n_pages + s * pps + i] + page_off, 0, 0))

    grid_spec = pltpu.PrefetchScalarGridSpec(
        num_scalar_prefetch=1,
        grid=(dbs, n_pages // pps),
        in_specs=[page_spec(i) for i in range(pps)],
        out_specs=pl.BlockSpec((None, blocks * n_kv, hd), lambda b, s, pt: (b, s, 0)),
    )
    return pl.pallas_call(
        functools.partial(_pool_means_kernel, ppb=ppb, n_kv=n_kv),
        grid_spec=grid_spec,
        out_shape=jax.ShapeDtypeStruct((dbs, (n_pages // ppb) * n_kv, hd), F32),
        compiler_params=_cparams("arbitrary", "arbitrary"),
    )(pt_flat, *([pool] * pps))


def _cols_expand_kernel(tab_ref, idx_ref, o_ref, *, buckets, mult):
    idx = idx_ref[...]
    acc = jnp.full(idx.shape, NEG, F32)
    for b in buckets:
        acc = jnp.where(idx == b, tab_ref[b:b + 1, :] * mult, acc)
    o_ref[...] = acc


def _cols_expand(tab_cols, idx_np, tile_rows, mult=1.0):
    rows, cols = idx_np.shape
    buckets = tuple(int(b) for b in np.unique(idx_np) if b >= 0)
    return pl.pallas_call(
        functools.partial(_cols_expand_kernel, buckets=buckets, mult=mult),
        grid=(rows // tile_rows,),
        in_specs=[pl.BlockSpec(tab_cols.shape, lambda i: (0, 0)),
                  pl.BlockSpec((tile_rows, cols), lambda i: (i, 0))],
        out_specs=pl.BlockSpec((tile_rows, cols), lambda i: (i, 0)),
        out_shape=jax.ShapeDtypeStruct((rows, cols), F32),
        compiler_params=_cparams("arbitrary"),
    )(tab_cols, jnp.asarray(idx_np))


def _moba_sample_kernel(pt_ref, *refs, pps, ppb, n_kv, group, tp, n_blk, n_pages, far_cls, scale):
    del pt_ref
    k_pages = refs[:pps]
    v_pages = refs[pps:2 * pps]
    q_ref, means_ref, bias_ref, own_ref, kn_ref, vn_ref, o_ref, sel_s, m_s, l_s, acc_s = refs[2 * pps:]
    step = pl.program_id(1)
    cols, hd = q_ref.shape

    @pl.when(step == 0)
    def _():
        q = q_ref[...]
        blk = lax.broadcasted_iota(jnp.int32, (n_blk, cols), 0)
        col_kv = lax.broadcasted_iota(jnp.int32, (n_blk, cols), 1) // (group * tp)
        gate = jnp.zeros((n_blk, cols), F32)
        for kv in range(n_kv):
            g = _dot_nt(means_ref[kv * n_blk:(kv + 1) * n_blk, :], q, precision=lax.Precision.HIGHEST)
            gate = jnp.where(col_kv == kv, g, gate)
        rank = _rank_rows(gate, blk, n_blk)
        sel_s[...] = jnp.where(rank < MOBA_TOPK, 0.0, NEG)
        m_s[...] = jnp.full(m_s.shape, NEG, F32)
        l_s[...] = jnp.zeros(l_s.shape, F32)
        acc_s[...] = jnp.zeros(acc_s.shape, F32)

    qb = (q_ref[...] * (scale * LOG2E)).astype(BF16)
    page_keys = k_pages[0].shape[0] // n_kv
    col_kv = lax.broadcasted_iota(jnp.int32, (page_keys, cols), 1) // (group * tp)

    def accumulate(scores, values, expand):
        m_prev = m_s[...]
        m_new = m_prev
        for s in scores:
            m_new = jnp.maximum(m_new, jnp.max(s, axis=0, keepdims=True))
        alpha = jnp.exp2(m_prev - m_new)
        l_new = alpha * l_s[...]
        acc = alpha * acc_s[...]
        for s, v in zip(scores, values):
            p = jnp.exp2(s - m_new)
            l_new = l_new + jnp.sum(p, axis=0, keepdims=True)
            acc = acc + _dot(v.T.astype(BF16), expand(p).astype(BF16))
        m_s[...] = m_new
        l_s[...] = l_new
        acc_s[...] = acc

    def by_head(ref):
        return jnp.concatenate([ref[pl.ds(kv, page_keys, stride=n_kv), :] for kv in range(n_kv)], axis=0)

    def spread(p):
        return jnp.concatenate([jnp.where(col_kv == kv, p, 0.0) for kv in range(n_kv)], axis=0)

    scores, values = [], []
    for i in range(pps):
        page = step * pps + i
        cls = jnp.minimum(n_pages - 1 - page, far_cls)
        full = _dot_nt(by_head(k_pages[i]).astype(BF16), qb)
        s = full[0:page_keys]
        for kv in range(1, n_kv):
            s = jnp.where(col_kv == kv, full[kv * page_keys:(kv + 1) * page_keys], s)
        scores.append(s + bias_ref[cls] + sel_s[pl.ds(page // ppb, 1), :])
        values.append(by_head(v_pages[i]))
    accumulate(scores, values, spread)

    @pl.when(step == pl.num_programs(1) - 1)
    def _():
        s_own = _dot_nt(kn_ref[...].astype(BF16), qb) + own_ref[...]
        accumulate([s_own], [vn_ref[...]], lambda p: p)
        o_ref[...] = (acc_s[...] / l_s[...]).T


def _moba_sample(q_cols, means_t, k_new, v_new, pool_k, pool_v, pt_flat, page_off, bias_pages, bias_own, *,
                 n_kv, group, tp, n_pages):
    dbs, cols, hd = q_cols.shape
    rows = pool_k.shape[1]
    ppb = MOBA_BLOCK // (rows // n_kv)
    n_blk = n_pages // ppb
    pps = min(_ATTN_PAGES_PER_STEP, n_pages)
    assert n_pages % pps == 0
    far_cls = bias_pages.shape[0] - 1
    n_own = k_new.shape[1]

    def page_spec(i):
        return pl.BlockSpec((None, rows, hd),
                            lambda b, s, pt: (pt[b * n_pages + s * pps + i] + page_off, 0, 0))

    per_seq = lambda r: pl.BlockSpec((None, r, hd), lambda b, s, pt: (b, 0, 0))
    in_specs = ([page_spec(i) for i in range(pps)] * 2
                + [per_seq(cols), per_seq(n_kv * n_blk),
                   pl.BlockSpec(bias_pages.shape, lambda b, s, pt: (0, 0, 0)),
                   pl.BlockSpec(bias_own.shape, lambda b, s, pt: (0, 0)),
                   per_seq(n_own), per_seq(n_own)])
    grid_spec = pltpu.PrefetchScalarGridSpec(
        num_scalar_prefetch=1,
        grid=(dbs, n_pages // pps),
        in_specs=in_specs,
        out_specs=pl.BlockSpec((None, cols, hd), lambda b, s, pt: (b, 0, 0)),
        scratch_shapes=[pltpu.VMEM((n_blk, cols), F32), pltpu.VMEM((1, cols), F32),
                        pltpu.VMEM((1, cols), F32), pltpu.VMEM((hd, cols), F32)],
    )
    return pl.pallas_call(
        functools.partial(_moba_sample_kernel, pps=pps, ppb=ppb, n_kv=n_kv, group=group, tp=tp, n_blk=n_blk,
                          n_pages=n_pages, far_cls=far_cls, scale=hd ** -0.5),
        grid_spec=grid_spec,
        out_shape=jax.ShapeDtypeStruct((dbs, cols, hd), F32),
        compiler_params=_cparams("arbitrary", "arbitrary"),
    )(pt_flat, *([pool_k] * pps), *([pool_v] * pps), q_cols, means_t, bias_pages, bias_own, k_new, v_new)


def _swa_prompt_bias_idx(window, nbk):
    qi = np.arange(window)[:, None]
    kj = np.arange(2 * window)[None, :]
    dist = qi + window - kj
    return np.where((dist >= 0) & (dist < window), _bucket_np(dist, nbk), -1).astype(np.int32)


def _swa_sample_bias_idx(window, n_heads, tp, t_new, nbk):
    t = np.arange(n_heads * tp) % tp
    r = np.arange(window)
    dist = t[None, :] + window - r[:, None]
    idx = np.where((dist >= 0) & (dist < window), _bucket_np(dist, nbk), -1)
    return np.where(t[None, :] < t_new, idx, 0).astype(np.int32)


def _moba_prompt_diag_idx(n_near, nbk):
    dist = np.arange(n_near)[:, None] * MOBA_BLOCK + np.arange(2 * MOBA_BLOCK)[None, :] - MOBA_BLOCK
    return np.where(dist >= 0, _bucket_np(dist, nbk), -1).astype(np.int32)


def _moba_sample_page_idx(far_cls, page, n_heads, tp, t_new, nbk):
    cols = n_heads * tp
    t = np.arange(cols) % tp
    r = np.arange(page)
    tiles = []
    for c in range(far_cls + 1):
        dist = (c + 1) * page + t[None, :] - r[:, None]
        b = _bucket_np(dist, nbk) if c < far_cls else np.full(dist.shape, nbk - 1, np.int32)
        tiles.append(np.where(t[None, :] < t_new, b, 0))
    return np.concatenate(tiles, axis=0).astype(np.int32)


def _moba_sample_own_idx(n_kv, n_heads, group, tp, t_new, nbk):
    cols = n_heads * tp
    h, t = np.arange(cols) // tp, np.arange(cols) % tp
    tk, kvc = np.arange(tp * n_kv) // n_kv, np.arange(tp * n_kv) % n_kv
    ok = ((kvc[:, None] == (h // group)[None, :]) & (tk[:, None] <= t[None, :]) & (t[None, :] < t_new))
    return np.where(ok, _bucket_np(t[None, :] - tk[:, None], nbk), -1).astype(np.int32)


def kernel(x_prompt, x_sample, c_prompt, c_sample, state_conv, cache_swa_k, cache_swa_v, cache_moba_k, cache_moba_v, page_table, state_ffn, ada_w, ada_b, norm_mix, norm_ffn, norm_final, rel_bias, conv_w1, conv_b1, conv_dw, conv_dw_b, conv_ln_g, conv_ln_b, conv_w2, conv_b2, swa_wqkv, swa_wo, swa_sinks, moba_wqkv, moba_wo, ffn_w_up, ffn_dw, ffn_dw_b, ffn_w_down):
    bsz, seq, d = x_prompt.shape
    dbs, t_new, _ = x_sample.shape
    depth = ada_w.shape[0]
    window, n_kv, hd = cache_swa_k.shape[2], cache_swa_k.shape[3], cache_swa_k.shape[4]
    nbk, n_heads = rel_bias.shape
    group = n_heads // n_kv
    qw, kw = n_heads * hd, n_kv * hd
    cw = conv_dw.shape[1]
    d_ff = ffn_dw.shape[2]
    n_pool, page_size = cache_moba_k.shape[1], cache_moba_k.shape[2]
    n_pages = page_table.shape[1]
    ppb = MOBA_BLOCK // page_size
    n_blk_past = n_pages // ppb
    tp = V7X_SUBLANES
    assert t_new <= tp and n_pages % ppb == 0 and n_blk_past >= MOBA_TOPK and seq % MOBA_BLOCK == 0
    n_mixers = 3

    n_c = bsz + dbs
    c_rows = -(-n_c // V7X_SUBLANES) * V7X_SUBLANES
    c_all = jnp.concatenate([c_prompt, c_sample, jnp.zeros((c_rows - n_c, d), F32)], axis=0)
    mod = _ada(c_all, ada_w, ada_b)

    def mods(layer, which):
        chunk = mod[layer, :, which * d:(which + 1) * d]
        return chunk[:bsz], chunk[bsz:bsz + dbs]

    d_sat = _saturation_distance(nbk)
    n_near = -(-(d_sat + MOBA_BLOCK - 1) // MOBA_BLOCK)
    swa_bias_p = swa_bias_s = swa_bias_own = moba_bias_p = moba_far = moba_bias_sel = moba_bias_own = None
    tab_cols = jnp.repeat(rel_bias, tp, axis=1)
    own_idx = _moba_sample_own_idx(n_kv, n_heads, group, tp, t_new, nbk)
    if depth > 1:
        swa_bias_p = _bias_expand(rel_bias, _swa_prompt_bias_idx(window, nbk), window,
                                  (n_heads, window, 2 * window), (None, window, 2 * window),
                                  lambda h: (h, 0, 0))
        swa_bias_s = _cols_expand(tab_cols, _swa_sample_bias_idx(window, n_heads, tp, t_new, nbk), window)
        swa_bias_own = _cols_expand(tab_cols, own_idx, tp * n_kv)
    if depth > 2:
        cols = group * MOBA_BLOCK
        moba_bias_p = _bias_expand(
            rel_bias, _moba_prompt_diag_idx(n_near, nbk), n_near,
            (n_heads, n_near, 2 * MOBA_BLOCK), (None, n_near, 2 * MOBA_BLOCK),
            lambda h: (h, 0, 0), mult=LOG2E)
        moba_far = jnp.repeat(rel_bias[nbk - 1].reshape(n_kv, group, 1), MOBA_BLOCK, axis=2).reshape(n_kv, 1, cols)
        moba_far = moba_far * LOG2E
        far_cls = -(-(d_sat + page_size - 1) // page_size) - 1
        moba_bias_sel = _cols_expand(
            tab_cols, _moba_sample_page_idx(far_cls, page_size, n_heads, tp, t_new, nbk),
            page_size, mult=LOG2E).reshape(far_cls + 1, page_size, n_heads * tp)
        moba_bias_own = _cols_expand(tab_cols, own_idx, tp * n_kv, mult=LOG2E)

    def to_tm(a):
        return jnp.swapaxes(a, 0, 1).reshape((a.shape[1] * dbs,) + a.shape[2:])

    def to_bm(a, t):
        return jnp.swapaxes(a.reshape((t, dbs) + a.shape[1:]), 0, 1)

    xp = x_prompt.reshape(bsz * seq, d)
    xs = to_tm(x_sample)
    m_s = t_new * dbs

    conv_p, conv_s, swa_k_p, swa_v_p, swa_k_s, swa_v_s = [], [], [], [], [], []
    moba_k_p, moba_v_p, moba_k_s, moba_v_s, ffn_p, ffn_s = [], [], [], [], [], []

    def gates(layer, which):
        gp, gs = mods(layer, which)
        return gp.reshape(bsz, 1, d), jnp.tile(gs, (t_new, 1)).reshape(1, m_s, d)

    def normed(x_p, x_s, g, layer, which_sh):
        shp, shs = mods(layer, which_sh)
        scp, scs = mods(layer, which_sh + 1)
        hp = _norm_mod(x_p.reshape(bsz, seq, d), g, scp.reshape(bsz, 1, d), shp.reshape(bsz, 1, d), 512)
        hs = _norm_mod(x_s.reshape(t_new, dbs, d), g, scs.reshape(1, dbs, d), shs.reshape(1, dbs, d), dbs)
        return hp.reshape(bsz * seq, d), hs.reshape(m_s, d)

    def sample_views(qkv_bm):
        pad = jnp.concatenate([qkv_bm, jnp.zeros((dbs, tp - t_new, qkv_bm.shape[2]), qkv_bm.dtype)], axis=1)
        q_cols = pad[:, :, :qw].reshape(dbs, tp, n_heads, hd).transpose(0, 2, 1, 3).reshape(dbs, n_heads * tp, hd)
        return (q_cols, pad[:, :, qw:qw + kw].reshape(dbs, tp * n_kv, hd),
                pad[:, :, qw + kw:].reshape(dbs, tp * n_kv, hd))

    def cols_to_bm(o_cols):
        o = o_cols.reshape(dbs, n_heads, tp, hd)[:, :, :t_new].transpose(0, 2, 1, 3)
        return o.reshape(dbs, t_new, qw)

    for layer in range(depth):
        kind, j = layer % n_mixers, layer // n_mixers
        hp, hs = normed(xp, xs, norm_mix[layer], layer, 0)
        g1p, g1s = gates(layer, 2)
        if kind == 0:
            up, us = _glu(hp, hs, conv_w1, conv_b1, j)
            yp = _conv_ln_prompt(up.reshape(bsz, seq, d), conv_dw[j], conv_dw_b[j], conv_ln_g[j], conv_ln_b[j])
            st_tm = jnp.swapaxes(state_conv[j], 0, 1)
            ys, nst_tm = _conv_ln_sample(st_tm, us.reshape(t_new, dbs, d), conv_dw[j], conv_dw_b[j],
                                         conv_ln_g[j], conv_ln_b[j])
            conv_p.append(up.reshape(bsz, seq, d)[:, seq - (cw - 1):])
            conv_s.append(jnp.swapaxes(nst_tm, 0, 1))
            xp, xs = _mm(yp.reshape(bsz * seq, d), conv_w2, j, bias=conv_b2, res=xp, gate=g1p,
                         side=(ys.reshape(m_s, d), xs, g1s[0]))
        elif kind == 1:
            qkv_p, qkv_s = _mm(hp, swa_wqkv, j, side=(hs, None, None))
            op = _swa_prompt(qkv_p, swa_sinks[j], swa_bias_p, bsz, n_heads, n_kv, hd, window)
            q_cols, k_new, v_new = sample_views(to_bm(qkv_s, t_new))
            buf_k = cache_swa_k[j].reshape(dbs, window * n_kv, hd)
            buf_v = cache_swa_v[j].reshape(dbs, window * n_kv, hd)
            sink_row = jnp.repeat(swa_sinks[j], tp).reshape(1, n_heads * tp)
            o_cols = _swa_sample(q_cols, buf_k, buf_v, k_new, v_new, swa_bias_s, swa_bias_own, sink_row,
                                 n_kv=n_kv, group=group, tp=tp)
            os_ = to_tm(cols_to_bm(o_cols))
            tail = qkv_p.reshape(bsz, seq, qw + 2 * kw)[:, seq - window:]
            swa_k_p.append(tail[:, :, qw:qw + kw].reshape(bsz, window, n_kv, hd))
            swa_v_p.append(tail[:, :, qw + kw:].reshape(bsz, window, n_kv, hd))
            keep = t_new * n_kv
            swa_k_s.append(jnp.concatenate([buf_k[:, keep:], k_new[:, :keep]], axis=1)
                           .reshape(dbs, window, n_kv, hd))
            swa_v_s.append(jnp.concatenate([buf_v[:, keep:], v_new[:, :keep]], axis=1)
                           .reshape(dbs, window, n_kv, hd))
            xp, xs = _mm(op, swa_wo, j, res=xp, gate=g1p, side=(os_, xs, g1s[0]))
        else:
            qkv_p, k_rows, v_rows, qkv_s = _mm(hp, moba_wqkv, j, kv_heads=(n_kv, hd), side=(hs, None, None))
            nb = seq // MOBA_BLOCK
            means_p = _block_means(qkv_p, bsz * nb, qw // kw, kw).reshape(bsz, nb, kw)
            op = _moba_prompt(qkv_p, means_p, moba_bias_p, moba_far, bsz, n_heads, n_kv, hd)
            moba_k_p.append(k_rows.reshape(bsz, seq // page_size, page_size, n_kv, hd))
            moba_v_p.append(v_rows.reshape(bsz, seq // page_size, page_size, n_kv, hd))

            qkv_bm = to_bm(qkv_s, t_new)
            q_cols, k_new, v_new = sample_views(qkv_bm)
            pool_k = cache_moba_k.reshape(cache_moba_k.shape[0] * n_pool, page_size * n_kv, hd)
            pool_v = cache_moba_v.reshape(cache_moba_v.shape[0] * n_pool, page_size * n_kv, hd)
            pt_flat = page_table.reshape(-1)
            means_s = _pool_means(pool_k, pt_flat, j * n_pool, dbs, n_pages, n_kv)
            means_t = means_s.reshape(dbs, n_blk_past, n_kv, hd).transpose(0, 2, 1, 3)
            means_t = means_t.reshape(dbs, n_kv * n_blk_past, hd)
            o_cols = _moba_sample(q_cols, means_t, k_new, v_new, pool_k, pool_v, pt_flat, j * n_pool,
                                  moba_bias_sel, moba_bias_own, n_kv=n_kv, group=group, tp=tp, n_pages=n_pages)
            os_bm = cols_to_bm(o_cols).astype(BF16)
            moba_k_s.append(qkv_bm[:, :, qw:qw + kw].reshape(dbs, t_new, n_kv, hd))
            moba_v_s.append(qkv_bm[:, :, qw + kw:].reshape(dbs, t_new, n_kv, hd))
            xp, xs = _mm(op, moba_wo, j, res=xp, gate=g1p, side=(to_tm(os_bm), xs, g1s[0]))

        hp, hs = normed(xp, xs, norm_ffn[layer], layer, 3)
        g2p, g2s = gates(layer, 5)
        prev_tm = jnp.swapaxes(state_ffn[layer], 0, 1).reshape(2 * dbs, d_ff)
        ap, stp, as_, sts, w_down_bf16 = _ffn_up(hp, hs, prev_tm, ffn_w_up, ffn_dw, ffn_dw_b, ffn_w_down, layer,
                                                 bsz, dbs)
        ffn_p.append(stp)
        ffn_s.append(jnp.swapaxes(sts.reshape(2, dbs, d_ff), 0, 1))
        xp, xs = _mm(ap, w_down_bf16, 0, res=xp, gate=g2p, tn=512, weights_outer=False,
                     side=(as_, xs, g2s[0]))

    y_p = _norm_mod(xp.reshape(bsz, seq, d), norm_final, None, None, 512, out_dtype=F32)
    y_s = _norm_mod(xs.reshape(t_new, dbs, d), norm_final, None, None, dbs, out_dtype=F32)
    y_s = jnp.swapaxes(y_s, 0, 1)

    return (y_p, y_s, jnp.stack(conv_p), jnp.stack(conv_s), jnp.stack(swa_k_p), jnp.stack(swa_v_p),
            jnp.stack(swa_k_s), jnp.stack(swa_v_s), jnp.stack(moba_k_p), jnp.stack(moba_v_p),
            jnp.stack(moba_k_s), jnp.stack(moba_v_s), jnp.stack(ffn_p), jnp.stack(ffn_s))
```

```python
import functools
import math

import numpy as np
import jax
import jax.numpy as jnp
from jax import lax
from jax.experimental import pallas as pl
from jax.experimental.pallas import tpu as pltpu

MOBA_BLOCK = 256
MOBA_TOPK = 3
REL_MAX_DISTANCE = 1024
RMS_EPS = 1e-6
LN_EPS = 1e-5
N_MOD = 6

V7X_LANES = 128
V7X_SUBLANES = 8
V7X_VMEM_LIMIT_BYTES = 56 * 1024 * 1024

NEG = -1e30
LOG2E = 1.4426950408889634
BF16 = jnp.bfloat16
F32 = jnp.float32


def _cparams(*sem):
    return pltpu.CompilerParams(dimension_semantics=sem, vmem_limit_bytes=V7X_VMEM_LIMIT_BYTES)


def _dot(a, b):
    return jnp.dot(a, b, preferred_element_type=F32)


def _dot_nt(a, b, precision=None):
    return lax.dot_general(a, b, (((1,), (1,)), ((), ())), precision=precision,
                           preferred_element_type=F32)


def _silu(x):
    return x * jax.nn.sigmoid(x)


def _bucket_np(dist, num_buckets):
    n = np.maximum(dist, 0)
    max_exact = num_buckets // 2
    nf = np.maximum(n, 1).astype(np.float32)
    ratio = np.log(nf / np.float32(max_exact)) / np.float32(math.log(REL_MAX_DISTANCE / max_exact))
    large = max_exact + (ratio * np.float32(num_buckets - max_exact)).astype(np.int32)
    return np.where(n < max_exact, n, np.minimum(large, num_buckets - 1)).astype(np.int32)


def _saturation_distance(num_buckets):
    d = np.arange(0, 4 * REL_MAX_DISTANCE)
    b = _bucket_np(d, num_buckets)
    below = np.nonzero(b < num_buckets - 1)[0]
    return int(below.max()) + 1


def _bias_expand_kernel(tab_ref, idx_ref, o_ref, *, tiles, mult):
    h = pl.program_id(0)
    for n, (r0, nr, buckets) in enumerate(tiles):
        idx = idx_ref[r0:r0 + nr, :]
        acc = jnp.full(idx.shape, NEG, F32)
        for b in buckets:
            acc = jnp.where(idx == b, tab_ref[b, h] * mult, acc)
        if len(o_ref.shape) == 3:
            o_ref[n] = acc
        else:
            o_ref[r0:r0 + nr, :] = acc


def _bias_expand(table, idx_np, tile_rows, out_shape, out_block, out_index, mult=1.0):
    n_heads = table.shape[1]
    rows, cols = idx_np.shape
    tiles = []
    for r0 in range(0, rows, tile_rows):
        present = np.unique(idx_np[r0:r0 + tile_rows])
        tiles.append((r0, min(tile_rows, rows - r0), tuple(int(b) for b in present if b >= 0)))
    return pl.pallas_call(
        functools.partial(_bias_expand_kernel, tiles=tuple(tiles), mult=mult),
        grid=(n_heads,),
        in_specs=[pl.BlockSpec(memory_space=pltpu.SMEM),
                  pl.BlockSpec((rows, cols), lambda h: (0, 0))],
        out_specs=pl.BlockSpec(out_block, out_index),
        out_shape=jax.ShapeDtypeStruct(out_shape, F32),
        compiler_params=_cparams("arbitrary"),
    )(table, jnp.asarray(idx_np))


def _ada_kernel(c_ref, w_ref, b_ref, o_ref):
    a = _silu(c_ref[...]).astype(BF16)
    o_ref[...] = _dot(a, w_ref[...].astype(BF16)) + b_ref[...]


def _ada(c_all, ada_w, ada_b):
    depth, d, n = ada_w.shape
    rows = c_all.shape[0]
    tn = min(n, 2048)
    return pl.pallas_call(
        _ada_kernel,
        grid=(depth, n // tn),
        in_specs=[pl.BlockSpec((rows, d), lambda l, j: (0, 0)),
                  pl.BlockSpec((None, d, tn), lambda l, j: (l, 0, j)),
                  pl.BlockSpec((None, 1, tn), lambda l, j: (l, 0, j))],
        out_specs=pl.BlockSpec((None, rows, tn), lambda l, j: (l, 0, j)),
        out_shape=jax.ShapeDtypeStruct((depth, rows, n), F32),
        compiler_params=_cparams("arbitrary", "arbitrary"),
    )(c_all, ada_w, ada_b.reshape(depth, 1, n))


def _norm_mod_kernel(x_ref, g_ref, sc_ref, sh_ref, o_ref):
    x = x_ref[...]
    ms = jnp.mean(x * x, axis=-1, keepdims=True)
    y = x * lax.rsqrt(ms + RMS_EPS) * g_ref[...]
    o_ref[...] = (y * (1.0 + sc_ref[...]) + sh_ref[...]).astype(o_ref.dtype)


def _rms_kernel(x_ref, g_ref, o_ref):
    x = x_ref[...]
    ms = jnp.mean(x * x, axis=-1, keepdims=True)
    o_ref[...] = (x * lax.rsqrt(ms + RMS_EPS) * g_ref[...]).astype(o_ref.dtype)


def _norm_mod(x3, g, sc3, sh3, rows_per_step, out_dtype=BF16):
    a, r, d = x3.shape
    tr = min(rows_per_step, r)
    g3 = g.reshape(1, 1, d)
    if sc3 is None:
        return pl.pallas_call(
            _rms_kernel,
            grid=(a, r // tr),
            in_specs=[pl.BlockSpec((1, tr, d), lambda i, j: (i, j, 0)),
                      pl.BlockSpec((1, 1, d), lambda i, j: (0, 0, 0))],
            out_specs=pl.BlockSpec((1, tr, d), lambda i, j: (i, j, 0)),
            out_shape=jax.ShapeDtypeStruct(x3.shape, out_dtype),
            compiler_params=_cparams("arbitrary", "arbitrary"),
        )(x3, g3)
    per_seq = sc3.shape[0] == a and sc3.shape[1] == 1
    if per_seq:
        mspec = pl.BlockSpec((1, 1, d), lambda i, j: (i, 0, 0))
    else:
        assert sc3.shape[0] == 1 and sc3.shape[1] == r and tr == r
        mspec = pl.BlockSpec((1, r, d), lambda i, j: (0, 0, 0))
    return pl.pallas_call(
        _norm_mod_kernel,
        grid=(a, r // tr),
        in_specs=[pl.BlockSpec((1, tr, d), lambda i, j: (i, j, 0)),
                  pl.BlockSpec((1, 1, d), lambda i, j: (0, 0, 0)),
                  mspec, mspec],
        out_specs=pl.BlockSpec((1, tr, d), lambda i, j: (i, j, 0)),
        out_shape=jax.ShapeDtypeStruct(x3.shape, out_dtype),
        compiler_params=_cparams("arbitrary", "arbitrary"),
    )(x3, g3, sc3, sh3)


def _mm_kernel(*refs, names, cache_w, kv_heads):
    r = dict(zip(names, refs))
    if cache_w:
        @pl.when(pl.program_id(1) == 0)
        def _():
            r["w_s"][...] = r["w"][...].astype(BF16)
        w = r["w_s"][...]
    else:
        w = r["w"][...].astype(BF16)

    def product(x_name, res_name, gate_name):
        acc = _dot(r[x_name][...], w)
        if "bias" in r:
            acc = acc + r["bias"][...]
        if res_name in r:
            acc = r[res_name][...] + r[gate_name][...] * acc
        return acc

    acc = product("x", "res", "gate")
    r["o"][...] = acc.astype(r["o"].dtype)
    if kv_heads:
        n_kv, hd = kv_heads
        rows = acc.shape[0]

        @pl.when(pl.program_id(0) == pl.num_programs(0) - 1)
        def _():
            for kv in range(n_kv):
                r["k"][pl.ds(kv, rows, stride=n_kv), :] = acc[:, kv * hd:(kv + 1) * hd]
                r["v"][pl.ds(kv, rows, stride=n_kv), :] = acc[:, (n_kv + kv) * hd:(n_kv + kv + 1) * hd]
    if "x2" in r:
        row_tile = pl.program_id(1) if cache_w else pl.program_id(0)
        last = (pl.num_programs(1) - 1) if cache_w else 0

        @pl.when(row_tile == last)
        def _():
            r["o2"][...] = product("x2", "res2", "gate2").astype(r["o2"].dtype)


def _mm(x, w, layer, *, bias=None, res=None, gate=None, out_dtype=F32, tm=1024, tn=1024, weights_outer=True,
        kv_heads=None, side=None):
    m, k = x.shape
    n = w.shape[2]
    tm, tn = min(tm, m), min(tn, n)
    assert m % tm == 0 and n % tn == 0
    if weights_outer:
        grid = (n // tn, m // tm)
        ij = lambda a, b: (b, a)
    else:
        grid = (m // tm, n // tn)
        ij = lambda a, b: (a, b)

    def spec(block, fn):
        return pl.BlockSpec(block, lambda a, b: fn(*ij(a, b)))

    names = ["x", "w"]
    in_specs = [spec((tm, k), lambda i, j: (i, 0)),
                spec((None, k, tn), lambda i, j: (layer, 0, j))]
    args = [x, w]
    if bias is not None:
        names.append("bias")
        in_specs.append(spec((None, 1, tn), lambda i, j: (layer, 0, j)))
        args.append(bias.reshape(bias.shape[0], 1, n))
    if res is not None:
        gs, gr, _ = gate.shape
        assert gr in (1, tm) and (m // tm) % gs == 0
        tiles_per_gate = (m // tm) // gs
        names += ["res", "gate"]
        in_specs.append(spec((tm, tn), lambda i, j: (i, j)))
        in_specs.append(spec((None, gr, tn), lambda i, j: (i // tiles_per_gate, 0, j)))
        args += [res, gate]
    if side is not None:
        x2, res2, gate2 = side
        m2 = x2.shape[0]
        nj_side = n // tn
        if weights_outer:
            side_col = lambda i, j: j
        else:
            side_col = lambda i, j: jnp.where(i == 0, j, nj_side - 1)
        names.append("x2")
        in_specs.append(spec((m2, k), lambda i, j: (0, 0)))
        args.append(x2)
        if res2 is not None:
            names += ["res2", "gate2"]
            in_specs += [spec((m2, tn), lambda i, j: (0, side_col(i, j)))] * 2
            args += [res2, gate2]
    names.append("o")
    out_specs = [spec((tm, tn), lambda i, j: (i, j))]
    out_shape = [jax.ShapeDtypeStruct((m, n), out_dtype)]
    if kv_heads:
        n_kv, hd = kv_heads
        nj = n // tn
        assert weights_outer and tn == 2 * n_kv * hd
        kv_spec = spec((tm * n_kv, hd), lambda i, j: (jnp.where(j == nj - 1, i, 0), 0))
        names += ["k", "v"]
        out_specs += [kv_spec, kv_spec]
        out_shape += [jax.ShapeDtypeStruct((m * n_kv, hd), F32)] * 2
    if side is not None:
        names.append("o2")
        out_specs.append(spec((m2, tn), lambda i, j: (0, side_col(i, j))))
        out_shape.append(jax.ShapeDtypeStruct((m2, n), out_dtype))
    if weights_outer:
        names.append("w_s")
    outs = pl.pallas_call(
        functools.partial(_mm_kernel, names=tuple(names), cache_w=weights_outer, kv_heads=kv_heads),
        grid=grid,
        in_specs=in_specs,
        out_specs=out_specs,
        out_shape=out_shape,
        scratch_shapes=[pltpu.VMEM((k, tn), BF16)] if weights_outer else [],
        compiler_params=_cparams("arbitrary", "arbitrary"),
    )(*args)
    return outs[0] if len(outs) == 1 else tuple(outs)


def _glu_kernel(x_ref, x2_ref, wa_ref, wg_ref, ba_ref, bg_ref, o_ref, o2_ref, wa_s, wg_s):
    @pl.when(pl.program_id(1) == 0)
    def _():
        wa_s[...] = wa_ref[...].astype(BF16)
        wg_s[...] = wg_ref[...].astype(BF16)

    def glu(x):
        a = _dot(x, wa_s[...]) + ba_ref[...]
        g = _dot(x, wg_s[...]) + bg_ref[...]
        return a * jax.nn.sigmoid(g)

    o_ref[...] = glu(x_ref[...])

    @pl.when(pl.program_id(1) == pl.num_programs(1) - 1)
    def _():
        o2_ref[...] = glu(x2_ref[...])


def _glu(x, x2, w1, b1, layer, *, tm=1024, tn=512):
    m, k = x.shape
    m2 = x2.shape[0]
    n = w1.shape[2] // 2
    tm, tn = min(tm, m), min(tn, n)
    nj = n // tn
    b3 = b1.reshape(b1.shape[0], 1, 2 * n)
    return pl.pallas_call(
        _glu_kernel,
        grid=(nj, m // tm),
        in_specs=[pl.BlockSpec((tm, k), lambda j, i: (i, 0)),
                  pl.BlockSpec((m2, k), lambda j, i: (0, 0)),
                  pl.BlockSpec((None, k, tn), lambda j, i: (layer, 0, j)),
                  pl.BlockSpec((None, k, tn), lambda j, i: (layer, 0, nj + j)),
                  pl.BlockSpec((None, 1, tn), lambda j, i: (layer, 0, j)),
                  pl.BlockSpec((None, 1, tn), lambda j, i: (layer, 0, nj + j))],
        out_specs=[pl.BlockSpec((tm, tn), lambda j, i: (i, j)),
                   pl.BlockSpec((m2, tn), lambda j, i: (0, j))],
        out_shape=[jax.ShapeDtypeStruct((m, n), F32), jax.ShapeDtypeStruct((m2, n), F32)],
        scratch_shapes=[pltpu.VMEM((k, tn), BF16), pltpu.VMEM((k, tn), BF16)],
        compiler_params=_cparams("arbitrary", "arbitrary"),
    )(x, x2, w1, w1, b3, b3)


def _layernorm_silu(y, lg, lb):
    mu = jnp.mean(y, axis=-1, keepdims=True)
    yc = y - mu
    var = jnp.mean(yc * yc, axis=-1, keepdims=True)
    return _silu(yc * lax.rsqrt(var + LN_EPS) * lg + lb)


def _conv_ln_prompt_kernel(u_ref, halo_ref, dw_ref, dwb_ref, lg_ref, lb_ref, o_ref, full_s, y_s,
                           *, tm, halo, cw, rc, lw):
    i = pl.program_id(1)
    d = u_ref.shape[-1]
    full_s[halo:halo + tm, :] = u_ref[...]

    @pl.when(i == 0)
    def _():
        full_s[0:halo, :] = jnp.zeros((halo, d), F32)

    @pl.when(i > 0)
    def _():
        full_s[0:halo, :] = halo_ref[...]

    off = halo - (cw - 1)
    win = rc + halo
    sub = V7X_SUBLANES

    def body(r, carry):
        r0 = pl.multiple_of(r * rc, rc)
        for c0 in range(0, d, lw):
            window = full_s[pl.ds(r0, win), c0:c0 + lw]
            acc = jnp.zeros((rc, lw), F32) + dwb_ref[:, c0:c0 + lw]
            for s in range(sub):
                taps = [k for k in range(cw) if (off + k) % sub == s]
                if not taps:
                    continue
                shifted = window if s == 0 else pltpu.roll(window, win - s, 0)
                for k in taps:
                    a = off + k - s
                    acc = acc + dw_ref[k:k + 1, c0:c0 + lw] * shifted[a:a + rc]
            y_s[pl.ds(r0, rc), c0:c0 + lw] = acc
        return carry
    lax.fori_loop(0, tm // rc, body, 0)

    o_ref[...] = _layernorm_silu(y_s[...], lg_ref[...], lb_ref[...]).astype(o_ref.dtype)


def _conv_ln_prompt(u3, dw, dwb, lg, lb, *, tm=256):
    bsz, t, d = u3.shape
    cw = dw.shape[0]
    halo = 32
    assert cw - 1 <= halo and t % tm == 0 and tm % halo == 0
    hb = tm // halo
    lw = min(d, V7X_LANES)
    rc = min(tm, 4 * halo)
    return pl.pallas_call(
        functools.partial(_conv_ln_prompt_kernel, tm=tm, halo=halo, cw=cw, rc=rc, lw=lw),
        grid=(bsz, t // tm),
        in_specs=[pl.BlockSpec((None, tm, d), lambda b, i: (b, i, 0)),
                  pl.BlockSpec((None, halo, d), lambda b, i: (b, jnp.maximum(i * hb - 1, 0), 0)),
                  pl.BlockSpec((cw, d), lambda b, i: (0, 0)),
                  pl.BlockSpec((1, d), lambda b, i: (0, 0)),
                  pl.BlockSpec((1, d), lambda b, i: (0, 0)),
                  pl.BlockSpec((1, d), lambda b, i: (0, 0))],
        out_specs=pl.BlockSpec((None, tm, d), lambda b, i: (b, i, 0)),
        out_shape=jax.ShapeDtypeStruct((bsz, t, d), BF16),
        scratch_shapes=[pltpu.VMEM((tm + halo, d), F32), pltpu.VMEM((tm, d), F32)],
        compiler_params=_cparams("arbitrary", "arbitrary"),
    )(u3, u3, dw, dwb.reshape(1, d), lg.reshape(1, d), lb.reshape(1, d))


def _conv_ln_sample_kernel(st_ref, u_ref, dw_ref, dwb_ref, lg_ref, lb_ref, o_ref, nst_ref, *, cw, lw):
    t_new, _, d = u_ref.shape
    n_st = cw - 1

    def row(idx, c0):
        if idx < n_st:
            return st_ref[idx, :, c0:c0 + lw]
        return u_ref[idx - n_st, :, c0:c0 + lw]

    for t in range(t_new):
        parts = []
        for c0 in range(0, d, lw):
            acc = dw_ref[0:1, c0:c0 + lw] * row(t, c0)
            for k in range(1, cw):
                acc = acc + dw_ref[k:k + 1, c0:c0 + lw] * row(t + k, c0)
            parts.append(acc + dwb_ref[:, c0:c0 + lw])
        y = jnp.concatenate(parts, axis=-1) if len(parts) > 1 else parts[0]
        o_ref[t] = _layernorm_silu(y, lg_ref[...], lb_ref[...]).astype(o_ref.dtype)
    for r in range(n_st):
        idx = t_new + r
        nst_ref[r] = st_ref[idx] if idx < n_st else u_ref[idx - n_st]


def _conv_ln_sample(st_tm, u_tm, dw, dwb, lg, lb):
    n_st, bsz, d = st_tm.shape
    t_new = u_tm.shape[0]
    cw = dw.shape[0]
    lw = min(d, 512)
    full = lambda shape: pl.BlockSpec(shape, lambda i: (0,) * len(shape))
    return pl.pallas_call(
        functools.partial(_conv_ln_sample_kernel, cw=cw, lw=lw),
        grid=(1,),
        in_specs=[full((n_st, bsz, d)), full((t_new, bsz, d)), full((cw, d)), full((1, d)), full((1, d)),
                  full((1, d))],
        out_specs=[full((t_new, bsz, d)), full((n_st, bsz, d))],
        out_shape=[jax.ShapeDtypeStruct((t_new, bsz, d), BF16),
                   jax.ShapeDtypeStruct((n_st, bsz, d), F32)],
        compiler_params=_cparams("arbitrary"),
    )(st_tm, u_tm, dw, dwb.reshape(1, d), lg.reshape(1, d), lb.reshape(1, d))


def _ffn_up_kernel(x_ref, x2_ref, prev2_ref, wg_ref, wv_ref, dw_ref, dwb_ref, wd_ref, a_ref, st_ref, a2_ref,
                   st2_ref, wd_bf_ref, wg_s, wv_s, carry_s, *, tiles_per_seq, bsz2):
    i = pl.program_id(1)
    wd_bf_ref[...] = wd_ref[...].astype(BF16)

    @pl.when(i == 0)
    def _():
        wg_s[...] = wg_ref[...].astype(BF16)
        wv_s[...] = wv_ref[...].astype(BF16)

    @pl.when(i % tiles_per_seq == 0)
    def _():
        carry_s[...] = jnp.zeros(carry_s.shape, F32)

    x = x_ref[...]
    g = _dot(x, wg_s[...])
    v = _dot(x, wv_s[...])
    tm = g.shape[0]
    row = lax.broadcasted_iota(jnp.int32, g.shape, 0)
    prev1 = carry_s[V7X_SUBLANES - 1:V7X_SUBLANES, :]
    prev2 = carry_s[V7X_SUBLANES - 2:V7X_SUBLANES - 1, :]
    g1 = jnp.where(row == 0, prev1, pltpu.roll(g, 1, 0))
    g2 = jnp.where(row == 0, prev2, jnp.where(row == 1, prev1, pltpu.roll(g, 2, 0)))
    gc = dw_ref[0:1, :] * g2 + dw_ref[1:2, :] * g1 + dw_ref[2:3, :] * g + dwb_ref[...]
    a_ref[...] = (_silu(gc) * v).astype(a_ref.dtype)
    carry_s[...] = g[tm - V7X_SUBLANES:tm, :]
    st_ref[...] = g[tm - 2:tm, :]

    @pl.when(i == pl.num_programs(1) - 1)
    def _():
        x2 = x2_ref[...]
        g_s = _dot(x2, wg_s[...])
        v_s = _dot(x2, wv_s[...])
        rows = g_s.shape[0]
        full = jnp.concatenate([prev2_ref[...], g_s], axis=0)
        gc_s = (dw_ref[0:1, :] * full[0:rows] + dw_ref[1:2, :] * full[bsz2:bsz2 + rows]
                + dw_ref[2:3, :] * full[2 * bsz2:2 * bsz2 + rows] + dwb_ref[...])
        a2_ref[...] = (_silu(gc_s) * v_s).astype(a2_ref.dtype)
        st2_ref[...] = full[rows:rows + 2 * bsz2]


def _ffn_up(x, x2, prev2_tm, w_up, dw, dwb, w_down, layer, bsz, bsz2, *, tm=1024, tn=512):
    m, k = x.shape
    m2 = x2.shape[0]
    n = w_up.shape[2] // 2
    t = m // bsz
    tm, tn = min(tm, t), min(tn, n)
    assert t % tm == 0 and n % tn == 0 and dw.shape[1] == 3 and m2 >= 2 * bsz2
    nj, ni = n // tn, m // tm
    tiles_per_seq = t // tm
    kd, nd = w_down.shape[1], w_down.shape[2]
    slab = kd // (nj * ni)
    assert slab * nj * ni == kd and slab % (2 * V7X_SUBLANES) == 0
    return pl.pallas_call(
        functools.partial(_ffn_up_kernel, tiles_per_seq=tiles_per_seq, bsz2=bsz2),
        grid=(nj, ni),
        in_specs=[pl.BlockSpec((tm, k), lambda j, i: (i, 0)),
                  pl.BlockSpec((m2, k), lambda j, i: (0, 0)),
                  pl.BlockSpec((2 * bsz2, tn), lambda j, i: (0, j)),
                  pl.BlockSpec((None, k, tn), lambda j, i: (layer, 0, j)),
                  pl.BlockSpec((None, k, tn), lambda j, i: (layer, 0, nj + j)),
                  pl.BlockSpec((None, 3, tn), lambda j, i: (layer, 0, j)),
                  pl.BlockSpec((None, 1, tn), lambda j, i: (layer, 0, j)),
                  pl.BlockSpec((None, slab, nd), lambda j, i: (layer, j * ni + i, 0))],
        out_specs=[pl.BlockSpec((tm, tn), lambda j, i: (i, j)),
                   pl.BlockSpec((None, 2, tn), lambda j, i: (i // tiles_per_seq, 0, j)),
                   pl.BlockSpec((m2, tn), lambda j, i: (0, j)),
                   pl.BlockSpec((2 * bsz2, tn), lambda j, i: (0, j)),
                   pl.BlockSpec((None, slab, nd), lambda j, i: (0, j * ni + i, 0))],
        out_shape=[jax.ShapeDtypeStruct((m, n), BF16),
                   jax.ShapeDtypeStruct((bsz, 2, n), F32),
                   jax.ShapeDtypeStruct((m2, n), BF16),
                   jax.ShapeDtypeStruct((2 * bsz2, n), F32),
                   jax.ShapeDtypeStruct((1, kd, nd), BF16)],
        scratch_shapes=[pltpu.VMEM((k, tn), BF16), pltpu.VMEM((k, tn), BF16),
                        pltpu.VMEM((V7X_SUBLANES, tn), F32)],
        compiler_params=_cparams("arbitrary", "arbitrary"),
    )(x, x2, prev2_tm, w_up, w_up, dw, dwb.reshape(dwb.shape[0], 1, n), w_down)


def _swa_prompt_kernel(sink_ref, q_ref, kc_ref, kp_ref, vc_ref, vp_ref, bias_ref, o_ref, *, n_heads, group, hd,
                       scale):
    n = pl.program_id(1)
    w = q_ref.shape[0]
    kcat = jnp.concatenate([kp_ref[...], kc_ref[...]], axis=0).astype(BF16)
    vcat = jnp.concatenate([vp_ref[...], vc_ref[...]], axis=0).astype(BF16)
    col = lax.broadcasted_iota(jnp.int32, (w, 2 * w), 1)
    keep = (col >= w) | (n > 0)
    scores = []
    for h in range(n_heads):
        kv = h // group
        q = q_ref[:, h * hd:(h + 1) * hd].astype(BF16)
        s = _dot_nt(q, kcat[:, kv * hd:(kv + 1) * hd]) * scale + bias_ref[h]
        scores.append(jnp.where(keep, s, NEG))
    probs, sums = [], []
    for h in range(n_heads):
        sink = sink_ref[h]
        m = jnp.maximum(jnp.max(scores[h], axis=-1, keepdims=True), sink)
        p = jnp.exp(scores[h] - m)
        sums.append(jnp.sum(p, axis=-1, keepdims=True) + jnp.exp(sink - m))
        probs.append(p.astype(BF16))
    for h in range(n_heads):
        kv = h // group
        o = _dot(probs[h], vcat[:, kv * hd:(kv + 1) * hd]) / sums[h]
        o_ref[:, h * hd:(h + 1) * hd] = o.astype(o_ref.dtype)


def _swa_prompt(qkv, sinks, bias, bsz, n_heads, n_kv, hd, window):
    m = qkv.shape[0]
    t = m // bsz
    nb = t // window
    qw, kw = n_heads * hd, n_kv * hd
    assert qw % kw == 0 and t % window == 0
    kcol, vcol = qw // kw, qw // kw + 1
    return pl.pallas_call(
        functools.partial(_swa_prompt_kernel, n_heads=n_heads, group=n_heads // n_kv, hd=hd, scale=hd ** -0.5),
        grid=(bsz, nb),
        in_specs=[pl.BlockSpec(memory_space=pltpu.SMEM),
                  pl.BlockSpec((window, qw), lambda b, n: (b * nb + n, 0)),
                  pl.BlockSpec((window, kw), lambda b, n: (b * nb + n, kcol)),
                  pl.BlockSpec((window, kw), lambda b, n: (b * nb + jnp.maximum(n - 1, 0), kcol)),
                  pl.BlockSpec((window, kw), lambda b, n: (b * nb + n, vcol)),
                  pl.BlockSpec((window, kw), lambda b, n: (b * nb + jnp.maximum(n - 1, 0), vcol)),
                  pl.BlockSpec((n_heads, window, 2 * window), lambda b, n: (0, 0, 0))],
        out_specs=pl.BlockSpec((window, qw), lambda b, n: (b * nb + n, 0)),
        out_shape=jax.ShapeDtypeStruct((m, qw), BF16),
        compiler_params=_cparams("arbitrary", "arbitrary"),
    )(sinks, qkv, qkv, qkv, qkv, qkv, bias)


def _swa_sample_kernel(q_ref, bk_ref, bv_ref, kn_ref, vn_ref, bias_ref, own_ref, sink_ref, o_ref,
                       *, n_kv, group, tp, scale):
    cols, hd = q_ref.shape
    keys = bk_ref.shape[0] // n_kv
    qb = q_ref[...].astype(BF16)
    col_kv = lax.broadcasted_iota(jnp.int32, (keys, cols), 1) // (group * tp)

    def by_head(ref):
        return jnp.concatenate([ref[pl.ds(kv, keys, stride=n_kv), :] for kv in range(n_kv)], axis=0)

    full = _dot_nt(by_head(bk_ref).astype(BF16), qb)
    s = full[0:keys]
    for kv in range(1, n_kv):
        s = jnp.where(col_kv == kv, full[kv * keys:(kv + 1) * keys], s)
    s = s * scale + bias_ref[...]
    s_own = _dot_nt(kn_ref[...].astype(BF16), qb) * scale + own_ref[...]
    sink = sink_ref[...]
    m = jnp.maximum(jnp.maximum(jnp.max(s, axis=0, keepdims=True), jnp.max(s_own, axis=0, keepdims=True)), sink)
    p = jnp.exp(s - m)
    p_own = jnp.exp(s_own - m)
    l = jnp.sum(p, axis=0, keepdims=True) + jnp.sum(p_own, axis=0, keepdims=True) + jnp.exp(sink - m)
    spread = jnp.concatenate([jnp.where(col_kv == kv, p, 0.0) for kv in range(n_kv)], axis=0)
    acc = (_dot(by_head(bv_ref).T.astype(BF16), spread.astype(BF16))
           + _dot(vn_ref[...].T.astype(BF16), p_own.astype(BF16)))
    o_ref[...] = (acc / l).T.astype(o_ref.dtype)


def _swa_sample(q_cols, buf_k, buf_v, k_new, v_new, bias, own, sink_row, *, n_kv, group, tp):
    dbs, cols, hd = q_cols.shape
    per_seq = lambda a: pl.BlockSpec((None,) + a.shape[1:], lambda b: (b, 0, 0))
    shared = lambda a: pl.BlockSpec(a.shape, lambda b: (0, 0))
    return pl.pallas_call(
        functools.partial(_swa_sample_kernel, n_kv=n_kv, group=group, tp=tp, scale=hd ** -0.5),
        grid=(dbs,),
        in_specs=[per_seq(q_cols), per_seq(buf_k), per_seq(buf_v), per_seq(k_new), per_seq(v_new),
                  shared(bias), shared(own), shared(sink_row)],
        out_specs=pl.BlockSpec((None, cols, hd), lambda b: (b, 0, 0)),
        out_shape=jax.ShapeDtypeStruct((dbs, cols, hd), BF16),
        compiler_params=_cparams("arbitrary"),
    )(q_cols, buf_k, buf_v, k_new, v_new, bias, own, sink_row)


def _block_means_kernel(k_ref, o_ref):
    o_ref[...] = jnp.mean(k_ref[...], axis=0, keepdims=True)


def _block_means(qkv, n_blocks_total, kcol, kw):
    return pl.pallas_call(
        _block_means_kernel,
        grid=(n_blocks_total,),
        in_specs=[pl.BlockSpec((MOBA_BLOCK, kw), lambda i: (i, kcol))],
        out_specs=pl.BlockSpec((None, 1, kw), lambda i: (i, 0, 0)),
        out_shape=jax.ShapeDtypeStruct((n_blocks_total, 1, kw), F32),
        compiler_params=_cparams("arbitrary"),
    )(qkv)


def _rank_rows(gate, blk, n_rows):
    rank = jnp.zeros(gate.shape, jnp.int32)
    for m in range(n_rows):
        gm = gate[m:m + 1, :]
        beats = (gm > gate) | ((gm == gate) & (blk > m))
        rank = rank + jnp.where(beats, 1, 0)
    return rank


def _moba_prompt_kernel(qb_ref, kb_ref, pt_ref, q_ref, k_ref, v_ref, means_ref, diag_ref, far_ref, *refs,
                        n_kv, group, hd, scale, n_near, nb, n_pool_pages, ppb):
    del pt_ref
    pool_pages = refs[:n_pool_pages]
    o_ref, pool_means_ref, qs, sel_s, m_s, l_s, acc_s = refs[n_pool_pages:]
    pool_means_ref[...] = _page_block_means(pool_pages, ppb, n_kv)
    qb = qb_ref[pl.program_id(1)]
    kb = kb_ref[pl.program_id(1)]
    blk_rows = q_ref.shape[0]
    cols = group * blk_rows

    @pl.when(kb == 0)
    def _():
        blk = lax.broadcasted_iota(jnp.int32, (nb, cols), 0)
        for kv in range(n_kv):
            qg = jnp.concatenate([q_ref[:, (kv * group + g) * hd:(kv * group + g + 1) * hd]
                                  for g in range(group)], axis=0)
            qs[kv] = (qg * (scale * LOG2E)).astype(BF16)
            gate = _dot_nt(means_ref[:, kv * hd:(kv + 1) * hd], qg, precision=lax.Precision.HIGHEST)
            valid = blk < qb
            gate = jnp.where(valid, gate, -jnp.inf)
            rank = _rank_rows(gate, blk, nb)
            sel_s[kv] = jnp.where(valid & (rank < MOBA_TOPK), 1.0, 0.0)
            m_s[kv] = jnp.full((1, cols), NEG, F32)
            l_s[kv] = jnp.zeros((1, cols), F32)
            acc_s[kv] = jnp.zeros((hd, cols), F32)

    def attend_all(adds_of):
        chunks = [(kv, g) for kv in range(n_kv) for g in range(group)]
        col = lambda g: slice(g * blk_rows, (g + 1) * blk_rows)
        scores = {}
        for kv in range(n_kv):
            k = k_ref[:, kv * hd:(kv + 1) * hd].astype(BF16)
            for g in range(group):
                s = _dot_nt(k, qs[kv, col(g), :])
                for a in adds_of(kv, g):
                    s = s + a
                scores[kv, g] = s
        alphas, probs = {}, {}
        for kv, g in chunks:
            m_prev = m_s[kv, :, col(g)]
            m_new = jnp.maximum(m_prev, jnp.max(scores[kv, g], axis=0, keepdims=True))
            alpha = jnp.exp2(m_prev - m_new)
            p = jnp.exp2(scores[kv, g] - m_new)
            l_s[kv, :, col(g)] = alpha * l_s[kv, :, col(g)] + jnp.sum(p, axis=0, keepdims=True)
            m_s[kv, :, col(g)] = m_new
            alphas[kv, g], probs[kv, g] = alpha, p.astype(BF16)
        for kv in range(n_kv):
            vt = v_ref[:, kv * hd:(kv + 1) * hd].T.astype(BF16)
            for g in range(group):
                acc_s[kv, :, col(g)] = alphas[kv, g] * acc_s[kv, :, col(g)] + _dot(vt, probs[kv, g])

    delta = qb - kb
    cs = lambda g: slice(g * blk_rows, (g + 1) * blk_rows)

    @pl.when(delta < n_near)
    def _():
        def adds_of(kv, g):
            diag = diag_ref[kv * group + g, pl.ds(delta, 1), :]
            tile = pltpu.roll(jnp.broadcast_to(diag, (blk_rows, 2 * blk_rows)), 0, 1, stride=1, stride_axis=0)
            chosen = (sel_s[kv, pl.ds(kb, 1), cs(g)] > 0.5) | (delta == 0)
            return tile[:, blk_rows:], jnp.where(chosen, 0.0, NEG)
        attend_all(adds_of)

    @pl.when(delta >= n_near)
    def _():
        def adds_of(kv, g):
            chosen = sel_s[kv, pl.ds(kb, 1), cs(g)] > 0.5
            return (jnp.where(chosen, far_ref[kv, :, cs(g)], NEG),)
        attend_all(adds_of)

    @pl.when(kb == qb)
    def _():
        for kv in range(n_kv):
            o = acc_s[kv] / l_s[kv]
            for g in range(group):
                h = kv * group + g
                o_ref[:, h * hd:(h + 1) * hd] = o[:, g * blk_rows:(g + 1) * blk_rows].T.astype(o_ref.dtype)


def _moba_prompt(qkv, means, diag, far_rows, pool, pt_flat, page_off, bsz, n_heads, n_kv, hd):
    m = qkv.shape[0]
    t = m // bsz
    nb = t // MOBA_BLOCK
    group = n_heads // n_kv
    qw, kw = n_heads * hd, n_kv * hd
    kcol, vcol = qw // kw, qw // kw + 1
    n_near = diag.shape[1]
    cols = group * MOBA_BLOCK
    pairs = [(qb, kb) for qb in range(nb) for kb in range(qb + 1)]
    qb_tab = jnp.asarray([p[0] for p in pairs], jnp.int32)
    kb_tab = jnp.asarray([p[1] for p in pairs], jnp.int32)
    n_pairs = len(pairs)

    page_rows = pool.shape[1]
    ppb = MOBA_BLOCK // (page_rows // n_kv)
    total_pages = pt_flat.shape[0]
    pps = _means_pages_per_step(total_pages, bsz * n_pairs, ppb, n_kv)
    mean_steps = total_pages // pps
    mean_rows = (pps // ppb) * n_kv
    slab = lambda b, s: jnp.minimum(b * n_pairs + s, mean_steps - 1)

    def page_spec(i):
        return pl.BlockSpec((None, page_rows, hd),
                            lambda b, s, qt, kt, pt: (pt[slab(b, s) * pps + i] + page_off, 0, 0))

    grid_spec = pltpu.PrefetchScalarGridSpec(
        num_scalar_prefetch=3,
        grid=(bsz, n_pairs),
        in_specs=[pl.BlockSpec((MOBA_BLOCK, qw), lambda b, s, qt, kt, pt: (b * nb + qt[s], 0)),
                  pl.BlockSpec((MOBA_BLOCK, kw), lambda b, s, qt, kt, pt: (b * nb + kt[s], kcol)),
                  pl.BlockSpec((MOBA_BLOCK, kw), lambda b, s, qt, kt, pt: (b * nb + kt[s], vcol)),
                  pl.BlockSpec((None, nb, kw), lambda b, s, qt, kt, pt: (b, 0, 0)),
                  pl.BlockSpec(diag.shape, lambda b, s, qt, kt, pt: (0, 0, 0)),
                  pl.BlockSpec((n_kv, 1, cols), lambda b, s, qt, kt, pt: (0, 0, 0))]
                 + [page_spec(i) for i in range(pps)],
        out_specs=[pl.BlockSpec((MOBA_BLOCK, qw), lambda b, s, qt, kt, pt: (b * nb + qt[s], 0)),
                   pl.BlockSpec((mean_rows, hd), lambda b, s, qt, kt, pt: (slab(b, s), 0))],
        scratch_shapes=[pltpu.VMEM((n_kv, cols, hd), BF16),
                        pltpu.VMEM((n_kv, nb, cols), F32),
                        pltpu.VMEM((n_kv, 1, cols), F32),
                        pltpu.VMEM((n_kv, 1, cols), F32),
                        pltpu.VMEM((n_kv, hd, cols), F32)],
    )
    return pl.pallas_call(
        functools.partial(_moba_prompt_kernel, n_kv=n_kv, group=group, hd=hd, scale=hd ** -0.5,
                          n_near=n_near, nb=nb, n_pool_pages=pps, ppb=ppb),
        grid_spec=grid_spec,
        out_shape=[jax.ShapeDtypeStruct((m, qw), BF16),
                   jax.ShapeDtypeStruct(((total_pages // ppb) * n_kv, hd), F32)],
        compiler_params=_cparams("arbitrary", "arbitrary"),
    )(qb_tab, kb_tab, pt_flat, qkv, qkv, qkv, means, diag, far_rows, *([pool] * pps))


_ATTN_PAGES_PER_STEP = 32


def _page_block_means(pages, ppb, n_kv):
    hd = pages[0].shape[1]
    sub = V7X_SUBLANES
    reps = sub // n_kv
    folded = []
    for blk in range(len(pages) // ppb):
        rows = pages[blk * ppb][...]
        for p in range(1, ppb):
            rows = rows + pages[blk * ppb + p][...]
        chains = sub if rows.shape[0] % (sub * sub) == 0 else 1
        tot = jnp.sum(jnp.sum(rows.reshape(chains, -1, sub, hd), axis=1), axis=0)
        full = tot
        for i in range(1, reps):
            full = full + pltpu.roll(tot, i * n_kv, 0)
        folded.append(full)
    row_grp = lax.broadcasted_iota(jnp.int32, (sub, hd), 0) // n_kv
    tiles = []
    for a in range(0, len(folded), reps):
        tile = folded[a]
        for i in range(1, reps):
            tile = jnp.where(row_grp == i, folded[a + i], tile)
        tiles.append(tile)
    n_keys = ppb * pages[0].shape[0] // n_kv
    return jnp.concatenate(tiles, axis=0) * (1.0 / n_keys)


def _means_pages_per_step(total_pages, steps, ppb, n_kv):
    unit = ppb * max(1, V7X_SUBLANES // n_kv)
    pps = unit
    while pps * steps < total_pages or total_pages % pps:
        pps += unit
    return pps


def _cols_expand_kernel(tab_ref, idx_ref, o_ref, *, buckets, mult):
    idx = idx_ref[...]
    acc = jnp.full(idx.shape, NEG, F32)
    for b in buckets:
        acc = jnp.where(idx == b, tab_ref[b:b + 1, :] * mult, acc)
    o_ref[...] = acc


def _cols_expand(tab_cols, idx_np, tile_rows, mult=1.0):
    rows, cols = idx_np.shape
    buckets = tuple(int(b) for b in np.unique(idx_np) if b >= 0)
    return pl.pallas_call(
        functools.partial(_cols_expand_kernel, buckets=buckets, mult=mult),
        grid=(rows // tile_rows,),
        in_specs=[pl.BlockSpec(tab_cols.shape, lambda i: (0, 0)),
                  pl.BlockSpec((tile_rows, cols), lambda i: (i, 0))],
        out_specs=pl.BlockSpec((tile_rows, cols), lambda i: (i, 0)),
        out_shape=jax.ShapeDtypeStruct((rows, cols), F32),
        compiler_params=_cparams("arbitrary"),
    )(tab_cols, jnp.asarray(idx_np))


def _moba_sample_kernel(pt_ref, *refs, pps, ppb, n_kv, group, tp, n_blk, n_pages, far_cls, scale):
    del pt_ref
    k_pages = refs[:pps]
    v_pages = refs[pps:2 * pps]
    q_ref, means_ref, bias_ref, own_ref, kn_ref, vn_ref, o_ref, sel_s, m_s, l_s, acc_s = refs[2 * pps:]
    step = pl.program_id(1)
    cols, hd = q_ref.shape

    @pl.when(step == 0)
    def _():
        q = q_ref[...]
        blk = lax.broadcasted_iota(jnp.int32, (n_blk, cols), 0)
        col_kv = lax.broadcasted_iota(jnp.int32, (n_blk, cols), 1) // (group * tp)
        gate = jnp.zeros((n_blk, cols), F32)
        for kv in range(n_kv):
            g = _dot_nt(means_ref[kv * n_blk:(kv + 1) * n_blk, :], q, precision=lax.Precision.HIGHEST)
            gate = jnp.where(col_kv == kv, g, gate)
        rank = _rank_rows(gate, blk, n_blk)
        sel_s[...] = jnp.where(rank < MOBA_TOPK, 0.0, NEG)
        m_s[...] = jnp.full(m_s.shape, NEG, F32)
        l_s[...] = jnp.zeros(l_s.shape, F32)
        acc_s[...] = jnp.zeros(acc_s.shape, F32)

    qb = (q_ref[...] * (scale * LOG2E)).astype(BF16)
    page_keys = k_pages[0].shape[0] // n_kv
    col_kv = lax.broadcasted_iota(jnp.int32, (page_keys, cols), 1) // (group * tp)

    def accumulate(scores, values, expand):
        m_prev = m_s[...]
        m_new = m_prev
        for s in scores:
            m_new = jnp.maximum(m_new, jnp.max(s, axis=0, keepdims=True))
        alpha = jnp.exp2(m_prev - m_new)
        l_new = alpha * l_s[...]
        acc = alpha * acc_s[...]
        for s, v in zip(scores, values):
            p = jnp.exp2(s - m_new)
            l_new = l_new + jnp.sum(p, axis=0, keepdims=True)
            acc = acc + _dot(v.T.astype(BF16), expand(p).astype(BF16))
        m_s[...] = m_new
        l_s[...] = l_new
        acc_s[...] = acc

    def by_head(ref):
        return jnp.concatenate([ref[pl.ds(kv, page_keys, stride=n_kv), :] for kv in range(n_kv)], axis=0)

    def spread(p):
        return jnp.concatenate([jnp.where(col_kv == kv, p, 0.0) for kv in range(n_kv)], axis=0)

    scores, values = [], []
    for i in range(pps):
        page = step * pps + i
        cls = jnp.minimum(n_pages - 1 - page, far_cls)
        full = _dot_nt(by_head(k_pages[i]).astype(BF16), qb)
        s = full[0:page_keys]
        for kv in range(1, n_kv):
            s = jnp.where(col_kv == kv, full[kv * page_keys:(kv + 1) * page_keys], s)
        scores.append(s + bias_ref[cls] + sel_s[pl.ds(page // ppb, 1), :])
        values.append(by_head(v_pages[i]))
    accumulate(scores, values, spread)

    @pl.when(step == pl.num_programs(1) - 1)
    def _():
        s_own = _dot_nt(kn_ref[...].astype(BF16), qb) + own_ref[...]
        accumulate([s_own], [vn_ref[...]], lambda p: p)
        o_ref[...] = (acc_s[...] / l_s[...]).T


def _moba_sample(q_cols, means_t, k_new, v_new, pool_k, pool_v, pt_flat, page_off, bias_pages, bias_own, *,
                 n_kv, group, tp, n_pages):
    dbs, cols, hd = q_cols.shape
    rows = pool_k.shape[1]
    ppb = MOBA_BLOCK // (rows // n_kv)
    n_blk = n_pages // ppb
    pps = min(_ATTN_PAGES_PER_STEP, n_pages)
    assert n_pages % pps == 0
    far_cls = bias_pages.shape[0] - 1
    n_own = k_new.shape[1]

    def page_spec(i):
        return pl.BlockSpec((None, rows, hd),
                            lambda b, s, pt: (pt[b * n_pages + s * pps + i] + page_off, 0, 0))

    per_seq = lambda r: pl.BlockSpec((None, r, hd), lambda b, s, pt: (b, 0, 0))
    in_specs = ([page_spec(i) for i in range(pps)] * 2
                + [per_seq(cols), per_seq(n_kv * n_blk),
                   pl.BlockSpec(bias_pages.shape, lambda b, s, pt: (0, 0, 0)),
                   pl.BlockSpec(bias_own.shape, lambda b, s, pt: (0, 0)),
                   per_seq(n_own), per_seq(n_own)])
    grid_spec = pltpu.PrefetchScalarGridSpec(
        num_scalar_prefetch=1,
        grid=(dbs, n_pages // pps),
        in_specs=in_specs,
        out_specs=pl.BlockSpec((None, cols, hd), lambda b, s, pt: (b, 0, 0)),
        scratch_shapes=[pltpu.VMEM((n_blk, cols), F32), pltpu.VMEM((1, cols), F32),
                        pltpu.VMEM((1, cols), F32), pltpu.VMEM((hd, cols), F32)],
    )
    return pl.pallas_call(
        functools.partial(_moba_sample_kernel, pps=pps, ppb=ppb, n_kv=n_kv, group=group, tp=tp, n_blk=n_blk,
                          n_pages=n_pages, far_cls=far_cls, scale=hd ** -0.5),
        grid_spec=grid_spec,
        out_shape=jax.ShapeDtypeStruct((dbs, cols, hd), F32),
        compiler_params=_cparams("arbitrary", "arbitrary"),
    )(pt_flat, *([pool_k] * pps), *([pool_v] * pps), q_cols, means_t, bias_pages, bias_own, k_new, v_new)


def _swa_prompt_bias_idx(window, nbk):
    qi = np.arange(window)[:, None]
    kj = np.arange(2 * window)[None, :]
    dist = qi + window - kj
    return np.where((dist >= 0) & (dist < window), _bucket_np(dist, nbk), -1).astype(np.int32)


def _swa_sample_bias_idx(window, n_heads, tp, t_new, nbk):
    t = np.arange(n_heads * tp) % tp
    r = np.arange(window)
    dist = t[None, :] + window - r[:, None]
    idx = np.where((dist >= 0) & (dist < window), _bucket_np(dist, nbk), -1)
    return np.where(t[None, :] < t_new, idx, 0).astype(np.int32)


def _moba_prompt_diag_idx(n_near, nbk):
    dist = np.arange(n_near)[:, None] * MOBA_BLOCK + np.arange(2 * MOBA_BLOCK)[None, :] - MOBA_BLOCK
    return np.where(dist >= 0, _bucket_np(dist, nbk), -1).astype(np.int32)


def _moba_sample_page_idx(far_cls, page, n_heads, tp, t_new, nbk):
    cols = n_heads * tp
    t = np.arange(cols) % tp
    r = np.arange(page)
    tiles = []
    for c in range(far_cls + 1):
        dist = (c + 1) * page + t[None, :] - r[:, None]
        b = _bucket_np(dist, nbk) if c < far_cls else np.full(dist.shape, nbk - 1, np.int32)
        tiles.append(np.where(t[None, :] < t_new, b, 0))
    return np.concatenate(tiles, axis=0).astype(np.int32)


def _moba_sample_own_idx(n_kv, n_heads, group, tp, t_new, nbk):
    cols = n_heads * tp
    h, t = np.arange(cols) // tp, np.arange(cols) % tp
    tk, kvc = np.arange(tp * n_kv) // n_kv, np.arange(tp * n_kv) % n_kv
    ok = ((kvc[:, None] == (h // group)[None, :]) & (tk[:, None] <= t[None, :]) & (t[None, :] < t_new))
    return np.where(ok, _bucket_np(t[None, :] - tk[:, None], nbk), -1).astype(np.int32)


def kernel(x_prompt, x_sample, c_prompt, c_sample, state_conv, cache_swa_k, cache_swa_v, cache_moba_k, cache_moba_v, page_table, state_ffn, ada_w, ada_b, norm_mix, norm_ffn, norm_final, rel_bias, conv_w1, conv_b1, conv_dw, conv_dw_b, conv_ln_g, conv_ln_b, conv_w2, conv_b2, swa_wqkv, swa_wo, swa_sinks, moba_wqkv, moba_wo, ffn_w_up, ffn_dw, ffn_dw_b, ffn_w_down):
    bsz, seq, d = x_prompt.shape
    dbs, t_new, _ = x_sample.shape
    depth = ada_w.shape[0]
    window, n_kv, hd = cache_swa_k.shape[2], cache_swa_k.shape[3], cache_swa_k.shape[4]
    nbk, n_heads = rel_bias.shape
    group = n_heads // n_kv
    qw, kw = n_heads * hd, n_kv * hd
    cw = conv_dw.shape[1]
    d_ff = ffn_dw.shape[2]
    n_pool, page_size = cache_moba_k.shape[1], cache_moba_k.shape[2]
    n_pages = page_table.shape[1]
    ppb = MOBA_BLOCK // page_size
    n_blk_past = n_pages // ppb
    tp = V7X_SUBLANES
    assert t_new <= tp and n_pages % ppb == 0 and n_blk_past >= MOBA_TOPK and seq % MOBA_BLOCK == 0
    n_mixers = 3

    n_c = bsz + dbs
    c_rows = -(-n_c // V7X_SUBLANES) * V7X_SUBLANES
    c_all = jnp.concatenate([c_prompt, c_sample, jnp.zeros((c_rows - n_c, d), F32)], axis=0)
    mod = _ada(c_all, ada_w, ada_b)

    def mods(layer, which):
        chunk = mod[layer, :, which * d:(which + 1) * d]
        return chunk[:bsz], chunk[bsz:bsz + dbs]

    d_sat = _saturation_distance(nbk)
    n_near = -(-(d_sat + MOBA_BLOCK - 1) // MOBA_BLOCK)
    swa_bias_p = swa_bias_s = swa_bias_own = moba_bias_p = moba_far = moba_bias_sel = moba_bias_own = None
    tab_cols = jnp.repeat(rel_bias, tp, axis=1)
    own_idx = _moba_sample_own_idx(n_kv, n_heads, group, tp, t_new, nbk)
    if depth > 1:
        swa_bias_p = _bias_expand(rel_bias, _swa_prompt_bias_idx(window, nbk), window,
                                  (n_heads, window, 2 * window), (None, window, 2 * window),
                                  lambda h: (h, 0, 0))
        swa_bias_s = _cols_expand(tab_cols, _swa_sample_bias_idx(window, n_heads, tp, t_new, nbk), window)
        swa_bias_own = _cols_expand(tab_cols, own_idx, tp * n_kv)
    if depth > 2:
        cols = group * MOBA_BLOCK
        moba_bias_p = _bias_expand(
            rel_bias, _moba_prompt_diag_idx(n_near, nbk), n_near,
            (n_heads, n_near, 2 * MOBA_BLOCK), (None, n_near, 2 * MOBA_BLOCK),
            lambda h: (h, 0, 0), mult=LOG2E)
        moba_far = jnp.repeat(rel_bias[nbk - 1].reshape(n_kv, group, 1), MOBA_BLOCK, axis=2).reshape(n_kv, 1, cols)
        moba_far = moba_far * LOG2E
        far_cls = -(-(d_sat + page_size - 1) // page_size) - 1
        moba_bias_sel = _cols_expand(
            tab_cols, _moba_sample_page_idx(far_cls, page_size, n_heads, tp, t_new, nbk),
            page_size, mult=LOG2E).reshape(far_cls + 1, page_size, n_heads * tp)
        moba_bias_own = _cols_expand(tab_cols, own_idx, tp * n_kv, mult=LOG2E)

    def to_tm(a):
        return jnp.swapaxes(a, 0, 1).reshape((a.shape[1] * dbs,) + a.shape[2:])

    def to_bm(a, t):
        return jnp.swapaxes(a.reshape((t, dbs) + a.shape[1:]), 0, 1)

    xp = x_prompt.reshape(bsz * seq, d)
    xs = to_tm(x_sample)
    m_s = t_new * dbs

    conv_p, conv_s, swa_k_p, swa_v_p, swa_k_s, swa_v_s = [], [], [], [], [], []
    moba_k_p, moba_v_p, moba_k_s, moba_v_s, ffn_p, ffn_s = [], [], [], [], [], []

    def gates(layer, which):
        gp, gs = mods(layer, which)
        return gp.reshape(bsz, 1, d), jnp.tile(gs, (t_new, 1)).reshape(1, m_s, d)

    def normed(x_p, x_s, g, layer, which_sh):
        shp, shs = mods(layer, which_sh)
        scp, scs = mods(layer, which_sh + 1)
        hp = _norm_mod(x_p.reshape(bsz, seq, d), g, scp.reshape(bsz, 1, d), shp.reshape(bsz, 1, d), 512)
        hs = _norm_mod(x_s.reshape(t_new, dbs, d), g, scs.reshape(1, dbs, d), shs.reshape(1, dbs, d), dbs)
        return hp.reshape(bsz * seq, d), hs.reshape(m_s, d)

    def sample_views(qkv_bm):
        pad = jnp.concatenate([qkv_bm, jnp.zeros((dbs, tp - t_new, qkv_bm.shape[2]), qkv_bm.dtype)], axis=1)
        q_cols = pad[:, :, :qw].reshape(dbs, tp, n_heads, hd).transpose(0, 2, 1, 3).reshape(dbs, n_heads * tp, hd)
        return (q_cols, pad[:, :, qw:qw + kw].reshape(dbs, tp * n_kv, hd),
                pad[:, :, qw + kw:].reshape(dbs, tp * n_kv, hd))

    def cols_to_bm(o_cols):
        o = o_cols.reshape(dbs, n_heads, tp, hd)[:, :, :t_new].transpose(0, 2, 1, 3)
        return o.reshape(dbs, t_new, qw)

    for layer in range(depth):
        kind, j = layer % n_mixers, layer // n_mixers
        hp, hs = normed(xp, xs, norm_mix[layer], layer, 0)
        g1p, g1s = gates(layer, 2)
        if kind == 0:
            up, us = _glu(hp, hs, conv_w1, conv_b1, j)
            yp = _conv_ln_prompt(up.reshape(bsz, seq, d), conv_dw[j], conv_dw_b[j], conv_ln_g[j], conv_ln_b[j])
            st_tm = jnp.swapaxes(state_conv[j], 0, 1)
            ys, nst_tm = _conv_ln_sample(st_tm, us.reshape(t_new, dbs, d), conv_dw[j], conv_dw_b[j],
                                         conv_ln_g[j], conv_ln_b[j])
            conv_p.append(up.reshape(bsz, seq, d)[:, seq - (cw - 1):])
            conv_s.append(jnp.swapaxes(nst_tm, 0, 1))
            xp, xs = _mm(yp.reshape(bsz * seq, d), conv_w2, j, bias=conv_b2, res=xp, gate=g1p,
                         side=(ys.reshape(m_s, d), xs, g1s[0]))
        elif kind == 1:
            qkv_p, qkv_s = _mm(hp, swa_wqkv, j, side=(hs, None, None))
            op = _swa_prompt(qkv_p, swa_sinks[j], swa_bias_p, bsz, n_heads, n_kv, hd, window)
            q_cols, k_new, v_new = sample_views(to_bm(qkv_s, t_new))
            buf_k = cache_swa_k[j].reshape(dbs, window * n_kv, hd)
            buf_v = cache_swa_v[j].reshape(dbs, window * n_kv, hd)
            sink_row = jnp.repeat(swa_sinks[j], tp).reshape(1, n_heads * tp)
            o_cols = _swa_sample(q_cols, buf_k, buf_v, k_new, v_new, swa_bias_s, swa_bias_own, sink_row,
                                 n_kv=n_kv, group=group, tp=tp)
            os_ = to_tm(cols_to_bm(o_cols))
            tail = qkv_p.reshape(bsz, seq, qw + 2 * kw)[:, seq - window:]
            swa_k_p.append(tail[:, :, qw:qw + kw].reshape(bsz, window, n_kv, hd))
            swa_v_p.append(tail[:, :, qw + kw:].reshape(bsz, window, n_kv, hd))
            keep = t_new * n_kv
            swa_k_s.append(jnp.concatenate([buf_k[:, keep:], k_new[:, :keep]], axis=1)
                           .reshape(dbs, window, n_kv, hd))
            swa_v_s.append(jnp.concatenate([buf_v[:, keep:], v_new[:, :keep]], axis=1)
                           .reshape(dbs, window, n_kv, hd))
            xp, xs = _mm(op, swa_wo, j, res=xp, gate=g1p, side=(os_, xs, g1s[0]))
        else:
            qkv_p, k_rows, v_rows, qkv_s = _mm(hp, moba_wqkv, j, kv_heads=(n_kv, hd), side=(hs, None, None))
            nb = seq // MOBA_BLOCK
            means_p = _block_means(qkv_p, bsz * nb, qw // kw, kw).reshape(bsz, nb, kw)
            pool_k = cache_moba_k.reshape(cache_moba_k.shape[0] * n_pool, page_size * n_kv, hd)
            pool_v = cache_moba_v.reshape(cache_moba_v.shape[0] * n_pool, page_size * n_kv, hd)
            pt_flat = page_table.reshape(-1)
            op, means_s = _moba_prompt(qkv_p, means_p, moba_bias_p, moba_far, pool_k, pt_flat, j * n_pool,
                                       bsz, n_heads, n_kv, hd)
            moba_k_p.append(k_rows.reshape(bsz, seq // page_size, page_size, n_kv, hd))
            moba_v_p.append(v_rows.reshape(bsz, seq // page_size, page_size, n_kv, hd))

            qkv_bm = to_bm(qkv_s, t_new)
            q_cols, k_new, v_new = sample_views(qkv_bm)
            means_t = means_s.reshape(dbs, n_blk_past, n_kv, hd).transpose(0, 2, 1, 3)
            means_t = means_t.reshape(dbs, n_kv * n_blk_past, hd)
            o_cols = _moba_sample(q_cols, means_t, k_new, v_new, pool_k, pool_v, pt_flat, j * n_pool,
                                  moba_bias_sel, moba_bias_own, n_kv=n_kv, group=group, tp=tp, n_pages=n_pages)
            os_bm = cols_to_bm(o_cols).astype(BF16)
            moba_k_s.append(qkv_bm[:, :, qw:qw + kw].reshape(dbs, t_new, n_kv, hd))
            moba_v_s.append(qkv_bm[:, :, qw + kw:].reshape(dbs, t_new, n_kv, hd))
            xp, xs = _mm(op, moba_wo, j, res=xp, gate=g1p, side=(to_tm(os_bm), xs, g1s[0]))

        hp, hs = normed(xp, xs, norm_ffn[layer], layer, 3)
        g2p, g2s = gates(layer, 5)
        prev_tm = jnp.swapaxes(state_ffn[layer], 0, 1).reshape(2 * dbs, d_ff)
        ap, stp, as_, sts, w_down_bf16 = _ffn_up(hp, hs, prev_tm, ffn_w_up, ffn_dw, ffn_dw_b, ffn_w_down, layer,
                                                 bsz, dbs)
        ffn_p.append(stp)
        ffn_s.append(jnp.swapaxes(sts.reshape(2, dbs, d_ff), 0, 1))
        xp, xs = _mm(ap, w_down_bf16, 0, res=xp, gate=g2p, tn=512, weights_outer=False,
                     side=(as_, xs, g2s[0]))

    y_p = _norm_mod(xp.reshape(bsz, seq, d), norm_final, None, None, 512, out_dtype=F32)
    y_s = _norm_mod(xs.reshape(t_new, dbs, d), norm_final, None, None, dbs, out_dtype=F32)
    y_s = jnp.swapaxes(y_s, 0, 1)

    return (y_p, y_s, jnp.stack(conv_p), jnp.stack(conv_s), jnp.stack(swa_k_p), jnp.stack(swa_v_p),
            jnp.stack(swa_k_s), jnp.stack(swa_v_s), jnp.stack(moba_k_p), jnp.stack(moba_v_p),
            jnp.stack(moba_k_s), jnp.stack(moba_v_s), jnp.stack(ffn_p), jnp.stack(ffn_s))
```

```python
import functools
import math

import numpy as np
import jax
import jax.numpy as jnp
from jax import lax
from jax.experimental import pallas as pl
from jax.experimental.pallas import tpu as pltpu

MOBA_BLOCK = 256
MOBA_TOPK = 3
REL_MAX_DISTANCE = 1024
RMS_EPS = 1e-6
LN_EPS = 1e-5
N_MOD = 6

V7X_LANES = 128
V7X_SUBLANES = 8
V7X_VMEM_LIMIT_BYTES = 56 * 1024 * 1024

NEG = -1e30
LOG2E = 1.4426950408889634
BF16 = jnp.bfloat16
F32 = jnp.float32


def _cparams(*sem):
    return pltpu.CompilerParams(dimension_semantics=sem, vmem_limit_bytes=V7X_VMEM_LIMIT_BYTES)


def _dot(a, b):
    return jnp.dot(a, b, preferred_element_type=F32)


def _dot_nt(a, b, precision=None):
    return lax.dot_general(a, b, (((1,), (1,)), ((), ())), precision=precision,
                           preferred_element_type=F32)


def _silu(x):
    return x * jax.nn.sigmoid(x)


def _bucket_np(dist, num_buckets):
    n = np.maximum(dist, 0)
    max_exact = num_buckets // 2
    nf = np.maximum(n, 1).astype(np.float32)
    ratio = np.log(nf / np.float32(max_exact)) / np.float32(math.log(REL_MAX_DISTANCE / max_exact))
    large = max_exact + (ratio * np.float32(num_buckets - max_exact)).astype(np.int32)
    return np.where(n < max_exact, n, np.minimum(large, num_buckets - 1)).astype(np.int32)


def _saturation_distance(num_buckets):
    d = np.arange(0, 4 * REL_MAX_DISTANCE)
    b = _bucket_np(d, num_buckets)
    below = np.nonzero(b < num_buckets - 1)[0]
    return int(below.max()) + 1


def _bias_expand_kernel(tab_ref, idx_ref, o_ref, *, tiles, mult):
    h = pl.program_id(0)
    for n, (r0, nr, buckets) in enumerate(tiles):
        idx = idx_ref[r0:r0 + nr, :]
        acc = jnp.full(idx.shape, NEG, F32)
        for b in buckets:
            acc = jnp.where(idx == b, tab_ref[b, h] * mult, acc)
        if len(o_ref.shape) == 3:
            o_ref[n] = acc
        else:
            o_ref[r0:r0 + nr, :] = acc


def _bias_expand(table, idx_np, tile_rows, out_shape, out_block, out_index, mult=1.0):
    n_heads = table.shape[1]
    rows, cols = idx_np.shape
    tiles = []
    for r0 in range(0, rows, tile_rows):
        present = np.unique(idx_np[r0:r0 + tile_rows])
        tiles.append((r0, min(tile_rows, rows - r0), tuple(int(b) for b in present if b >= 0)))
    return pl.pallas_call(
        functools.partial(_bias_expand_kernel, tiles=tuple(tiles), mult=mult),
        grid=(n_heads,),
        in_specs=[pl.BlockSpec(memory_space=pltpu.SMEM),
                  pl.BlockSpec((rows, cols), lambda h: (0, 0))],
        out_specs=pl.BlockSpec(out_block, out_index),
        out_shape=jax.ShapeDtypeStruct(out_shape, F32),
        compiler_params=_cparams("arbitrary"),
    )(table, jnp.asarray(idx_np))


def _ada_kernel(c_ref, w_ref, b_ref, o_ref):
    a = _silu(c_ref[...]).astype(BF16)
    o_ref[...] = _dot(a, w_ref[...].astype(BF16)) + b_ref[...]


def _ada(c_all, ada_w, ada_b):
    depth, d, n = ada_w.shape
    rows = c_all.shape[0]
    tn = min(n, 2048)
    return pl.pallas_call(
        _ada_kernel,
        grid=(depth, n // tn),
        in_specs=[pl.BlockSpec((rows, d), lambda l, j: (0, 0)),
                  pl.BlockSpec((None, d, tn), lambda l, j: (l, 0, j)),
                  pl.BlockSpec((None, 1, tn), lambda l, j: (l, 0, j))],
        out_specs=pl.BlockSpec((None, rows, tn), lambda l, j: (l, 0, j)),
        out_shape=jax.ShapeDtypeStruct((depth, rows, n), F32),
        compiler_params=_cparams("arbitrary", "arbitrary"),
    )(c_all, ada_w, ada_b.reshape(depth, 1, n))


def _norm_mod_kernel(x_ref, g_ref, sc_ref, sh_ref, o_ref):
    x = x_ref[...]
    ms = jnp.mean(x * x, axis=-1, keepdims=True)
    y = x * lax.rsqrt(ms + RMS_EPS) * g_ref[...]
    o_ref[...] = (y * (1.0 + sc_ref[...]) + sh_ref[...]).astype(o_ref.dtype)


def _rms_kernel(x_ref, g_ref, o_ref):
    x = x_ref[...]
    ms = jnp.mean(x * x, axis=-1, keepdims=True)
    o_ref[...] = (x * lax.rsqrt(ms + RMS_EPS) * g_ref[...]).astype(o_ref.dtype)


def _norm_mod(x3, g, sc3, sh3, rows_per_step, out_dtype=BF16):
    a, r, d = x3.shape
    tr = min(rows_per_step, r)
    g3 = g.reshape(1, 1, d)
    if sc3 is None:
        return pl.pallas_call(
            _rms_kernel,
            grid=(a, r // tr),
            in_specs=[pl.BlockSpec((1, tr, d), lambda i, j: (i, j, 0)),
                      pl.BlockSpec((1, 1, d), lambda i, j: (0, 0, 0))],
            out_specs=pl.BlockSpec((1, tr, d), lambda i, j: (i, j, 0)),
            out_shape=jax.ShapeDtypeStruct(x3.shape, out_dtype),
            compiler_params=_cparams("arbitrary", "arbitrary"),
        )(x3, g3)
    per_seq = sc3.shape[0] == a and sc3.shape[1] == 1
    if per_seq:
        mspec = pl.BlockSpec((1, 1, d), lambda i, j: (i, 0, 0))
    else:
        assert sc3.shape[0] == 1 and sc3.shape[1] == r and tr == r
        mspec = pl.BlockSpec((1, r, d), lambda i, j: (0, 0, 0))
    return pl.pallas_call(
        _norm_mod_kernel,
        grid=(a, r // tr),
        in_specs=[pl.BlockSpec((1, tr, d), lambda i, j: (i, j, 0)),
                  pl.BlockSpec((1, 1, d), lambda i, j: (0, 0, 0)),
                  mspec, mspec],
        out_specs=pl.BlockSpec((1, tr, d), lambda i, j: (i, j, 0)),
        out_shape=jax.ShapeDtypeStruct(x3.shape, out_dtype),
        compiler_params=_cparams("arbitrary", "arbitrary"),
    )(x3, g3, sc3, sh3)


def _mm_kernel(*refs, names, cache_w, kv_heads):
    r = dict(zip(names, refs))
    if cache_w:
        @pl.when(pl.program_id(1) == 0)
        def _():
            r["w_s"][...] = r["w"][...].astype(BF16)
        w = r["w_s"][...]
    else:
        w = r["w"][...].astype(BF16)

    def product(x_name, res_name, gate_name):
        acc = _dot(r[x_name][...], w)
        if "bias" in r:
            acc = acc + r["bias"][...]
        if res_name in r:
            acc = r[res_name][...] + r[gate_name][...] * acc
        return acc

    acc = product("x", "res", "gate")
    r["o"][...] = acc.astype(r["o"].dtype)
    if kv_heads:
        n_kv, hd = kv_heads
        rows = acc.shape[0]

        @pl.when(pl.program_id(0) == pl.num_programs(0) - 1)
        def _():
            for kv in range(n_kv):
                r["k"][pl.ds(kv, rows, stride=n_kv), :] = acc[:, kv * hd:(kv + 1) * hd]
                r["v"][pl.ds(kv, rows, stride=n_kv), :] = acc[:, (n_kv + kv) * hd:(n_kv + kv + 1) * hd]
    if "x2" in r:
        row_tile = pl.program_id(1) if cache_w else pl.program_id(0)
        last = (pl.num_programs(1) - 1) if cache_w else 0

        @pl.when(row_tile == last)
        def _():
            r["o2"][...] = product("x2", "res2", "gate2").astype(r["o2"].dtype)


def _mm(x, w, layer, *, bias=None, res=None, gate=None, out_dtype=F32, tm=1024, tn=1024, weights_outer=True,
        kv_heads=None, side=None):
    m, k = x.shape
    n = w.shape[2]
    tm, tn = min(tm, m), min(tn, n)
    assert m % tm == 0 and n % tn == 0
    if weights_outer:
        grid = (n // tn, m // tm)
        ij = lambda a, b: (b, a)
    else:
        grid = (m // tm, n // tn)
        ij = lambda a, b: (a, b)

    def spec(block, fn):
        return pl.BlockSpec(block, lambda a, b: fn(*ij(a, b)))

    names = ["x", "w"]
    in_specs = [spec((tm, k), lambda i, j: (i, 0)),
                spec((None, k, tn), lambda i, j: (layer, 0, j))]
    args = [x, w]
    if bias is not None:
        names.append("bias")
        in_specs.append(spec((None, 1, tn), lambda i, j: (layer, 0, j)))
        args.append(bias.reshape(bias.shape[0], 1, n))
    if res is not None:
        gs, gr, _ = gate.shape
        assert gr in (1, tm) and (m // tm) % gs == 0
        tiles_per_gate = (m // tm) // gs
        names += ["res", "gate"]
        in_specs.append(spec((tm, tn), lambda i, j: (i, j)))
        in_specs.append(spec((None, gr, tn), lambda i, j: (i // tiles_per_gate, 0, j)))
        args += [res, gate]
    if side is not None:
        x2, res2, gate2 = side
        m2 = x2.shape[0]
        nj_side = n // tn
        if weights_outer:
            side_col = lambda i, j: j
        else:
            side_col = lambda i, j: jnp.where(i == 0, j, nj_side - 1)
        names.append("x2")
        in_specs.append(spec((m2, k), lambda i, j: (0, 0)))
        args.append(x2)
        if res2 is not None:
            names += ["res2", "gate2"]
            in_specs += [spec((m2, tn), lambda i, j: (0, side_col(i, j)))] * 2
            args += [res2, gate2]
    names.append("o")
    out_specs = [spec((tm, tn), lambda i, j: (i, j))]
    out_shape = [jax.ShapeDtypeStruct((m, n), out_dtype)]
    if kv_heads:
        n_kv, hd = kv_heads
        nj = n // tn
        assert weights_outer and tn == 2 * n_kv * hd
        kv_spec = spec((tm * n_kv, hd), lambda i, j: (jnp.where(j == nj - 1, i, 0), 0))
        names += ["k", "v"]
        out_specs += [kv_spec, kv_spec]
        out_shape += [jax.ShapeDtypeStruct((m * n_kv, hd), F32)] * 2
    if side is not None:
        names.append("o2")
        out_specs.append(spec((m2, tn), lambda i, j: (0, side_col(i, j))))
        out_shape.append(jax.ShapeDtypeStruct((m2, n), out_dtype))
    if weights_outer:
        names.append("w_s")
    outs = pl.pallas_call(
        functools.partial(_mm_kernel, names=tuple(names), cache_w=weights_outer, kv_heads=kv_heads),
        grid=grid,
        in_specs=in_specs,
        out_specs=out_specs,
        out_shape=out_shape,
        scratch_shapes=[pltpu.VMEM((k, tn), BF16)] if weights_outer else [],
        compiler_params=_cparams("arbitrary", "arbitrary"),
    )(*args)
    return outs[0] if len(outs) == 1 else tuple(outs)


def _glu_kernel(x_ref, x2_ref, wa_ref, wg_ref, ba_ref, bg_ref, o_ref, o2_ref, wa_s, wg_s):
    @pl.when(pl.program_id(1) == 0)
    def _():
        wa_s[...] = wa_ref[...].astype(BF16)
        wg_s[...] = wg_ref[...].astype(BF16)

    def glu(x):
        a = _dot(x, wa_s[...]) + ba_ref[...]
        g = _dot(x, wg_s[...]) + bg_ref[...]
        return a * jax.nn.sigmoid(g)

    o_ref[...] = glu(x_ref[...])

    @pl.when(pl.program_id(1) == pl.num_programs(1) - 1)
    def _():
        o2_ref[...] = glu(x2_ref[...])


def _glu(x, x2, w1, b1, layer, *, tm=1024, tn=512):
    m, k = x.shape
    m2 = x2.shape[0]
    n = w1.shape[2] // 2
    tm, tn = min(tm, m), min(tn, n)
    nj = n // tn
    b3 = b1.reshape(b1.shape[0], 1, 2 * n)
    return pl.pallas_call(
        _glu_kernel,
        grid=(nj, m // tm),
        in_specs=[pl.BlockSpec((tm, k), lambda j, i: (i, 0)),
                  pl.BlockSpec((m2, k), lambda j, i: (0, 0)),
                  pl.BlockSpec((None, k, tn), lambda j, i: (layer, 0, j)),
                  pl.BlockSpec((None, k, tn), lambda j, i: (layer, 0, nj + j)),
                  pl.BlockSpec((None, 1, tn), lambda j, i: (layer, 0, j)),
                  pl.BlockSpec((None, 1, tn), lambda j, i: (layer, 0, nj + j))],
        out_specs=[pl.BlockSpec((tm, tn), lambda j, i: (i, j)),
                   pl.BlockSpec((m2, tn), lambda j, i: (0, j))],
        out_shape=[jax.ShapeDtypeStruct((m, n), F32), jax.ShapeDtypeStruct((m2, n), F32)],
        scratch_shapes=[pltpu.VMEM((k, tn), BF16), pltpu.VMEM((k, tn), BF16)],
        compiler_params=_cparams("arbitrary", "arbitrary"),
    )(x, x2, w1, w1, b3, b3)


def _layernorm_silu(y, lg, lb):
    mu = jnp.mean(y, axis=-1, keepdims=True)
    yc = y - mu
    var = jnp.mean(yc * yc, axis=-1, keepdims=True)
    return _silu(yc * lax.rsqrt(var + LN_EPS) * lg + lb)


def _conv_ln_prompt_kernel(u_ref, halo_ref, dw_ref, dwb_ref, lg_ref, lb_ref, o_ref, full_s, y_s,
                           *, tm, halo, cw, rc, lw):
    i = pl.program_id(1)
    d = u_ref.shape[-1]
    full_s[halo:halo + tm, :] = u_ref[...]

    @pl.when(i == 0)
    def _():
        full_s[0:halo, :] = jnp.zeros((halo, d), F32)

    @pl.when(i > 0)
    def _():
        full_s[0:halo, :] = halo_ref[...]

    off = halo - (cw - 1)
    win = rc + halo
    sub = V7X_SUBLANES

    def body(r, carry):
        r0 = pl.multiple_of(r * rc, rc)
        for c0 in range(0, d, lw):
            window = full_s[pl.ds(r0, win), c0:c0 + lw]
            acc = jnp.zeros((rc, lw), F32) + dwb_ref[:, c0:c0 + lw]
            for s in range(sub):
                taps = [k for k in range(cw) if (off + k) % sub == s]
                if not taps:
                    continue
                shifted = window if s == 0 else pltpu.roll(window, win - s, 0)
                for k in taps:
                    a = off + k - s
                    acc = acc + dw_ref[k:k + 1, c0:c0 + lw] * shifted[a:a + rc]
            y_s[pl.ds(r0, rc), c0:c0 + lw] = acc
        return carry
    lax.fori_loop(0, tm // rc, body, 0)

    o_ref[...] = _layernorm_silu(y_s[...], lg_ref[...], lb_ref[...]).astype(o_ref.dtype)


def _conv_ln_prompt(u3, dw, dwb, lg, lb, *, tm=512):
    bsz, t, d = u3.shape
    cw = dw.shape[0]
    halo = 32
    assert cw - 1 <= halo and t % tm == 0 and tm % halo == 0
    hb = tm // halo
    lw = min(d, V7X_LANES)
    rc = min(tm, 4 * halo)
    return pl.pallas_call(
        functools.partial(_conv_ln_prompt_kernel, tm=tm, halo=halo, cw=cw, rc=rc, lw=lw),
        grid=(bsz, t // tm),
        in_specs=[pl.BlockSpec((None, tm, d), lambda b, i: (b, i, 0)),
                  pl.BlockSpec((None, halo, d), lambda b, i: (b, jnp.maximum(i * hb - 1, 0), 0)),
                  pl.BlockSpec((cw, d), lambda b, i: (0, 0)),
                  pl.BlockSpec((1, d), lambda b, i: (0, 0)),
                  pl.BlockSpec((1, d), lambda b, i: (0, 0)),
                  pl.BlockSpec((1, d), lambda b, i: (0, 0))],
        out_specs=pl.BlockSpec((None, tm, d), lambda b, i: (b, i, 0)),
        out_shape=jax.ShapeDtypeStruct((bsz, t, d), BF16),
        scratch_shapes=[pltpu.VMEM((tm + halo, d), F32), pltpu.VMEM((tm, d), F32)],
        compiler_params=_cparams("arbitrary", "arbitrary"),
    )(u3, u3, dw, dwb.reshape(1, d), lg.reshape(1, d), lb.reshape(1, d))


def _conv_ln_sample_kernel(st_ref, u_ref, dw_ref, dwb_ref, lg_ref, lb_ref, o_ref, nst_ref, *, cw, lw):
    t_new, _, d = u_ref.shape
    n_st = cw - 1

    def row(idx, c0):
        if idx < n_st:
            return st_ref[idx, :, c0:c0 + lw]
        return u_ref[idx - n_st, :, c0:c0 + lw]

    for t in range(t_new):
        parts = []
        for c0 in range(0, d, lw):
            acc = dw_ref[0:1, c0:c0 + lw] * row(t, c0)
            for k in range(1, cw):
                acc = acc + dw_ref[k:k + 1, c0:c0 + lw] * row(t + k, c0)
            parts.append(acc + dwb_ref[:, c0:c0 + lw])
        y = jnp.concatenate(parts, axis=-1) if len(parts) > 1 else parts[0]
        o_ref[t] = _layernorm_silu(y, lg_ref[...], lb_ref[...]).astype(o_ref.dtype)
    for r in range(n_st):
        idx = t_new + r
        nst_ref[r] = st_ref[idx] if idx < n_st else u_ref[idx - n_st]


def _conv_ln_sample(st_tm, u_tm, dw, dwb, lg, lb):
    n_st, bsz, d = st_tm.shape
    t_new = u_tm.shape[0]
    cw = dw.shape[0]
    lw = min(d, 512)
    full = lambda shape: pl.BlockSpec(shape, lambda i: (0,) * len(shape))
    return pl.pallas_call(
        functools.partial(_conv_ln_sample_kernel, cw=cw, lw=lw),
        grid=(1,),
        in_specs=[full((n_st, bsz, d)), full((t_new, bsz, d)), full((cw, d)), full((1, d)), full((1, d)),
                  full((1, d))],
        out_specs=[full((t_new, bsz, d)), full((n_st, bsz, d))],
        out_shape=[jax.ShapeDtypeStruct((t_new, bsz, d), BF16),
                   jax.ShapeDtypeStruct((n_st, bsz, d), F32)],
        compiler_params=_cparams("arbitrary"),
    )(st_tm, u_tm, dw, dwb.reshape(1, d), lg.reshape(1, d), lb.reshape(1, d))


def _ffn_up_kernel(x_ref, x2_ref, prev2_ref, wg_ref, wv_ref, dw_ref, dwb_ref, wd_ref, a_ref, st_ref, a2_ref,
                   st2_ref, wd_bf_ref, wg_s, wv_s, carry_s, *, tiles_per_seq, bsz2):
    i = pl.program_id(1)
    wd_bf_ref[...] = wd_ref[...].astype(BF16)

    @pl.when(i == 0)
    def _():
        wg_s[...] = wg_ref[...].astype(BF16)
        wv_s[...] = wv_ref[...].astype(BF16)

    @pl.when(i % tiles_per_seq == 0)
    def _():
        carry_s[...] = jnp.zeros(carry_s.shape, F32)

    x = x_ref[...]
    g = _dot(x, wg_s[...])
    v = _dot(x, wv_s[...])
    tm = g.shape[0]
    row = lax.broadcasted_iota(jnp.int32, g.shape, 0)
    prev1 = carry_s[V7X_SUBLANES - 1:V7X_SUBLANES, :]
    prev2 = carry_s[V7X_SUBLANES - 2:V7X_SUBLANES - 1, :]
    g1 = jnp.where(row == 0, prev1, pltpu.roll(g, 1, 0))
    g2 = jnp.where(row == 0, prev2, jnp.where(row == 1, prev1, pltpu.roll(g, 2, 0)))
    gc = dw_ref[0:1, :] * g2 + dw_ref[1:2, :] * g1 + dw_ref[2:3, :] * g + dwb_ref[...]
    a_ref[...] = (_silu(gc) * v).astype(a_ref.dtype)
    carry_s[...] = g[tm - V7X_SUBLANES:tm, :]
    st_ref[...] = g[tm - 2:tm, :]

    @pl.when(i == pl.num_programs(1) - 1)
    def _():
        x2 = x2_ref[...]
        g_s = _dot(x2, wg_s[...])
        v_s = _dot(x2, wv_s[...])
        rows = g_s.shape[0]
        full = jnp.concatenate([prev2_ref[...], g_s], axis=0)
        gc_s = (dw_ref[0:1, :] * full[0:rows] + dw_ref[1:2, :] * full[bsz2:bsz2 + rows]
                + dw_ref[2:3, :] * full[2 * bsz2:2 * bsz2 + rows] + dwb_ref[...])
        a2_ref[...] = (_silu(gc_s) * v_s).astype(a2_ref.dtype)
        st2_ref[...] = full[rows:rows + 2 * bsz2]


def _ffn_up(x, x2, prev2_tm, w_up, dw, dwb, w_down, layer, bsz, bsz2, *, tm=1024, tn=512):
    m, k = x.shape
    m2 = x2.shape[0]
    n = w_up.shape[2] // 2
    t = m // bsz
    tm, tn = min(tm, t), min(tn, n)
    assert t % tm == 0 and n % tn == 0 and dw.shape[1] == 3 and m2 >= 2 * bsz2
    nj, ni = n // tn, m // tm
    tiles_per_seq = t // tm
    kd, nd = w_down.shape[1], w_down.shape[2]
    slab = kd // (nj * ni)
    assert slab * nj * ni == kd and slab % (2 * V7X_SUBLANES) == 0
    return pl.pallas_call(
        functools.partial(_ffn_up_kernel, tiles_per_seq=tiles_per_seq, bsz2=bsz2),
        grid=(nj, ni),
        in_specs=[pl.BlockSpec((tm, k), lambda j, i: (i, 0)),
                  pl.BlockSpec((m2, k), lambda j, i: (0, 0)),
                  pl.BlockSpec((2 * bsz2, tn), lambda j, i: (0, j)),
                  pl.BlockSpec((None, k, tn), lambda j, i: (layer, 0, j)),
                  pl.BlockSpec((None, k, tn), lambda j, i: (layer, 0, nj + j)),
                  pl.BlockSpec((None, 3, tn), lambda j, i: (layer, 0, j)),
                  pl.BlockSpec((None, 1, tn), lambda j, i: (layer, 0, j)),
                  pl.BlockSpec((None, slab, nd), lambda j, i: (layer, j * ni + i, 0))],
        out_specs=[pl.BlockSpec((tm, tn), lambda j, i: (i, j)),
                   pl.BlockSpec((None, 2, tn), lambda j, i: (i // tiles_per_seq, 0, j)),
                   pl.BlockSpec((m2, tn), lambda j, i: (0, j)),
                   pl.BlockSpec((2 * bsz2, tn), lambda j, i: (0, j)),
                   pl.BlockSpec((None, slab, nd), lambda j, i: (0, j * ni + i, 0))],
        out_shape=[jax.ShapeDtypeStruct((m, n), BF16),
                   jax.ShapeDtypeStruct((bsz, 2, n), F32),
                   jax.ShapeDtypeStruct((m2, n), BF16),
                   jax.ShapeDtypeStruct((2 * bsz2, n), F32),
                   jax.ShapeDtypeStruct((1, kd, nd), BF16)],
        scratch_shapes=[pltpu.VMEM((k, tn), BF16), pltpu.VMEM((k, tn), BF16),
                        pltpu.VMEM((V7X_SUBLANES, tn), F32)],
        compiler_params=_cparams("arbitrary", "arbitrary"),
    )(x, x2, prev2_tm, w_up, w_up, dw, dwb.reshape(dwb.shape[0], 1, n), w_down)


def _swa_prompt_kernel(sink_ref, q_ref, kc_ref, kp_ref, vc_ref, vp_ref, bias_ref, o_ref, *, n_heads, group, hd,
                       scale):
    n = pl.program_id(1)
    w = q_ref.shape[0]
    kcat = jnp.concatenate([kp_ref[...], kc_ref[...]], axis=0).astype(BF16)
    vcat = jnp.concatenate([vp_ref[...], vc_ref[...]], axis=0).astype(BF16)
    col = lax.broadcasted_iota(jnp.int32, (w, 2 * w), 1)
    keep = (col >= w) | (n > 0)
    scores = []
    for h in range(n_heads):
        kv = h // group
        q = q_ref[:, h * hd:(h + 1) * hd].astype(BF16)
        s = _dot_nt(q, kcat[:, kv * hd:(kv + 1) * hd]) * scale + bias_ref[h]
        scores.append(jnp.where(keep, s, NEG))
    probs, sums = [], []
    for h in range(n_heads):
        sink = sink_ref[h]
        m = jnp.maximum(jnp.max(scores[h], axis=-1, keepdims=True), sink)
        p = jnp.exp(scores[h] - m)
        sums.append(jnp.sum(p, axis=-1, keepdims=True) + jnp.exp(sink - m))
        probs.append(p.astype(BF16))
    for h in range(n_heads):
        kv = h // group
        o = _dot(probs[h], vcat[:, kv * hd:(kv + 1) * hd]) / sums[h]
        o_ref[:, h * hd:(h + 1) * hd] = o.astype(o_ref.dtype)


def _swa_prompt(qkv, sinks, bias, bsz, n_heads, n_kv, hd, window):
    m = qkv.shape[0]
    t = m // bsz
    nb = t // window
    qw, kw = n_heads * hd, n_kv * hd
    assert qw % kw == 0 and t % window == 0
    kcol, vcol = qw // kw, qw // kw + 1
    return pl.pallas_call(
        functools.partial(_swa_prompt_kernel, n_heads=n_heads, group=n_heads // n_kv, hd=hd, scale=hd ** -0.5),
        grid=(bsz, nb),
        in_specs=[pl.BlockSpec(memory_space=pltpu.SMEM),
                  pl.BlockSpec((window, qw), lambda b, n: (b * nb + n, 0)),
                  pl.BlockSpec((window, kw), lambda b, n: (b * nb + n, kcol)),
                  pl.BlockSpec((window, kw), lambda b, n: (b * nb + jnp.maximum(n - 1, 0), kcol)),
                  pl.BlockSpec((window, kw), lambda b, n: (b * nb + n, vcol)),
                  pl.BlockSpec((window, kw), lambda b, n: (b * nb + jnp.maximum(n - 1, 0), vcol)),
                  pl.BlockSpec((n_heads, window, 2 * window), lambda b, n: (0, 0, 0))],
        out_specs=pl.BlockSpec((window, qw), lambda b, n: (b * nb + n, 0)),
        out_shape=jax.ShapeDtypeStruct((m, qw), BF16),
        compiler_params=_cparams("arbitrary", "arbitrary"),
    )(sinks, qkv, qkv, qkv, qkv, qkv, bias)


def _swa_sample_kernel(q_ref, bk_ref, bv_ref, kn_ref, vn_ref, bias_ref, own_ref, sink_ref, o_ref,
                       *, n_kv, group, tp, scale):
    cols, hd = q_ref.shape
    keys = bk_ref.shape[0] // n_kv
    qb = q_ref[...].astype(BF16)
    col_kv = lax.broadcasted_iota(jnp.int32, (keys, cols), 1) // (group * tp)

    def by_head(ref):
        return jnp.concatenate([ref[pl.ds(kv, keys, stride=n_kv), :] for kv in range(n_kv)], axis=0)

    full = _dot_nt(by_head(bk_ref).astype(BF16), qb)
    s = full[0:keys]
    for kv in range(1, n_kv):
        s = jnp.where(col_kv == kv, full[kv * keys:(kv + 1) * keys], s)
    s = s * scale + bias_ref[...]
    s_own = _dot_nt(kn_ref[...].astype(BF16), qb) * scale + own_ref[...]
    sink = sink_ref[...]
    m = jnp.maximum(jnp.maximum(jnp.max(s, axis=0, keepdims=True), jnp.max(s_own, axis=0, keepdims=True)), sink)
    p = jnp.exp(s - m)
    p_own = jnp.exp(s_own - m)
    l = jnp.sum(p, axis=0, keepdims=True) + jnp.sum(p_own, axis=0, keepdims=True) + jnp.exp(sink - m)
    spread = jnp.concatenate([jnp.where(col_kv == kv, p, 0.0) for kv in range(n_kv)], axis=0)
    acc = (_dot(by_head(bv_ref).T.astype(BF16), spread.astype(BF16))
           + _dot(vn_ref[...].T.astype(BF16), p_own.astype(BF16)))
    o_ref[...] = (acc / l).T.astype(o_ref.dtype)


def _swa_sample(q_cols, buf_k, buf_v, k_new, v_new, bias, own, sink_row, *, n_kv, group, tp):
    dbs, cols, hd = q_cols.shape
    per_seq = lambda a: pl.BlockSpec((None,) + a.shape[1:], lambda b: (b, 0, 0))
    shared = lambda a: pl.BlockSpec(a.shape, lambda b: (0, 0))
    return pl.pallas_call(
        functools.partial(_swa_sample_kernel, n_kv=n_kv, group=group, tp=tp, scale=hd ** -0.5),
        grid=(dbs,),
        in_specs=[per_seq(q_cols), per_seq(buf_k), per_seq(buf_v), per_seq(k_new), per_seq(v_new),
                  shared(bias), shared(own), shared(sink_row)],
        out_specs=pl.BlockSpec((None, cols, hd), lambda b: (b, 0, 0)),
        out_shape=jax.ShapeDtypeStruct((dbs, cols, hd), BF16),
        compiler_params=_cparams("arbitrary"),
    )(q_cols, buf_k, buf_v, k_new, v_new, bias, own, sink_row)


def _block_means_kernel(k_ref, o_ref):
    o_ref[...] = jnp.mean(k_ref[...], axis=0, keepdims=True)


def _block_means(qkv, n_blocks_total, kcol, kw):
    return pl.pallas_call(
        _block_means_kernel,
        grid=(n_blocks_total,),
        in_specs=[pl.BlockSpec((MOBA_BLOCK, kw), lambda i: (i, kcol))],
        out_specs=pl.BlockSpec((None, 1, kw), lambda i: (i, 0, 0)),
        out_shape=jax.ShapeDtypeStruct((n_blocks_total, 1, kw), F32),
        compiler_params=_cparams("arbitrary"),
    )(qkv)


def _rank_rows(gate, blk, n_rows):
    rank = jnp.zeros(gate.shape, jnp.int32)
    for m in range(n_rows):
        gm = gate[m:m + 1, :]
        beats = (gm > gate) | ((gm == gate) & (blk > m))
        rank = rank + jnp.where(beats, 1, 0)
    return rank


def _moba_prompt_kernel(qb_ref, kb_ref, pt_ref, q_ref, k_ref, v_ref, means_ref, diag_ref, far_ref, *refs,
                        n_kv, group, hd, scale, n_near, nb, n_pool_pages, ppb):
    del pt_ref
    pool_pages = refs[:n_pool_pages]
    o_ref, pool_means_ref, qs, sel_s, m_s, l_s, acc_s = refs[n_pool_pages:]
    pool_means_ref[...] = _page_block_means(pool_pages, ppb, n_kv)
    qb = qb_ref[pl.program_id(1)]
    kb = kb_ref[pl.program_id(1)]
    blk_rows = q_ref.shape[0]
    cols = group * blk_rows

    @pl.when(kb == 0)
    def _():
        blk = lax.broadcasted_iota(jnp.int32, (nb, cols), 0)
        for kv in range(n_kv):
            qg = jnp.concatenate([q_ref[:, (kv * group + g) * hd:(kv * group + g + 1) * hd]
                                  for g in range(group)], axis=0)
            qs[kv] = (qg * (scale * LOG2E)).astype(BF16)
            gate = _dot_nt(means_ref[:, kv * hd:(kv + 1) * hd], qg, precision=lax.Precision.HIGHEST)
            valid = blk < qb
            gate = jnp.where(valid, gate, -jnp.inf)
            rank = _rank_rows(gate, blk, nb)
            sel_s[kv] = jnp.where(valid & (rank < MOBA_TOPK), 1.0, 0.0)
            m_s[kv] = jnp.full((1, cols), NEG, F32)
            l_s[kv] = jnp.zeros((1, cols), F32)
            acc_s[kv] = jnp.zeros((hd, cols), F32)

    def attend_all(adds_of):
        chunks = [(kv, g) for kv in range(n_kv) for g in range(group)]
        col = lambda g: slice(g * blk_rows, (g + 1) * blk_rows)
        scores = {}
        for kv in range(n_kv):
            k = k_ref[:, kv * hd:(kv + 1) * hd].astype(BF16)
            for g in range(group):
                s = _dot_nt(k, qs[kv, col(g), :])
                for a in adds_of(kv, g):
                    s = s + a
                scores[kv, g] = s
        alphas, probs = {}, {}
        for kv, g in chunks:
            m_prev = m_s[kv, :, col(g)]
            m_new = jnp.maximum(m_prev, jnp.max(scores[kv, g], axis=0, keepdims=True))
            alpha = jnp.exp2(m_prev - m_new)
            p = jnp.exp2(scores[kv, g] - m_new)
            l_s[kv, :, col(g)] = alpha * l_s[kv, :, col(g)] + jnp.sum(p, axis=0, keepdims=True)
            m_s[kv, :, col(g)] = m_new
            alphas[kv, g], probs[kv, g] = alpha, p.astype(BF16)
        for kv in range(n_kv):
            vt = v_ref[:, kv * hd:(kv + 1) * hd].T.astype(BF16)
            for g in range(group):
                acc_s[kv, :, col(g)] = alphas[kv, g] * acc_s[kv, :, col(g)] + _dot(vt, probs[kv, g])

    delta = qb - kb
    cs = lambda g: slice(g * blk_rows, (g + 1) * blk_rows)

    @pl.when(delta < n_near)
    def _():
        def adds_of(kv, g):
            diag = diag_ref[kv * group + g, pl.ds(delta, 1), :]
            tile = pltpu.roll(jnp.broadcast_to(diag, (blk_rows, 2 * blk_rows)), 0, 1, stride=1, stride_axis=0)
            chosen = (sel_s[kv, pl.ds(kb, 1), cs(g)] > 0.5) | (delta == 0)
            return tile[:, blk_rows:], jnp.where(chosen, 0.0, NEG)
        attend_all(adds_of)

    @pl.when(delta >= n_near)
    def _():
        def adds_of(kv, g):
            chosen = sel_s[kv, pl.ds(kb, 1), cs(g)] > 0.5
            return (jnp.where(chosen, far_ref[kv, :, cs(g)], NEG),)
        attend_all(adds_of)

    @pl.when(kb == qb)
    def _():
        for kv in range(n_kv):
            o = acc_s[kv] / l_s[kv]
            for g in range(group):
                h = kv * group + g
                o_ref[:, h * hd:(h + 1) * hd] = o[:, g * blk_rows:(g + 1) * blk_rows].T.astype(o_ref.dtype)


def _moba_prompt(qkv, means, diag, far_rows, pool, pt_flat, page_off, bsz, n_heads, n_kv, hd):
    m = qkv.shape[0]
    t = m // bsz
    nb = t // MOBA_BLOCK
    group = n_heads // n_kv
    qw, kw = n_heads * hd, n_kv * hd
    kcol, vcol = qw // kw, qw // kw + 1
    n_near = diag.shape[1]
    cols = group * MOBA_BLOCK
    pairs = [(qb, kb) for qb in range(nb) for kb in range(qb + 1)]
    qb_tab = jnp.asarray([p[0] for p in pairs], jnp.int32)
    kb_tab = jnp.asarray([p[1] for p in pairs], jnp.int32)
    n_pairs = len(pairs)

    page_rows = pool.shape[1]
    ppb = MOBA_BLOCK // (page_rows // n_kv)
    total_pages = pt_flat.shape[0]
    pps = _means_pages_per_step(total_pages, bsz * n_pairs, ppb, n_kv)
    mean_steps = total_pages // pps
    mean_rows = (pps // ppb) * n_kv
    slab = lambda b, s: jnp.minimum(b * n_pairs + s, mean_steps - 1)

    def page_spec(i):
        return pl.BlockSpec((None, page_rows, hd),
                            lambda b, s, qt, kt, pt: (pt[slab(b, s) * pps + i] + page_off, 0, 0))

    grid_spec = pltpu.PrefetchScalarGridSpec(
        num_scalar_prefetch=3,
        grid=(bsz, n_pairs),
        in_specs=[pl.BlockSpec((MOBA_BLOCK, qw), lambda b, s, qt, kt, pt: (b * nb + qt[s], 0)),
                  pl.BlockSpec((MOBA_BLOCK, kw), lambda b, s, qt, kt, pt: (b * nb + kt[s], kcol)),
                  pl.BlockSpec((MOBA_BLOCK, kw), lambda b, s, qt, kt, pt: (b * nb + kt[s], vcol)),
                  pl.BlockSpec((None, nb, kw), lambda b, s, qt, kt, pt: (b, 0, 0)),
                  pl.BlockSpec(diag.shape, lambda b, s, qt, kt, pt: (0, 0, 0)),
                  pl.BlockSpec((n_kv, 1, cols), lambda b, s, qt, kt, pt: (0, 0, 0))]
                 + [page_spec(i) for i in range(pps)],
        out_specs=[pl.BlockSpec((MOBA_BLOCK, qw), lambda b, s, qt, kt, pt: (b * nb + qt[s], 0)),
                   pl.BlockSpec((mean_rows, hd), lambda b, s, qt, kt, pt: (slab(b, s), 0))],
        scratch_shapes=[pltpu.VMEM((n_kv, cols, hd), BF16),
                        pltpu.VMEM((n_kv, nb, cols), F32),
                        pltpu.VMEM((n_kv, 1, cols), F32),
                        pltpu.VMEM((n_kv, 1, cols), F32),
                        pltpu.VMEM((n_kv, hd, cols), F32)],
    )
    return pl.pallas_call(
        functools.partial(_moba_prompt_kernel, n_kv=n_kv, group=group, hd=hd, scale=hd ** -0.5,
                          n_near=n_near, nb=nb, n_pool_pages=pps, ppb=ppb),
        grid_spec=grid_spec,
        out_shape=[jax.ShapeDtypeStruct((m, qw), BF16),
                   jax.ShapeDtypeStruct(((total_pages // ppb) * n_kv, hd), F32)],
        compiler_params=_cparams("arbitrary", "arbitrary"),
    )(qb_tab, kb_tab, pt_flat, qkv, qkv, qkv, means, diag, far_rows, *([pool] * pps))


_ATTN_PAGES_PER_STEP = 32


def _page_block_means(pages, ppb, n_kv):
    hd = pages[0].shape[1]
    sub = V7X_SUBLANES
    reps = sub // n_kv
    folded = []
    for blk in range(len(pages) // ppb):
        rows = pages[blk * ppb][...]
        for p in range(1, ppb):
            rows = rows + pages[blk * ppb + p][...]
        chains = sub if rows.shape[0] % (sub * sub) == 0 else 1
        tot = jnp.sum(jnp.sum(rows.reshape(chains, -1, sub, hd), axis=1), axis=0)
        full = tot
        for i in range(1, reps):
            full = full + pltpu.roll(tot, i * n_kv, 0)
        folded.append(full)
    row_grp = lax.broadcasted_iota(jnp.int32, (sub, hd), 0) // n_kv
    tiles = []
    for a in range(0, len(folded), reps):
        tile = folded[a]
        for i in range(1, reps):
            tile = jnp.where(row_grp == i, folded[a + i], tile)
        tiles.append(tile)
    n_keys = ppb * pages[0].shape[0] // n_kv
    return jnp.concatenate(tiles, axis=0) * (1.0 / n_keys)


def _means_pages_per_step(total_pages, steps, ppb, n_kv):
    unit = ppb * max(1, V7X_SUBLANES // n_kv)
    pps = unit
    while pps * steps < total_pages or total_pages % pps:
        pps += unit
    return pps


def _cols_expand_kernel(tab_ref, idx_ref, o_ref, *, buckets, mult):
    idx = idx_ref[...]
    acc = jnp.full(idx.shape, NEG, F32)
    for b in buckets:
        acc = jnp.where(idx == b, tab_ref[b:b + 1, :] * mult, acc)
    o_ref[...] = acc


def _cols_expand(tab_cols, idx_np, tile_rows, mult=1.0):
    rows, cols = idx_np.shape
    buckets = tuple(int(b) for b in np.unique(idx_np) if b >= 0)
    return pl.pallas_call(
        functools.partial(_cols_expand_kernel, buckets=buckets, mult=mult),
        grid=(rows // tile_rows,),
        in_specs=[pl.BlockSpec(tab_cols.shape, lambda i: (0, 0)),
                  pl.BlockSpec((tile_rows, cols), lambda i: (i, 0))],
        out_specs=pl.BlockSpec((tile_rows, cols), lambda i: (i, 0)),
        out_shape=jax.ShapeDtypeStruct((rows, cols), F32),
        compiler_params=_cparams("arbitrary"),
    )(tab_cols, jnp.asarray(idx_np))


def _moba_sample_kernel(pt_ref, *refs, pps, ppb, n_kv, group, tp, n_blk, n_pages, far_cls, scale):
    del pt_ref
    k_pages = refs[:pps]
    v_pages = refs[pps:2 * pps]
    q_ref, means_ref, bias_ref, own_ref, kn_ref, vn_ref, o_ref, sel_s, m_s, l_s, acc_s = refs[2 * pps:]
    step = pl.program_id(1)
    cols, hd = q_ref.shape

    @pl.when(step == 0)
    def _():
        q = q_ref[...]
        blk = lax.broadcasted_iota(jnp.int32, (n_blk, cols), 0)
        col_kv = lax.broadcasted_iota(jnp.int32, (n_blk, cols), 1) // (group * tp)
        gate = jnp.zeros((n_blk, cols), F32)
        for kv in range(n_kv):
            g = _dot_nt(means_ref[kv * n_blk:(kv + 1) * n_blk, :], q, precision=lax.Precision.HIGHEST)
            gate = jnp.where(col_kv == kv, g, gate)
        rank = _rank_rows(gate, blk, n_blk)
        sel_s[...] = jnp.where(rank < MOBA_TOPK, 0.0, NEG)
        m_s[...] = jnp.full(m_s.shape, NEG, F32)
        l_s[...] = jnp.zeros(l_s.shape, F32)
        acc_s[...] = jnp.zeros(acc_s.shape, F32)

    qb = (q_ref[...] * (scale * LOG2E)).astype(BF16)
    page_keys = k_pages[0].shape[0] // n_kv
    col_kv = lax.broadcasted_iota(jnp.int32, (page_keys, cols), 1) // (group * tp)

    def accumulate(scores, values, expand):
        m_prev = m_s[...]
        m_new = m_prev
        for s in scores:
            m_new = jnp.maximum(m_new, jnp.max(s, axis=0, keepdims=True))
        alpha = jnp.exp2(m_prev - m_new)
        l_new = alpha * l_s[...]
        acc = alpha * acc_s[...]
        for s, v in zip(scores, values):
            p = jnp.exp2(s - m_new)
            l_new = l_new + jnp.sum(p, axis=0, keepdims=True)
            acc = acc + _dot(v.T.astype(BF16), expand(p).astype(BF16))
        m_s[...] = m_new
        l_s[...] = l_new
        acc_s[...] = acc

    def by_head(ref):
        return jnp.concatenate([ref[pl.ds(kv, page_keys, stride=n_kv), :] for kv in range(n_kv)], axis=0)

    def spread(p):
        return jnp.concatenate([jnp.where(col_kv == kv, p, 0.0) for kv in range(n_kv)], axis=0)

    scores, values = [], []
    for i in range(pps):
        page = step * pps + i
        cls = jnp.minimum(n_pages - 1 - page, far_cls)
        full = _dot_nt(by_head(k_pages[i]).astype(BF16), qb)
        s = full[0:page_keys]
        for kv in range(1, n_kv):
            s = jnp.where(col_kv == kv, full[kv * page_keys:(kv + 1) * page_keys], s)
        scores.append(s + bias_ref[cls] + sel_s[pl.ds(page // ppb, 1), :])
        values.append(by_head(v_pages[i]))
    accumulate(scores, values, spread)

    @pl.when(step == pl.num_programs(1) - 1)
    def _():
        s_own = _dot_nt(kn_ref[...].astype(BF16), qb) + own_ref[...]
        accumulate([s_own], [vn_ref[...]], lambda p: p)
        o_ref[...] = (acc_s[...] / l_s[...]).T


def _moba_sample(q_cols, means_t, k_new, v_new, pool_k, pool_v, pt_flat, page_off, bias_pages, bias_own, *,
                 n_kv, group, tp, n_pages):
    dbs, cols, hd = q_cols.shape
    rows = pool_k.shape[1]
    ppb = MOBA_BLOCK // (rows // n_kv)
    n_blk = n_pages // ppb
    pps = min(_ATTN_PAGES_PER_STEP, n_pages)
    assert n_pages % pps == 0
    far_cls = bias_pages.shape[0] - 1
    n_own = k_new.shape[1]

    def page_spec(i):
        return pl.BlockSpec((None, rows, hd),
                            lambda b, s, pt: (pt[b * n_pages + s * pps + i] + page_off, 0, 0))

    per_seq = lambda r: pl.BlockSpec((None, r, hd), lambda b, s, pt: (b, 0, 0))
    in_specs = ([page_spec(i) for i in range(pps)] * 2
                + [per_seq(cols), per_seq(n_kv * n_blk),
                   pl.BlockSpec(bias_pages.shape, lambda b, s, pt: (0, 0, 0)),
                   pl.BlockSpec(bias_own.shape, lambda b, s, pt: (0, 0)),
                   per_seq(n_own), per_seq(n_own)])
    grid_spec = pltpu.PrefetchScalarGridSpec(
        num_scalar_prefetch=1,
        grid=(dbs, n_pages // pps),
        in_specs=in_specs,
        out_specs=pl.BlockSpec((None, cols, hd), lambda b, s, pt: (b, 0, 0)),
        scratch_shapes=[pltpu.VMEM((n_blk, cols), F32), pltpu.VMEM((1, cols), F32),
                        pltpu.VMEM((1, cols), F32), pltpu.VMEM((hd, cols), F32)],
    )
    return pl.pallas_call(
        functools.partial(_moba_sample_kernel, pps=pps, ppb=ppb, n_kv=n_kv, group=group, tp=tp, n_blk=n_blk,
                          n_pages=n_pages, far_cls=far_cls, scale=hd ** -0.5),
        grid_spec=grid_spec,
        out_shape=jax.ShapeDtypeStruct((dbs, cols, hd), F32),
        compiler_params=_cparams("arbitrary", "arbitrary"),
    )(pt_flat, *([pool_k] * pps), *([pool_v] * pps), q_cols, means_t, bias_pages, bias_own, k_new, v_new)


def _swa_prompt_bias_idx(window, nbk):
    qi = np.arange(window)[:, None]
    kj = np.arange(2 * window)[None, :]
    dist = qi + window - kj
    return np.where((dist >= 0) & (dist < window), _bucket_np(dist, nbk), -1).astype(np.int32)


def _swa_sample_bias_idx(window, n_heads, tp, t_new, nbk):
    t = np.arange(n_heads * tp) % tp
    r = np.arange(window)
    dist = t[None, :] + window - r[:, None]
    idx = np.where((dist >= 0) & (dist < window), _bucket_np(dist, nbk), -1)
    return np.where(t[None, :] < t_new, idx, 0).astype(np.int32)


def _moba_prompt_diag_idx(n_near, nbk):
    dist = np.arange(n_near)[:, None] * MOBA_BLOCK + np.arange(2 * MOBA_BLOCK)[None, :] - MOBA_BLOCK
    return np.where(dist >= 0, _bucket_np(dist, nbk), -1).astype(np.int32)


def _moba_sample_page_idx(far_cls, page, n_heads, tp, t_new, nbk):
    cols = n_heads * tp
    t = np.arange(cols) % tp
    r = np.arange(page)
    tiles = []
    for c in range(far_cls + 1):
        dist = (c + 1) * page + t[None, :] - r[:, None]
        b = _bucket_np(dist, nbk) if c < far_cls else np.full(dist.shape, nbk - 1, np.int32)
        tiles.append(np.where(t[None, :] < t_new, b, 0))
    return np.concatenate(tiles, axis=0).astype(np.int32)


def _moba_sample_own_idx(n_kv, n_heads, group, tp, t_new, nbk):
    cols = n_heads * tp
    h, t = np.arange(cols) // tp, np.arange(cols) % tp
    tk, kvc = np.arange(tp * n_kv) // n_kv, np.arange(tp * n_kv) % n_kv
    ok = ((kvc[:, None] == (h // group)[None, :]) & (tk[:, None] <= t[None, :]) & (t[None, :] < t_new))
    return np.where(ok, _bucket_np(t[None, :] - tk[:, None], nbk), -1).astype(np.int32)


def kernel(x_prompt, x_sample, c_prompt, c_sample, state_conv, cache_swa_k, cache_swa_v, cache_moba_k, cache_moba_v, page_table, state_ffn, ada_w, ada_b, norm_mix, norm_ffn, norm_final, rel_bias, conv_w1, conv_b1, conv_dw, conv_dw_b, conv_ln_g, conv_ln_b, conv_w2, conv_b2, swa_wqkv, swa_wo, swa_sinks, moba_wqkv, moba_wo, ffn_w_up, ffn_dw, ffn_dw_b, ffn_w_down):
    bsz, seq, d = x_prompt.shape
    dbs, t_new, _ = x_sample.shape
    depth = ada_w.shape[0]
    window, n_kv, hd = cache_swa_k.shape[2], cache_swa_k.shape[3], cache_swa_k.shape[4]
    nbk, n_heads = rel_bias.shape
    group = n_heads // n_kv
    qw, kw = n_heads * hd, n_kv * hd
    cw = conv_dw.shape[1]
    d_ff = ffn_dw.shape[2]
    n_pool, page_size = cache_moba_k.shape[1], cache_moba_k.shape[2]
    n_pages = page_table.shape[1]
    ppb = MOBA_BLOCK // page_size
    n_blk_past = n_pages // ppb
    tp = V7X_SUBLANES
    assert t_new <= tp and n_pages % ppb == 0 and n_blk_past >= MOBA_TOPK and seq % MOBA_BLOCK == 0
    assert ada_w.shape[2] == N_MOD * d
    n_mixers = 3

    n_c = bsz + dbs
    c_rows = -(-n_c // V7X_SUBLANES) * V7X_SUBLANES
    c_all = jnp.concatenate([c_prompt, c_sample, jnp.zeros((c_rows - n_c, d), F32)], axis=0)
    mod = _ada(c_all, ada_w, ada_b)

    def mods(layer, which):
        chunk = mod[layer, :, which * d:(which + 1) * d]
        return chunk[:bsz], chunk[bsz:bsz + dbs]

    d_sat = _saturation_distance(nbk)
    n_near = -(-(d_sat + MOBA_BLOCK - 1) // MOBA_BLOCK)
    swa_bias_p = swa_bias_s = swa_bias_own = moba_bias_p = moba_far = moba_bias_sel = moba_bias_own = None
    tab_cols = jnp.repeat(rel_bias, tp, axis=1)
    own_idx = _moba_sample_own_idx(n_kv, n_heads, group, tp, t_new, nbk)
    if depth > 1:
        swa_bias_p = _bias_expand(rel_bias, _swa_prompt_bias_idx(window, nbk), window,
                                  (n_heads, window, 2 * window), (None, window, 2 * window),
                                  lambda h: (h, 0, 0))
        swa_bias_s = _cols_expand(tab_cols, _swa_sample_bias_idx(window, n_heads, tp, t_new, nbk), window)
        swa_bias_own = _cols_expand(tab_cols, own_idx, tp * n_kv)
    if depth > 2:
        cols = group * MOBA_BLOCK
        moba_bias_p = _bias_expand(
            rel_bias, _moba_prompt_diag_idx(n_near, nbk), n_near,
            (n_heads, n_near, 2 * MOBA_BLOCK), (None, n_near, 2 * MOBA_BLOCK),
            lambda h: (h, 0, 0), mult=LOG2E)
        moba_far = jnp.repeat(rel_bias[nbk - 1].reshape(n_kv, group, 1), MOBA_BLOCK, axis=2).reshape(n_kv, 1, cols)
        moba_far = moba_far * LOG2E
        far_cls = -(-(d_sat + page_size - 1) // page_size) - 1
        moba_bias_sel = _cols_expand(
            tab_cols, _moba_sample_page_idx(far_cls, page_size, n_heads, tp, t_new, nbk),
            page_size, mult=LOG2E).reshape(far_cls + 1, page_size, n_heads * tp)
        moba_bias_own = _cols_expand(tab_cols, own_idx, tp * n_kv, mult=LOG2E)

    def to_tm(a):
        return jnp.swapaxes(a, 0, 1).reshape((a.shape[1] * dbs,) + a.shape[2:])

    def to_bm(a, t):
        return jnp.swapaxes(a.reshape((t, dbs) + a.shape[1:]), 0, 1)

    xp = x_prompt.reshape(bsz * seq, d)
    xs = to_tm(x_sample)
    m_s = t_new * dbs

    conv_p, conv_s, swa_k_p, swa_v_p, swa_k_s, swa_v_s = [], [], [], [], [], []
    moba_k_p, moba_v_p, moba_k_s, moba_v_s, ffn_p, ffn_s = [], [], [], [], [], []

    def gates(layer, which):
        gp, gs = mods(layer, which)
        return gp.reshape(bsz, 1, d), jnp.tile(gs, (t_new, 1)).reshape(1, m_s, d)

    def normed(x_p, x_s, g, layer, which_sh):
        shp, shs = mods(layer, which_sh)
        scp, scs = mods(layer, which_sh + 1)
        hp = _norm_mod(x_p.reshape(bsz, seq, d), g, scp.reshape(bsz, 1, d), shp.reshape(bsz, 1, d), 1024)
        hs = _norm_mod(x_s.reshape(t_new, dbs, d), g, scs.reshape(1, dbs, d), shs.reshape(1, dbs, d), dbs)
        return hp.reshape(bsz * seq, d), hs.reshape(m_s, d)

    def sample_views(qkv_bm):
        pad = jnp.concatenate([qkv_bm, jnp.zeros((dbs, tp - t_new, qkv_bm.shape[2]), qkv_bm.dtype)], axis=1)
        q_cols = pad[:, :, :qw].reshape(dbs, tp, n_heads, hd).transpose(0, 2, 1, 3).reshape(dbs, n_heads * tp, hd)
        return (q_cols, pad[:, :, qw:qw + kw].reshape(dbs, tp * n_kv, hd),
                pad[:, :, qw + kw:].reshape(dbs, tp * n_kv, hd))

    def cols_to_bm(o_cols):
        o = o_cols.reshape(dbs, n_heads, tp, hd)[:, :, :t_new].transpose(0, 2, 1, 3)
        return o.reshape(dbs, t_new, qw)

    for layer in range(depth):
        kind, j = layer % n_mixers, layer // n_mixers
        hp, hs = normed(xp, xs, norm_mix[layer], layer, 0)
        g1p, g1s = gates(layer, 2)
        if kind == 0:
            up, us = _glu(hp, hs, conv_w1, conv_b1, j)
            yp = _conv_ln_prompt(up.reshape(bsz, seq, d), conv_dw[j], conv_dw_b[j], conv_ln_g[j], conv_ln_b[j])
            st_tm = jnp.swapaxes(state_conv[j], 0, 1)
            ys, nst_tm = _conv_ln_sample(st_tm, us.reshape(t_new, dbs, d), conv_dw[j], conv_dw_b[j],
                                         conv_ln_g[j], conv_ln_b[j])
            conv_p.append(up.reshape(bsz, seq, d)[:, seq - (cw - 1):])
            conv_s.append(jnp.swapaxes(nst_tm, 0, 1))
            xp, xs = _mm(yp.reshape(bsz * seq, d), conv_w2, j, bias=conv_b2, res=xp, gate=g1p,
                         side=(ys.reshape(m_s, d), xs, g1s[0]))
        elif kind == 1:
            qkv_p, qkv_s = _mm(hp, swa_wqkv, j, side=(hs, None, None))
            op = _swa_prompt(qkv_p, swa_sinks[j], swa_bias_p, bsz, n_heads, n_kv, hd, window)
            q_cols, k_new, v_new = sample_views(to_bm(qkv_s, t_new))
            buf_k = cache_swa_k[j].reshape(dbs, window * n_kv, hd)
            buf_v = cache_swa_v[j].reshape(dbs, window * n_kv, hd)
            sink_row = jnp.repeat(swa_sinks[j], tp).reshape(1, n_heads * tp)
            o_cols = _swa_sample(q_cols, buf_k, buf_v, k_new, v_new, swa_bias_s, swa_bias_own, sink_row,
                                 n_kv=n_kv, group=group, tp=tp)
            os_ = to_tm(cols_to_bm(o_cols))
            tail = qkv_p.reshape(bsz, seq, qw + 2 * kw)[:, seq - window:]
            swa_k_p.append(tail[:, :, qw:qw + kw].reshape(bsz, window, n_kv, hd))
            swa_v_p.append(tail[:, :, qw + kw:].reshape(bsz, window, n_kv, hd))
            keep = t_new * n_kv
            swa_k_s.append(jnp.concatenate([buf_k[:, keep:], k_new[:, :keep]], axis=1)
                           .reshape(dbs, window, n_kv, hd))
            swa_v_s.append(jnp.concatenate([buf_v[:, keep:], v_new[:, :keep]], axis=1)
                           .reshape(dbs, window, n_kv, hd))
            xp, xs = _mm(op, swa_wo, j, res=xp, gate=g1p, side=(os_, xs, g1s[0]))
        else:
            qkv_p, k_rows, v_rows, qkv_s = _mm(hp, moba_wqkv, j, kv_heads=(n_kv, hd), side=(hs, None, None))
            nb = seq // MOBA_BLOCK
            means_p = _block_means(qkv_p, bsz * nb, qw // kw, kw).reshape(bsz, nb, kw)
            pool_k = cache_moba_k.reshape(cache_moba_k.shape[0] * n_pool, page_size * n_kv, hd)
            pool_v = cache_moba_v.reshape(cache_moba_v.shape[0] * n_pool, page_size * n_kv, hd)
            pt_flat = page_table.reshape(-1)
            op, means_s = _moba_prompt(qkv_p, means_p, moba_bias_p, moba_far, pool_k, pt_flat, j * n_pool,
                                       bsz, n_heads, n_kv, hd)
            moba_k_p.append(k_rows.reshape(bsz, seq // page_size, page_size, n_kv, hd))
            moba_v_p.append(v_rows.reshape(bsz, seq // page_size, page_size, n_kv, hd))

            qkv_bm = to_bm(qkv_s, t_new)
            q_cols, k_new, v_new = sample_views(qkv_bm)
            means_t = means_s.reshape(dbs, n_blk_past, n_kv, hd).transpose(0, 2, 1, 3)
            means_t = means_t.reshape(dbs, n_kv * n_blk_past, hd)
            o_cols = _moba_sample(q_cols, means_t, k_new, v_new, pool_k, pool_v, pt_flat, j * n_pool,
                                  moba_bias_sel, moba_bias_own, n_kv=n_kv, group=group, tp=tp, n_pages=n_pages)
            os_bm = cols_to_bm(o_cols).astype(BF16)
            moba_k_s.append(qkv_bm[:, :, qw:qw + kw].reshape(dbs, t_new, n_kv, hd))
            moba_v_s.append(qkv_bm[:, :, qw + kw:].reshape(dbs, t_new, n_kv, hd))
            xp, xs = _mm(op, moba_wo, j, res=xp, gate=g1p, side=(to_tm(os_bm), xs, g1s[0]))

        hp, hs = normed(xp, xs, norm_ffn[layer], layer, 3)
        g2p, g2s = gates(layer, 5)
        prev_tm = jnp.swapaxes(state_ffn[layer], 0, 1).reshape(2 * dbs, d_ff)
        ap, stp, as_, sts, w_down_bf16 = _ffn_up(hp, hs, prev_tm, ffn_w_up, ffn_dw, ffn_dw_b, ffn_w_down, layer,
                                                 bsz, dbs)
        ffn_p.append(stp)
        ffn_s.append(jnp.swapaxes(sts.reshape(2, dbs, d_ff), 0, 1))
        xp, xs = _mm(ap, w_down_bf16, 0, res=xp, gate=g2p, tn=512, weights_outer=False,
                     side=(as_, xs, g2s[0]))

    y_p = _norm_mod(xp.reshape(bsz, seq, d), norm_final, None, None, 1024, out_dtype=F32)
    y_s = _norm_mod(xs.reshape(t_new, dbs, d), norm_final, None, None, dbs, out_dtype=F32)
    y_s = jnp.swapaxes(y_s, 0, 1)

    return (y_p, y_s, jnp.stack(conv_p), jnp.stack(conv_s), jnp.stack(swa_k_p), jnp.stack(swa_v_p),
            jnp.stack(swa_k_s), jnp.stack(swa_v_s), jnp.stack(moba_k_p), jnp.stack(moba_v_p),
            jnp.stack(moba_k_s), jnp.stack(moba_v_s), jnp.stack(ffn_p), jnp.stack(ffn_s))
```

```python
import functools
import math

import numpy as np
import jax
import jax.numpy as jnp
from jax import lax
from jax.experimental import pallas as pl
from jax.experimental.pallas import tpu as pltpu

MOBA_BLOCK = 256
MOBA_TOPK = 3
REL_MAX_DISTANCE = 1024
RMS_EPS = 1e-6
LN_EPS = 1e-5
N_MOD = 6

V7X_LANES = 128
V7X_SUBLANES = 8
V7X_VMEM_LIMIT_BYTES = 56 * 1024 * 1024

NEG = -1e30
LOG2E = 1.4426950408889634
BF16 = jnp.bfloat16
F32 = jnp.float32


def _cparams(*sem):
    return pltpu.CompilerParams(dimension_semantics=sem, vmem_limit_bytes=V7X_VMEM_LIMIT_BYTES)


def _dot(a, b):
    return jnp.dot(a, b, preferred_element_type=F32)


def _dot_nt(a, b, precision=None):
    return lax.dot_general(a, b, (((1,), (1,)), ((), ())), precision=precision,
                           preferred_element_type=F32)


def _silu(x):
    return x * jax.nn.sigmoid(x)


def _bucket_np(dist, num_buckets):
    n = np.maximum(dist, 0)
    max_exact = num_buckets // 2
    nf = np.maximum(n, 1).astype(np.float32)
    ratio = np.log(nf / np.float32(max_exact)) / np.float32(math.log(REL_MAX_DISTANCE / max_exact))
    large = max_exact + (ratio * np.float32(num_buckets - max_exact)).astype(np.int32)
    return np.where(n < max_exact, n, np.minimum(large, num_buckets - 1)).astype(np.int32)


def _saturation_distance(num_buckets):
    d = np.arange(0, 4 * REL_MAX_DISTANCE)
    b = _bucket_np(d, num_buckets)
    below = np.nonzero(b < num_buckets - 1)[0]
    return int(below.max()) + 1


def _bias_expand_kernel(tab_ref, idx_ref, o_ref, *, tiles, mult):
    h = pl.program_id(0)
    for n, (r0, nr, buckets) in enumerate(tiles):
        idx = idx_ref[r0:r0 + nr, :]
        acc = jnp.full(idx.shape, NEG, F32)
        for b in buckets:
            acc = jnp.where(idx == b, tab_ref[b, h] * mult, acc)
        if len(o_ref.shape) == 3:
            o_ref[n] = acc
        else:
            o_ref[r0:r0 + nr, :] = acc


def _bias_expand(table, idx_np, tile_rows, out_shape, out_block, out_index, mult=1.0):
    n_heads = table.shape[1]
    rows, cols = idx_np.shape
    tiles = []
    for r0 in range(0, rows, tile_rows):
        present = np.unique(idx_np[r0:r0 + tile_rows])
        tiles.append((r0, min(tile_rows, rows - r0), tuple(int(b) for b in present if b >= 0)))
    return pl.pallas_call(
        functools.partial(_bias_expand_kernel, tiles=tuple(tiles), mult=mult),
        grid=(n_heads,),
        in_specs=[pl.BlockSpec(memory_space=pltpu.SMEM),
                  pl.BlockSpec((rows, cols), lambda h: (0, 0))],
        out_specs=pl.BlockSpec(out_block, out_index),
        out_shape=jax.ShapeDtypeStruct(out_shape, F32),
        compiler_params=_cparams("arbitrary"),
    )(table, jnp.asarray(idx_np))


def _ada_kernel(c_ref, w_ref, b_ref, o_ref):
    a = _silu(c_ref[...]).astype(BF16)
    o_ref[...] = _dot(a, w_ref[...].astype(BF16)) + b_ref[...]


def _ada(c_all, ada_w, ada_b, n_layers):
    depth, d, n = ada_w.shape
    rows = c_all.shape[0]
    tn = min(n, 2048)
    return pl.pallas_call(
        _ada_kernel,
        grid=(n_layers, n // tn),
        in_specs=[pl.BlockSpec((rows, d), lambda l, j: (0, 0)),
                  pl.BlockSpec((None, d, tn), lambda l, j: (l, 0, j)),
                  pl.BlockSpec((None, 1, tn), lambda l, j: (l, 0, j))],
        out_specs=pl.BlockSpec((None, rows, tn), lambda l, j: (l, 0, j)),
        out_shape=jax.ShapeDtypeStruct((n_layers, rows, n), F32),
        compiler_params=_cparams("arbitrary", "arbitrary"),
    )(c_all, ada_w, ada_b.reshape(depth, 1, n))


def _norm_mod_kernel(x_ref, g_ref, sc_ref, sh_ref, o_ref):
    x = x_ref[...]
    ms = jnp.mean(x * x, axis=-1, keepdims=True)
    y = x * lax.rsqrt(ms + RMS_EPS) * g_ref[...]
    o_ref[...] = (y * (1.0 + sc_ref[...]) + sh_ref[...]).astype(o_ref.dtype)


def _rms_kernel(x_ref, g_ref, o_ref):
    x = x_ref[...]
    ms = jnp.mean(x * x, axis=-1, keepdims=True)
    o_ref[...] = (x * lax.rsqrt(ms + RMS_EPS) * g_ref[...]).astype(o_ref.dtype)


def _norm_mod(x3, g, sc3, sh3, rows_per_step, out_dtype=BF16):
    a, r, d = x3.shape
    tr = min(rows_per_step, r)
    g3 = g.reshape(1, 1, d)
    if sc3 is None:
        return pl.pallas_call(
            _rms_kernel,
            grid=(a, r // tr),
            in_specs=[pl.BlockSpec((1, tr, d), lambda i, j: (i, j, 0)),
                      pl.BlockSpec((1, 1, d), lambda i, j: (0, 0, 0))],
            out_specs=pl.BlockSpec((1, tr, d), lambda i, j: (i, j, 0)),
            out_shape=jax.ShapeDtypeStruct(x3.shape, out_dtype),
            compiler_params=_cparams("arbitrary", "arbitrary"),
        )(x3, g3)
    per_seq = sc3.shape[0] == a and sc3.shape[1] == 1
    if per_seq:
        mspec = pl.BlockSpec((1, 1, d), lambda i, j: (i, 0, 0))
    else:
        assert sc3.shape[0] == 1 and sc3.shape[1] == r and tr == r
        mspec = pl.BlockSpec((1, r, d), lambda i, j: (0, 0, 0))
    return pl.pallas_call(
        _norm_mod_kernel,
        grid=(a, r // tr),
        in_specs=[pl.BlockSpec((1, tr, d), lambda i, j: (i, j, 0)),
                  pl.BlockSpec((1, 1, d), lambda i, j: (0, 0, 0)),
                  mspec, mspec],
        out_specs=pl.BlockSpec((1, tr, d), lambda i, j: (i, j, 0)),
        out_shape=jax.ShapeDtypeStruct(x3.shape, out_dtype),
        compiler_params=_cparams("arbitrary", "arbitrary"),
    )(x3, g3, sc3, sh3)


def _mm_kernel(*refs, names, cache_w, kv_heads):
    r = dict(zip(names, refs))
    if cache_w:
        @pl.when(pl.program_id(1) == 0)
        def _():
            r["w_s"][...] = r["w"][...].astype(BF16)
        w = r["w_s"][...]
    else:
        w = r["w"][...].astype(BF16)

    def product(x_name, res_name, gate_name):
        acc = _dot(r[x_name][...], w)
        if "bias" in r:
            acc = acc + r["bias"][...]
        if res_name in r:
            acc = r[res_name][...] + r[gate_name][...] * acc
        return acc

    acc = product("x", "res", "gate")
    r["o"][...] = acc.astype(r["o"].dtype)
    if kv_heads:
        n_kv, hd = kv_heads
        rows = acc.shape[0]

        @pl.when(pl.program_id(0) == pl.num_programs(0) - 1)
        def _():
            for kv in range(n_kv):
                r["k"][pl.ds(kv, rows, stride=n_kv), :] = acc[:, kv * hd:(kv + 1) * hd]
                r["v"][pl.ds(kv, rows, stride=n_kv), :] = acc[:, (n_kv + kv) * hd:(n_kv + kv + 1) * hd]
    if "x2" in r:
        row_tile = pl.program_id(1) if cache_w else pl.program_id(0)
        last = (pl.num_programs(1) - 1) if cache_w else 0

        @pl.when(row_tile == last)
        def _():
            r["o2"][...] = product("x2", "res2", "gate2").astype(r["o2"].dtype)


def _mm(x, w, layer, *, bias=None, res=None, gate=None, out_dtype=F32, tm=1024, tn=1024, weights_outer=True,
        kv_heads=None, side=None):
    m, k = x.shape
    n = w.shape[2]
    tm, tn = min(tm, m), min(tn, n)
    assert m % tm == 0 and n % tn == 0
    if weights_outer:
        grid = (n // tn, m // tm)
        ij = lambda a, b: (b, a)
    else:
        grid = (m // tm, n // tn)
        ij = lambda a, b: (a, b)

    def spec(block, fn):
        return pl.BlockSpec(block, lambda a, b: fn(*ij(a, b)))

    names = ["x", "w"]
    in_specs = [spec((tm, k), lambda i, j: (i, 0)),
                spec((None, k, tn), lambda i, j: (layer, 0, j))]
    args = [x, w]
    if bias is not None:
        names.append("bias")
        in_specs.append(spec((None, 1, tn), lambda i, j: (layer, 0, j)))
        args.append(bias.reshape(bias.shape[0], 1, n))
    if res is not None:
        gs, gr, _ = gate.shape
        assert gr in (1, tm) and (m // tm) % gs == 0
        tiles_per_gate = (m // tm) // gs
        names += ["res", "gate"]
        in_specs.append(spec((tm, tn), lambda i, j: (i, j)))
        in_specs.append(spec((None, gr, tn), lambda i, j: (i // tiles_per_gate, 0, j)))
        args += [res, gate]
    if side is not None:
        x2, res2, gate2 = side
        m2 = x2.shape[0]
        nj_side = n // tn
        if weights_outer:
            side_col = lambda i, j: j
        else:
            side_col = lambda i, j: jnp.where(i == 0, j, nj_side - 1)
        names.append("x2")
        in_specs.append(spec((m2, k), lambda i, j: (0, 0)))
        args.append(x2)
        if res2 is not None:
            names += ["res2", "gate2"]
            in_specs += [spec((m2, tn), lambda i, j: (0, side_col(i, j)))] * 2
            args += [res2, gate2]
    names.append("o")
    out_specs = [spec((tm, tn), lambda i, j: (i, j))]
    out_shape = [jax.ShapeDtypeStruct((m, n), out_dtype)]
    if kv_heads:
        n_kv, hd = kv_heads
        nj = n // tn
        assert weights_outer and tn == 2 * n_kv * hd
        kv_spec = spec((tm * n_kv, hd), lambda i, j: (jnp.where(j == nj - 1, i, 0), 0))
        names += ["k", "v"]
        out_specs += [kv_spec, kv_spec]
        out_shape += [jax.ShapeDtypeStruct((m * n_kv, hd), F32)] * 2
    if side is not None:
        names.append("o2")
        out_specs.append(spec((m2, tn), lambda i, j: (0, side_col(i, j))))
        out_shape.append(jax.ShapeDtypeStruct((m2, n), out_dtype))
    if weights_outer:
        names.append("w_s")
    outs = pl.pallas_call(
        functools.partial(_mm_kernel, names=tuple(names), cache_w=weights_outer, kv_heads=kv_heads),
        grid=grid,
        in_specs=in_specs,
        out_specs=out_specs,
        out_shape=out_shape,
        scratch_shapes=[pltpu.VMEM((k, tn), BF16)] if weights_outer else [],
        compiler_params=_cparams("arbitrary", "arbitrary"),
    )(*args)
    return outs[0] if len(outs) == 1 else tuple(outs)


def _glu_kernel(x_ref, x2_ref, wa_ref, wg_ref, ba_ref, bg_ref, o_ref, o2_ref, wa_s, wg_s):
    @pl.when(pl.program_id(1) == 0)
    def _():
        wa_s[...] = wa_ref[...].astype(BF16)
        wg_s[...] = wg_ref[...].astype(BF16)

    def glu(x):
        a = _dot(x, wa_s[...]) + ba_ref[...]
        g = _dot(x, wg_s[...]) + bg_ref[...]
        return a * jax.nn.sigmoid(g)

    o_ref[...] = glu(x_ref[...])

    @pl.when(pl.program_id(1) == pl.num_programs(1) - 1)
    def _():
        o2_ref[...] = glu(x2_ref[...])


def _glu(x, x2, w1, b1, layer, *, tm=1024, tn=512):
    m, k = x.shape
    m2 = x2.shape[0]
    n = w1.shape[2] // 2
    tm, tn = min(tm, m), min(tn, n)
    nj = n // tn
    b3 = b1.reshape(b1.shape[0], 1, 2 * n)
    return pl.pallas_call(
        _glu_kernel,
        grid=(nj, m // tm),
        in_specs=[pl.BlockSpec((tm, k), lambda j, i: (i, 0)),
                  pl.BlockSpec((m2, k), lambda j, i: (0, 0)),
                  pl.BlockSpec((None, k, tn), lambda j, i: (layer, 0, j)),
                  pl.BlockSpec((None, k, tn), lambda j, i: (layer, 0, nj + j)),
                  pl.BlockSpec((None, 1, tn), lambda j, i: (layer, 0, j)),
                  pl.BlockSpec((None, 1, tn), lambda j, i: (layer, 0, nj + j))],
        out_specs=[pl.BlockSpec((tm, tn), lambda j, i: (i, j)),
                   pl.BlockSpec((m2, tn), lambda j, i: (0, j))],
        out_shape=[jax.ShapeDtypeStruct((m, n), F32), jax.ShapeDtypeStruct((m2, n), F32)],
        scratch_shapes=[pltpu.VMEM((k, tn), BF16), pltpu.VMEM((k, tn), BF16)],
        compiler_params=_cparams("arbitrary", "arbitrary"),
    )(x, x2, w1, w1, b3, b3)


def _layernorm_silu(y, lg, lb):
    mu = jnp.mean(y, axis=-1, keepdims=True)
    yc = y - mu
    var = jnp.mean(yc * yc, axis=-1, keepdims=True)
    return _silu(yc * lax.rsqrt(var + LN_EPS) * lg + lb)


def _conv_ln_prompt_kernel(u_ref, halo_ref, dw_ref, dwb_ref, lg_ref, lb_ref, o_ref, full_s, y_s,
                           *, tm, halo, cw, rc, lw):
    i = pl.program_id(1)
    d = u_ref.shape[-1]
    full_s[halo:halo + tm, :] = u_ref[...]

    @pl.when(i == 0)
    def _():
        full_s[0:halo, :] = jnp.zeros((halo, d), F32)

    @pl.when(i > 0)
    def _():
        full_s[0:halo, :] = halo_ref[...]

    off = halo - (cw - 1)
    win = rc + halo
    sub = V7X_SUBLANES

    def body(r, carry):
        r0 = pl.multiple_of(r * rc, rc)
        for c0 in range(0, d, lw):
            window = full_s[pl.ds(r0, win), c0:c0 + lw]
            acc = jnp.zeros((rc, lw), F32) + dwb_ref[:, c0:c0 + lw]
            for s in range(sub):
                taps = [k for k in range(cw) if (off + k) % sub == s]
                if not taps:
                    continue
                shifted = window if s == 0 else pltpu.roll(window, win - s, 0)
                for k in taps:
                    a = off + k - s
                    acc = acc + dw_ref[k:k + 1, c0:c0 + lw] * shifted[a:a + rc]
            y_s[pl.ds(r0, rc), c0:c0 + lw] = acc
        return carry
    lax.fori_loop(0, tm // rc, body, 0)

    o_ref[...] = _layernorm_silu(y_s[...], lg_ref[...], lb_ref[...]).astype(o_ref.dtype)


def _conv_ln_prompt(u3, dw, dwb, lg, lb, *, tm=512):
    bsz, t, d = u3.shape
    cw = dw.shape[0]
    halo = 32
    assert cw - 1 <= halo and t % tm == 0 and tm % halo == 0
    hb = tm // halo
    lw = min(d, V7X_LANES)
    rc = min(tm, 4 * halo)
    return pl.pallas_call(
        functools.partial(_conv_ln_prompt_kernel, tm=tm, halo=halo, cw=cw, rc=rc, lw=lw),
        grid=(bsz, t // tm),
        in_specs=[pl.BlockSpec((None, tm, d), lambda b, i: (b, i, 0)),
                  pl.BlockSpec((None, halo, d), lambda b, i: (b, jnp.maximum(i * hb - 1, 0), 0)),
                  pl.BlockSpec((cw, d), lambda b, i: (0, 0)),
                  pl.BlockSpec((1, d), lambda b, i: (0, 0)),
                  pl.BlockSpec((1, d), lambda b, i: (0, 0)),
                  pl.BlockSpec((1, d), lambda b, i: (0, 0))],
        out_specs=pl.BlockSpec((None, tm, d), lambda b, i: (b, i, 0)),
        out_shape=jax.ShapeDtypeStruct((bsz, t, d), BF16),
        scratch_shapes=[pltpu.VMEM((tm + halo, d), F32), pltpu.VMEM((tm, d), F32)],
        compiler_params=_cparams("arbitrary", "arbitrary"),
    )(u3, u3, dw, dwb.reshape(1, d), lg.reshape(1, d), lb.reshape(1, d))


def _conv_ln_sample_kernel(st_ref, u_ref, dw_ref, dwb_ref, lg_ref, lb_ref, o_ref, nst_ref, *, cw, lw):
    t_new, _, d = u_ref.shape
    n_st = cw - 1

    def row(idx, c0):
        if idx < n_st:
            return st_ref[idx, :, c0:c0 + lw]
        return u_ref[idx - n_st, :, c0:c0 + lw]

    for t in range(t_new):
        parts = []
        for c0 in range(0, d, lw):
            acc = dw_ref[0:1, c0:c0 + lw] * row(t, c0)
            for k in range(1, cw):
                acc = acc + dw_ref[k:k + 1, c0:c0 + lw] * row(t + k, c0)
            parts.append(acc + dwb_ref[:, c0:c0 + lw])
        y = jnp.concatenate(parts, axis=-1) if len(parts) > 1 else parts[0]
        o_ref[t] = _layernorm_silu(y, lg_ref[...], lb_ref[...]).astype(o_ref.dtype)
    for r in range(n_st):
        idx = t_new + r
        nst_ref[r] = st_ref[idx] if idx < n_st else u_ref[idx - n_st]


def _conv_ln_sample(st_tm, u_tm, dw, dwb, lg, lb):
    n_st, bsz, d = st_tm.shape
    t_new = u_tm.shape[0]
    cw = dw.shape[0]
    lw = min(d, 512)
    full = lambda shape: pl.BlockSpec(shape, lambda i: (0,) * len(shape))
    return pl.pallas_call(
        functools.partial(_conv_ln_sample_kernel, cw=cw, lw=lw),
        grid=(1,),
        in_specs=[full((n_st, bsz, d)), full((t_new, bsz, d)), full((cw, d)), full((1, d)), full((1, d)),
                  full((1, d))],
        out_specs=[full((t_new, bsz, d)), full((n_st, bsz, d))],
        out_shape=[jax.ShapeDtypeStruct((t_new, bsz, d), BF16),
                   jax.ShapeDtypeStruct((n_st, bsz, d), F32)],
        compiler_params=_cparams("arbitrary"),
    )(st_tm, u_tm, dw, dwb.reshape(1, d), lg.reshape(1, d), lb.reshape(1, d))


def _ffn_up_kernel(x_ref, x2_ref, prev2_ref, wg_ref, wv_ref, dw_ref, dwb_ref, wd_ref, *rest, tiles_per_seq, bsz2,
                   with_ada):
    if with_ada:
        c_ref, adaw_ref, adab_ref = rest[:3]
        a_ref, st_ref, a2_ref, st2_ref, wd_bf_ref, mods_ref, wg_s, wv_s, carry_s = rest[3:]
        mods_ref[...] = _dot(_silu(c_ref[...]).astype(BF16), adaw_ref[...].astype(BF16)) + adab_ref[...]
    else:
        a_ref, st_ref, a2_ref, st2_ref, wd_bf_ref, wg_s, wv_s, carry_s = rest
    i = pl.program_id(1)
    wd_bf_ref[...] = wd_ref[...].astype(BF16)

    @pl.when(i == 0)
    def _():
        wg_s[...] = wg_ref[...].astype(BF16)
        wv_s[...] = wv_ref[...].astype(BF16)

    @pl.when(i % tiles_per_seq == 0)
    def _():
        carry_s[...] = jnp.zeros(carry_s.shape, F32)

    x = x_ref[...]
    g = _dot(x, wg_s[...])
    v = _dot(x, wv_s[...])
    tm = g.shape[0]
    row = lax.broadcasted_iota(jnp.int32, g.shape, 0)
    prev1 = carry_s[V7X_SUBLANES - 1:V7X_SUBLANES, :]
    prev2 = carry_s[V7X_SUBLANES - 2:V7X_SUBLANES - 1, :]
    g1 = jnp.where(row == 0, prev1, pltpu.roll(g, 1, 0))
    g2 = jnp.where(row == 0, prev2, jnp.where(row == 1, prev1, pltpu.roll(g, 2, 0)))
    gc = dw_ref[0:1, :] * g2 + dw_ref[1:2, :] * g1 + dw_ref[2:3, :] * g + dwb_ref[...]
    a_ref[...] = (_silu(gc) * v).astype(a_ref.dtype)
    carry_s[...] = g[tm - V7X_SUBLANES:tm, :]
    st_ref[...] = g[tm - 2:tm, :]

    @pl.when(i == pl.num_programs(1) - 1)
    def _():
        x2 = x2_ref[...]
        g_s = _dot(x2, wg_s[...])
        v_s = _dot(x2, wv_s[...])
        rows = g_s.shape[0]
        full = jnp.concatenate([prev2_ref[...], g_s], axis=0)
        gc_s = (dw_ref[0:1, :] * full[0:rows] + dw_ref[1:2, :] * full[bsz2:bsz2 + rows]
                + dw_ref[2:3, :] * full[2 * bsz2:2 * bsz2 + rows] + dwb_ref[...])
        a2_ref[...] = (_silu(gc_s) * v_s).astype(a2_ref.dtype)
        st2_ref[...] = full[rows:rows + 2 * bsz2]


def _ffn_up(x, x2, prev2_tm, w_up, dw, dwb, w_down, layer, bsz, bsz2, *, tm=1024, tn=512, ada=None):
    m, k = x.shape
    m2 = x2.shape[0]
    n = w_up.shape[2] // 2
    t = m // bsz
    tm, tn = min(tm, t), min(tn, n)
    assert t % tm == 0 and n % tn == 0 and dw.shape[1] == 3 and m2 >= 2 * bsz2
    nj, ni = n // tn, m // tm
    tiles_per_seq = t // tm
    kd, nd = w_down.shape[1], w_down.shape[2]
    slab = kd // (nj * ni)
    assert slab * nj * ni == kd and slab % (2 * V7X_SUBLANES) == 0
    in_specs = [pl.BlockSpec((tm, k), lambda j, i: (i, 0)),
                pl.BlockSpec((m2, k), lambda j, i: (0, 0)),
                pl.BlockSpec((2 * bsz2, tn), lambda j, i: (0, j)),
                pl.BlockSpec((None, k, tn), lambda j, i: (layer, 0, j)),
                pl.BlockSpec((None, k, tn), lambda j, i: (layer, 0, nj + j)),
                pl.BlockSpec((None, 3, tn), lambda j, i: (layer, 0, j)),
                pl.BlockSpec((None, 1, tn), lambda j, i: (layer, 0, j)),
                pl.BlockSpec((None, slab, nd), lambda j, i: (layer, j * ni + i, 0))]
    out_specs = [pl.BlockSpec((tm, tn), lambda j, i: (i, j)),
                 pl.BlockSpec((None, 2, tn), lambda j, i: (i // tiles_per_seq, 0, j)),
                 pl.BlockSpec((m2, tn), lambda j, i: (0, j)),
                 pl.BlockSpec((2 * bsz2, tn), lambda j, i: (0, j)),
                 pl.BlockSpec((None, slab, nd), lambda j, i: (0, j * ni + i, 0))]
    out_shape = [jax.ShapeDtypeStruct((m, n), BF16),
                 jax.ShapeDtypeStruct((bsz, 2, n), F32),
                 jax.ShapeDtypeStruct((m2, n), BF16),
                 jax.ShapeDtypeStruct((2 * bsz2, n), F32),
                 jax.ShapeDtypeStruct((1, kd, nd), BF16)]
    args = [x, x2, prev2_tm, w_up, w_up, dw, dwb.reshape(dwb.shape[0], 1, n), w_down]
    if ada is not None:
        c_all, ada_w, ada_b, first = ada
        depth, kc, n_mod = ada_w.shape
        ta = next(w for w in range(V7X_LANES, n_mod + 1, V7X_LANES)
                  if n_mod % w == 0 and (depth - first) * (n_mod // w) <= nj * ni)
        per_layer = n_mod // ta
        n_slabs = (depth - first) * per_layer
        pos = lambda j, i: jnp.minimum(j * ni + i, n_slabs - 1)
        in_specs += [pl.BlockSpec(c_all.shape, lambda j, i: (0, 0)),
                     pl.BlockSpec((None, kc, ta), lambda j, i: (first + pos(j, i) // per_layer, 0,
                                                                pos(j, i) % per_layer)),
                     pl.BlockSpec((None, 1, ta), lambda j, i: (first + pos(j, i) // per_layer, 0,
                                                               pos(j, i) % per_layer))]
        out_specs.append(pl.BlockSpec((None, c_all.shape[0], ta),
                                      lambda j, i: (pos(j, i) // per_layer, 0, pos(j, i) % per_layer)))
        out_shape.append(jax.ShapeDtypeStruct((depth - first, c_all.shape[0], n_mod), F32))
        args += [c_all, ada_w, ada_b.reshape(depth, 1, n_mod)]
    return pl.pallas_call(
        functools.partial(_ffn_up_kernel, tiles_per_seq=tiles_per_seq, bsz2=bsz2, with_ada=ada is not None),
        grid=(nj, ni),
        in_specs=in_specs,
        out_specs=out_specs,
        out_shape=out_shape,
        scratch_shapes=[pltpu.VMEM((k, tn), BF16), pltpu.VMEM((k, tn), BF16),
                        pltpu.VMEM((V7X_SUBLANES, tn), F32)],
        compiler_params=_cparams("arbitrary", "arbitrary"),
    )(*args)


def _swa_prompt_kernel(sink_ref, q_ref, kc_ref, kp_ref, vc_ref, vp_ref, bias_ref, o_ref, *, n_heads, group, hd,
                       scale):
    n = pl.program_id(1)
    w = q_ref.shape[0]
    kcat = jnp.concatenate([kp_ref[...], kc_ref[...]], axis=0).astype(BF16)
    vcat = jnp.concatenate([vp_ref[...], vc_ref[...]], axis=0).astype(BF16)
    col = lax.broadcasted_iota(jnp.int32, (w, 2 * w), 1)
    keep = (col >= w) | (n > 0)
    scores = []
    for h in range(n_heads):
        kv = h // group
        q = q_ref[:, h * hd:(h + 1) * hd].astype(BF16)
        s = _dot_nt(q, kcat[:, kv * hd:(kv + 1) * hd]) * scale + bias_ref[h]
        scores.append(jnp.where(keep, s, NEG))
    probs, sums = [], []
    for h in range(n_heads):
        sink = sink_ref[h]
        m = jnp.maximum(jnp.max(scores[h], axis=-1, keepdims=True), sink)
        p = jnp.exp(scores[h] - m)
        sums.append(jnp.sum(p, axis=-1, keepdims=True) + jnp.exp(sink - m))
        probs.append(p.astype(BF16))
    for h in range(n_heads):
        kv = h // group
        o = _dot(probs[h], vcat[:, kv * hd:(kv + 1) * hd]) / sums[h]
        o_ref[:, h * hd:(h + 1) * hd] = o.astype(o_ref.dtype)


def _swa_prompt(qkv, sinks, bias, bsz, n_heads, n_kv, hd, window):
    m = qkv.shape[0]
    t = m // bsz
    nb = t // window
    qw, kw = n_heads * hd, n_kv * hd
    assert qw % kw == 0 and t % window == 0
    kcol, vcol = qw // kw, qw // kw + 1
    return pl.pallas_call(
        functools.partial(_swa_prompt_kernel, n_heads=n_heads, group=n_heads // n_kv, hd=hd, scale=hd ** -0.5),
        grid=(bsz, nb),
        in_specs=[pl.BlockSpec(memory_space=pltpu.SMEM),
                  pl.BlockSpec((window, qw), lambda b, n: (b * nb + n, 0)),
                  pl.BlockSpec((window, kw), lambda b, n: (b * nb + n, kcol)),
                  pl.BlockSpec((window, kw), lambda b, n: (b * nb + jnp.maximum(n - 1, 0), kcol)),
                  pl.BlockSpec((window, kw), lambda b, n: (b * nb + n, vcol)),
                  pl.BlockSpec((window, kw), lambda b, n: (b * nb + jnp.maximum(n - 1, 0), vcol)),
                  pl.BlockSpec((n_heads, window, 2 * window), lambda b, n: (0, 0, 0))],
        out_specs=pl.BlockSpec((window, qw), lambda b, n: (b * nb + n, 0)),
        out_shape=jax.ShapeDtypeStruct((m, qw), BF16),
        compiler_params=_cparams("arbitrary", "arbitrary"),
    )(sinks, qkv, qkv, qkv, qkv, qkv, bias)


def _swa_sample_kernel(q_ref, bk_ref, bv_ref, kn_ref, vn_ref, bias_ref, own_ref, sink_ref, o_ref,
                       *, n_kv, group, tp, scale):
    cols, hd = q_ref.shape
    keys = bk_ref.shape[0] // n_kv
    qb = q_ref[...].astype(BF16)
    col_kv = lax.broadcasted_iota(jnp.int32, (keys, cols), 1) // (group * tp)

    def by_head(ref):
        return jnp.concatenate([ref[pl.ds(kv, keys, stride=n_kv), :] for kv in range(n_kv)], axis=0)

    full = _dot_nt(by_head(bk_ref).astype(BF16), qb)
    s = full[0:keys]
    for kv in range(1, n_kv):
        s = jnp.where(col_kv == kv, full[kv * keys:(kv + 1) * keys], s)
    s = s * scale + bias_ref[...]
    s_own = _dot_nt(kn_ref[...].astype(BF16), qb) * scale + own_ref[...]
    sink = sink_ref[...]
    m = jnp.maximum(jnp.maximum(jnp.max(s, axis=0, keepdims=True), jnp.max(s_own, axis=0, keepdims=True)), sink)
    p = jnp.exp(s - m)
    p_own = jnp.exp(s_own - m)
    l = jnp.sum(p, axis=0, keepdims=True) + jnp.sum(p_own, axis=0, keepdims=True) + jnp.exp(sink - m)
    spread = jnp.concatenate([jnp.where(col_kv == kv, p, 0.0) for kv in range(n_kv)], axis=0)
    acc = (_dot(by_head(bv_ref).T.astype(BF16), spread.astype(BF16))
           + _dot(vn_ref[...].T.astype(BF16), p_own.astype(BF16)))
    o_ref[...] = (acc / l).T.astype(o_ref.dtype)


def _swa_sample(q_cols, buf_k, buf_v, k_new, v_new, bias, own, sink_row, *, n_kv, group, tp):
    dbs, cols, hd = q_cols.shape
    per_seq = lambda a: pl.BlockSpec((None,) + a.shape[1:], lambda b: (b, 0, 0))
    shared = lambda a: pl.BlockSpec(a.shape, lambda b: (0, 0))
    return pl.pallas_call(
        functools.partial(_swa_sample_kernel, n_kv=n_kv, group=group, tp=tp, scale=hd ** -0.5),
        grid=(dbs,),
        in_specs=[per_seq(q_cols), per_seq(buf_k), per_seq(buf_v), per_seq(k_new), per_seq(v_new),
                  shared(bias), shared(own), shared(sink_row)],
        out_specs=pl.BlockSpec((None, cols, hd), lambda b: (b, 0, 0)),
        out_shape=jax.ShapeDtypeStruct((dbs, cols, hd), BF16),
        compiler_params=_cparams("arbitrary"),
    )(q_cols, buf_k, buf_v, k_new, v_new, bias, own, sink_row)


def _block_means_kernel(k_ref, o_ref):
    o_ref[...] = jnp.mean(k_ref[...], axis=0, keepdims=True)


def _block_means(qkv, n_blocks_total, kcol, kw):
    return pl.pallas_call(
        _block_means_kernel,
        grid=(n_blocks_total,),
        in_specs=[pl.BlockSpec((MOBA_BLOCK, kw), lambda i: (i, kcol))],
        out_specs=pl.BlockSpec((None, 1, kw), lambda i: (i, 0, 0)),
        out_shape=jax.ShapeDtypeStruct((n_blocks_total, 1, kw), F32),
        compiler_params=_cparams("arbitrary"),
    )(qkv)


def _rank_rows(gate, blk, n_rows):
    rank = jnp.zeros(gate.shape, jnp.int32)
    for m in range(n_rows):
        gm = gate[m:m + 1, :]
        beats = (gm > gate) | ((gm == gate) & (blk > m))
        rank = rank + jnp.where(beats, 1, 0)
    return rank


def _moba_prompt_kernel(qb_ref, kb_ref, pt_ref, q_ref, k_ref, v_ref, means_ref, diag_ref, far_ref, *refs,
                        n_kv, group, hd, scale, n_near, nb, n_pool_pages, ppb):
    del pt_ref
    pool_pages = refs[:n_pool_pages]
    o_ref, pool_means_ref, qs, sel_s, m_s, l_s, acc_s = refs[n_pool_pages:]
    pool_means_ref[...] = _page_block_means(pool_pages, ppb, n_kv)
    qb = qb_ref[pl.program_id(1)]
    kb = kb_ref[pl.program_id(1)]
    blk_rows = q_ref.shape[0]
    cols = group * blk_rows

    @pl.when(kb == 0)
    def _():
        blk = lax.broadcasted_iota(jnp.int32, (nb, cols), 0)
        for kv in range(n_kv):
            qg = jnp.concatenate([q_ref[:, (kv * group + g) * hd:(kv * group + g + 1) * hd]
                                  for g in range(group)], axis=0)
            qs[kv] = (qg * (scale * LOG2E)).astype(BF16)
            gate = _dot_nt(means_ref[:, kv * hd:(kv + 1) * hd], qg, precision=lax.Precision.HIGHEST)
            valid = blk < qb
            gate = jnp.where(valid, gate, -jnp.inf)
            rank = _rank_rows(gate, blk, nb)
            sel_s[kv] = jnp.where(valid & (rank < MOBA_TOPK), 1.0, 0.0)
            m_s[kv] = jnp.full((1, cols), NEG, F32)
            l_s[kv] = jnp.zeros((1, cols), F32)
            acc_s[kv] = jnp.zeros((hd, cols), F32)

    def attend_all(adds_of):
        chunks = [(kv, g) for kv in range(n_kv) for g in range(group)]
        col = lambda g: slice(g * blk_rows, (g + 1) * blk_rows)
        scores = {}
        for kv in range(n_kv):
            k = k_ref[:, kv * hd:(kv + 1) * hd].astype(BF16)
            for g in range(group):
                s = _dot_nt(k, qs[kv, col(g), :])
                for a in adds_of(kv, g):
                    s = s + a
                scores[kv, g] = s
        alphas, probs = {}, {}
        for kv, g in chunks:
            m_prev = m_s[kv, :, col(g)]
            m_new = jnp.maximum(m_prev, jnp.max(scores[kv, g], axis=0, keepdims=True))
            alpha = jnp.exp2(m_prev - m_new)
            p = jnp.exp2(scores[kv, g] - m_new)
            l_s[kv, :, col(g)] = alpha * l_s[kv, :, col(g)] + jnp.sum(p, axis=0, keepdims=True)
            m_s[kv, :, col(g)] = m_new
            alphas[kv, g], probs[kv, g] = alpha, p.astype(BF16)
        for kv in range(n_kv):
            vt = v_ref[:, kv * hd:(kv + 1) * hd].T.astype(BF16)
            for g in range(group):
                acc_s[kv, :, col(g)] = alphas[kv, g] * acc_s[kv, :, col(g)] + _dot(vt, probs[kv, g])

    delta = qb - kb
    cs = lambda g: slice(g * blk_rows, (g + 1) * blk_rows)

    @pl.when(delta < n_near)
    def _():
        def adds_of(kv, g):
            diag = diag_ref[kv * group + g, pl.ds(delta, 1), :]
            tile = pltpu.roll(jnp.broadcast_to(diag, (blk_rows, 2 * blk_rows)), 0, 1, stride=1, stride_axis=0)
            chosen = (sel_s[kv, pl.ds(kb, 1), cs(g)] > 0.5) | (delta == 0)
            return tile[:, blk_rows:], jnp.where(chosen, 0.0, NEG)
        attend_all(adds_of)

    @pl.when(delta >= n_near)
    def _():
        def adds_of(kv, g):
            chosen = sel_s[kv, pl.ds(kb, 1), cs(g)] > 0.5
            return (jnp.where(chosen, far_ref[kv, :, cs(g)], NEG),)
        attend_all(adds_of)

    @pl.when(kb == qb)
    def _():
        for kv in range(n_kv):
            o = acc_s[kv] / l_s[kv]
            for g in range(group):
                h = kv * group + g
                o_ref[:, h * hd:(h + 1) * hd] = o[:, g * blk_rows:(g + 1) * blk_rows].T.astype(o_ref.dtype)


def _moba_prompt(qkv, means, diag, far_rows, pool, pt_flat, page_off, bsz, n_heads, n_kv, hd):
    m = qkv.shape[0]
    t = m // bsz
    nb = t // MOBA_BLOCK
    group = n_heads // n_kv
    qw, kw = n_heads * hd, n_kv * hd
    kcol, vcol = qw // kw, qw // kw + 1
    n_near = diag.shape[1]
    cols = group * MOBA_BLOCK
    pairs = [(qb, kb) for qb in range(nb) for kb in range(qb + 1)]
    qb_tab = jnp.asarray([p[0] for p in pairs], jnp.int32)
    kb_tab = jnp.asarray([p[1] for p in pairs], jnp.int32)
    n_pairs = len(pairs)

    page_rows = pool.shape[1]
    ppb = MOBA_BLOCK // (page_rows // n_kv)
    total_pages = pt_flat.shape[0]
    pps = _means_pages_per_step(total_pages, bsz * n_pairs, ppb, n_kv)
    mean_steps = total_pages // pps
    mean_rows = (pps // ppb) * n_kv
    slab = lambda b, s: jnp.minimum(b * n_pairs + s, mean_steps - 1)

    def page_spec(i):
        return pl.BlockSpec((None, page_rows, hd),
                            lambda b, s, qt, kt, pt: (pt[slab(b, s) * pps + i] + page_off, 0, 0))

    grid_spec = pltpu.PrefetchScalarGridSpec(
        num_scalar_prefetch=3,
        grid=(bsz, n_pairs),
        in_specs=[pl.BlockSpec((MOBA_BLOCK, qw), lambda b, s, qt, kt, pt: (b * nb + qt[s], 0)),
                  pl.BlockSpec((MOBA_BLOCK, kw), lambda b, s, qt, kt, pt: (b * nb + kt[s], kcol)),
                  pl.BlockSpec((MOBA_BLOCK, kw), lambda b, s, qt, kt, pt: (b * nb + kt[s], vcol)),
                  pl.BlockSpec((None, nb, kw), lambda b, s, qt, kt, pt: (b, 0, 0)),
                  pl.BlockSpec(diag.shape, lambda b, s, qt, kt, pt: (0, 0, 0)),
                  pl.BlockSpec((n_kv, 1, cols), lambda b, s, qt, kt, pt: (0, 0, 0))]
                 + [page_spec(i) for i in range(pps)],
        out_specs=[pl.BlockSpec((MOBA_BLOCK, qw), lambda b, s, qt, kt, pt: (b * nb + qt[s], 0)),
                   pl.BlockSpec((mean_rows, hd), lambda b, s, qt, kt, pt: (slab(b, s), 0))],
        scratch_shapes=[pltpu.VMEM((n_kv, cols, hd), BF16),
                        pltpu.VMEM((n_kv, nb, cols), F32),
                        pltpu.VMEM((n_kv, 1, cols), F32),
                        pltpu.VMEM((n_kv, 1, cols), F32),
                        pltpu.VMEM((n_kv, hd, cols), F32)],
    )
    return pl.pallas_call(
        functools.partial(_moba_prompt_kernel, n_kv=n_kv, group=group, hd=hd, scale=hd ** -0.5,
                          n_near=n_near, nb=nb, n_pool_pages=pps, ppb=ppb),
        grid_spec=grid_spec,
        out_shape=[jax.ShapeDtypeStruct((m, qw), BF16),
                   jax.ShapeDtypeStruct(((total_pages // ppb) * n_kv, hd), F32)],
        compiler_params=_cparams("arbitrary", "arbitrary"),
    )(qb_tab, kb_tab, pt_flat, qkv, qkv, qkv, means, diag, far_rows, *([pool] * pps))


_ATTN_PAGES_PER_STEP = 32


def _page_block_means(pages, ppb, n_kv):
    hd = pages[0].shape[1]
    sub = V7X_SUBLANES
    reps = sub // n_kv
    folded = []
    for blk in range(len(pages) // ppb):
        rows = pages[blk * ppb][...]
        for p in range(1, ppb):
            rows = rows + pages[blk * ppb + p][...]
        chains = sub if rows.shape[0] % (sub * sub) == 0 else 1
        tot = jnp.sum(jnp.sum(rows.reshape(chains, -1, sub, hd), axis=1), axis=0)
        full = tot
        for i in range(1, reps):
            full = full + pltpu.roll(tot, i * n_kv, 0)
        folded.append(full)
    row_grp = lax.broadcasted_iota(jnp.int32, (sub, hd), 0) // n_kv
    tiles = []
    for a in range(0, len(folded), reps):
        tile = folded[a]
        for i in range(1, reps):
            tile = jnp.where(row_grp == i, folded[a + i], tile)
        tiles.append(tile)
    n_keys = ppb * pages[0].shape[0] // n_kv
    return jnp.concatenate(tiles, axis=0) * (1.0 / n_keys)


def _means_pages_per_step(total_pages, steps, ppb, n_kv):
    unit = ppb * max(1, V7X_SUBLANES // n_kv)
    pps = unit
    while pps * steps < total_pages or total_pages % pps:
        pps += unit
    return pps


def _cols_expand_kernel(tab_ref, idx_ref, o_ref, *, buckets, mult):
    idx = idx_ref[...]
    acc = jnp.full(idx.shape, NEG, F32)
    for b in buckets:
        acc = jnp.where(idx == b, tab_ref[b:b + 1, :] * mult, acc)
    o_ref[...] = acc


def _cols_expand(tab_cols, idx_np, tile_rows, mult=1.0):
    rows, cols = idx_np.shape
    buckets = tuple(int(b) for b in np.unique(idx_np) if b >= 0)
    return pl.pallas_call(
        functools.partial(_cols_expand_kernel, buckets=buckets, mult=mult),
        grid=(rows // tile_rows,),
        in_specs=[pl.BlockSpec(tab_cols.shape, lambda i: (0, 0)),
                  pl.BlockSpec((tile_rows, cols), lambda i: (i, 0))],
        out_specs=pl.BlockSpec((tile_rows, cols), lambda i: (i, 0)),
        out_shape=jax.ShapeDtypeStruct((rows, cols), F32),
        compiler_params=_cparams("arbitrary"),
    )(tab_cols, jnp.asarray(idx_np))


def _moba_sample_kernel(pt_ref, *refs, pps, ppb, n_kv, group, tp, n_blk, n_pages, far_cls, scale):
    del pt_ref
    k_pages = refs[:pps]
    v_pages = refs[pps:2 * pps]
    q_ref, means_ref, bias_ref, own_ref, kn_ref, vn_ref, o_ref, sel_s, m_s, l_s, acc_s = refs[2 * pps:]
    step = pl.program_id(1)
    cols, hd = q_ref.shape

    @pl.when(step == 0)
    def _():
        q = q_ref[...]
        blk = lax.broadcasted_iota(jnp.int32, (n_blk, cols), 0)
        col_kv = lax.broadcasted_iota(jnp.int32, (n_blk, cols), 1) // (group * tp)
        gate = jnp.zeros((n_blk, cols), F32)
        for kv in range(n_kv):
            g = _dot_nt(means_ref[kv * n_blk:(kv + 1) * n_blk, :], q, precision=lax.Precision.HIGHEST)
            gate = jnp.where(col_kv == kv, g, gate)
        rank = _rank_rows(gate, blk, n_blk)
        sel_s[...] = jnp.where(rank < MOBA_TOPK, 0.0, NEG)
        m_s[...] = jnp.full(m_s.shape, NEG, F32)
        l_s[...] = jnp.zeros(l_s.shape, F32)
        acc_s[...] = jnp.zeros(acc_s.shape, F32)

    qb = (q_ref[...] * (scale * LOG2E)).astype(BF16)
    page_keys = k_pages[0].shape[0] // n_kv
    col_kv = lax.broadcasted_iota(jnp.int32, (page_keys, cols), 1) // (group * tp)

    def accumulate(scores, values, expand):
        m_prev = m_s[...]
        m_new = m_prev
        for s in scores:
            m_new = jnp.maximum(m_new, jnp.max(s, axis=0, keepdims=True))
        alpha = jnp.exp2(m_prev - m_new)
        l_new = alpha * l_s[...]
        acc = alpha * acc_s[...]
        for s, v in zip(scores, values):
            p = jnp.exp2(s - m_new)
            l_new = l_new + jnp.sum(p, axis=0, keepdims=True)
            acc = acc + _dot(v.T.astype(BF16), expand(p).astype(BF16))
        m_s[...] = m_new
        l_s[...] = l_new
        acc_s[...] = acc

    def by_head(ref):
        return jnp.concatenate([ref[pl.ds(kv, page_keys, stride=n_kv), :] for kv in range(n_kv)], axis=0)

    def spread(p):
        return jnp.concatenate([jnp.where(col_kv == kv, p, 0.0) for kv in range(n_kv)], axis=0)

    scores, values = [], []
    for i in range(pps):
        page = step * pps + i
        cls = jnp.minimum(n_pages - 1 - page, far_cls)
        full = _dot_nt(by_head(k_pages[i]).astype(BF16), qb)
        s = full[0:page_keys]
        for kv in range(1, n_kv):
            s = jnp.where(col_kv == kv, full[kv * page_keys:(kv + 1) * page_keys], s)
        scores.append(s + bias_ref[cls] + sel_s[pl.ds(page // ppb, 1), :])
        values.append(by_head(v_pages[i]))
    accumulate(scores, values, spread)

    @pl.when(step == pl.num_programs(1) - 1)
    def _():
        s_own = _dot_nt(kn_ref[...].astype(BF16), qb) + own_ref[...]
        accumulate([s_own], [vn_ref[...]], lambda p: p)
        o_ref[...] = (acc_s[...] / l_s[...]).T


def _moba_sample(q_cols, means_t, k_new, v_new, pool_k, pool_v, pt_flat, page_off, bias_pages, bias_own, *,
                 n_kv, group, tp, n_pages):
    dbs, cols, hd = q_cols.shape
    rows = pool_k.shape[1]
    ppb = MOBA_BLOCK // (rows // n_kv)
    n_blk = n_pages // ppb
    pps = min(_ATTN_PAGES_PER_STEP, n_pages)
    assert n_pages % pps == 0
    far_cls = bias_pages.shape[0] - 1
    n_own = k_new.shape[1]

    def page_spec(i):
        return pl.BlockSpec((None, rows, hd),
                            lambda b, s, pt: (pt[b * n_pages + s * pps + i] + page_off, 0, 0))

    per_seq = lambda r: pl.BlockSpec((None, r, hd), lambda b, s, pt: (b, 0, 0))
    in_specs = ([page_spec(i) for i in range(pps)] * 2
                + [per_seq(cols), per_seq(n_kv * n_blk),
                   pl.BlockSpec(bias_pages.shape, lambda b, s, pt: (0, 0, 0)),
                   pl.BlockSpec(bias_own.shape, lambda b, s, pt: (0, 0)),
                   per_seq(n_own), per_seq(n_own)])
    grid_spec = pltpu.PrefetchScalarGridSpec(
        num_scalar_prefetch=1,
        grid=(dbs, n_pages // pps),
        in_specs=in_specs,
        out_specs=pl.BlockSpec((None, cols, hd), lambda b, s, pt: (b, 0, 0)),
        scratch_shapes=[pltpu.VMEM((n_blk, cols), F32), pltpu.VMEM((1, cols), F32),
                        pltpu.VMEM((1, cols), F32), pltpu.VMEM((hd, cols), F32)],
    )
    return pl.pallas_call(
        functools.partial(_moba_sample_kernel, pps=pps, ppb=ppb, n_kv=n_kv, group=group, tp=tp, n_blk=n_blk,
                          n_pages=n_pages, far_cls=far_cls, scale=hd ** -0.5),
        grid_spec=grid_spec,
        out_shape=jax.ShapeDtypeStruct((dbs, cols, hd), F32),
        compiler_params=_cparams("arbitrary", "arbitrary"),
    )(pt_flat, *([pool_k] * pps), *([pool_v] * pps), q_cols, means_t, bias_pages, bias_own, k_new, v_new)


def _swa_prompt_bias_idx(window, nbk):
    qi = np.arange(window)[:, None]
    kj = np.arange(2 * window)[None, :]
    dist = qi + window - kj
    return np.where((dist >= 0) & (dist < window), _bucket_np(dist, nbk), -1).astype(np.int32)


def _swa_sample_bias_idx(window, n_heads, tp, t_new, nbk):
    t = np.arange(n_heads * tp) % tp
    r = np.arange(window)
    dist = t[None, :] + window - r[:, None]
    idx = np.where((dist >= 0) & (dist < window), _bucket_np(dist, nbk), -1)
    return np.where(t[None, :] < t_new, idx, 0).astype(np.int32)


def _moba_prompt_diag_idx(n_near, nbk):
    dist = np.arange(n_near)[:, None] * MOBA_BLOCK + np.arange(2 * MOBA_BLOCK)[None, :] - MOBA_BLOCK
    return np.where(dist >= 0, _bucket_np(dist, nbk), -1).astype(np.int32)


def _moba_sample_page_idx(far_cls, page, n_heads, tp, t_new, nbk):
    cols = n_heads * tp
    t = np.arange(cols) % tp
    r = np.arange(page)
    tiles = []
    for c in range(far_cls + 1):
        dist = (c + 1) * page + t[None, :] - r[:, None]
        b = _bucket_np(dist, nbk) if c < far_cls else np.full(dist.shape, nbk - 1, np.int32)
        tiles.append(np.where(t[None, :] < t_new, b, 0))
    return np.concatenate(tiles, axis=0).astype(np.int32)


def _moba_sample_own_idx(n_kv, n_heads, group, tp, t_new, nbk):
    cols = n_heads * tp
    h, t = np.arange(cols) // tp, np.arange(cols) % tp
    tk, kvc = np.arange(tp * n_kv) // n_kv, np.arange(tp * n_kv) % n_kv
    ok = ((kvc[:, None] == (h // group)[None, :]) & (tk[:, None] <= t[None, :]) & (t[None, :] < t_new))
    return np.where(ok, _bucket_np(t[None, :] - tk[:, None], nbk), -1).astype(np.int32)


def kernel(x_prompt, x_sample, c_prompt, c_sample, state_conv, cache_swa_k, cache_swa_v, cache_moba_k, cache_moba_v, page_table, state_ffn, ada_w, ada_b, norm_mix, norm_ffn, norm_final, rel_bias, conv_w1, conv_b1, conv_dw, conv_dw_b, conv_ln_g, conv_ln_b, conv_w2, conv_b2, swa_wqkv, swa_wo, swa_sinks, moba_wqkv, moba_wo, ffn_w_up, ffn_dw, ffn_dw_b, ffn_w_down):
    bsz, seq, d = x_prompt.shape
    dbs, t_new, _ = x_sample.shape
    depth = ada_w.shape[0]
    window, n_kv, hd = cache_swa_k.shape[2], cache_swa_k.shape[3], cache_swa_k.shape[4]
    nbk, n_heads = rel_bias.shape
    group = n_heads // n_kv
    qw, kw = n_heads * hd, n_kv * hd
    cw = conv_dw.shape[1]
    d_ff = ffn_dw.shape[2]
    n_pool, page_size = cache_moba_k.shape[1], cache_moba_k.shape[2]
    n_pages = page_table.shape[1]
    ppb = MOBA_BLOCK // page_size
    n_blk_past = n_pages // ppb
    tp = V7X_SUBLANES
    assert t_new <= tp and n_pages % ppb == 0 and n_blk_past >= MOBA_TOPK and seq % MOBA_BLOCK == 0
    assert ada_w.shape[2] == N_MOD * d
    n_mixers = 3

    n_c = bsz + dbs
    c_rows = -(-n_c // V7X_SUBLANES) * V7X_SUBLANES
    c_all = jnp.concatenate([c_prompt, c_sample, jnp.zeros((c_rows - n_c, d), F32)], axis=0)
    mod_parts = [_ada(c_all, ada_w, ada_b, 1)]

    def mods(layer, which):
        part = mod_parts[0 if layer == 0 else 1]
        chunk = part[layer if layer == 0 else layer - 1, :, which * d:(which + 1) * d]
        return chunk[:bsz], chunk[bsz:bsz + dbs]

    d_sat = _saturation_distance(nbk)
    n_near = -(-(d_sat + MOBA_BLOCK - 1) // MOBA_BLOCK)
    swa_bias_p = swa_bias_s = swa_bias_own = moba_bias_p = moba_far = moba_bias_sel = moba_bias_own = None
    tab_cols = jnp.repeat(rel_bias, tp, axis=1)
    own_idx = _moba_sample_own_idx(n_kv, n_heads, group, tp, t_new, nbk)
    if depth > 1:
        swa_bias_p = _bias_expand(rel_bias, _swa_prompt_bias_idx(window, nbk), window,
                                  (n_heads, window, 2 * window), (None, window, 2 * window),
                                  lambda h: (h, 0, 0))
        swa_bias_s = _cols_expand(tab_cols, _swa_sample_bias_idx(window, n_heads, tp, t_new, nbk), window)
        swa_bias_own = _cols_expand(tab_cols, own_idx, tp * n_kv)
    if depth > 2:
        cols = group * MOBA_BLOCK
        moba_bias_p = _bias_expand(
            rel_bias, _moba_prompt_diag_idx(n_near, nbk), n_near,
            (n_heads, n_near, 2 * MOBA_BLOCK), (None, n_near, 2 * MOBA_BLOCK),
            lambda h: (h, 0, 0), mult=LOG2E)
        moba_far = jnp.repeat(rel_bias[nbk - 1].reshape(n_kv, group, 1), MOBA_BLOCK, axis=2).reshape(n_kv, 1, cols)
        moba_far = moba_far * LOG2E
        far_cls = -(-(d_sat + page_size - 1) // page_size) - 1
        moba_bias_sel = _cols_expand(
            tab_cols, _moba_sample_page_idx(far_cls, page_size, n_heads, tp, t_new, nbk),
            page_size, mult=LOG2E).reshape(far_cls + 1, page_size, n_heads * tp)
        moba_bias_own = _cols_expand(tab_cols, own_idx, tp * n_kv, mult=LOG2E)

    def to_tm(a):
        return jnp.swapaxes(a, 0, 1).reshape((a.shape[1] * dbs,) + a.shape[2:])

    def to_bm(a, t):
        return jnp.swapaxes(a.reshape((t, dbs) + a.shape[1:]), 0, 1)

    xp = x_prompt.reshape(bsz * seq, d)
    xs = to_tm(x_sample)
    m_s = t_new * dbs

    conv_p, conv_s, swa_k_p, swa_v_p, swa_k_s, swa_v_s = [], [], [], [], [], []
    moba_k_p, moba_v_p, moba_k_s, moba_v_s, ffn_p, ffn_s = [], [], [], [], [], []

    def gates(layer, which):
        gp, gs = mods(layer, which)
        return gp.reshape(bsz, 1, d), jnp.tile(gs, (t_new, 1)).reshape(1, m_s, d)

    def normed(x_p, x_s, g, layer, which_sh):
        shp, shs = mods(layer, which_sh)
        scp, scs = mods(layer, which_sh + 1)
        hp = _norm_mod(x_p.reshape(bsz, seq, d), g, scp.reshape(bsz, 1, d), shp.reshape(bsz, 1, d), 1024)
        hs = _norm_mod(x_s.reshape(t_new, dbs, d), g, scs.reshape(1, dbs, d), shs.reshape(1, dbs, d), dbs)
        return hp.reshape(bsz * seq, d), hs.reshape(m_s, d)

    def sample_views(qkv_bm):
        pad = jnp.concatenate([qkv_bm, jnp.zeros((dbs, tp - t_new, qkv_bm.shape[2]), qkv_bm.dtype)], axis=1)
        q_cols = pad[:, :, :qw].reshape(dbs, tp, n_heads, hd).transpose(0, 2, 1, 3).reshape(dbs, n_heads * tp, hd)
        return (q_cols, pad[:, :, qw:qw + kw].reshape(dbs, tp * n_kv, hd),
                pad[:, :, qw + kw:].reshape(dbs, tp * n_kv, hd))

    def cols_to_bm(o_cols):
        o = o_cols.reshape(dbs, n_heads, tp, hd)[:, :, :t_new].transpose(0, 2, 1, 3)
        return o.reshape(dbs, t_new, qw)

    for layer in range(depth):
        kind, j = layer % n_mixers, layer // n_mixers
        hp, hs = normed(xp, xs, norm_mix[layer], layer, 0)
        g1p, g1s = gates(layer, 2)
        if kind == 0:
            up, us = _glu(hp, hs, conv_w1, conv_b1, j)
            yp = _conv_ln_prompt(up.reshape(bsz, seq, d), conv_dw[j], conv_dw_b[j], conv_ln_g[j], conv_ln_b[j])
            st_tm = jnp.swapaxes(state_conv[j], 0, 1)
            ys, nst_tm = _conv_ln_sample(st_tm, us.reshape(t_new, dbs, d), conv_dw[j], conv_dw_b[j],
                                         conv_ln_g[j], conv_ln_b[j])
            conv_p.append(up.reshape(bsz, seq, d)[:, seq - (cw - 1):])
            conv_s.append(jnp.swapaxes(nst_tm, 0, 1))
            xp, xs = _mm(yp.reshape(bsz * seq, d), conv_w2, j, bias=conv_b2, res=xp, gate=g1p,
                         side=(ys.reshape(m_s, d), xs, g1s[0]))
        elif kind == 1:
            qkv_p, qkv_s = _mm(hp, swa_wqkv, j, side=(hs, None, None))
            op = _swa_prompt(qkv_p, swa_sinks[j], swa_bias_p, bsz, n_heads, n_kv, hd, window)
            q_cols, k_new, v_new = sample_views(to_bm(qkv_s, t_new))
            buf_k = cache_swa_k[j].reshape(dbs, window * n_kv, hd)
            buf_v = cache_swa_v[j].reshape(dbs, window * n_kv, hd)
            sink_row = jnp.repeat(swa_sinks[j], tp).reshape(1, n_heads * tp)
            o_cols = _swa_sample(q_cols, buf_k, buf_v, k_new, v_new, swa_bias_s, swa_bias_own, sink_row,
                                 n_kv=n_kv, group=group, tp=tp)
            os_ = to_tm(cols_to_bm(o_cols))
            tail = qkv_p.reshape(bsz, seq, qw + 2 * kw)[:, seq - window:]
            swa_k_p.append(tail[:, :, qw:qw + kw].reshape(bsz, window, n_kv, hd))
            swa_v_p.append(tail[:, :, qw + kw:].reshape(bsz, window, n_kv, hd))
            keep = t_new * n_kv
            swa_k_s.append(jnp.concatenate([buf_k[:, keep:], k_new[:, :keep]], axis=1)
                           .reshape(dbs, window, n_kv, hd))
            swa_v_s.append(jnp.concatenate([buf_v[:, keep:], v_new[:, :keep]], axis=1)
                           .reshape(dbs, window, n_kv, hd))
            xp, xs = _mm(op, swa_wo, j, res=xp, gate=g1p, side=(os_, xs, g1s[0]))
        else:
            qkv_p, k_rows, v_rows, qkv_s = _mm(hp, moba_wqkv, j, kv_heads=(n_kv, hd), side=(hs, None, None))
            nb = seq // MOBA_BLOCK
            means_p = _block_means(qkv_p, bsz * nb, qw // kw, kw).reshape(bsz, nb, kw)
            pool_k = cache_moba_k.reshape(cache_moba_k.shape[0] * n_pool, page_size * n_kv, hd)
            pool_v = cache_moba_v.reshape(cache_moba_v.shape[0] * n_pool, page_size * n_kv, hd)
            pt_flat = page_table.reshape(-1)
            op, means_s = _moba_prompt(qkv_p, means_p, moba_bias_p, moba_far, pool_k, pt_flat, j * n_pool,
                                       bsz, n_heads, n_kv, hd)
            moba_k_p.append(k_rows.reshape(bsz, seq // page_size, page_size, n_kv, hd))
            moba_v_p.append(v_rows.reshape(bsz, seq // page_size, page_size, n_kv, hd))

            qkv_bm = to_bm(qkv_s, t_new)
            q_cols, k_new, v_new = sample_views(qkv_bm)
            means_t = means_s.reshape(dbs, n_blk_past, n_kv, hd).transpose(0, 2, 1, 3)
            means_t = means_t.reshape(dbs, n_kv * n_blk_past, hd)
            o_cols = _moba_sample(q_cols, means_t, k_new, v_new, pool_k, pool_v, pt_flat, j * n_pool,
                                  moba_bias_sel, moba_bias_own, n_kv=n_kv, group=group, tp=tp, n_pages=n_pages)
            os_bm = cols_to_bm(o_cols).astype(BF16)
            moba_k_s.append(qkv_bm[:, :, qw:qw + kw].reshape(dbs, t_new, n_kv, hd))
            moba_v_s.append(qkv_bm[:, :, qw + kw:].reshape(dbs, t_new, n_kv, hd))
            xp, xs = _mm(op, moba_wo, j, res=xp, gate=g1p, side=(to_tm(os_bm), xs, g1s[0]))

        hp, hs = normed(xp, xs, norm_ffn[layer], layer, 3)
        g2p, g2s = gates(layer, 5)
        prev_tm = jnp.swapaxes(state_ffn[layer], 0, 1).reshape(2 * dbs, d_ff)
        ride = (c_all, ada_w, ada_b, 1) if (layer == 0 and depth > 1) else None
        outs = _ffn_up(hp, hs, prev_tm, ffn_w_up, ffn_dw, ffn_dw_b, ffn_w_down, layer, bsz, dbs, ada=ride)
        ap, stp, as_, sts, w_down_bf16 = outs[:5]
        if ride is not None:
            mod_parts.append(outs[5])
        ffn_p.append(stp)
        ffn_s.append(jnp.swapaxes(sts.reshape(2, dbs, d_ff), 0, 1))
        xp, xs = _mm(ap, w_down_bf16, 0, res=xp, gate=g2p, tn=512, weights_outer=False,
                     side=(as_, xs, g2s[0]))

    y_p = _norm_mod(xp.reshape(bsz, seq, d), norm_final, None, None, 1024, out_dtype=F32)
    y_s = _norm_mod(xs.reshape(t_new, dbs, d), norm_final, None, None, dbs, out_dtype=F32)
    y_s = jnp.swapaxes(y_s, 0, 1)

    return (y_p, y_s, jnp.stack(conv_p), jnp.stack(conv_s), jnp.stack(swa_k_p), jnp.stack(swa_v_p),
            jnp.stack(swa_k_s), jnp.stack(swa_v_s), jnp.stack(moba_k_p), jnp.stack(moba_v_p),
            jnp.stack(moba_k_s), jnp.stack(moba_v_s), jnp.stack(ffn_p), jnp.stack(ffn_s))
```
